```python
import jax, jax.numpy as jnp
from jax import lax
import numpy as np

D_MODEL = 1024
BATCH = 4
SEQ = 4096
DEPTH = 1
DEC_BATCH = 32
DEC_SEQ = 8
PAST_LEN = 8192
PAGE_SIZE = 128

NSA_HEADS = 8
NSA_KV_HEADS = 2
NSA_GROUP = NSA_HEADS // NSA_KV_HEADS
NSA_HD = 64
CMP_BLOCK = 32
CMP_STRIDE = 16
CMP_HIDDEN = 2 * NSA_HD
SLC_BLOCK = 64
N_SELECT = 16
N_LOCAL_FORCED = 2
FORCE_BONUS = 1.0e4
WINDOW = 512
Q_BLOCK = 128
RET_HEADS = 4
RET_DK = 64
RET_DV = 128
RET_CHUNK = 128
ROPE_BASE = 10000.0
MEM_LEN = 256
MEM_HEADS = 4
MEM_HD = 128
N_EXPERTS = 32
TOP_K = 4
D_FF = D_MODEL
SWIGLU_ALPHA = 1.702
SWIGLU_LIMIT = 7.0
N_BRANCHES = 3
EPS = 1e-6
NEG = -1e30

NSA_Q_W = NSA_HEADS * NSA_HD
NSA_KV_W = NSA_KV_HEADS * NSA_HD
RET_QK_W = RET_HEADS * RET_DK
RET_V_W = RET_HEADS * RET_DV
MEM_W = MEM_HEADS * MEM_HD
COL_WIDTHS = (NSA_Q_W,) + (NSA_KV_W,) * 6 + (3 * NSA_HEADS, RET_QK_W, RET_QK_W, RET_V_W, RET_V_W, MEM_W, N_BRANCHES * D_MODEL)
D_IN = sum(COL_WIDTHS)

kernel_name = 'hybrid_nsa_retention_moe_step'


def rms_norm(x, g):
    xf = x.astype(jnp.float32)
    y = xf * lax.rsqrt(jnp.mean(xf * xf, axis=-1, keepdims=True) + EPS)
    return (y * g.astype(jnp.float32)).astype(x.dtype)


def masked_softmax(s, mask):
    s = jnp.where(mask, s.astype(jnp.float32), NEG)
    p = jnp.exp(s - jnp.max(s, axis=-1, keepdims=True)) * mask
    return p / jnp.maximum(jnp.sum(p, axis=-1, keepdims=True), 1e-30)


def rotary(x, pos):
    half = x.shape[-1] // 2
    inv = ROPE_BASE ** (-jnp.arange(half, dtype=jnp.float32) / half)
    ang = pos.astype(jnp.float32)[:, None] * inv[None, :]
    cos = jnp.cos(ang)[None, :, None, :]
    sin = jnp.sin(ang)[None, :, None, :]
    x1, x2 = x[..., :half], x[..., half:]
    return jnp.concatenate([x1 * cos - x2 * sin, x1 * sin + x2 * cos], axis=-1)


def project_tokens(x, pos, g_attn, w_in, g_q_nsa, g_k_slc, g_k_win, g_q_mem):
    B, T, _ = x.shape
    h = rms_norm(x, g_attn) @ w_in
    (q, kc, vc, ks, vs, kw, vw, ng, rq, rk, rv, rg, mq, bg) = jnp.split(h, np.cumsum(COL_WIDTHS)[:-1].tolist(), axis=-1)
    kvh = lambda a: a.reshape(B, T, NSA_KV_HEADS, NSA_HD)
    q = rms_norm(q.reshape(B, T, NSA_KV_HEADS, NSA_GROUP, NSA_HD), g_q_nsa) * NSA_HD ** -0.5
    rq = rotary(rq.reshape(B, T, RET_HEADS, RET_DK), pos)
    rk = rotary(rk.reshape(B, T, RET_HEADS, RET_DK), pos) * RET_DK ** -0.5
    rv = rv.reshape(B, T, RET_HEADS, RET_DV)
    mq = rms_norm(mq.reshape(B, T, MEM_HEADS, MEM_HD), g_q_mem) * MEM_HD ** -0.5
    return (q, kvh(kc), kvh(vc), rms_norm(kvh(ks), g_k_slc), kvh(vs), rms_norm(kvh(kw), g_k_win), kvh(vw), ng, rq, rk, rv, rg, mq, bg)


def compress(rows, pe, w1, w2):
    B, L, H, Dh = rows.shape
    n = L // CMP_STRIDE
    r = CMP_BLOCK // CMP_STRIDE
    nc = n - r + 1
    ch = rows.reshape(B, n, CMP_STRIDE, H, Dh)
    blk = jnp.concatenate([ch[:, i:i + nc] for i in range(r)], axis=2) + pe[:, None, :]
    flat = jnp.moveaxis(blk, 3, 2).reshape(B, nc, H, CMP_BLOCK * Dh)
    return jax.nn.silu(flat @ w1) @ w2


def nsa_global(q, kc, vc, ks, vs, q_pos, pe_ck, w_ck1, w_ck2, pe_cv, w_cv1, w_cv2, g_k_cmp):
    B, L, H, Dh = kc.shape
    Q = q.shape[1]
    Lp = -(-L // SLC_BLOCK) * SLC_BLOCK
    padw = ((0, 0), (0, Lp - L), (0, 0), (0, 0))
    kc, vc, ks, vs = (jnp.pad(a, padw) for a in (kc, vc, ks, vs))
    ck = rms_norm(compress(kc, pe_ck, w_ck1, w_ck2), g_k_cmp)
    cv = compress(vc, pe_cv, w_cv1, w_cv2)
    nc = ck.shape[1]
    c_start = jnp.arange(nc) * CMP_STRIDE
    cmask = (c_start + CMP_BLOCK - 1)[None, :] <= q_pos[:, None]
    pc = masked_softmax(jnp.einsum('bqhgd,bnhd->bqhgn', q, ck), cmask[None, :, None, None, :])
    o_cmp = jnp.einsum('bqhgn,bnhd->bqhgd', pc.astype(cv.dtype), cv)
    ns = Lp // SLC_BLOCK
    s_start = jnp.arange(ns) * SLC_BLOCK
    overlap = ((c_start[:, None] < s_start[None, :] + SLC_BLOCK) & (c_start[:, None] + CMP_BLOCK > s_start[None, :])).astype(jnp.float32)
    imp = jnp.einsum('bqhn,ns->bqhs', pc.sum(axis=3), overlap)
    blk = jnp.arange(ns)[None, :]
    cur = (q_pos // SLC_BLOCK)[:, None]
    valid = blk <= cur
    forced = (blk == 0) | (valid & (blk > cur - N_LOCAL_FORCED))
    score = jnp.where(valid[None, :, None, :], imp + jnp.where(forced, FORCE_BONUS, 0.0)[None, :, None, :], NEG)
    _, idx = lax.top_k(score, min(N_SELECT, ns))
    kb = jnp.moveaxis(ks.reshape(B, ns, SLC_BLOCK, H, Dh), 3, 1)
    vb = jnp.moveaxis(vs.reshape(B, ns, SLC_BLOCK, H, Dh), 3, 1)
    take = jax.vmap(jax.vmap(lambda t, i: t[i], in_axes=(0, 1), out_axes=1))

    def slc(args):
        qb, ib, pb = args
        qn = ib.shape[1]
        kg = take(kb, ib)
        vg = take(vb, ib)
        kpos = ib[..., None] * SLC_BLOCK + jnp.arange(SLC_BLOCK)
        m = (kpos <= pb[None, :, None, None, None]).reshape(B, qn, H, 1, -1)
        s = jnp.einsum('bqhgd,bqhkld->bqhgkl', qb, kg).reshape(B, qn, H, NSA_GROUP, -1)
        p = masked_softmax(s, m)
        return jnp.einsum('bqhgn,bqhnd->bqhgd', p.astype(vg.dtype), vg.reshape(B, qn, H, -1, Dh))

    if Q <= Q_BLOCK:
        o_slc = slc((q, idx, q_pos))
    else:
        nq = Q // Q_BLOCK
        blocks = lambda a: jnp.moveaxis(a.reshape(B, nq, Q_BLOCK, *a.shape[2:]), 1, 0)
        o = lax.map(slc, (blocks(q), blocks(idx), q_pos.reshape(nq, Q_BLOCK)))
        o_slc = jnp.moveaxis(o, 0, 1).reshape(q.shape)
    return o_cmp, o_slc


def window_prompt(q, kw, vw):
    B, S, H, Dh = kw.shape
    nb = S // Q_BLOCK
    nw = WINDOW // Q_BLOCK
    padw = ((0, 0), (WINDOW, 0), (0, 0), (0, 0))

    def band(a):
        ap = jnp.pad(a, padw).reshape(B, nb + nw, Q_BLOCK, H, Dh)
        return jnp.concatenate([ap[:, i:i + nb] for i in range(nw + 1)], axis=2)

    kband, vband = band(kw), band(vw)
    qpos = jnp.arange(nb)[:, None] * Q_BLOCK + jnp.arange(Q_BLOCK)[None, :]
    kpos = jnp.arange(nb)[:, None] * Q_BLOCK - WINDOW + jnp.arange((nw + 1) * Q_BLOCK)[None, :]
    d = qpos[:, :, None] - kpos[:, None, :]
    mask = (d >= 0) & (d < WINDOW) & (kpos[:, None, :] >= 0)
    s = jnp.einsum('bnqhgd,bnkhd->bnqhgk', q.reshape(B, nb, Q_BLOCK, H, NSA_GROUP, Dh), kband)
    p = masked_softmax(s, mask[None, :, :, None, None, :])
    o = jnp.einsum('bnqhgk,bnkhd->bnqhgd', p.astype(vband.dtype), vband)
    return o.reshape(B, S, H, NSA_GROUP, Dh)


def window_sample(q, kw, vw, cache_win, q_pos):
    wb = cache_win.shape[1]
    kv_all = jnp.concatenate([cache_win, jnp.stack([kw, vw], axis=2)], axis=1)
    kpos = PAST_LEN - wb + jnp.arange(kv_all.shape[1])
    d = q_pos[:, None] - kpos[None, :]
    mask = (d >= 0) & (d < WINDOW)
    s = jnp.einsum('bqhgd,bkhd->bqhgk', q, kv_all[:, :, 0])
    p = masked_softmax(s, mask[None, :, None, None, :])
    o = jnp.einsum('bqhgk,bkhd->bqhgd', p.astype(kv_all.dtype), kv_all[:, :, 1])
    return o, kv_all[:, -wb:]


def ret_log_gamma():
    return jnp.log(1.0 - jnp.exp2(-5.0 - jnp.arange(RET_HEADS, dtype=jnp.float32)))


def retention_chunk(q, k, v, s0):
    q, k, v = (a.astype(jnp.float32) for a in (q, k, v))
    C = q.shape[1]
    lg = ret_log_gamma()
    i = jnp.arange(C, dtype=jnp.float32)
    diff = i[:, None] - i[None, :]
    dmat = jnp.where(diff >= 0, jnp.exp(jnp.maximum(diff, 0.0)[None] * lg[:, None, None]), 0.0)
    o = jnp.einsum('bhij,bjhe->bihe', jnp.einsum('bihd,bjhd->bhij', q, k) * dmat[None], v)
    o = o + jnp.einsum('bihd,bhde->bihe', q, s0) * jnp.exp((i + 1.0)[:, None] * lg[None, :])[None, :, :, None]
    kdec = k * jnp.exp((C - 1.0 - i)[:, None] * lg[None, :])[None, :, :, None]
    s_new = s0 * jnp.exp(C * lg)[None, :, None, None] + jnp.einsum('bjhd,bjhe->bhde', kdec, v)
    return o, s_new


def retention_prompt(q, k, v):
    B, S = q.shape[:2]
    nc = S // RET_CHUNK
    chunks = lambda a: jnp.moveaxis(a.reshape(B, nc, RET_CHUNK, *a.shape[2:]), 1, 0)
    s0 = jnp.zeros((B, RET_HEADS, RET_DK, RET_DV), jnp.float32)

    def step(s, xs):
        o, s = retention_chunk(xs[0], xs[1], xs[2], s)
        return s, o

    s_fin, o = lax.scan(step, s0, (chunks(q), chunks(k), chunks(v)))
    return jnp.moveaxis(o, 0, 1).reshape(B, S, RET_HEADS, RET_DV), s_fin


def mem_kv(mem, g_mem, w_mem_kv, g_k_mem):
    B, M, _ = mem.shape
    kv = (rms_norm(mem, g_mem) @ w_mem_kv).reshape(B, M, 2, MEM_HEADS, MEM_HD)
    return jnp.stack([rms_norm(kv[:, :, 0], g_k_mem), kv[:, :, 1]], axis=2)


def mem_attend(q, mkv):
    s = jnp.einsum('bthd,bmhd->bthm', q, mkv[:, :, 0]).astype(jnp.float32)
    p = jax.nn.softmax(s, axis=-1)
    return jnp.einsum('bthm,bmhd->bthd', p.astype(mkv.dtype), mkv[:, :, 1])


def gather_pages(cache, page_table):
    g = cache[page_table]
    return g.reshape(page_table.shape[0], -1, *cache.shape[2:])


def merge_and_ffn(x, o_cmp, o_slc, o_win, ng, o_ret, rg, o_mem, bg, g_ret_out, w_br_nsa, w_br_ret, w_br_mem, w_out,
                  g_ffn, w_router, b_router, w_gate_up, b_gate_up, w_down, b_down):
    B, T, _ = x.shape
    ng = jax.nn.sigmoid(ng.reshape(B, T, 3, NSA_KV_HEADS, NSA_GROUP, 1))
    o_nsa = (ng[:, :, 0] * o_cmp + ng[:, :, 1] * o_slc + ng[:, :, 2] * o_win).reshape(B, T, NSA_Q_W)
    o_ret = jax.nn.silu(rg) * rms_norm(o_ret, g_ret_out).reshape(B, T, RET_V_W)
    bg = jax.nn.sigmoid(bg.reshape(B, T, N_BRANCHES, D_MODEL))
    mixed = (bg[:, :, 0] * (o_nsa @ w_br_nsa) + bg[:, :, 1] * (o_ret @ w_br_ret)
             + bg[:, :, 2] * (o_mem.reshape(B, T, MEM_W) @ w_br_mem))
    h = x + mixed @ w_out
    hn = rms_norm(h, g_ffn).reshape(B * T, D_MODEL)
    logits = (hn @ w_router + b_router).astype(jnp.float32)
    top_v, top_i = lax.top_k(logits, TOP_K)
    combine = jnp.einsum('tk,tke->te', jax.nn.softmax(top_v, axis=-1), jax.nn.one_hot(top_i, N_EXPERTS, dtype=jnp.float32))
    out = jnp.zeros((B * T, D_MODEL), jnp.float32)
    for e in range(N_EXPERTS):
        gu = hn @ w_gate_up[e] + b_gate_up[e]
        gate = jnp.minimum(gu[:, :D_FF], SWIGLU_LIMIT)
        up = jnp.clip(gu[:, D_FF:], -SWIGLU_LIMIT, SWIGLU_LIMIT)
        act = (up + 1.0) * gate * jax.nn.sigmoid(SWIGLU_ALPHA * gate)
        out = out + combine[:, e:e + 1] * (act @ w_down[e] + b_down[e])
    return h + out.reshape(B, T, D_MODEL).astype(h.dtype)


def setup_inputs(seed: int = 0) -> dict:
    key = jax.random.key(seed)
    keys = iter(jax.random.split(key, 64))
    nrm = lambda shape, scale=1.0: jax.random.normal(next(keys), shape, jnp.float32) * scale
    gain = lambda n: 1.0 + nrm((n,), 0.05)
    n_pages = PAST_LEN // PAGE_SIZE
    n_used = DEC_BATCH * n_pages
    n_pool = n_used + max(1, n_used // 4)
    wb = min(WINDOW, PAST_LEN)
    page_table = jax.random.permutation(next(keys), n_pool)[:n_used].reshape(DEC_BATCH, n_pages).astype(jnp.int32)
    return {
        'x_prompt': nrm((BATCH, SEQ, D_MODEL)),
        'x_sample': nrm((DEC_BATCH, DEC_SEQ, D_MODEL)),
        'cache_cmp': nrm((n_pool, PAGE_SIZE, 2, NSA_KV_HEADS, NSA_HD)),
        'cache_slc': nrm((n_pool, PAGE_SIZE, 2, NSA_KV_HEADS, NSA_HD)),
        'cache_win': nrm((DEC_BATCH, wb, 2, NSA_KV_HEADS, NSA_HD)),
        'state_ret': nrm((DEC_BATCH, RET_HEADS, RET_DK, RET_DV), 0.5),
        'cache_mem': nrm((DEC_BATCH, MEM_LEN, 2, MEM_HEADS, MEM_HD)),
        'page_table': page_table,
        'mem_prompt': nrm((BATCH, MEM_LEN, D_MODEL)),
        'g_attn': gain(D_MODEL),
        'w_in': nrm((D_MODEL, D_IN), D_MODEL ** -0.5),
        'g_q_nsa': gain(NSA_HD),
        'g_k_cmp': gain(NSA_HD),
        'g_k_slc': gain(NSA_HD),
        'g_k_win': gain(NSA_HD),
        'pe_ck': nrm((CMP_BLOCK, NSA_HD), 0.1),
        'w_ck1': nrm((CMP_BLOCK * NSA_HD, CMP_HIDDEN), (CMP_BLOCK * NSA_HD) ** -0.5),
        'w_ck2': nrm((CMP_HIDDEN, NSA_HD), CMP_HIDDEN ** -0.5),
        'pe_cv': nrm((CMP_BLOCK, NSA_HD), 0.1),
        'w_cv1': nrm((CMP_BLOCK * NSA_HD, CMP_HIDDEN), (CMP_BLOCK * NSA_HD) ** -0.5),
        'w_cv2': nrm((CMP_HIDDEN, NSA_HD), CMP_HIDDEN ** -0.5),
        'g_ret_out': gain(RET_DV),
        'g_mem': gain(D_MODEL),
        'w_mem_kv': nrm((D_MODEL, 2 * MEM_W), D_MODEL ** -0.5),
        'g_q_mem': gain(MEM_HD),
        'g_k_mem': gain(MEM_HD),
        'w_br_nsa': nrm((NSA_Q_W, D_MODEL), NSA_Q_W ** -0.5),
        'w_br_ret': nrm((RET_V_W, D_MODEL), RET_V_W ** -0.5),
        'w_br_mem': nrm((MEM_W, D_MODEL), MEM_W ** -0.5),
        'w_out': nrm((D_MODEL, D_MODEL), D_MODEL ** -0.5),
        'g_ffn': gain(D_MODEL),
        'w_router': nrm((D_MODEL, N_EXPERTS), D_MODEL ** -0.5),
        'b_router': nrm((N_EXPERTS,), 0.01),
        'w_gate_up': nrm((N_EXPERTS, D_MODEL, 2 * D_FF), D_MODEL ** -0.5),
        'b_gate_up': nrm((N_EXPERTS, 2 * D_FF), 0.01),
        'w_down': nrm((N_EXPERTS, D_FF, D_MODEL), D_FF ** -0.5),
        'b_down': nrm((N_EXPERTS, D_MODEL), 0.01),
    }


def reference(x_prompt, x_sample, cache_cmp, cache_slc, cache_win, state_ret, cache_mem, page_table, mem_prompt,
              g_attn, w_in, g_q_nsa, g_k_cmp, g_k_slc, g_k_win, pe_ck, w_ck1, w_ck2, pe_cv, w_cv1, w_cv2,
              g_ret_out, g_mem, w_mem_kv, g_q_mem, g_k_mem, w_br_nsa, w_br_ret, w_br_mem, w_out,
              g_ffn, w_router, b_router, w_gate_up, b_gate_up, w_down, b_down):
    cmp_w = (pe_ck, w_ck1, w_ck2, pe_cv, w_cv1, w_cv2, g_k_cmp)
    ffn_w = (g_ret_out, w_br_nsa, w_br_ret, w_br_mem, w_out, g_ffn, w_router, b_router, w_gate_up, b_gate_up, w_down, b_down)
    pos_p = jnp.arange(SEQ, dtype=jnp.int32)
    (q, kc, vc, ks, vs, kw, vw, ng, rq, rk, rv, rg, mq, bg) = project_tokens(
        x_prompt, pos_p, g_attn, w_in, g_q_nsa, g_k_slc, g_k_win, g_q_mem)
    o_cmp, o_slc = nsa_global(q, kc, vc, ks, vs, pos_p, *cmp_w)
    o_win = window_prompt(q, kw, vw)
    o_ret, ret_state_prompt = retention_prompt(rq, rk, rv)
    mem_kv_prompt = mem_kv(mem_prompt, g_mem, w_mem_kv, g_k_mem)
    o_mem = mem_attend(mq, mem_kv_prompt)
    y_prompt = merge_and_ffn(x_prompt, o_cmp, o_slc, o_win, ng, o_ret, rg, o_mem, bg, *ffn_w)
    new_cmp_prompt = jnp.stack([kc, vc], axis=2)
    new_slc_prompt = jnp.stack([ks, vs], axis=2)
    new_win_prompt = jnp.stack([kw, vw], axis=2)[:, SEQ - min(WINDOW, SEQ):]
    pos_s = PAST_LEN + jnp.arange(DEC_SEQ, dtype=jnp.int32)
    (q, kc, vc, ks, vs, kw, vw, ng, rq, rk, rv, rg, mq, bg) = project_tokens(
        x_sample, pos_s, g_attn, w_in, g_q_nsa, g_k_slc, g_k_win, g_q_mem)
    past_cmp = gather_pages(cache_cmp, page_table)
    past_slc = gather_pages(cache_slc, page_table)
    o_cmp, o_slc = nsa_global(q,
                              jnp.concatenate([past_cmp[:, :, 0], kc], axis=1),
                              jnp.concatenate([past_cmp[:, :, 1], vc], axis=1),
                              jnp.concatenate([past_slc[:, :, 0], ks], axis=1),
                              jnp.concatenate([past_slc[:, :, 1], vs], axis=1),
                              pos_s, *cmp_w)
    o_win, new_win_sample = window_sample(q, kw, vw, cache_win, pos_s)
    o_ret, ret_state_sample = retention_chunk(rq, rk, rv, state_ret)
    o_mem = mem_attend(mq, cache_mem)
    y_sample = merge_and_ffn(x_sample, o_cmp, o_slc, o_win, ng, o_ret, rg, o_mem, bg, *ffn_w)
    new_cmp_sample = jnp.stack([kc, vc], axis=2)
    new_slc_sample = jnp.stack([ks, vs], axis=2)
    return (y_prompt, y_sample, new_cmp_prompt, new_slc_prompt, new_win_prompt, ret_state_prompt, mem_kv_prompt,
            new_cmp_sample, new_slc_sample, new_win_sample, ret_state_sample)
```

```python
import functools
import math

import jax
import jax.numpy as jnp
from jax import lax
from jax.experimental import pallas as pl
from jax.experimental.pallas import tpu as pltpu

F32 = jnp.float32
BF16 = jnp.bfloat16

D_MODEL = 1024
NSA_HEADS = 8
NSA_KV_HEADS = 2
NSA_GROUP = 4
NSA_HD = 64
CMP_BLOCK = 32
CMP_STRIDE = 16
CMP_HIDDEN = 128
SLC_BLOCK = 64
N_SELECT = 16
N_LOCAL_FORCED = 2
FORCE_BONUS = 1.0e4
WINDOW = 512
Q_BLOCK = 128
PAGE_SIZE = 128
RET_HEADS = 4
RET_DK = 64
RET_DV = 128
RET_CHUNK = 128
ROPE_BASE = 10000.0
MEM_HEADS = 4
MEM_HD = 128
N_EXPERTS = 32
TOP_K = 4
D_FF = 1024
SWIGLU_ALPHA = 1.702
SWIGLU_LIMIT = 7.0
EPS = 1e-6
NEG = -1e30

LANES = 128
VMEM_LIMIT = 56 * 1024 * 1024


def _dot(a, b):
    return jnp.dot(a, b, preferred_element_type=F32)


def _dot_nt(a, b):
    return lax.dot_general(a, b, (((1,), (1,)), ((), ())), preferred_element_type=F32)


def _sigmoid(x):
    return 1.0 / (1.0 + jnp.exp(-x))


def _split3(x):
    a = x.astype(BF16)
    r = x - a.astype(F32)
    b = r.astype(BF16)
    c = (r - b.astype(F32)).astype(BF16)
    return a, b, c


def _msoftmax(s, m):
    s = jnp.where(m, s, NEG)
    mx = jnp.max(s, axis=-1, keepdims=True)
    p = jnp.where(m, jnp.exp(s - mx), 0.0)
    return p / jnp.maximum(jnp.sum(p, axis=-1, keepdims=True), 1e-30)


def _halfnorm(t, g2):
    lane = lax.broadcasted_iota(jnp.int32, t.shape, 1)
    lo = lane < NSA_HD
    t2 = t * t
    s0 = jnp.sum(jnp.where(lo, t2, 0.0), axis=-1, keepdims=True)
    s1 = jnp.sum(jnp.where(lo, 0.0, t2), axis=-1, keepdims=True)
    r = jnp.where(lo, lax.rsqrt(s0 * (1.0 / NSA_HD) + EPS), lax.rsqrt(s1 * (1.0 / NSA_HD) + EPS))
    return t * r * g2


def _tilenorm(t, g, width):
    return t * lax.rsqrt(jnp.sum(t * t, axis=-1, keepdims=True) * (1.0 / width) + EPS) * g


def _rot(x, cos, sin):
    lane = lax.broadcasted_iota(jnp.int32, x.shape, 1)
    first = (lane % RET_DK) < (RET_DK // 2)
    n = x.shape[1]
    sw = jnp.where(first, pltpu.roll(x, n - RET_DK // 2, 1), pltpu.roll(x, RET_DK // 2, 1))
    return x * cos + sw * sin


C_Q, C_KV, C_NG, C_RET, C_MQ = 1024, 768, 128, 1536, 512
O_KV = C_Q
O_NG = O_KV + C_KV
O_RET = O_NG + C_NG
O_MQ = O_RET + C_RET
W1_COLS = O_MQ + C_MQ


def _proj_kernel(nbp, xp_ref, xs_ref, g_ref, w_ref, cos_ref, sin_ref, gq_ref, gks_ref, gkw_ref, gqm_ref,
                 q_ref, kvc_ref, kvs_ref, kvw_ref, ng_ref, rq_ref, rk_ref, rv_ref, rg_ref, mq_ref):
    i = pl.program_id(0)
    x = jnp.where(i < nbp, xp_ref[...], xs_ref[...])
    xn = (x * lax.rsqrt(jnp.mean(x * x, axis=-1, keepdims=True) + EPS) * g_ref[...]).astype(BF16)

    hq = _dot(xn, w_ref[:, 0:C_Q])
    for hh in range(NSA_HEADS):
        t = hq[:, LANES * hh:LANES * (hh + 1)]
        q_ref[:, LANES * hh:LANES * (hh + 1)] = (_tilenorm(t, gq_ref[...], NSA_HD) * NSA_HD ** -0.5).astype(BF16)

    hkv = _dot(xn, w_ref[:, O_KV:O_KV + C_KV])
    kvc_ref[...] = hkv[:, 0:256]
    kvs_ref[:, 0:128] = _halfnorm(hkv[:, 256:384], gks_ref[...])
    kvs_ref[:, 128:256] = hkv[:, 384:512]
    kvw_ref[:, 0:128] = _halfnorm(hkv[:, 512:640], gkw_ref[...])
    kvw_ref[:, 128:256] = hkv[:, 640:768]

    ng_ref[...] = _sigmoid(_dot(xn, w_ref[:, O_NG:O_NG + C_NG]))

    hr = _dot(xn, w_ref[:, O_RET:O_RET + C_RET])
    cos = cos_ref[...]
    sin = sin_ref[...]
    rq_ref[...] = _rot(hr[:, 0:256], cos, sin)
    rk_ref[...] = _rot(hr[:, 256:512], cos, sin) * RET_DK ** -0.5
    rv_ref[...] = hr[:, 512:1024]
    rg = hr[:, 1024:1536]
    rg_ref[...] = rg * _sigmoid(rg)

    hm = _dot(xn, w_ref[:, O_MQ:O_MQ + C_MQ])
    for hd in range(MEM_HEADS):
        t = hm[:, LANES * hd:LANES * (hd + 1)]
        mq_ref[:, LANES * hd:LANES * (hd + 1)] = (_tilenorm(t, gqm_ref[...], MEM_HD) * MEM_HD ** -0.5).astype(BF16)


def _project(xp, xs, g_attn, w1, cos_t, sin_t, gq2, gks2, gkw2, gqm, seq, tm):
    n_p, n_s = xp.shape[0], xs.shape[0]
    nbp, nbs = n_p // tm, n_s // tm
    n_all = n_p + n_s
    bps = seq // tm
    row = lambda w: pl.BlockSpec((tm, w), lambda i: (i, 0))
    const = lambda a: pl.BlockSpec(a.shape, lambda i: (0,) * a.ndim)
    rope = pl.BlockSpec((tm, 256), lambda i: (jnp.where(i < nbp, i % bps, bps + i - nbp), 0))
    widths = (1024, 256, 256, 256, 128, 256, 256, 512, 512, 512)
    dtypes = (BF16, F32, F32, F32, F32, F32, F32, F32, F32, BF16)
    return pl.pallas_call(
        functools.partial(_proj_kernel, nbp),
        grid=(nbp + nbs,),
        in_specs=[pl.BlockSpec((tm, D_MODEL), lambda i: (jnp.minimum(i, nbp - 1), 0)),
                  pl.BlockSpec((tm, D_MODEL), lambda i: (jnp.maximum(i - nbp, 0), 0)),
                  const(g_attn), const(w1), rope, rope, const(gq2), const(gks2), const(gkw2), const(gqm)],
        out_specs=[row(w) for w in widths],
        out_shape=[jax.ShapeDtypeStruct((n_all, w), d) for w, d in zip(widths, dtypes)],
        compiler_params=pltpu.CompilerParams(dimension_semantics=("arbitrary",), vmem_limit_bytes=VMEM_LIMIT),
        name="proj",
    )(xp, xs, g_attn, w1, cos_t, sin_t, gq2, gks2, gkw2, gqm)


CHUNK_W = CMP_STRIDE * 256
CHUNKS_PER_PAGE = PAGE_SIZE // CMP_STRIDE


def _compress_kernel(n_pages, pt_ref, pool_ref, wbig_ref, pe_ref, w2k_ref, w2v_ref, gk_ref,
                     ck_ref, cv_ref, x_scr, r_scr, sem):
    b = pl.program_id(0)
    n = n_pages * CHUNKS_PER_PAGE

    def page_copy(j):
        return pltpu.make_async_copy(pool_ref.at[pt_ref[b, j]],
                                     x_scr.at[pl.ds(CHUNKS_PER_PAGE * j, CHUNKS_PER_PAGE)], sem)

    for j in range(n_pages):
        page_copy(j).start()
    x_scr[pl.ds(n, 8), :] = pe_ref[...]
    for j in range(n_pages):
        page_copy(j).wait()

    r_scr[...] = _dot(x_scr[...].astype(BF16), wbig_ref[...])
    cvec = r_scr[n:n + 1, 0:512] + r_scr[n + 1:n + 2, 512:1024]
    hid = r_scr[0:n, 0:512] + r_scr[pl.ds(1, n), 512:1024] + cvec
    hb = (hid * _sigmoid(hid)).astype(BF16)
    ck_ref[0] = _halfnorm(_dot(hb[:, 0:256], w2k_ref[...]), gk_ref[...])
    cv_ref[0] = _dot(hb[:, 256:512], w2v_ref[...])


def _compress(page_table, pool, wbig, pe2, w2k, w2v, gk2):
    nb, n_pages = page_table.shape
    n = n_pages * CHUNKS_PER_PAGE
    const = lambda a: pl.BlockSpec(a.shape, lambda b, pt: (0,) * a.ndim)
    out = pl.BlockSpec((1, n, LANES), lambda b, pt: (b, 0, 0))
    return pl.pallas_call(
        functools.partial(_compress_kernel, n_pages),
        grid_spec=pltpu.PrefetchScalarGridSpec(
            num_scalar_prefetch=1, grid=(nb,),
            in_specs=[pl.BlockSpec(memory_space=pl.ANY), const(wbig), const(pe2), const(w2k), const(w2v), const(gk2)],
            out_specs=[out, out],
            scratch_shapes=[pltpu.VMEM((n + 8, CHUNK_W), F32), pltpu.VMEM((n + 8, 1024), F32),
                            pltpu.SemaphoreType.DMA(())]),
        out_shape=[jax.ShapeDtypeStruct((nb, n, LANES), F32)] * 2,
        compiler_params=pltpu.CompilerParams(dimension_semantics=("arbitrary",), vmem_limit_bytes=VMEM_LIMIT),
        name="compress",
    )(page_table, pool, wbig, pe2, w2k, w2v, gk2)


def _overlap_t(ns_rows, n_cmp_cols, n_cmp):
    s = lax.broadcasted_iota(jnp.int32, (ns_rows, n_cmp_cols), 0)
    n = lax.broadcasted_iota(jnp.int32, (ns_rows, n_cmp_cols), 1)
    ov = (n * CMP_STRIDE < s * SLC_BLOCK + SLC_BLOCK) & (n * CMP_STRIDE + CMP_BLOCK > s * SLC_BLOCK) & (n < n_cmp)
    return ov.astype(BF16)


def _importance_t(ov_t, pcs):
    a, b, c = _split3(pcs)
    return _dot_nt(ov_t, a) + _dot_nt(ov_t, b) + _dot_nt(ov_t, c)


def _block_scores_t(imp_t, tl):
    blk = lax.broadcasted_iota(jnp.int32, imp_t.shape, 0)
    cur = tl // SLC_BLOCK
    valid = blk <= cur
    forced = (blk == 0) | (valid & (blk > cur - N_LOCAL_FORCED))
    return jnp.where(valid, imp_t + jnp.where(forced, FORCE_BONUS, 0.0), NEG), blk


SLC_CHUNK = 512


def _nsa_prompt_kernel(seq, n_cmp, q_ref, kvs_ref, kvw_ref, ck_ref, cv_ref, ng_ref, e_ref, o_ref, sele_scr):
    i = pl.program_id(1)
    h = pl.program_id(2)
    ns = seq // SLC_BLOCK
    q4 = q_ref[...]
    qs = jnp.concatenate([q4[:, LANES * g:LANES * (g + 1)] for g in range(NSA_GROUP)], axis=0)
    tq = i * Q_BLOCK + lax.broadcasted_iota(jnp.int32, (Q_BLOCK, 1), 0)
    rep = lambda a: jnp.concatenate([a] * NSA_GROUP, axis=0)
    t4 = rep(tq)

    ckb = ck_ref[0].astype(BF16)
    sc = _dot_nt(qs, ckb)
    jn = lax.broadcasted_iota(jnp.int32, sc.shape, 1)
    pc = _msoftmax(sc, (jn * CMP_STRIDE + CMP_BLOCK - 1) <= t4)
    o_cmp = _dot(pc.astype(BF16), cv_ref[0].astype(BF16))

    pcs = pc[0:128] + pc[128:256] + pc[256:384] + pc[384:512]
    imp_t = _importance_t(_overlap_t(ns, pcs.shape[1], n_cmp), pcs)
    tl = i * Q_BLOCK + lax.broadcasted_iota(jnp.int32, (ns, Q_BLOCK), 1)
    score, blk = _block_scores_t(imp_t, tl)
    rank = jnp.zeros(score.shape, F32)
    for k in range(ns):
        sk = score[k:k + 1, :]
        rank = rank + jnp.where((sk > score) | ((sk == score) & (blk > k)), 1.0, 0.0)
    sel_t = jnp.where(rank < min(N_SELECT, ns), 1.0, 0.0).astype(BF16)
    eye = (lax.broadcasted_iota(jnp.int32, (Q_BLOCK, Q_BLOCK), 0)
           == lax.broadcasted_iota(jnp.int32, (Q_BLOCK, Q_BLOCK), 1)).astype(BF16)
    sel = _dot_nt(eye, sel_t)
    sele_scr[...] = _dot(sel.astype(BF16), e_ref[...])

    kc = min(SLC_CHUNK, seq)

    def body(c, carry):
        m, l, acc = carry
        k0 = pl.multiple_of(c * kc, kc)
        kk = kvs_ref[pl.ds(k0, kc), 0:128].astype(BF16)
        vv = kvs_ref[pl.ds(k0, kc), 128:256].astype(BF16)
        s = _dot_nt(qs, kk)
        kpos = k0 + lax.broadcasted_iota(jnp.int32, (Q_BLOCK, kc), 1)
        mf = jnp.where((sele_scr[:, pl.ds(k0, kc)] > 0.5) & (kpos <= tq), 1.0, 0.0)
        msk = rep(mf) > 0.5
        s = jnp.where(msk, s, NEG)
        m_new = jnp.maximum(m, jnp.max(s, axis=-1, keepdims=True))
        alpha = jnp.exp(m - m_new)
        p = jnp.where(msk, jnp.exp(s - m_new), 0.0)
        l = alpha * l + jnp.sum(p, axis=-1, keepdims=True)
        acc = alpha * acc + _dot(p.astype(BF16), vv)
        return m_new, l, acc

    rows = NSA_GROUP * Q_BLOCK
    init = (jnp.full((rows, 1), NEG, F32), jnp.zeros((rows, 1), F32), jnp.zeros((rows, LANES), F32))
    n_chunks = (i * Q_BLOCK + Q_BLOCK - 1) // kc + 1
    _, l, acc = lax.fori_loop(0, n_chunks, body, init)
    o_slc = acc / jnp.maximum(l, 1e-30)

    wk = WINDOW + Q_BLOCK
    start = pl.multiple_of(Q_BLOCK * jnp.maximum(i - WINDOW // Q_BLOCK, 0), Q_BLOCK)
    kk = kvw_ref[pl.ds(start, wk), 0:128].astype(BF16)
    vv = kvw_ref[pl.ds(start, wk), 128:256].astype(BF16)
    sw = _dot_nt(qs, kk)
    d = tq - (start + lax.broadcasted_iota(jnp.int32, (Q_BLOCK, wk), 1))
    mw = rep(jnp.where((d >= 0) & (d < WINDOW), 1.0, 0.0)) > 0.5
    o_win = _dot(_msoftmax(sw, mw).astype(BF16), vv)

    n_g = 3 * NSA_GROUP
    col = lax.broadcasted_iota(jnp.int32, (LANES, n_g * LANES), 1) // LANES
    row = lax.broadcasted_iota(jnp.int32, (LANES, n_g * LANES), 0)
    eg = (row == (col // NSA_GROUP) * NSA_HEADS + NSA_GROUP * h + col % NSA_GROUP).astype(BF16)
    a, b, c = _split3(ng_ref[...])
    gexp = _dot(a, eg) + _dot(b, eg) + _dot(c, eg)
    for g in range(NSA_GROUP):
        rs = slice(Q_BLOCK * g, Q_BLOCK * (g + 1))
        gate = lambda br: gexp[:, LANES * (NSA_GROUP * br + g):LANES * (NSA_GROUP * br + g + 1)]
        o = gate(0) * o_cmp[rs] + gate(1) * o_slc[rs] + gate(2) * o_win[rs]
        o_ref[:, LANES * g:LANES * (g + 1)] = o.astype(BF16)


def _nsa_prompt(q, kvs, kvw, ck, cv, ng, e_p, nb, seq, n_cmp):
    nq = seq // Q_BLOCK
    ncp = ck.shape[1]
    return pl.pallas_call(
        functools.partial(_nsa_prompt_kernel, seq, n_cmp),
        grid=(nb, nq, NSA_KV_HEADS),
        in_specs=[pl.BlockSpec((Q_BLOCK, 512), lambda b, i, h: (b * nq + i, h)),
                  pl.BlockSpec((seq, 256), lambda b, i, h: (b, 0)),
                  pl.BlockSpec((seq, 256), lambda b, i, h: (b, 0)),
                  pl.BlockSpec((1, ncp, LANES), lambda b, i, h: (b, 0, 0)),
                  pl.BlockSpec((1, ncp, LANES), lambda b, i, h: (b, 0, 0)),
                  pl.BlockSpec((Q_BLOCK, LANES), lambda b, i, h: (b * nq + i, 0)),
                  pl.BlockSpec(e_p.shape, lambda b, i, h: (0, 0))],
        out_specs=pl.BlockSpec((Q_BLOCK, 512), lambda b, i, h: (b * nq + i, h)),
        out_shape=jax.ShapeDtypeStruct((nb * seq, 1024), BF16),
        scratch_shapes=[pltpu.VMEM((Q_BLOCK, seq), F32)],
        compiler_params=pltpu.CompilerParams(dimension_semantics=("arbitrary",) * 3, vmem_limit_bytes=VMEM_LIMIT),
        name="nsa_prompt",
    )(q, kvs, kvw, ck, cv, ng, e_p)


def _nsa_sample_kernel(past, n_pages, dseq, wb, pt_ref, q_ref, kvs_ref, kvw_ref, ck_ref, cv_ref, ng_ref, cwin_ref,
                       e_ref, pool_ref, o_ref, nwin_ref, kv_scr, kw_scr, sc_scr, sem):
    b = pl.program_id(0)

    def page_copy(j):
        return pltpu.make_async_copy(pool_ref.at[pt_ref[b, j]], kv_scr.at[pl.ds(PAGE_SIZE * j, PAGE_SIZE)], sem)

    for j in range(n_pages):
        page_copy(j).start()

    n_heads = NSA_KV_HEADS * NSA_GROUP
    rows = n_heads * dseq
    kpad = past + PAGE_SIZE
    ns = (past + dseq + SLC_BLOCK - 1) // SLC_BLOCK
    nsp = e_ref.shape[0]
    qf = q_ref[0].astype(F32)
    qs = jnp.concatenate([qf[:, LANES * hh:LANES * (hh + 1)] for hh in range(n_heads)], axis=0).astype(BF16)
    t1 = past + lax.broadcasted_iota(jnp.int32, (dseq, 1), 0)
    t = jnp.concatenate([t1] * n_heads, axis=0)

    sc = _dot_nt(qs, ck_ref[0].astype(BF16))
    jn = lax.broadcasted_iota(jnp.int32, sc.shape, 1)
    pc = _msoftmax(sc, (jn * CMP_STRIDE + CMP_BLOCK - 1) <= t)
    o_cmp = _dot(pc.astype(BF16), cv_ref[0].astype(BF16))

    per_kv = []
    for kvh in range(NSA_KV_HEADS):
        base = kvh * NSA_GROUP * dseq
        s = pc[base:base + dseq]
        for g in range(1, NSA_GROUP):
            s = s + pc[base + g * dseq:base + (g + 1) * dseq]
        per_kv += [s] * NSA_GROUP
    pcs = jnp.concatenate(per_kv, axis=0)
    n_cmp = (ns * SLC_BLOCK) // CMP_STRIDE - CMP_BLOCK // CMP_STRIDE + 1
    imp_t = _importance_t(_overlap_t(nsp, pcs.shape[1], n_cmp), pcs)
    tl = past + lax.broadcasted_iota(jnp.int32, (nsp, rows), 1) % dseq
    score, blk = _block_scores_t(imp_t, tl)
    sc_scr[...] = score

    def rank_body(k, rank):
        sk = sc_scr[pl.ds(k, 1), :]
        return rank + jnp.where((sk > score) | ((sk == score) & (blk > k)), 1.0, 0.0)

    rank = lax.fori_loop(0, ns, rank_body, jnp.zeros(score.shape, F32))
    sel_t = jnp.where(rank < min(N_SELECT, ns), 1.0, 0.0).astype(BF16)
    eye = (lax.broadcasted_iota(jnp.int32, (rows, rows), 0)
           == lax.broadcasted_iota(jnp.int32, (rows, rows), 1)).astype(BF16)
    sel = _dot_nt(eye, sel_t)
    sele = _dot(sel.astype(BF16), e_ref[...])

    kv_scr[pl.ds(past, dseq), :] = kvs_ref[0]
    kv_scr[pl.ds(past + dseq, PAGE_SIZE - dseq), :] = jnp.zeros((PAGE_SIZE - dseq, 256), F32)
    for j in range(n_pages):
        page_copy(j).wait()
    s = _dot_nt(qs, kv_scr[:, 0:128].astype(BF16))
    kpos = lax.broadcasted_iota(jnp.int32, (rows, kpad), 1)
    p = _msoftmax(s, (sele > 0.5) & (kpos <= t))
    o_slc = _dot(p.astype(BF16), kv_scr[:, 128:256].astype(BF16))

    wpad = kw_scr.shape[0]
    kw_scr[pl.ds(0, wb), :] = cwin_ref[0]
    kw_scr[pl.ds(wb, dseq), :] = kvw_ref[0]
    kw_scr[pl.ds(wb + dseq, wpad - wb - dseq), :] = jnp.zeros((wpad - wb - dseq, 256), F32)
    sw = _dot_nt(qs, kw_scr[:, 0:128].astype(BF16))
    d = t - (past - wb + lax.broadcasted_iota(jnp.int32, (rows, wpad), 1))
    pw = _msoftmax(sw, (d >= 0) & (d < WINDOW))
    o_win = _dot(pw.astype(BF16), kw_scr[:, 128:256].astype(BF16))
    nwin_ref[0, pl.ds(0, wb - dseq), :] = cwin_ref[0, pl.ds(dseq, wb - dseq), :]
    nwin_ref[0, pl.ds(wb - dseq, dseq), :] = kvw_ref[0]

    ngv = ng_ref[0]
    for hh in range(n_heads):
        rs = slice(dseq * hh, dseq * (hh + 1))
        gate = lambda br: ngv[:, br * NSA_HEADS + hh:br * NSA_HEADS + hh + 1]
        o_ref[0, :, LANES * hh:LANES * (hh + 1)] = gate(0) * o_cmp[rs] + gate(1) * o_slc[rs] + gate(2) * o_win[rs]


def _nsa_sample(page_table, q, kvs, kvw, ck, cv, ng, cwin, e_s, pool, past):
    nb, n_pages = page_table.shape
    dseq = q.shape[1]
    wb = cwin.shape[1]
    kpad = past + PAGE_SIZE
    wpad = wb + PAGE_SIZE
    nsp = e_s.shape[0]
    rows = NSA_HEADS * dseq
    per_b = lambda a: pl.BlockSpec((1,) + a.shape[1:], lambda b, pt: (b,) + (0,) * (a.ndim - 1))
    return pl.pallas_call(
        functools.partial(_nsa_sample_kernel, past, n_pages, dseq, wb),
        grid_spec=pltpu.PrefetchScalarGridSpec(
            num_scalar_prefetch=1, grid=(nb,),
            in_specs=[per_b(q), per_b(kvs), per_b(kvw), per_b(ck), per_b(cv), per_b(ng), per_b(cwin),
                      pl.BlockSpec(e_s.shape, lambda b, pt: (0, 0)), pl.BlockSpec(memory_space=pl.ANY)],
            out_specs=[pl.BlockSpec((1, dseq, 1024), lambda b, pt: (b, 0, 0)),
                       pl.BlockSpec((1, wb, 256), lambda b, pt: (b, 0, 0))],
            scratch_shapes=[pltpu.VMEM((kpad, 256), F32), pltpu.VMEM((wpad, 256), F32),
                            pltpu.VMEM((nsp, rows), F32), pltpu.SemaphoreType.DMA(())]),
        out_shape=[jax.ShapeDtypeStruct((nb, dseq, 1024), F32), jax.ShapeDtypeStruct((nb, wb, 256), F32)],
        compiler_params=pltpu.CompilerParams(dimension_semantics=("arbitrary",), vmem_limit_bytes=VMEM_LIMIT),
        name="nsa_sample",
    )(page_table, q, kvs, kvw, ck, cv, ng, cwin, e_s, pool)


def _ret_kernel(c_true, rq_ref, rk_ref, rv_ref, rg_ref, s0_ref, gro_ref, o_ref, sout_ref, s_scr):
    c = pl.program_id(1)
    cs = RET_CHUNK

    @pl.when(c == 0)
    def _():
        for tile in range(RET_HEADS // 2):
            s_scr[tile] = jnp.concatenate([s0_ref[0, 2 * tile], s0_ref[0, 2 * tile + 1]], axis=0)

    diff = (lax.broadcasted_iota(jnp.int32, (cs, cs), 0) - lax.broadcasted_iota(jnp.int32, (cs, cs), 1)).astype(F32)
    ic = lax.broadcasted_iota(jnp.int32, (cs, 1), 0).astype(F32)
    lane = lax.broadcasted_iota(jnp.int32, (cs, LANES), 1)
    for hd in range(RET_HEADS):
        lg = math.log(1.0 - 2.0 ** (-5.0 - hd))
        tile, half = hd // 2, hd % 2
        in_half = (lane >= RET_DK * half) & (lane < RET_DK * (half + 1))
        qm = jnp.where(in_half, rq_ref[:, LANES * tile:LANES * (tile + 1)], 0.0).astype(BF16)
        k2 = rk_ref[:, LANES * tile:LANES * (tile + 1)]
        vb = rv_ref[:, LANES * hd:LANES * (hd + 1)].astype(BF16)
        dmat = jnp.where(diff >= 0, jnp.exp(jnp.maximum(diff, 0.0) * lg), 0.0)
        o = _dot((_dot_nt(qm, k2.astype(BF16)) * dmat).astype(BF16), vb)
        st = s_scr[tile]
        o = o + _dot(qm, st.astype(BF16)) * jnp.exp((ic + 1.0) * lg)
        kdec_t = (k2 * jnp.exp((c_true - 1.0 - ic) * lg)).T
        upd = _dot(kdec_t[RET_DK * half:RET_DK * (half + 1)].astype(BF16), vb)
        s_new = st[RET_DK * half:RET_DK * (half + 1)] * math.exp(c_true * lg) + upd
        s_scr[tile, pl.ds(RET_DK * half, RET_DK), :] = s_new
        on = _tilenorm(o, gro_ref[...], RET_DV)
        o_ref[:, LANES * hd:LANES * (hd + 1)] = (rg_ref[:, LANES * hd:LANES * (hd + 1)] * on).astype(BF16)

    @pl.when(c == pl.num_programs(1) - 1)
    def _():
        for hd in range(RET_HEADS):
            sout_ref[0, hd] = s_scr[hd // 2, pl.ds(RET_DK * (hd % 2), RET_DK), :]


def _retention(rq, rk, rv, rg, s0, gro, nb, n_chunks, c_true):
    cs = RET_CHUNK
    n_rows = nb * n_chunks * cs
    row = lambda w: pl.BlockSpec((cs, w), lambda b, c: (b * n_chunks + c, 0))
    return pl.pallas_call(
        functools.partial(_ret_kernel, float(c_true)),
        grid=(nb, n_chunks),
        in_specs=[row(256), row(256), row(512), row(512),
                  pl.BlockSpec((1, RET_HEADS, RET_DK, RET_DV), lambda b, c: (b, 0, 0, 0)),
                  pl.BlockSpec(gro.shape, lambda b, c: (0, 0))],
        out_specs=[row(512), pl.BlockSpec((1, RET_HEADS, RET_DK, RET_DV), lambda b, c: (b, 0, 0, 0))],
        out_shape=[jax.ShapeDtypeStruct((n_rows, 512), BF16),
                   jax.ShapeDtypeStruct((nb, RET_HEADS, RET_DK, RET_DV), F32)],
        scratch_shapes=[pltpu.VMEM((RET_HEADS // 2, LANES, RET_DV), F32)],
        compiler_params=pltpu.CompilerParams(dimension_semantics=("arbitrary", "arbitrary")),
        name="retention",
    )(rq, rk, rv, rg, s0, gro)


def _memkv_kernel(m_ref, g_ref, w_ref, gk_ref, o_ref):
    x = m_ref[...]
    xn = (x * lax.rsqrt(jnp.mean(x * x, axis=-1, keepdims=True) + EPS) * g_ref[...]).astype(BF16)
    hk = _dot(xn, w_ref[...])
    half = MEM_HEADS * MEM_HD
    for hd in range(MEM_HEADS):
        o_ref[:, LANES * hd:LANES * (hd + 1)] = _tilenorm(hk[:, LANES * hd:LANES * (hd + 1)], gk_ref[...], MEM_HD)
    o_ref[:, half:2 * half] = hk[:, half:2 * half]


def _mem_kv(mem2d, g_mem, w_mem, gk_mem, tm):
    n = mem2d.shape[0]
    const = lambda a: pl.BlockSpec(a.shape, lambda i: (0,) * a.ndim)
    return pl.pallas_call(
        _memkv_kernel,
        grid=(n // tm,),
        in_specs=[pl.BlockSpec((tm, D_MODEL), lambda i: (i, 0)), const(g_mem), const(w_mem), const(gk_mem)],
        out_specs=pl.BlockSpec((tm, 1024), lambda i: (i, 0)),
        out_shape=jax.ShapeDtypeStruct((n, 1024), F32),
        compiler_params=pltpu.CompilerParams(dimension_semantics=("arbitrary",)),
        name="mem_kv",
    )(mem2d, g_mem, w_mem, gk_mem)


def _memattn_kernel(q_ref, kv_ref, o_ref):
    half = MEM_HEADS * MEM_HD
    for hd in range(MEM_HEADS):
        cols = slice(LANES * hd, LANES * (hd + 1))
        s = _dot_nt(q_ref[:, cols].astype(BF16), kv_ref[:, cols].astype(BF16))
        p = jnp.exp(s - jnp.max(s, axis=-1, keepdims=True))
        p = p / jnp.sum(p, axis=-1, keepdims=True)
        o = _dot(p.astype(BF16), kv_ref[:, half + LANES * hd:half + LANES * (hd + 1)].astype(BF16))
        o_ref[:, cols] = o.astype(o_ref.dtype)


def _mem_attend(q2d, kv2d, nb, blocks_per_b, tq, mem_len, out_rows, out_dtype):
    return pl.pallas_call(
        _memattn_kernel,
        grid=(nb, blocks_per_b),
        in_specs=[pl.BlockSpec((tq, 512), lambda b, i: (b * blocks_per_b + i, 0)),
                  pl.BlockSpec((mem_len, 1024), lambda b, i: (b, 0))],
        out_specs=pl.BlockSpec((tq, 512), lambda b, i: (b * blocks_per_b + i, 0)),
        out_shape=jax.ShapeDtypeStruct((out_rows, 512), out_dtype),
        compiler_params=pltpu.CompilerParams(dimension_semantics=("arbitrary", "arbitrary")),
        name="mem_attend",
    )(q2d, kv2d)


def _merge_kernel(nbp, xp_ref, xs_ref, onsap_ref, onsas_ref, oretp_ref, orets_ref, omemp_ref, omems_ref, g_ref,
                  wbg_ref, wn_ref, wr_ref, wm_ref, wo_ref, gffn_ref, wrt_ref, brt_ref, h_ref, hn_ref, comb_ref):
    i = pl.program_id(0)
    pick = lambda p_ref, s_ref: jnp.where(i < nbp, p_ref[...], s_ref[...])
    x = pick(xp_ref, xs_ref)
    xn = (x * lax.rsqrt(jnp.mean(x * x, axis=-1, keepdims=True) + EPS) * g_ref[...]).astype(BF16)
    bg = _sigmoid(_dot(xn, wbg_ref[...]))
    mixed = (bg[:, 0:1024] * _dot(pick(onsap_ref, onsas_ref), wn_ref[...])
             + bg[:, 1024:2048] * _dot(pick(oretp_ref, orets_ref), wr_ref[...])
             + bg[:, 2048:3072] * _dot(pick(omemp_ref, omems_ref), wm_ref[...]))
    hres = x + _dot(mixed.astype(BF16), wo_ref[...])
    h_ref[...] = hres
    hn = hres * lax.rsqrt(jnp.mean(hres * hres, axis=-1, keepdims=True) + EPS) * gffn_ref[...]
    hn_ref[...] = hn.astype(BF16)

    a, b, _ = _split3(hn)
    logits = _dot(a, wrt_ref[0]) + _dot(a, wrt_ref[1]) + _dot(b, wrt_ref[0]) + brt_ref[...]
    lane = lax.broadcasted_iota(jnp.int32, logits.shape, 1).astype(F32)
    vals, hots = [], []
    for _k in range(TOP_K):
        mx = jnp.max(logits, axis=-1, keepdims=True)
        idx = jnp.min(jnp.where(logits == mx, lane, float(LANES)), axis=-1, keepdims=True)
        hot = lane == idx
        vals.append(mx)
        hots.append(hot)
        logits = jnp.where(hot, -3.0e38, logits)
    es = [jnp.exp(v - vals[0]) for v in vals]
    den = es[0] + es[1] + es[2] + es[3]
    comb = jnp.zeros(logits.shape, F32)
    for e_k, hot in zip(es, hots):
        comb = comb + jnp.where(hot, e_k / den, 0.0)
    comb_ref[...] = comb


def _merge(xp, xs, onsa, onsa_s, oret, oret_s, omem, omem_s, g_attn, wbg, wn, wr, wm, wo, gffn, wrt, brt, tm):
    n_p, n_s = xp.shape[0], xs.shape[0]
    nbp, nbs = n_p // tm, n_s // tm
    n_all = n_p + n_s
    row = lambda w: pl.BlockSpec((tm, w), lambda i: (i, 0))
    rowp = lambda w: pl.BlockSpec((tm, w), lambda i: (jnp.minimum(i, nbp - 1), 0))
    rows = lambda w: pl.BlockSpec((tm, w), lambda i: (jnp.maximum(i - nbp, 0), 0))
    const = lambda a: pl.BlockSpec(a.shape, lambda i: (0,) * a.ndim)
    return pl.pallas_call(
        functools.partial(_merge_kernel, nbp),
        grid=(nbp + nbs,),
        in_specs=[rowp(1024), rows(1024), rowp(1024), rows(1024), rowp(512), rows(512), rowp(512), rows(512),
                  const(g_attn), const(wbg), const(wn), const(wr), const(wm),
                  const(wo), const(gffn), const(wrt), const(brt)],
        out_specs=[row(1024), row(1024), row(LANES)],
        out_shape=[jax.ShapeDtypeStruct((n_all, 1024), F32), jax.ShapeDtypeStruct((n_all, 1024), BF16),
                   jax.ShapeDtypeStruct((n_all, LANES), F32)],
        compiler_params=pltpu.CompilerParams(dimension_semantics=("arbitrary",), vmem_limit_bytes=VMEM_LIMIT),
        name="merge",
    )(xp, xs, onsa, onsa_s, oret, oret_s, omem, omem_s, g_attn, wbg, wn, wr, wm, wo, gffn, wrt, brt)


FF_SPLIT = 2
FF_BLK = D_FF // FF_SPLIT


def _moe_kernel(hn_ref, h_ref, comb_ref, wg_ref, wu_ref, wd_ref, bg_ref, bu_ref, bd_ref, y_ref):
    e = pl.program_id(1)
    j = pl.program_id(2)

    @pl.when((e == 0) & (j == 0))
    def _():
        y_ref[...] = h_ref[...]

    hn = hn_ref[...]
    gate = jnp.minimum(_dot(hn, wg_ref[0].astype(BF16)) + bg_ref[0], SWIGLU_LIMIT)
    up = jnp.clip(_dot(hn, wu_ref[0].astype(BF16)) + bu_ref[0], -SWIGLU_LIMIT, SWIGLU_LIMIT)
    act = (up + 1.0) * gate * _sigmoid(SWIGLU_ALPHA * gate)
    y = _dot(act.astype(BF16), wd_ref[0].astype(BF16)) + jnp.where(j == 0, 1.0, 0.0) * bd_ref[0]
    lane = lax.broadcasted_iota(jnp.int32, comb_ref.shape, 1)
    cw = jnp.sum(jnp.where(lane == e, comb_ref[...], 0.0), axis=-1, keepdims=True)
    y_ref[...] += cw * y


def _moe(hn, h, comb, w_gate_up, w_down, b_gate_up, b_down, tb):
    n_all = hn.shape[0]
    row = lambda w: pl.BlockSpec((tb, w), lambda i, e, j: (i, 0))
    return pl.pallas_call(
        _moe_kernel,
        grid=(n_all // tb, N_EXPERTS, FF_SPLIT),
        in_specs=[row(1024), row(1024), row(LANES),
                  pl.BlockSpec((1, D_MODEL, FF_BLK), lambda i, e, j: (e, 0, j)),
                  pl.BlockSpec((1, D_MODEL, FF_BLK), lambda i, e, j: (e, 0, FF_SPLIT + j)),
                  pl.BlockSpec((1, FF_BLK, D_MODEL), lambda i, e, j: (e, j, 0)),
                  pl.BlockSpec((1, 1, FF_BLK), lambda i, e, j: (e, 0, j)),
                  pl.BlockSpec((1, 1, FF_BLK), lambda i, e, j: (e, 0, FF_SPLIT + j)),
                  pl.BlockSpec((1, 1, D_MODEL), lambda i, e, j: (e, 0, 0))],
        out_specs=row(1024),
        out_shape=jax.ShapeDtypeStruct((n_all, 1024), F32),
        compiler_params=pltpu.CompilerParams(dimension_semantics=("arbitrary",) * 3, vmem_limit_bytes=VMEM_LIMIT),
        name="moe",
    )(hn, h, comb, w_gate_up, w_gate_up, w_down, b_gate_up, b_gate_up, b_down)


def _pick_tile(n, pref):
    t = pref
    while n % t:
        t //= 2
    return t


def kernel(x_prompt, x_sample, cache_cmp, cache_slc, cache_win, state_ret, cache_mem, page_table, mem_prompt,
           g_attn, w_in, g_q_nsa, g_k_cmp, g_k_slc, g_k_win, pe_ck, w_ck1, w_ck2, pe_cv, w_cv1, w_cv2,
           g_ret_out, g_mem, w_mem_kv, g_q_mem, g_k_mem, w_br_nsa, w_br_ret, w_br_mem, w_out,
           g_ffn, w_router, b_router, w_gate_up, b_gate_up, w_down, b_down):
    nb, seq, _ = x_prompt.shape
    db, dseq, _ = x_sample.shape
    n_pages = page_table.shape[1]
    past = n_pages * PAGE_SIZE
    wb = cache_win.shape[1]
    mem_len = mem_prompt.shape[1]
    n_p, n_s = nb * seq, db * dseq
    n_all = n_p + n_s
    tm = _pick_tile(math.gcd(seq, n_s), 256)

    o = 0
    cols = {}
    for name, wdt in (("q", 512), ("kv", 768), ("ng", 24), ("rq", 256), ("rk", 256), ("rv", 512), ("rg", 512),
                      ("mq", 512), ("bg", 3072)):
        cols[name] = w_in[:, o:o + wdt]
        o += wdt
    wq = cols["q"].reshape(D_MODEL, NSA_HEADS, NSA_HD)
    zq = jnp.zeros_like(wq)
    wq_pad = jnp.concatenate(
        [jnp.concatenate([wq[:, hh], zq[:, hh]] if hh < NSA_GROUP else [zq[:, hh], wq[:, hh]], axis=-1)
         for hh in range(NSA_HEADS)], axis=-1)
    w_ng = jnp.pad(cols["ng"], ((0, 0), (0, C_NG - 24)))
    w1 = jnp.concatenate([wq_pad, cols["kv"], w_ng, cols["rq"], cols["rk"], cols["rv"], cols["rg"], cols["mq"]],
                         axis=-1).astype(BF16)
    w_bg = cols["bg"].astype(BF16)
    two = lambda g: jnp.concatenate([g, g]).reshape(1, LANES)
    r1 = lambda g: g.reshape(1, -1)

    half = RET_DK // 2
    inv = ROPE_BASE ** (-jnp.arange(half, dtype=F32) / half)
    pos = jnp.concatenate([jnp.arange(seq, dtype=jnp.int32),
                           jnp.tile(past + jnp.arange(dseq, dtype=jnp.int32), db)]).astype(F32)
    ang = pos[:, None] * inv[None, :]
    cos_t = jnp.tile(jnp.cos(ang), (1, 2 * RET_HEADS))
    sin_t = jnp.tile(jnp.concatenate([-jnp.sin(ang), jnp.sin(ang)], axis=-1), (1, RET_HEADS))

    xp = x_prompt.reshape(n_p, D_MODEL)
    xs = x_sample.reshape(n_s, D_MODEL)
    (q, kvc, kvs, kvw, ng, rq, rk, rv, rg, mq) = _project(
        xp, xs, r1(g_attn), w1, cos_t, sin_t, two(g_q_nsa), two(g_k_slc), two(g_k_win), r1(g_q_mem), seq, tm)

    w1s = jnp.stack([w_ck1, w_ck1, w_cv1, w_cv1]).reshape(4, 2, CMP_STRIDE, NSA_HD, CMP_HIDDEN)
    wbig = jnp.einsum("shpdc,st->psdhtc", w1s, jnp.eye(4, dtype=F32)).reshape(CHUNK_W, 2 * 4 * CMP_HIDDEN).astype(BF16)
    pes = jnp.stack([pe_ck, pe_ck, pe_cv, pe_cv]).reshape(4, 2, CMP_STRIDE, NSA_HD)
    pe2 = jnp.pad(jnp.transpose(pes, (1, 2, 0, 3)).reshape(2, CHUNK_W), ((0, 6), (0, 0)))
    zc = jnp.zeros_like(w_ck2)
    bd2 = lambda w: jnp.concatenate([jnp.concatenate([w, zc], 1), jnp.concatenate([zc, w], 1)], 0).astype(BF16)
    gk2 = two(g_k_cmp)

    pages_p = seq // PAGE_SIZE
    pool_p = (kvc if n_all % PAGE_SIZE == 0 else kvc[:n_p]).reshape(-1, CHUNKS_PER_PAGE, CHUNK_W)
    pt_p = jnp.arange(nb * pages_p, dtype=jnp.int32).reshape(nb, pages_p)
    ck_p, cv_p = _compress(pt_p, pool_p, wbig, pe2, bd2(w_ck2), bd2(w_cv2), gk2)
    pool_s = cache_cmp.reshape(-1, CHUNKS_PER_PAGE, CHUNK_W)
    ck_s, cv_s = _compress(page_table, pool_s, wbig, pe2, bd2(w_ck2), bd2(w_cv2), gk2)

    ns_p = seq // SLC_BLOCK
    e_p = (jnp.arange(seq)[None, :] // SLC_BLOCK == jnp.arange(ns_p)[:, None]).astype(BF16)
    n_cmp_p = seq // CMP_STRIDE - CMP_BLOCK // CMP_STRIDE + 1
    onsa = _nsa_prompt(q, kvs, kvw, ck_p, cv_p, ng, e_p, nb, seq, n_cmp_p)

    ns_s = (past + dseq + SLC_BLOCK - 1) // SLC_BLOCK
    nsp = (ns_s + 7) // 8 * 8
    kpad = past + PAGE_SIZE
    e_s = (jnp.arange(kpad)[None, :] // SLC_BLOCK == jnp.arange(nsp)[:, None]).astype(BF16)
    s3 = lambda a: a[n_p:].reshape(db, dseq, a.shape[1])
    onsa_s, new_win_s = _nsa_sample(page_table, s3(q), s3(kvs), s3(kvw), ck_s, cv_s, s3(ng),
                                    cache_win.reshape(db, wb, 256), e_s,
                                    cache_slc.reshape(-1, PAGE_SIZE, 256), past)
    onsa_s = onsa_s.reshape(n_s, 1024).astype(BF16)

    gro = r1(g_ret_out)
    oret, ret_state_p = _retention(rq, rk, rv, rg, jnp.zeros((nb, RET_HEADS, RET_DK, RET_DV), F32), gro,
                                   nb, seq // RET_CHUNK, RET_CHUNK)
    padc = lambda a: jnp.pad(s3(a), ((0, 0), (0, RET_CHUNK - dseq), (0, 0))).reshape(db * RET_CHUNK, a.shape[1])
    oret_s, ret_state_s = _retention(padc(rq), padc(rk), padc(rv), padc(rg), state_ret, gro, db, 1, dseq)
    oret_s = oret_s.reshape(db, RET_CHUNK, 512)[:, :dseq].reshape(n_s, 512)

    mem_kv_p = _mem_kv(mem_prompt.reshape(nb * mem_len, D_MODEL), r1(g_mem), w_mem_kv.astype(BF16), r1(g_k_mem),
                       _pick_tile(nb * mem_len, 256))
    tq = _pick_tile(seq, 512)
    omem = _mem_attend(mq, mem_kv_p, nb, seq // tq, tq, mem_len, n_p, BF16)
    omem_s = _mem_attend(s3(mq).astype(F32).reshape(n_s, 512), cache_mem.reshape(db * mem_len, 1024), db, 1, dseq,
                         mem_len, n_s, F32).astype(BF16)

    wn = w_br_nsa.reshape(NSA_HEADS, NSA_HD, D_MODEL)
    zn = jnp.zeros_like(wn)
    wn_pad = jnp.concatenate(
        [jnp.concatenate([wn[hh], zn[hh]] if hh < NSA_GROUP else [zn[hh], wn[hh]], axis=0) for hh in range(NSA_HEADS)],
        axis=0).astype(BF16)
    wr_pad = jnp.pad(w_router, ((0, 0), (0, LANES - N_EXPERTS)))
    wr_hi = wr_pad.astype(BF16)
    wr_lo = (wr_pad - wr_hi.astype(F32)).astype(BF16)
    brt = jnp.concatenate([b_router, jnp.full((LANES - N_EXPERTS,), NEG, F32)]).reshape(1, LANES)
    h, hn, comb = _merge(xp, xs, onsa, onsa_s, oret, oret_s, omem, omem_s, r1(g_attn), w_bg, wn_pad, w_br_ret.astype(BF16),
                         w_br_mem.astype(BF16), w_out.astype(BF16), r1(g_ffn), jnp.stack([wr_hi, wr_lo]), brt, tm)

    tb = tm
    for cand in range(2048, tm - 1, -tm):
        if n_all % cand == 0:
            tb = cand
            break
    y = _moe(hn, h, comb, w_gate_up, w_down, b_gate_up.reshape(N_EXPERTS, 1, 2 * D_FF),
             b_down.reshape(N_EXPERTS, 1, D_MODEL), tb)

    kv5 = lambda a, bsz, t: a.reshape(bsz, t, 2, NSA_KV_HEADS, NSA_HD)
    wp = min(WINDOW, seq)
    return (y[:n_p].reshape(nb, seq, D_MODEL), y[n_p:].reshape(db, dseq, D_MODEL),
            kv5(kvc[:n_p], nb, seq), kv5(kvs[:n_p], nb, seq),
            kv5(kvw[:n_p].reshape(nb, seq, 256)[:, seq - wp:], nb, wp),
            ret_state_p, mem_kv_p.reshape(nb, mem_len, 2, MEM_HEADS, MEM_HD),
            kv5(kvc[n_p:], db, dseq), kv5(kvs[n_p:], db, dseq), kv5(new_win_s, db, wb), ret_state_s)
```

```python
import functools
import math

import jax
import jax.numpy as jnp
from jax import lax
from jax.experimental import pallas as pl
from jax.experimental.pallas import tpu as pltpu

F32 = jnp.float32
BF16 = jnp.bfloat16

D_MODEL = 1024
NSA_HEADS = 8
NSA_KV_HEADS = 2
NSA_GROUP = 4
NSA_HD = 64
CMP_BLOCK = 32
CMP_STRIDE = 16
CMP_HIDDEN = 128
SLC_BLOCK = 64
N_SELECT = 16
N_LOCAL_FORCED = 2
FORCE_BONUS = 1.0e4
WINDOW = 512
Q_BLOCK = 128
PAGE_SIZE = 128
RET_HEADS = 4
RET_DK = 64
RET_DV = 128
RET_CHUNK = 128
ROPE_BASE = 10000.0
MEM_HEADS = 4
MEM_HD = 128
N_EXPERTS = 32
TOP_K = 4
D_FF = 1024
SWIGLU_ALPHA = 1.702
SWIGLU_LIMIT = 7.0
EPS = 1e-6
NEG = -1e30

LANES = 128
VMEM_LIMIT = 56 * 1024 * 1024


def _dot(a, b):
    return jnp.dot(a, b, preferred_element_type=F32)


def _dot_nt(a, b):
    return lax.dot_general(a, b, (((1,), (1,)), ((), ())), preferred_element_type=F32)


def _sigmoid(x):
    return 1.0 / (1.0 + jnp.exp(-x))


def _split3(x):
    a = x.astype(BF16)
    r = x - a.astype(F32)
    b = r.astype(BF16)
    c = (r - b.astype(F32)).astype(BF16)
    return a, b, c


def _msoftmax(s, m):
    s = jnp.where(m, s, NEG)
    mx = jnp.max(s, axis=-1, keepdims=True)
    p = jnp.where(m, jnp.exp(s - mx), 0.0)
    return p / jnp.maximum(jnp.sum(p, axis=-1, keepdims=True), 1e-30)


def _halfnorm(t, g2):
    lane = lax.broadcasted_iota(jnp.int32, t.shape, 1)
    lo = lane < NSA_HD
    t2 = t * t
    s0 = jnp.sum(jnp.where(lo, t2, 0.0), axis=-1, keepdims=True)
    s1 = jnp.sum(jnp.where(lo, 0.0, t2), axis=-1, keepdims=True)
    r = jnp.where(lo, lax.rsqrt(s0 * (1.0 / NSA_HD) + EPS), lax.rsqrt(s1 * (1.0 / NSA_HD) + EPS))
    return t * r * g2


def _tilenorm(t, g, width):
    return t * lax.rsqrt(jnp.sum(t * t, axis=-1, keepdims=True) * (1.0 / width) + EPS) * g


def _rot(x, cos, sin):
    lane = lax.broadcasted_iota(jnp.int32, x.shape, 1)
    first = (lane % RET_DK) < (RET_DK // 2)
    n = x.shape[1]
    sw = jnp.where(first, pltpu.roll(x, n - RET_DK // 2, 1), pltpu.roll(x, RET_DK // 2, 1))
    return x * cos + sw * sin


C_Q, C_KV, C_NG, C_RET, C_MQ = 1024, 768, 128, 1536, 512
O_KV = C_Q
O_NG = O_KV + C_KV
O_RET = O_NG + C_NG
O_MQ = O_RET + C_RET
W1_COLS = O_MQ + C_MQ


def _proj_kernel(nbp, xp_ref, xs_ref, g_ref, w_ref, cos_ref, sin_ref, gq_ref, gks_ref, gkw_ref, gqm_ref,
                 q_ref, kvc_ref, kvs_ref, kvw_ref, ng_ref, rq_ref, rk_ref, rv_ref, rg_ref, mq_ref):
    i = pl.program_id(0)
    x = jnp.where(i < nbp, xp_ref[...], xs_ref[...])
    xn = (x * lax.rsqrt(jnp.mean(x * x, axis=-1, keepdims=True) + EPS) * g_ref[...]).astype(BF16)

    hq = _dot(xn, w_ref[:, 0:C_Q])
    for hh in range(NSA_HEADS):
        t = hq[:, LANES * hh:LANES * (hh + 1)]
        q_ref[:, LANES * hh:LANES * (hh + 1)] = (_tilenorm(t, gq_ref[...], NSA_HD) * NSA_HD ** -0.5).astype(BF16)

    hkv = _dot(xn, w_ref[:, O_KV:O_KV + C_KV])
    kvc_ref[...] = hkv[:, 0:256]
    kvs_ref[:, 0:128] = _halfnorm(hkv[:, 256:384], gks_ref[...])
    kvs_ref[:, 128:256] = hkv[:, 384:512]
    kvw_ref[:, 0:128] = _halfnorm(hkv[:, 512:640], gkw_ref[...])
    kvw_ref[:, 128:256] = hkv[:, 640:768]

    ng_ref[...] = _sigmoid(_dot(xn, w_ref[:, O_NG:O_NG + C_NG]))

    hr = _dot(xn, w_ref[:, O_RET:O_RET + C_RET])
    cos = cos_ref[...]
    sin = sin_ref[...]
    rq_ref[...] = _rot(hr[:, 0:256], cos, sin)
    rk_ref[...] = _rot(hr[:, 256:512], cos, sin) * RET_DK ** -0.5
    rv_ref[...] = hr[:, 512:1024]
    rg = hr[:, 1024:1536]
    rg_ref[...] = rg * _sigmoid(rg)

    hm = _dot(xn, w_ref[:, O_MQ:O_MQ + C_MQ])
    for hd in range(MEM_HEADS):
        t = hm[:, LANES * hd:LANES * (hd + 1)]
        mq_ref[:, LANES * hd:LANES * (hd + 1)] = (_tilenorm(t, gqm_ref[...], MEM_HD) * MEM_HD ** -0.5).astype(BF16)


def _project(xp, xs, g_attn, w1, cos_t, sin_t, gq2, gks2, gkw2, gqm, seq, tm):
    n_p, n_s = xp.shape[0], xs.shape[0]
    nbp, nbs = n_p // tm, n_s // tm
    n_all = n_p + n_s
    bps = seq // tm
    row = lambda w: pl.BlockSpec((tm, w), lambda i: (i, 0))
    const = lambda a: pl.BlockSpec(a.shape, lambda i: (0,) * a.ndim)
    rope = pl.BlockSpec((tm, 256), lambda i: (jnp.where(i < nbp, i % bps, bps + i - nbp), 0))
    widths = (1024, 256, 256, 256, 128, 256, 256, 512, 512, 512)
    dtypes = (BF16, F32, F32, F32, F32, F32, F32, F32, F32, BF16)
    return pl.pallas_call(
        functools.partial(_proj_kernel, nbp),
        grid=(nbp + nbs,),
        in_specs=[pl.BlockSpec((tm, D_MODEL), lambda i: (jnp.minimum(i, nbp - 1), 0)),
                  pl.BlockSpec((tm, D_MODEL), lambda i: (jnp.maximum(i - nbp, 0), 0)),
                  const(g_attn), const(w1), rope, rope, const(gq2), const(gks2), const(gkw2), const(gqm)],
        out_specs=[row(w) for w in widths],
        out_shape=[jax.ShapeDtypeStruct((n_all, w), d) for w, d in zip(widths, dtypes)],
        compiler_params=pltpu.CompilerParams(dimension_semantics=("arbitrary",), vmem_limit_bytes=VMEM_LIMIT),
        name="proj",
    )(xp, xs, g_attn, w1, cos_t, sin_t, gq2, gks2, gkw2, gqm)


CHUNK_W = CMP_STRIDE * 256
CHUNKS_PER_PAGE = PAGE_SIZE // CMP_STRIDE


def _compress_kernel(n_pages, pt_ref, pool_ref, wbig_ref, pe_ref, w2k_ref, w2v_ref, gk_ref,
                     ck_ref, cv_ref, x_scr, r_scr, sem):
    b = pl.program_id(0)
    n = n_pages * CHUNKS_PER_PAGE

    def page_copy(j):
        return pltpu.make_async_copy(pool_ref.at[pt_ref[b, j]],
                                     x_scr.at[pl.ds(CHUNKS_PER_PAGE * j, CHUNKS_PER_PAGE)], sem)

    for j in range(n_pages):
        page_copy(j).start()
    x_scr[pl.ds(n, 8), :] = pe_ref[...]
    for j in range(n_pages):
        page_copy(j).wait()

    r_scr[...] = _dot(x_scr[...].astype(BF16), wbig_ref[...])
    cvec = r_scr[n:n + 1, 0:512] + r_scr[n + 1:n + 2, 512:1024]
    hid = r_scr[0:n, 0:512] + r_scr[pl.ds(1, n), 512:1024] + cvec
    hb = (hid * _sigmoid(hid)).astype(BF16)
    ck_ref[0] = _halfnorm(_dot(hb[:, 0:256], w2k_ref[...]), gk_ref[...])
    cv_ref[0] = _dot(hb[:, 256:512], w2v_ref[...])


def _compress(page_table, pool, wbig, pe2, w2k, w2v, gk2):
    nb, n_pages = page_table.shape
    n = n_pages * CHUNKS_PER_PAGE
    const = lambda a: pl.BlockSpec(a.shape, lambda b, pt: (0,) * a.ndim)
    out = pl.BlockSpec((1, n, LANES), lambda b, pt: (b, 0, 0))
    return pl.pallas_call(
        functools.partial(_compress_kernel, n_pages),
        grid_spec=pltpu.PrefetchScalarGridSpec(
            num_scalar_prefetch=1, grid=(nb,),
            in_specs=[pl.BlockSpec(memory_space=pl.ANY), const(wbig), const(pe2), const(w2k), const(w2v), const(gk2)],
            out_specs=[out, out],
            scratch_shapes=[pltpu.VMEM((n + 8, CHUNK_W), F32), pltpu.VMEM((n + 8, 1024), F32),
                            pltpu.SemaphoreType.DMA(())]),
        out_shape=[jax.ShapeDtypeStruct((nb, n, LANES), F32)] * 2,
        compiler_params=pltpu.CompilerParams(dimension_semantics=("arbitrary",), vmem_limit_bytes=VMEM_LIMIT),
        name="compress",
    )(page_table, pool, wbig, pe2, w2k, w2v, gk2)


def _overlap_t(ns_rows, n_cmp_cols, n_cmp):
    s = lax.broadcasted_iota(jnp.int32, (ns_rows, n_cmp_cols), 0)
    n = lax.broadcasted_iota(jnp.int32, (ns_rows, n_cmp_cols), 1)
    ov = (n * CMP_STRIDE < s * SLC_BLOCK + SLC_BLOCK) & (n * CMP_STRIDE + CMP_BLOCK > s * SLC_BLOCK) & (n < n_cmp)
    return ov.astype(BF16)


def _importance_t(ov_t, pcs):
    a, b, c = _split3(pcs)
    return _dot_nt(ov_t, a) + _dot_nt(ov_t, b) + _dot_nt(ov_t, c)


def _block_scores_t(imp_t, tl):
    blk = lax.broadcasted_iota(jnp.int32, imp_t.shape, 0)
    cur = tl // SLC_BLOCK
    valid = blk <= cur
    forced = (blk == 0) | (valid & (blk > cur - N_LOCAL_FORCED))
    return jnp.where(valid, imp_t + jnp.where(forced, FORCE_BONUS, 0.0), NEG), blk


SLC_CHUNK = 512
RANK_ACCS = 4


def _nsa_prompt_kernel(seq, n_cmp, q_ref, kvs_ref, kvw_ref, ck_ref, cv_ref, ng_ref, et_ref, eg_ref, o_ref):
    i = pl.program_id(1)
    ns = seq // SLC_BLOCK
    q4 = q_ref[...]
    qs = jnp.concatenate([q4[:, LANES * g:LANES * (g + 1)] for g in range(NSA_GROUP)], axis=0)
    tq = i * Q_BLOCK + lax.broadcasted_iota(jnp.int32, (Q_BLOCK, 1), 0)
    rep = lambda a: jnp.concatenate([a] * NSA_GROUP, axis=0)
    t4 = rep(tq)

    ckb = ck_ref[0].astype(BF16)
    sc = _dot_nt(qs, ckb)
    jn = lax.broadcasted_iota(jnp.int32, sc.shape, 1)
    pc = _msoftmax(sc, (jn * CMP_STRIDE + CMP_BLOCK - 1) <= t4)
    o_cmp = _dot(pc.astype(BF16), cv_ref[0].astype(BF16))

    pcs = pc[0:128] + pc[128:256] + pc[256:384] + pc[384:512]
    imp_t = _importance_t(_overlap_t(ns, pcs.shape[1], n_cmp), pcs)
    tl = i * Q_BLOCK + lax.broadcasted_iota(jnp.int32, (ns, Q_BLOCK), 1)
    score, blk = _block_scores_t(imp_t, tl)
    ranks = [jnp.zeros(score.shape, F32) for _ in range(RANK_ACCS)]
    for k in range(ns):
        sk = score[k:k + 1, :]
        ranks[k % RANK_ACCS] += jnp.where((sk > score) | ((sk == score) & (blk > k)), 1.0, 0.0)
    rank = functools.reduce(lambda x, y: x + y, ranks)
    sel_t = jnp.where(rank < min(N_SELECT, ns), 1.0, 0.0).astype(BF16)
    if ns < LANES:
        sel_t = jnp.concatenate([sel_t, jnp.zeros((LANES - ns, Q_BLOCK), BF16)], axis=0)
    eye = (lax.broadcasted_iota(jnp.int32, (Q_BLOCK, Q_BLOCK), 0)
           == lax.broadcasted_iota(jnp.int32, (Q_BLOCK, Q_BLOCK), 1)).astype(BF16)
    sel = _dot_nt(eye, sel_t)

    qx = jnp.concatenate([qs, rep(((sel - 1.0) * -NEG).astype(BF16))], axis=1)
    kc = min(SLC_CHUNK, seq)

    def chunk(c, carry, bias):
        m, l, acc = carry
        k0 = pl.multiple_of(c * kc, kc)
        kx = jnp.concatenate([kvs_ref[pl.ds(k0, kc), 0:128].astype(BF16), et_ref[pl.ds(k0, kc), :]], axis=1)
        vv = kvs_ref[pl.ds(k0, kc), 128:256].astype(BF16)
        s = _dot_nt(qx, kx)
        if bias is not None:
            s = s + bias
        m_new = jnp.maximum(m, jnp.max(s, axis=-1, keepdims=True))
        alpha = jnp.exp(m - m_new)
        p = jnp.exp(s - m_new)
        l = alpha * l + jnp.sum(p, axis=-1, keepdims=True)
        acc = alpha * acc + _dot(p.astype(BF16), vv)
        return m_new, l, acc

    rows = NSA_GROUP * Q_BLOCK
    init = (jnp.full((rows, 1), NEG, F32), jnp.zeros((rows, 1), F32), jnp.zeros((rows, LANES), F32))
    c_last = (i * Q_BLOCK) // kc
    carry = lax.fori_loop(0, c_last, lambda c, cr: chunk(c, cr, None), init)
    kpos = c_last * kc + lax.broadcasted_iota(jnp.int32, (Q_BLOCK, kc), 1)
    _, l, acc = chunk(c_last, carry, rep(jnp.where(kpos <= tq, 0.0, NEG)))
    o_slc = acc / l

    wk = WINDOW + Q_BLOCK
    start = pl.multiple_of(Q_BLOCK * jnp.maximum(i - WINDOW // Q_BLOCK, 0), Q_BLOCK)
    kk = kvw_ref[pl.ds(start, wk), 0:128].astype(BF16)
    vv = kvw_ref[pl.ds(start, wk), 128:256].astype(BF16)
    d = tq - (start + lax.broadcasted_iota(jnp.int32, (Q_BLOCK, wk), 1))
    sw = _dot_nt(qs, kk) + rep(jnp.where((d >= 0) & (d < WINDOW), 0.0, NEG))
    pw = jnp.exp(sw - jnp.max(sw, axis=-1, keepdims=True))
    o_win = _dot(pw.astype(BF16), vv) / jnp.sum(pw, axis=-1, keepdims=True)

    eg = eg_ref[0]
    a, b, c = _split3(ng_ref[...])
    gexp = _dot(a, eg) + _dot(b, eg) + _dot(c, eg)
    for g in range(NSA_GROUP):
        rs = slice(Q_BLOCK * g, Q_BLOCK * (g + 1))
        gate = lambda br: gexp[:, LANES * (NSA_GROUP * br + g):LANES * (NSA_GROUP * br + g + 1)]
        o = gate(0) * o_cmp[rs] + gate(1) * o_slc[rs] + gate(2) * o_win[rs]
        o_ref[:, LANES * g:LANES * (g + 1)] = o.astype(BF16)


def _nsa_prompt(q, kvs, kvw, ck, cv, ng, nb, seq, n_cmp):
    nq = seq // Q_BLOCK
    ncp = ck.shape[1]
    ns = seq // SLC_BLOCK
    assert ns <= LANES and seq >= WINDOW + Q_BLOCK
    et = (jnp.arange(seq)[:, None] // SLC_BLOCK == jnp.arange(LANES)[None, :]).astype(BF16)
    n_g = 3 * NSA_GROUP
    col = jnp.arange(n_g * LANES)[None, None, :] // LANES
    src = (col // NSA_GROUP) * NSA_HEADS + NSA_GROUP * jnp.arange(NSA_KV_HEADS)[:, None, None] + col % NSA_GROUP
    eg = (jnp.arange(LANES)[None, :, None] == src).astype(BF16)
    return pl.pallas_call(
        functools.partial(_nsa_prompt_kernel, seq, n_cmp),
        grid=(nb, nq, NSA_KV_HEADS),
        in_specs=[pl.BlockSpec((Q_BLOCK, 512), lambda b, i, h: (b * nq + i, h)),
                  pl.BlockSpec((seq, 256), lambda b, i, h: (b, 0)),
                  pl.BlockSpec((seq, 256), lambda b, i, h: (b, 0)),
                  pl.BlockSpec((1, ncp, LANES), lambda b, i, h: (b, 0, 0)),
                  pl.BlockSpec((1, ncp, LANES), lambda b, i, h: (b, 0, 0)),
                  pl.BlockSpec((Q_BLOCK, LANES), lambda b, i, h: (b * nq + i, 0)),
                  pl.BlockSpec((seq, LANES), lambda b, i, h: (0, 0)),
                  pl.BlockSpec((1, LANES, n_g * LANES), lambda b, i, h: (h, 0, 0))],
        out_specs=pl.BlockSpec((Q_BLOCK, 512), lambda b, i, h: (b * nq + i, h)),
        out_shape=jax.ShapeDtypeStruct((nb * seq, 1024), BF16),
        compiler_params=pltpu.CompilerParams(dimension_semantics=("arbitrary",) * 3, vmem_limit_bytes=VMEM_LIMIT),
        name="nsa_prompt",
    )(q, kvs, kvw, ck, cv, ng, et, eg)


def _nsa_sample_kernel(past, n_pages, dseq, wb, pt_ref, q_ref, kvs_ref, kvw_ref, ck_ref, cv_ref, ng_ref, cwin_ref,
                       e_ref, pool_ref, o_ref, nwin_ref, kv_scr, kw_scr, sc_scr, sem):
    b = pl.program_id(0)

    def page_copy(j):
        return pltpu.make_async_copy(pool_ref.at[pt_ref[b, j]], kv_scr.at[pl.ds(PAGE_SIZE * j, PAGE_SIZE)], sem)

    for j in range(n_pages):
        page_copy(j).start()

    n_heads = NSA_KV_HEADS * NSA_GROUP
    rows = n_heads * dseq
    kpad = past + PAGE_SIZE
    ns = (past + dseq + SLC_BLOCK - 1) // SLC_BLOCK
    nsp = e_ref.shape[0]
    qf = q_ref[0].astype(F32)
    qs = jnp.concatenate([qf[:, LANES * hh:LANES * (hh + 1)] for hh in range(n_heads)], axis=0).astype(BF16)
    t1 = past + lax.broadcasted_iota(jnp.int32, (dseq, 1), 0)
    t = jnp.concatenate([t1] * n_heads, axis=0)

    sc = _dot_nt(qs, ck_ref[0].astype(BF16))
    jn = lax.broadcasted_iota(jnp.int32, sc.shape, 1)
    pc = _msoftmax(sc, (jn * CMP_STRIDE + CMP_BLOCK - 1) <= t)
    o_cmp = _dot(pc.astype(BF16), cv_ref[0].astype(BF16))

    per_kv = []
    for kvh in range(NSA_KV_HEADS):
        base = kvh * NSA_GROUP * dseq
        s = pc[base:base + dseq]
        for g in range(1, NSA_GROUP):
            s = s + pc[base + g * dseq:base + (g + 1) * dseq]
        per_kv += [s] * NSA_GROUP
    pcs = jnp.concatenate(per_kv, axis=0)
    n_cmp = (ns * SLC_BLOCK) // CMP_STRIDE - CMP_BLOCK // CMP_STRIDE + 1
    imp_t = _importance_t(_overlap_t(nsp, pcs.shape[1], n_cmp), pcs)
    tl = past + lax.broadcasted_iota(jnp.int32, (nsp, rows), 1) % dseq
    score, blk = _block_scores_t(imp_t, tl)
    sc_scr[...] = score

    def rank_body(k, rank):
        sk = sc_scr[pl.ds(k, 1), :]
        return rank + jnp.where((sk > score) | ((sk == score) & (blk > k)), 1.0, 0.0)

    rank = lax.fori_loop(0, ns, rank_body, jnp.zeros(score.shape, F32))
    sel_t = jnp.where(rank < min(N_SELECT, ns), 1.0, 0.0).astype(BF16)
    eye = (lax.broadcasted_iota(jnp.int32, (rows, rows), 0)
           == lax.broadcasted_iota(jnp.int32, (rows, rows), 1)).astype(BF16)
    sel = _dot_nt(eye, sel_t)
    sele = _dot(sel.astype(BF16), e_ref[...])

    kv_scr[pl.ds(past, dseq), :] = kvs_ref[0]
    kv_scr[pl.ds(past + dseq, PAGE_SIZE - dseq), :] = jnp.zeros((PAGE_SIZE - dseq, 256), F32)
    for j in range(n_pages):
        page_copy(j).wait()
    s = _dot_nt(qs, kv_scr[:, 0:128].astype(BF16))
    kpos = lax.broadcasted_iota(jnp.int32, (rows, kpad), 1)
    p = _msoftmax(s, (sele > 0.5) & (kpos <= t))
    o_slc = _dot(p.astype(BF16), kv_scr[:, 128:256].astype(BF16))

    wpad = kw_scr.shape[0]
    kw_scr[pl.ds(0, wb), :] = cwin_ref[0]
    kw_scr[pl.ds(wb, dseq), :] = kvw_ref[0]
    kw_scr[pl.ds(wb + dseq, wpad - wb - dseq), :] = jnp.zeros((wpad - wb - dseq, 256), F32)
    sw = _dot_nt(qs, kw_scr[:, 0:128].astype(BF16))
    d = t - (past - wb + lax.broadcasted_iota(jnp.int32, (rows, wpad), 1))
    pw = _msoftmax(sw, (d >= 0) & (d < WINDOW))
    o_win = _dot(pw.astype(BF16), kw_scr[:, 128:256].astype(BF16))
    nwin_ref[0, pl.ds(0, wb - dseq), :] = cwin_ref[0, pl.ds(dseq, wb - dseq), :]
    nwin_ref[0, pl.ds(wb - dseq, dseq), :] = kvw_ref[0]

    ngv = ng_ref[0]
    for hh in range(n_heads):
        rs = slice(dseq * hh, dseq * (hh + 1))
        gate = lambda br: ngv[:, br * NSA_HEADS + hh:br * NSA_HEADS + hh + 1]
        o_ref[0, :, LANES * hh:LANES * (hh + 1)] = gate(0) * o_cmp[rs] + gate(1) * o_slc[rs] + gate(2) * o_win[rs]


def _nsa_sample(page_table, q, kvs, kvw, ck, cv, ng, cwin, e_s, pool, past):
    nb, n_pages = page_table.shape
    dseq = q.shape[1]
    wb = cwin.shape[1]
    kpad = past + PAGE_SIZE
    wpad = wb + PAGE_SIZE
    nsp = e_s.shape[0]
    rows = NSA_HEADS * dseq
    per_b = lambda a: pl.BlockSpec((1,) + a.shape[1:], lambda b, pt: (b,) + (0,) * (a.ndim - 1))
    return pl.pallas_call(
        functools.partial(_nsa_sample_kernel, past, n_pages, dseq, wb),
        grid_spec=pltpu.PrefetchScalarGridSpec(
            num_scalar_prefetch=1, grid=(nb,),
            in_specs=[per_b(q), per_b(kvs), per_b(kvw), per_b(ck), per_b(cv), per_b(ng), per_b(cwin),
                      pl.BlockSpec(e_s.shape, lambda b, pt: (0, 0)), pl.BlockSpec(memory_space=pl.ANY)],
            out_specs=[pl.BlockSpec((1, dseq, 1024), lambda b, pt: (b, 0, 0)),
                       pl.BlockSpec((1, wb, 256), lambda b, pt: (b, 0, 0))],
            scratch_shapes=[pltpu.VMEM((kpad, 256), F32), pltpu.VMEM((wpad, 256), F32),
                            pltpu.VMEM((nsp, rows), F32), pltpu.SemaphoreType.DMA(())]),
        out_shape=[jax.ShapeDtypeStruct((nb, dseq, 1024), F32), jax.ShapeDtypeStruct((nb, wb, 256), F32)],
        compiler_params=pltpu.CompilerParams(dimension_semantics=("arbitrary",), vmem_limit_bytes=VMEM_LIMIT),
        name="nsa_sample",
    )(page_table, q, kvs, kvw, ck, cv, ng, cwin, e_s, pool)


def _ret_kernel(c_true, rq_ref, rk_ref, rv_ref, rg_ref, s0_ref, gro_ref, o_ref, sout_ref, s_scr):
    c = pl.program_id(1)
    cs = RET_CHUNK

    @pl.when(c == 0)
    def _():
        for tile in range(RET_HEADS // 2):
            s_scr[tile] = jnp.concatenate([s0_ref[0, 2 * tile], s0_ref[0, 2 * tile + 1]], axis=0)

    diff = (lax.broadcasted_iota(jnp.int32, (cs, cs), 0) - lax.broadcasted_iota(jnp.int32, (cs, cs), 1)).astype(F32)
    ic = lax.broadcasted_iota(jnp.int32, (cs, 1), 0).astype(F32)
    lane = lax.broadcasted_iota(jnp.int32, (cs, LANES), 1)
    for hd in range(RET_HEADS):
        lg = math.log(1.0 - 2.0 ** (-5.0 - hd))
        tile, half = hd // 2, hd % 2
        in_half = (lane >= RET_DK * half) & (lane < RET_DK * (half + 1))
        qm = jnp.where(in_half, rq_ref[:, LANES * tile:LANES * (tile + 1)], 0.0).astype(BF16)
        k2 = rk_ref[:, LANES * tile:LANES * (tile + 1)]
        vb = rv_ref[:, LANES * hd:LANES * (hd + 1)].astype(BF16)
        dmat = jnp.where(diff >= 0, jnp.exp(jnp.maximum(diff, 0.0) * lg), 0.0)
        o = _dot((_dot_nt(qm, k2.astype(BF16)) * dmat).astype(BF16), vb)
        st = s_scr[tile]
        o = o + _dot(qm, st.astype(BF16)) * jnp.exp((ic + 1.0) * lg)
        kdec_t = (k2 * jnp.exp((c_true - 1.0 - ic) * lg)).T
        upd = _dot(kdec_t[RET_DK * half:RET_DK * (half + 1)].astype(BF16), vb)
        s_new = st[RET_DK * half:RET_DK * (half + 1)] * math.exp(c_true * lg) + upd
        s_scr[tile, pl.ds(RET_DK * half, RET_DK), :] = s_new
        on = _tilenorm(o, gro_ref[...], RET_DV)
        o_ref[:, LANES * hd:LANES * (hd + 1)] = (rg_ref[:, LANES * hd:LANES * (hd + 1)] * on).astype(BF16)

    @pl.when(c == pl.num_programs(1) - 1)
    def _():
        for hd in range(RET_HEADS):
            sout_ref[0, hd] = s_scr[hd // 2, pl.ds(RET_DK * (hd % 2), RET_DK), :]


def _retention(rq, rk, rv, rg, s0, gro, nb, n_chunks, c_true):
    cs = RET_CHUNK
    n_rows = nb * n_chunks * cs
    row = lambda w: pl.BlockSpec((cs, w), lambda b, c: (b * n_chunks + c, 0))
    return pl.pallas_call(
        functools.partial(_ret_kernel, float(c_true)),
        grid=(nb, n_chunks),
        in_specs=[row(256), row(256), row(512), row(512),
                  pl.BlockSpec((1, RET_HEADS, RET_DK, RET_DV), lambda b, c: (b, 0, 0, 0)),
                  pl.BlockSpec(gro.shape, lambda b, c: (0, 0))],
        out_specs=[row(512), pl.BlockSpec((1, RET_HEADS, RET_DK, RET_DV), lambda b, c: (b, 0, 0, 0))],
        out_shape=[jax.ShapeDtypeStruct((n_rows, 512), BF16),
                   jax.ShapeDtypeStruct((nb, RET_HEADS, RET_DK, RET_DV), F32)],
        scratch_shapes=[pltpu.VMEM((RET_HEADS // 2, LANES, RET_DV), F32)],
        compiler_params=pltpu.CompilerParams(dimension_semantics=("arbitrary", "arbitrary")),
        name="retention",
    )(rq, rk, rv, rg, s0, gro)


def _memkv_kernel(m_ref, g_ref, w_ref, gk_ref, o_ref):
    x = m_ref[...]
    xn = (x * lax.rsqrt(jnp.mean(x * x, axis=-1, keepdims=True) + EPS) * g_ref[...]).astype(BF16)
    hk = _dot(xn, w_ref[...])
    half = MEM_HEADS * MEM_HD
    for hd in range(MEM_HEADS):
        o_ref[:, LANES * hd:LANES * (hd + 1)] = _tilenorm(hk[:, LANES * hd:LANES * (hd + 1)], gk_ref[...], MEM_HD)
    o_ref[:, half:2 * half] = hk[:, half:2 * half]


def _mem_kv(mem2d, g_mem, w_mem, gk_mem, tm):
    n = mem2d.shape[0]
    const = lambda a: pl.BlockSpec(a.shape, lambda i: (0,) * a.ndim)
    return pl.pallas_call(
        _memkv_kernel,
        grid=(n // tm,),
        in_specs=[pl.BlockSpec((tm, D_MODEL), lambda i: (i, 0)), const(g_mem), const(w_mem), const(gk_mem)],
        out_specs=pl.BlockSpec((tm, 1024), lambda i: (i, 0)),
        out_shape=jax.ShapeDtypeStruct((n, 1024), F32),
        compiler_params=pltpu.CompilerParams(dimension_semantics=("arbitrary",)),
        name="mem_kv",
    )(mem2d, g_mem, w_mem, gk_mem)


def _memattn_kernel(q_ref, kv_ref, o_ref):
    half = MEM_HEADS * MEM_HD
    for hd in range(MEM_HEADS):
        cols = slice(LANES * hd, LANES * (hd + 1))
        s = _dot_nt(q_ref[:, cols].astype(BF16), kv_ref[:, cols].astype(BF16))
        p = jnp.exp(s - jnp.max(s, axis=-1, keepdims=True))
        p = p / jnp.sum(p, axis=-1, keepdims=True)
        o = _dot(p.astype(BF16), kv_ref[:, half + LANES * hd:half + LANES * (hd + 1)].astype(BF16))
        o_ref[:, cols] = o.astype(o_ref.dtype)


def _mem_attend(q2d, kv2d, nb, blocks_per_b, tq, mem_len, out_rows, out_dtype):
    return pl.pallas_call(
        _memattn_kernel,
        grid=(nb, blocks_per_b),
        in_specs=[pl.BlockSpec((tq, 512), lambda b, i: (b * blocks_per_b + i, 0)),
                  pl.BlockSpec((mem_len, 1024), lambda b, i: (b, 0))],
        out_specs=pl.BlockSpec((tq, 512), lambda b, i: (b * blocks_per_b + i, 0)),
        out_shape=jax.ShapeDtypeStruct((out_rows, 512), out_dtype),
        compiler_params=pltpu.CompilerParams(dimension_semantics=("arbitrary", "arbitrary")),
        name="mem_attend",
    )(q2d, kv2d)


def _merge_kernel(nbp, xp_ref, xs_ref, onsap_ref, onsas_ref, oretp_ref, orets_ref, omemp_ref, omems_ref, g_ref,
                  wbg_ref, wn_ref, wr_ref, wm_ref, wo_ref, gffn_ref, wrt_ref, brt_ref, h_ref, hn_ref, comb_ref):
    i = pl.program_id(0)
    pick = lambda p_ref, s_ref: jnp.where(i < nbp, p_ref[...], s_ref[...])
    x = pick(xp_ref, xs_ref)
    xn = (x * lax.rsqrt(jnp.mean(x * x, axis=-1, keepdims=True) + EPS) * g_ref[...]).astype(BF16)
    bg = _sigmoid(_dot(xn, wbg_ref[...]))
    mixed = (bg[:, 0:1024] * _dot(pick(onsap_ref, onsas_ref), wn_ref[...])
             + bg[:, 1024:2048] * _dot(pick(oretp_ref, orets_ref), wr_ref[...])
             + bg[:, 2048:3072] * _dot(pick(omemp_ref, omems_ref), wm_ref[...]))
    hres = x + _dot(mixed.astype(BF16), wo_ref[...])
    h_ref[...] = hres
    hn = hres * lax.rsqrt(jnp.mean(hres * hres, axis=-1, keepdims=True) + EPS) * gffn_ref[...]
    hn_ref[...] = hn.astype(BF16)

    a, b, _ = _split3(hn)
    logits = _dot(a, wrt_ref[0]) + _dot(a, wrt_ref[1]) + _dot(b, wrt_ref[0]) + brt_ref[...]
    lane = lax.broadcasted_iota(jnp.int32, logits.shape, 1).astype(F32)
    vals, hots = [], []
    for _k in range(TOP_K):
        mx = jnp.max(logits, axis=-1, keepdims=True)
        idx = jnp.min(jnp.where(logits == mx, lane, float(LANES)), axis=-1, keepdims=True)
        hot = lane == idx
        vals.append(mx)
        hots.append(hot)
        logits = jnp.where(hot, -3.0e38, logits)
    es = [jnp.exp(v - vals[0]) for v in vals]
    den = es[0] + es[1] + es[2] + es[3]
    comb = jnp.zeros(logits.shape, F32)
    for e_k, hot in zip(es, hots):
        comb = comb + jnp.where(hot, e_k / den, 0.0)
    comb_ref[...] = comb


def _merge(xp, xs, onsa, onsa_s, oret, oret_s, omem, omem_s, g_attn, wbg, wn, wr, wm, wo, gffn, wrt, brt, tm):
    n_p, n_s = xp.shape[0], xs.shape[0]
    nbp, nbs = n_p // tm, n_s // tm
    n_all = n_p + n_s
    row = lambda w: pl.BlockSpec((tm, w), lambda i: (i, 0))
    rowp = lambda w: pl.BlockSpec((tm, w), lambda i: (jnp.minimum(i, nbp - 1), 0))
    rows = lambda w: pl.BlockSpec((tm, w), lambda i: (jnp.maximum(i - nbp, 0), 0))
    const = lambda a: pl.BlockSpec(a.shape, lambda i: (0,) * a.ndim)
    return pl.pallas_call(
        functools.partial(_merge_kernel, nbp),
        grid=(nbp + nbs,),
        in_specs=[rowp(1024), rows(1024), rowp(1024), rows(1024), rowp(512), rows(512), rowp(512), rows(512),
                  const(g_attn), const(wbg), const(wn), const(wr), const(wm),
                  const(wo), const(gffn), const(wrt), const(brt)],
        out_specs=[row(1024), row(1024), row(LANES)],
        out_shape=[jax.ShapeDtypeStruct((n_all, 1024), F32), jax.ShapeDtypeStruct((n_all, 1024), BF16),
                   jax.ShapeDtypeStruct((n_all, LANES), F32)],
        compiler_params=pltpu.CompilerParams(dimension_semantics=("arbitrary",), vmem_limit_bytes=VMEM_LIMIT),
        name="merge",
    )(xp, xs, onsa, onsa_s, oret, oret_s, omem, omem_s, g_attn, wbg, wn, wr, wm, wo, gffn, wrt, brt)


MOE_BLOCK_CAP = 1664
MOE_CHUNK = 256
ROUTE_TILE = 128


def _route_kernel(comb_ref, pos_ref, post_ref, cnt_ref):
    tb = comb_ref.shape[0]
    rt = ROUTE_TILE
    r = lax.broadcasted_iota(jnp.int32, (rt, rt), 0)
    c = lax.broadcasted_iota(jnp.int32, (rt, rt), 1)
    ltri = (c < r).astype(BF16)
    eye = (c == r).astype(BF16)
    carry = jnp.zeros((1, LANES), F32)
    for s in range(tb // rt):
        sel = jnp.where(comb_ref[pl.ds(rt * s, rt), :] > 0.0, 1.0, 0.0)
        pos = jnp.where(sel > 0.0, _dot(ltri, sel.astype(BF16)) + carry, -1.0)
        carry = carry + jnp.sum(sel, axis=0, keepdims=True)
        pos_ref[pl.ds(rt * s, rt), :] = pos
        hi = jnp.floor((pos + 1.0) * (1.0 / 256.0))
        lo = (pos + 1.0) - 256.0 * hi
        post_ref[:, rt * s:rt * (s + 1)] = 256.0 * _dot_nt(eye, hi.astype(BF16)) + _dot_nt(eye, lo.astype(BF16)) - 1.0
    cnt_ref[0] = jnp.broadcast_to(carry, (8, LANES))


def _route(comb, tb):
    n_all = comb.shape[0]
    nblk = n_all // tb
    return pl.pallas_call(
        _route_kernel,
        grid=(nblk,),
        in_specs=[pl.BlockSpec((tb, LANES), lambda i: (i, 0))],
        out_specs=[pl.BlockSpec((tb, LANES), lambda i: (i, 0)), pl.BlockSpec((LANES, tb), lambda i: (0, i)),
                   pl.BlockSpec((1, 8, LANES), lambda i: (i, 0, 0))],
        out_shape=[jax.ShapeDtypeStruct((n_all, LANES), F32), jax.ShapeDtypeStruct((LANES, n_all), F32),
                   jax.ShapeDtypeStruct((nblk, 8, LANES), F32)],
        compiler_params=pltpu.CompilerParams(dimension_semantics=("arbitrary",)),
        name="route",
    )(comb)


def _moe_kernel(cnt_ref, hn_ref, h_ref, comb_ref, pos_ref, post_ref, wgu_ref, wd_ref, bgu_ref, bd_ref, y_ref):
    i = pl.program_id(0)
    e = pl.program_id(1)
    tb = hn_ref.shape[0]
    ch = MOE_CHUNK

    @pl.when(e == 0)
    def _():
        y_ref[...] = h_ref[...]

    n_rows = cnt_ref[i * N_EXPERTS + e]
    lane = lax.broadcasted_iota(jnp.int32, (tb, LANES), 1)
    pcol = jnp.sum(jnp.where(lane == e, pos_ref[...], 0.0), axis=-1, keepdims=True)
    ccol = jnp.sum(jnp.where(lane == e, comb_ref[...], 0.0), axis=-1, keepdims=True)
    prow = post_ref[pl.ds(e, 1), :]

    def chunk(c, carry):
        r0 = (c * ch).astype(F32)
        slot_g = r0 + lax.broadcasted_iota(jnp.int32, (ch, tb), 0).astype(F32)
        gather = jnp.where(prow == slot_g, 1.0, 0.0).astype(BF16)
        xg = _dot(gather, hn_ref[...]).astype(BF16)
        gu = _dot(xg, wgu_ref[0]) + bgu_ref[0]
        gate = jnp.minimum(gu[:, 0:D_FF], SWIGLU_LIMIT)
        up = jnp.clip(gu[:, D_FF:2 * D_FF], -SWIGLU_LIMIT, SWIGLU_LIMIT)
        act = (up + 1.0) * gate * _sigmoid(SWIGLU_ALPHA * gate)
        yc = _dot(act.astype(BF16), wd_ref[0]) + bd_ref[0]
        slot_s = r0 + lax.broadcasted_iota(jnp.int32, (tb, ch), 1).astype(F32)
        scatter = jnp.where(pcol == slot_s, ccol, 0.0).astype(BF16)
        y_ref[...] += _dot(scatter, yc.astype(BF16))
        return carry

    lax.fori_loop(0, (n_rows + ch - 1) // ch, chunk, 0)


def _moe(cnt, hn, h, comb, pos, post, wgu, wd, b_gate_up, b_down, tb):
    n_all = hn.shape[0]
    row = lambda w: pl.BlockSpec((tb, w), lambda i, e, cnt: (i, 0))
    return pl.pallas_call(
        _moe_kernel,
        grid_spec=pltpu.PrefetchScalarGridSpec(
            num_scalar_prefetch=1, grid=(n_all // tb, N_EXPERTS),
            in_specs=[row(1024), row(1024), row(LANES), row(LANES),
                      pl.BlockSpec((LANES, tb), lambda i, e, cnt: (0, i)),
                      pl.BlockSpec((1, D_MODEL, 2 * D_FF), lambda i, e, cnt: (e, 0, 0)),
                      pl.BlockSpec((1, D_FF, D_MODEL), lambda i, e, cnt: (e, 0, 0)),
                      pl.BlockSpec((1, 1, 2 * D_FF), lambda i, e, cnt: (e, 0, 0)),
                      pl.BlockSpec((1, 1, D_MODEL), lambda i, e, cnt: (e, 0, 0))],
            out_specs=row(1024)),
        out_shape=jax.ShapeDtypeStruct((n_all, 1024), F32),
        compiler_params=pltpu.CompilerParams(dimension_semantics=("arbitrary",) * 2, vmem_limit_bytes=VMEM_LIMIT),
        name="moe",
    )(cnt, hn, h, comb, pos, post, wgu, wd, b_gate_up, b_down)


def _pick_tile(n, pref):
    t = pref
    while n % t:
        t //= 2
    return t


def kernel(x_prompt, x_sample, cache_cmp, cache_slc, cache_win, state_ret, cache_mem, page_table, mem_prompt,
           g_attn, w_in, g_q_nsa, g_k_cmp, g_k_slc, g_k_win, pe_ck, w_ck1, w_ck2, pe_cv, w_cv1, w_cv2,
           g_ret_out, g_mem, w_mem_kv, g_q_mem, g_k_mem, w_br_nsa, w_br_ret, w_br_mem, w_out,
           g_ffn, w_router, b_router, w_gate_up, b_gate_up, w_down, b_down):
    nb, seq, _ = x_prompt.shape
    db, dseq, _ = x_sample.shape
    n_pages = page_table.shape[1]
    past = n_pages * PAGE_SIZE
    wb = cache_win.shape[1]
    mem_len = mem_prompt.shape[1]
    n_p, n_s = nb * seq, db * dseq
    n_all = n_p + n_s
    tm = _pick_tile(math.gcd(seq, n_s), 256)

    o = 0
    cols = {}
    for name, wdt in (("q", 512), ("kv", 768), ("ng", 24), ("rq", 256), ("rk", 256), ("rv", 512), ("rg", 512),
                      ("mq", 512), ("bg", 3072)):
        cols[name] = w_in[:, o:o + wdt]
        o += wdt
    wq = cols["q"].reshape(D_MODEL, NSA_HEADS, NSA_HD)
    zq = jnp.zeros_like(wq)
    wq_pad = jnp.concatenate(
        [jnp.concatenate([wq[:, hh], zq[:, hh]] if hh < NSA_GROUP else [zq[:, hh], wq[:, hh]], axis=-1)
         for hh in range(NSA_HEADS)], axis=-1)
    w_ng = jnp.pad(cols["ng"], ((0, 0), (0, C_NG - 24)))
    w1 = jnp.concatenate([wq_pad, cols["kv"], w_ng, cols["rq"], cols["rk"], cols["rv"], cols["rg"], cols["mq"]],
                         axis=-1).astype(BF16)
    w_bg = cols["bg"].astype(BF16)
    two = lambda g: jnp.concatenate([g, g]).reshape(1, LANES)
    r1 = lambda g: g.reshape(1, -1)

    half = RET_DK // 2
    inv = ROPE_BASE ** (-jnp.arange(half, dtype=F32) / half)
    pos = jnp.concatenate([jnp.arange(seq, dtype=jnp.int32),
                           jnp.tile(past + jnp.arange(dseq, dtype=jnp.int32), db)]).astype(F32)
    ang = pos[:, None] * inv[None, :]
    cos_t = jnp.tile(jnp.cos(ang), (1, 2 * RET_HEADS))
    sin_t = jnp.tile(jnp.concatenate([-jnp.sin(ang), jnp.sin(ang)], axis=-1), (1, RET_HEADS))

    xp = x_prompt.reshape(n_p, D_MODEL)
    xs = x_sample.reshape(n_s, D_MODEL)
    (q, kvc, kvs, kvw, ng, rq, rk, rv, rg, mq) = _project(
        xp, xs, r1(g_attn), w1, cos_t, sin_t, two(g_q_nsa), two(g_k_slc), two(g_k_win), r1(g_q_mem), seq, tm)

    w1s = jnp.stack([w_ck1, w_ck1, w_cv1, w_cv1]).reshape(4, 2, CMP_STRIDE, NSA_HD, CMP_HIDDEN)
    wbig = jnp.einsum("shpdc,st->psdhtc", w1s, jnp.eye(4, dtype=F32)).reshape(CHUNK_W, 2 * 4 * CMP_HIDDEN).astype(BF16)
    pes = jnp.stack([pe_ck, pe_ck, pe_cv, pe_cv]).reshape(4, 2, CMP_STRIDE, NSA_HD)
    pe2 = jnp.pad(jnp.transpose(pes, (1, 2, 0, 3)).reshape(2, CHUNK_W), ((0, 6), (0, 0)))
    zc = jnp.zeros_like(w_ck2)
    bd2 = lambda w: jnp.concatenate([jnp.concatenate([w, zc], 1), jnp.concatenate([zc, w], 1)], 0).astype(BF16)
    gk2 = two(g_k_cmp)

    pages_p = seq // PAGE_SIZE
    pool_p = (kvc if n_all % PAGE_SIZE == 0 else kvc[:n_p]).reshape(-1, CHUNKS_PER_PAGE, CHUNK_W)
    pt_p = jnp.arange(nb * pages_p, dtype=jnp.int32).reshape(nb, pages_p)
    ck_p, cv_p = _compress(pt_p, pool_p, wbig, pe2, bd2(w_ck2), bd2(w_cv2), gk2)
    pool_s = cache_cmp.reshape(-1, CHUNKS_PER_PAGE, CHUNK_W)
    ck_s, cv_s = _compress(page_table, pool_s, wbig, pe2, bd2(w_ck2), bd2(w_cv2), gk2)

    n_cmp_p = seq // CMP_STRIDE - CMP_BLOCK // CMP_STRIDE + 1
    onsa = _nsa_prompt(q, kvs, kvw, ck_p, cv_p, ng, nb, seq, n_cmp_p)

    ns_s = (past + dseq + SLC_BLOCK - 1) // SLC_BLOCK
    nsp = (ns_s + 7) // 8 * 8
    kpad = past + PAGE_SIZE
    e_s = (jnp.arange(kpad)[None, :] // SLC_BLOCK == jnp.arange(nsp)[:, None]).astype(BF16)
    s3 = lambda a: a[n_p:].reshape(db, dseq, a.shape[1])
    onsa_s, new_win_s = _nsa_sample(page_table, s3(q), s3(kvs), s3(kvw), ck_s, cv_s, s3(ng),
                                    cache_win.reshape(db, wb, 256), e_s,
                                    cache_slc.reshape(-1, PAGE_SIZE, 256), past)
    onsa_s = onsa_s.reshape(n_s, 1024).astype(BF16)

    gro = r1(g_ret_out)
    oret, ret_state_p = _retention(rq, rk, rv, rg, jnp.zeros((nb, RET_HEADS, RET_DK, RET_DV), F32), gro,
                                   nb, seq // RET_CHUNK, RET_CHUNK)
    padc = lambda a: jnp.pad(s3(a), ((0, 0), (0, RET_CHUNK - dseq), (0, 0))).reshape(db * RET_CHUNK, a.shape[1])
    oret_s, ret_state_s = _retention(padc(rq), padc(rk), padc(rv), padc(rg), state_ret, gro, db, 1, dseq)
    oret_s = oret_s.reshape(db, RET_CHUNK, 512)[:, :dseq].reshape(n_s, 512)

    mem_kv_p = _mem_kv(mem_prompt.reshape(nb * mem_len, D_MODEL), r1(g_mem), w_mem_kv.astype(BF16), r1(g_k_mem),
                       _pick_tile(nb * mem_len, 256))
    tq = _pick_tile(seq, 512)
    omem = _mem_attend(mq, mem_kv_p, nb, seq // tq, tq, mem_len, n_p, BF16)
    omem_s = _mem_attend(s3(mq).astype(F32).reshape(n_s, 512), cache_mem.reshape(db * mem_len, 1024), db, 1, dseq,
                         mem_len, n_s, F32).astype(BF16)

    wn = w_br_nsa.reshape(NSA_HEADS, NSA_HD, D_MODEL)
    zn = jnp.zeros_like(wn)
    wn_pad = jnp.concatenate(
        [jnp.concatenate([wn[hh], zn[hh]] if hh < NSA_GROUP else [zn[hh], wn[hh]], axis=0) for hh in range(NSA_HEADS)],
        axis=0).astype(BF16)
    wr_pad = jnp.pad(w_router, ((0, 0), (0, LANES - N_EXPERTS)))
    wr_hi = wr_pad.astype(BF16)
    wr_lo = (wr_pad - wr_hi.astype(F32)).astype(BF16)
    brt = jnp.concatenate([b_router, jnp.full((LANES - N_EXPERTS,), NEG, F32)]).reshape(1, LANES)
    h, hn, comb = _merge(xp, xs, onsa, onsa_s, oret, oret_s, omem, omem_s, r1(g_attn), w_bg, wn_pad, w_br_ret.astype(BF16),
                         w_br_mem.astype(BF16), w_out.astype(BF16), r1(g_ffn), jnp.stack([wr_hi, wr_lo]), brt, tm)

    tb = max(c for c in range(LANES, MOE_BLOCK_CAP + 1, LANES) if n_all % c == 0)
    pos, post, cnt = _route(comb, tb)
    cnt = cnt[:, 0, :N_EXPERTS].astype(jnp.int32).reshape(-1)
    y = _moe(cnt, hn, h, comb, pos, post, w_gate_up.astype(BF16), w_down.astype(BF16),
             b_gate_up.reshape(N_EXPERTS, 1, 2 * D_FF), b_down.reshape(N_EXPERTS, 1, D_MODEL), tb)

    kv5 = lambda a, bsz, t: a.reshape(bsz, t, 2, NSA_KV_HEADS, NSA_HD)
    wp = min(WINDOW, seq)
    return (y[:n_p].reshape(nb, seq, D_MODEL), y[n_p:].reshape(db, dseq, D_MODEL),
            kv5(kvc[:n_p], nb, seq), kv5(kvs[:n_p], nb, seq),
            kv5(kvw[:n_p].reshape(nb, seq, 256)[:, seq - wp:], nb, wp),
            ret_state_p, mem_kv_p.reshape(nb, mem_len, 2, MEM_HEADS, MEM_HD),
            kv5(kvc[n_p:], db, dseq), kv5(kvs[n_p:], db, dseq), kv5(new_win_s, db, wb), ret_state_s)
```

```python
import functools
import math

import jax
import jax.numpy as jnp
from jax import lax
from jax.experimental import pallas as pl
from jax.experimental.pallas import tpu as pltpu

F32 = jnp.float32
BF16 = jnp.bfloat16

D_MODEL = 1024
NSA_HEADS = 8
NSA_KV_HEADS = 2
NSA_GROUP = 4
NSA_HD = 64
CMP_BLOCK = 32
CMP_STRIDE = 16
CMP_HIDDEN = 128
SLC_BLOCK = 64
N_SELECT = 16
N_LOCAL_FORCED = 2
FORCE_BONUS = 1.0e4
WINDOW = 512
Q_BLOCK = 128
PAGE_SIZE = 128
RET_HEADS = 4
RET_DK = 64
RET_DV = 128
RET_CHUNK = 128
ROPE_BASE = 10000.0
MEM_HEADS = 4
MEM_HD = 128
N_EXPERTS = 32
TOP_K = 4
D_FF = 1024
SWIGLU_ALPHA = 1.702
SWIGLU_LIMIT = 7.0
EPS = 1e-6
NEG = -1e30

LANES = 128
VMEM_LIMIT = 56 * 1024 * 1024


def _dot(a, b):
    return jnp.dot(a, b, preferred_element_type=F32)


def _dot_nt(a, b):
    return lax.dot_general(a, b, (((1,), (1,)), ((), ())), preferred_element_type=F32)


def _sigmoid(x):
    return 1.0 / (1.0 + jnp.exp(-x))


def _split3(x):
    a = x.astype(BF16)
    r = x - a.astype(F32)
    b = r.astype(BF16)
    c = (r - b.astype(F32)).astype(BF16)
    return a, b, c


def _msoftmax(s, m):
    s = jnp.where(m, s, NEG)
    mx = jnp.max(s, axis=-1, keepdims=True)
    p = jnp.where(m, jnp.exp(s - mx), 0.0)
    return p / jnp.maximum(jnp.sum(p, axis=-1, keepdims=True), 1e-30)


def _halfnorm(t, g2):
    lane = lax.broadcasted_iota(jnp.int32, t.shape, 1)
    lo = lane < NSA_HD
    t2 = t * t
    s0 = jnp.sum(jnp.where(lo, t2, 0.0), axis=-1, keepdims=True)
    s1 = jnp.sum(jnp.where(lo, 0.0, t2), axis=-1, keepdims=True)
    r = jnp.where(lo, lax.rsqrt(s0 * (1.0 / NSA_HD) + EPS), lax.rsqrt(s1 * (1.0 / NSA_HD) + EPS))
    return t * r * g2


def _tilenorm(t, g, width):
    return t * lax.rsqrt(jnp.sum(t * t, axis=-1, keepdims=True) * (1.0 / width) + EPS) * g


def _rot(x, cos, sin):
    lane = lax.broadcasted_iota(jnp.int32, x.shape, 1)
    first = (lane % RET_DK) < (RET_DK // 2)
    n = x.shape[1]
    sw = jnp.where(first, pltpu.roll(x, n - RET_DK // 2, 1), pltpu.roll(x, RET_DK // 2, 1))
    return x * cos + sw * sin


C_Q, C_KV, C_NG, C_RET, C_MQ = 1024, 768, 128, 1536, 512
O_KV = C_Q
O_NG = O_KV + C_KV
O_RET = O_NG + C_NG
O_MQ = O_RET + C_RET
W1_COLS = O_MQ + C_MQ


def _proj_kernel(nbp, xp_ref, xs_ref, g_ref, w_ref, cos_ref, sin_ref, gq_ref, gks_ref, gkw_ref, gqm_ref,
                 q_ref, kvc_ref, kvs_ref, kvw_ref, ng_ref, rq_ref, rk_ref, rv_ref, rg_ref, mq_ref):
    i = pl.program_id(0)
    x = jnp.where(i < nbp, xp_ref[...], xs_ref[...])
    xn = (x * lax.rsqrt(jnp.mean(x * x, axis=-1, keepdims=True) + EPS) * g_ref[...]).astype(BF16)

    hq = _dot(xn, w_ref[:, 0:C_Q])
    for hh in range(NSA_HEADS):
        t = hq[:, LANES * hh:LANES * (hh + 1)]
        q_ref[:, LANES * hh:LANES * (hh + 1)] = (_tilenorm(t, gq_ref[...], NSA_HD) * NSA_HD ** -0.5).astype(BF16)

    hkv = _dot(xn, w_ref[:, O_KV:O_KV + C_KV])
    kvc_ref[...] = hkv[:, 0:256]
    kvs_ref[:, 0:128] = _halfnorm(hkv[:, 256:384], gks_ref[...])
    kvs_ref[:, 128:256] = hkv[:, 384:512]
    kvw_ref[:, 0:128] = _halfnorm(hkv[:, 512:640], gkw_ref[...])
    kvw_ref[:, 128:256] = hkv[:, 640:768]

    ng_ref[...] = _sigmoid(_dot(xn, w_ref[:, O_NG:O_NG + C_NG]))

    hr = _dot(xn, w_ref[:, O_RET:O_RET + C_RET])
    cos = cos_ref[...]
    sin = sin_ref[...]
    rq_ref[...] = _rot(hr[:, 0:256], cos, sin)
    rk_ref[...] = _rot(hr[:, 256:512], cos, sin) * RET_DK ** -0.5
    rv_ref[...] = hr[:, 512:1024]
    rg = hr[:, 1024:1536]
    rg_ref[...] = rg * _sigmoid(rg)

    hm = _dot(xn, w_ref[:, O_MQ:O_MQ + C_MQ])
    for hd in range(MEM_HEADS):
        t = hm[:, LANES * hd:LANES * (hd + 1)]
        mq_ref[:, LANES * hd:LANES * (hd + 1)] = (_tilenorm(t, gqm_ref[...], MEM_HD) * MEM_HD ** -0.5).astype(BF16)


def _project(xp, xs, g_attn, w1, cos_t, sin_t, gq2, gks2, gkw2, gqm, seq, tm):
    n_p, n_s = xp.shape[0], xs.shape[0]
    nbp, nbs = n_p // tm, n_s // tm
    n_all = n_p + n_s
    bps = seq // tm
    row = lambda w: pl.BlockSpec((tm, w), lambda i: (i, 0))
    const = lambda a: pl.BlockSpec(a.shape, lambda i: (0,) * a.ndim)
    rope = pl.BlockSpec((tm, 256), lambda i: (jnp.where(i < nbp, i % bps, bps + i - nbp), 0))
    widths = (1024, 256, 256, 256, 128, 256, 256, 512, 512, 512)
    dtypes = (BF16, F32, F32, F32, F32, F32, F32, F32, F32, BF16)
    return pl.pallas_call(
        functools.partial(_proj_kernel, nbp),
        grid=(nbp + nbs,),
        in_specs=[pl.BlockSpec((tm, D_MODEL), lambda i: (jnp.minimum(i, nbp - 1), 0)),
                  pl.BlockSpec((tm, D_MODEL), lambda i: (jnp.maximum(i - nbp, 0), 0)),
                  const(g_attn), const(w1), rope, rope, const(gq2), const(gks2), const(gkw2), const(gqm)],
        out_specs=[row(w) for w in widths],
        out_shape=[jax.ShapeDtypeStruct((n_all, w), d) for w, d in zip(widths, dtypes)],
        compiler_params=pltpu.CompilerParams(dimension_semantics=("arbitrary",), vmem_limit_bytes=VMEM_LIMIT),
        name="proj",
    )(xp, xs, g_attn, w1, cos_t, sin_t, gq2, gks2, gkw2, gqm)


CHUNK_W = CMP_STRIDE * 256
CHUNKS_PER_PAGE = PAGE_SIZE // CMP_STRIDE


def _compress_kernel(n_pages, pt_ref, pool_ref, wbig_ref, pe_ref, w2k_ref, w2v_ref, gk_ref,
                     ck_ref, cv_ref, x_scr, r_scr, sem):
    b = pl.program_id(0)
    n = n_pages * CHUNKS_PER_PAGE

    def page_copy(j):
        return pltpu.make_async_copy(pool_ref.at[pt_ref[b, j]],
                                     x_scr.at[pl.ds(CHUNKS_PER_PAGE * j, CHUNKS_PER_PAGE)], sem)

    for j in range(n_pages):
        page_copy(j).start()
    x_scr[pl.ds(n, 8), :] = pe_ref[...]
    for j in range(n_pages):
        page_copy(j).wait()

    r_scr[...] = _dot(x_scr[...].astype(BF16), wbig_ref[...])
    cvec = r_scr[n:n + 1, 0:512] + r_scr[n + 1:n + 2, 512:1024]
    hid = r_scr[0:n, 0:512] + r_scr[pl.ds(1, n), 512:1024] + cvec
    hb = (hid * _sigmoid(hid)).astype(BF16)
    ck_ref[0] = _halfnorm(_dot(hb[:, 0:256], w2k_ref[...]), gk_ref[...])
    cv_ref[0] = _dot(hb[:, 256:512], w2v_ref[...])


def _compress(page_table, pool, wbig, pe2, w2k, w2v, gk2):
    nb, n_pages = page_table.shape
    n = n_pages * CHUNKS_PER_PAGE
    const = lambda a: pl.BlockSpec(a.shape, lambda b, pt: (0,) * a.ndim)
    out = pl.BlockSpec((1, n, LANES), lambda b, pt: (b, 0, 0))
    return pl.pallas_call(
        functools.partial(_compress_kernel, n_pages),
        grid_spec=pltpu.PrefetchScalarGridSpec(
            num_scalar_prefetch=1, grid=(nb,),
            in_specs=[pl.BlockSpec(memory_space=pl.ANY), const(wbig), const(pe2), const(w2k), const(w2v), const(gk2)],
            out_specs=[out, out],
            scratch_shapes=[pltpu.VMEM((n + 8, CHUNK_W), F32), pltpu.VMEM((n + 8, 1024), F32),
                            pltpu.SemaphoreType.DMA(())]),
        out_shape=[jax.ShapeDtypeStruct((nb, n, LANES), F32)] * 2,
        compiler_params=pltpu.CompilerParams(dimension_semantics=("arbitrary",), vmem_limit_bytes=VMEM_LIMIT),
        name="compress",
    )(page_table, pool, wbig, pe2, w2k, w2v, gk2)


def _compress_t_kernel(n_pages, pt_ref, pool_ref, wbig_ref, pe_ref, w2k_ref, w2v_ref, gk_ref,
                       ck_ref, cv_ref, pg_scr, t_scr, r_scr, cvec_scr, sem):
    b = pl.program_id(0)
    slot = b % 2
    n = n_pages * CHUNKS_PER_PAGE

    def page_copy(bb, sl, j):
        return pltpu.make_async_copy(pool_ref.at[pt_ref[bb, j]], pg_scr.at[sl, j], sem.at[sl])

    @pl.when(b == 0)
    def _():
        for j in range(n_pages):
            page_copy(0, 0, j).start()
        r = _dot(pe_ref[...].astype(BF16), wbig_ref[...])
        cvec_scr[...] = jnp.broadcast_to(r[0:1, 0:512] + r[1:2, 512:1024], cvec_scr.shape)
        r_scr[pl.ds(n, 8), :] = jnp.zeros((8, 1024), F32)

    @pl.when(b + 1 < pl.num_programs(0))
    def _():
        for j in range(n_pages):
            page_copy(b + 1, 1 - slot, j).start()

    for j in range(n_pages):
        page_copy(b, slot, j).wait()

    for j in range(n_pages):
        rows = pl.ds(PAGE_SIZE * j, PAGE_SIZE)
        t_scr[0, rows, :] = pg_scr[slot, j, 0:LANES, :].T
        t_scr[1, rows, :] = pg_scr[slot, j, LANES:2 * LANES, :].T

    acc = jnp.zeros((n, 1024), F32)
    for p in range(CMP_STRIDE):
        rows = pl.ds(p, n, stride=CMP_STRIDE)
        a = jnp.concatenate([t_scr[0, rows, :], t_scr[1, rows, :]], axis=1).astype(BF16)
        acc = acc + _dot(a, wbig_ref[256 * p:256 * (p + 1), :])
    r_scr[pl.ds(0, n), :] = acc
    hid = acc[:, 0:512] + r_scr[pl.ds(1, n), 512:1024] + cvec_scr[0:1, :]
    hb = (hid * _sigmoid(hid)).astype(BF16)
    ck_ref[0] = _halfnorm(_dot(hb[:, 0:256], w2k_ref[...]), gk_ref[...])
    cv_ref[0] = _dot(hb[:, 256:512], w2v_ref[...])


def _compress_t(page_table, pool_t, wbig, pe2, w2k, w2v, gk2):
    nb, n_pages = page_table.shape
    n = n_pages * CHUNKS_PER_PAGE
    const = lambda a: pl.BlockSpec(a.shape, lambda b, pt: (0,) * a.ndim)
    out = pl.BlockSpec((1, n, LANES), lambda b, pt: (b, 0, 0))
    return pl.pallas_call(
        functools.partial(_compress_t_kernel, n_pages),
        grid_spec=pltpu.PrefetchScalarGridSpec(
            num_scalar_prefetch=1, grid=(nb,),
            in_specs=[pl.BlockSpec(memory_space=pl.ANY), const(wbig), const(pe2), const(w2k), const(w2v), const(gk2)],
            out_specs=[out, out],
            scratch_shapes=[pltpu.VMEM((2, n_pages, 256, PAGE_SIZE), F32), pltpu.VMEM((2, n_pages * PAGE_SIZE, LANES), F32),
                            pltpu.VMEM((n + 8, 1024), F32), pltpu.VMEM((8, 512), F32),
                            pltpu.SemaphoreType.DMA((2,))]),
        out_shape=[jax.ShapeDtypeStruct((nb, n, LANES), F32)] * 2,
        compiler_params=pltpu.CompilerParams(dimension_semantics=("arbitrary",), vmem_limit_bytes=VMEM_LIMIT),
        name="compress_t",
    )(page_table, pool_t, wbig, pe2, w2k, w2v, gk2)


def _overlap_t(ns_rows, n_cmp_cols, n_cmp):
    s = lax.broadcasted_iota(jnp.int32, (ns_rows, n_cmp_cols), 0)
    n = lax.broadcasted_iota(jnp.int32, (ns_rows, n_cmp_cols), 1)
    ov = (n * CMP_STRIDE < s * SLC_BLOCK + SLC_BLOCK) & (n * CMP_STRIDE + CMP_BLOCK > s * SLC_BLOCK) & (n < n_cmp)
    return ov.astype(BF16)


def _importance_t(ov_t, pcs):
    a, b, c = _split3(pcs)
    return _dot_nt(ov_t, a) + _dot_nt(ov_t, b) + _dot_nt(ov_t, c)


def _block_scores_t(imp_t, tl):
    blk = lax.broadcasted_iota(jnp.int32, imp_t.shape, 0)
    cur = tl // SLC_BLOCK
    valid = blk <= cur
    forced = (blk == 0) | (valid & (blk > cur - N_LOCAL_FORCED))
    return jnp.where(valid, imp_t + jnp.where(forced, FORCE_BONUS, 0.0), NEG), blk


SLC_CHUNK = 512
RANK_ACCS = 4


def _nsa_prompt_kernel(seq, n_cmp, q_ref, kvs_ref, kvw_ref, ck_ref, cv_ref, ng_ref, et_ref, eg_ref, o_ref):
    i = pl.program_id(1)
    ns = seq // SLC_BLOCK
    q4 = q_ref[...]
    qs = jnp.concatenate([q4[:, LANES * g:LANES * (g + 1)] for g in range(NSA_GROUP)], axis=0)
    tq = i * Q_BLOCK + lax.broadcasted_iota(jnp.int32, (Q_BLOCK, 1), 0)
    rep = lambda a: jnp.concatenate([a] * NSA_GROUP, axis=0)
    t4 = rep(tq)

    ckb = ck_ref[0].astype(BF16)
    sc = _dot_nt(qs, ckb)
    jn = lax.broadcasted_iota(jnp.int32, sc.shape, 1)
    pc = _msoftmax(sc, (jn * CMP_STRIDE + CMP_BLOCK - 1) <= t4)
    o_cmp = _dot(pc.astype(BF16), cv_ref[0].astype(BF16))

    pcs = pc[0:128] + pc[128:256] + pc[256:384] + pc[384:512]
    imp_t = _importance_t(_overlap_t(ns, pcs.shape[1], n_cmp), pcs)
    tl = i * Q_BLOCK + lax.broadcasted_iota(jnp.int32, (ns, Q_BLOCK), 1)
    score, blk = _block_scores_t(imp_t, tl)
    ranks = [jnp.zeros(score.shape, F32) for _ in range(RANK_ACCS)]
    for k in range(ns):
        sk = score[k:k + 1, :]
        ranks[k % RANK_ACCS] += jnp.where((sk > score) | ((sk == score) & (blk > k)), 1.0, 0.0)
    rank = functools.reduce(lambda x, y: x + y, ranks)
    sel_t = jnp.where(rank < min(N_SELECT, ns), 1.0, 0.0).astype(BF16)
    if ns < LANES:
        sel_t = jnp.concatenate([sel_t, jnp.zeros((LANES - ns, Q_BLOCK), BF16)], axis=0)
    eye = (lax.broadcasted_iota(jnp.int32, (Q_BLOCK, Q_BLOCK), 0)
           == lax.broadcasted_iota(jnp.int32, (Q_BLOCK, Q_BLOCK), 1)).astype(BF16)
    sel = _dot_nt(eye, sel_t)

    qx = jnp.concatenate([qs, rep(((sel - 1.0) * -NEG).astype(BF16))], axis=1)
    kc = min(SLC_CHUNK, seq)

    def chunk(c, carry, bias):
        m, l, acc = carry
        k0 = pl.multiple_of(c * kc, kc)
        kx = jnp.concatenate([kvs_ref[pl.ds(k0, kc), 0:128].astype(BF16), et_ref[pl.ds(k0, kc), :]], axis=1)
        vv = kvs_ref[pl.ds(k0, kc), 128:256].astype(BF16)
        s = _dot_nt(qx, kx)
        if bias is not None:
            s = s + bias
        m_new = jnp.maximum(m, jnp.max(s, axis=-1, keepdims=True))
        alpha = jnp.exp(m - m_new)
        p = jnp.exp(s - m_new)
        l = alpha * l + jnp.sum(p, axis=-1, keepdims=True)
        acc = alpha * acc + _dot(p.astype(BF16), vv)
        return m_new, l, acc

    rows = NSA_GROUP * Q_BLOCK
    init = (jnp.full((rows, 1), NEG, F32), jnp.zeros((rows, 1), F32), jnp.zeros((rows, LANES), F32))
    c_last = (i * Q_BLOCK) // kc
    carry = lax.fori_loop(0, c_last, lambda c, cr: chunk(c, cr, None), init)
    kpos = c_last * kc + lax.broadcasted_iota(jnp.int32, (Q_BLOCK, kc), 1)
    _, l, acc = chunk(c_last, carry, rep(jnp.where(kpos <= tq, 0.0, NEG)))
    o_slc = acc / l

    wk = WINDOW + Q_BLOCK
    start = pl.multiple_of(Q_BLOCK * jnp.maximum(i - WINDOW // Q_BLOCK, 0), Q_BLOCK)
    kk = kvw_ref[pl.ds(start, wk), 0:128].astype(BF16)
    vv = kvw_ref[pl.ds(start, wk), 128:256].astype(BF16)
    d = tq - (start + lax.broadcasted_iota(jnp.int32, (Q_BLOCK, wk), 1))
    sw = _dot_nt(qs, kk) + rep(jnp.where((d >= 0) & (d < WINDOW), 0.0, NEG))
    pw = jnp.exp(sw - jnp.max(sw, axis=-1, keepdims=True))
    o_win = _dot(pw.astype(BF16), vv) / jnp.sum(pw, axis=-1, keepdims=True)

    eg = eg_ref[0]
    a, b, c = _split3(ng_ref[...])
    gexp = _dot(a, eg) + _dot(b, eg) + _dot(c, eg)
    for g in range(NSA_GROUP):
        rs = slice(Q_BLOCK * g, Q_BLOCK * (g + 1))
        gate = lambda br: gexp[:, LANES * (NSA_GROUP * br + g):LANES * (NSA_GROUP * br + g + 1)]
        o = gate(0) * o_cmp[rs] + gate(1) * o_slc[rs] + gate(2) * o_win[rs]
        o_ref[:, LANES * g:LANES * (g + 1)] = o.astype(BF16)


def _nsa_prompt(q, kvs, kvw, ck, cv, ng, nb, seq, n_cmp):
    nq = seq // Q_BLOCK
    ncp = ck.shape[1]
    ns = seq // SLC_BLOCK
    assert ns <= LANES and seq >= WINDOW + Q_BLOCK
    et = (jnp.arange(seq)[:, None] // SLC_BLOCK == jnp.arange(LANES)[None, :]).astype(BF16)
    n_g = 3 * NSA_GROUP
    col = jnp.arange(n_g * LANES)[None, None, :] // LANES
    src = (col // NSA_GROUP) * NSA_HEADS + NSA_GROUP * jnp.arange(NSA_KV_HEADS)[:, None, None] + col % NSA_GROUP
    eg = (jnp.arange(LANES)[None, :, None] == src).astype(BF16)
    return pl.pallas_call(
        functools.partial(_nsa_prompt_kernel, seq, n_cmp),
        grid=(nb, nq, NSA_KV_HEADS),
        in_specs=[pl.BlockSpec((Q_BLOCK, 512), lambda b, i, h: (b * nq + i, h)),
                  pl.BlockSpec((seq, 256), lambda b, i, h: (b, 0)),
                  pl.BlockSpec((seq, 256), lambda b, i, h: (b, 0)),
                  pl.BlockSpec((1, ncp, LANES), lambda b, i, h: (b, 0, 0)),
                  pl.BlockSpec((1, ncp, LANES), lambda b, i, h: (b, 0, 0)),
                  pl.BlockSpec((Q_BLOCK, LANES), lambda b, i, h: (b * nq + i, 0)),
                  pl.BlockSpec((seq, LANES), lambda b, i, h: (0, 0)),
                  pl.BlockSpec((1, LANES, n_g * LANES), lambda b, i, h: (h, 0, 0))],
        out_specs=pl.BlockSpec((Q_BLOCK, 512), lambda b, i, h: (b * nq + i, h)),
        out_shape=jax.ShapeDtypeStruct((nb * seq, 1024), BF16),
        compiler_params=pltpu.CompilerParams(dimension_semantics=("arbitrary",) * 3, vmem_limit_bytes=VMEM_LIMIT),
        name="nsa_prompt",
    )(q, kvs, kvw, ck, cv, ng, et, eg)


def _nsa_sample_kernel(past, n_pages, dseq, wb, pt_ref, q_ref, kvs_ref, kvw_ref, ck_ref, cv_ref, ng_ref, cwin_ref,
                       e_ref, pool_ref, o_ref, nwin_ref, kvt_scr, kw_scr, sc_scr, sem):
    b = pl.program_id(0)
    slot = b % 2

    def page_copy(bb, sl, j):
        return pltpu.make_async_copy(pool_ref.at[pt_ref[bb, j]], kvt_scr.at[sl, :, pl.ds(PAGE_SIZE * j, PAGE_SIZE)],
                                     sem.at[sl])

    @pl.when(b == 0)
    def _():
        for j in range(n_pages):
            page_copy(0, 0, j).start()

    @pl.when(b + 1 < pl.num_programs(0))
    def _():
        for j in range(n_pages):
            page_copy(b + 1, 1 - slot, j).start()

    n_heads = NSA_KV_HEADS * NSA_GROUP
    rows = n_heads * dseq
    kpad = past + PAGE_SIZE
    ns = (past + dseq + SLC_BLOCK - 1) // SLC_BLOCK
    nsp = e_ref.shape[0]
    qf = q_ref[0].astype(F32)
    qs = jnp.concatenate([qf[:, LANES * hh:LANES * (hh + 1)] for hh in range(n_heads)], axis=0).astype(BF16)
    t1 = past + lax.broadcasted_iota(jnp.int32, (dseq, 1), 0)
    t = jnp.concatenate([t1] * n_heads, axis=0)

    sc = _dot_nt(qs, ck_ref[0].astype(BF16))
    jn = lax.broadcasted_iota(jnp.int32, sc.shape, 1)
    pc = _msoftmax(sc, (jn * CMP_STRIDE + CMP_BLOCK - 1) <= t)
    o_cmp = _dot(pc.astype(BF16), cv_ref[0].astype(BF16))

    per_kv = []
    for kvh in range(NSA_KV_HEADS):
        base = kvh * NSA_GROUP * dseq
        s = pc[base:base + dseq]
        for g in range(1, NSA_GROUP):
            s = s + pc[base + g * dseq:base + (g + 1) * dseq]
        per_kv += [s] * NSA_GROUP
    pcs = jnp.concatenate(per_kv, axis=0)
    n_cmp = (ns * SLC_BLOCK) // CMP_STRIDE - CMP_BLOCK // CMP_STRIDE + 1
    imp_t = _importance_t(_overlap_t(nsp, pcs.shape[1], n_cmp), pcs)
    tl = past + lax.broadcasted_iota(jnp.int32, (nsp, rows), 1) % dseq
    score, blk = _block_scores_t(imp_t, tl)
    sc_scr[...] = score

    def rank_body(k, rank):
        sk = sc_scr[pl.ds(k, 1), :]
        return rank + jnp.where((sk > score) | ((sk == score) & (blk > k)), 1.0, 0.0)

    rank = lax.fori_loop(0, ns, rank_body, jnp.zeros(score.shape, F32))
    sel_t = jnp.where(rank < min(N_SELECT, ns), 1.0, 0.0).astype(BF16)
    eye = (lax.broadcasted_iota(jnp.int32, (rows, rows), 0)
           == lax.broadcasted_iota(jnp.int32, (rows, rows), 1)).astype(BF16)
    sel = _dot_nt(eye, sel_t)
    sele = _dot(sel.astype(BF16), e_ref[...])

    knew = jnp.concatenate([kvs_ref[0], jnp.zeros((PAGE_SIZE - dseq, 256), F32)], axis=0).astype(BF16)
    for j in range(n_pages):
        page_copy(b, slot, j).wait()
    s = jnp.concatenate([_dot(qs, kvt_scr[slot, 0:128, :].astype(BF16)), _dot_nt(qs, knew[:, 0:128])], axis=1)
    kpos = lax.broadcasted_iota(jnp.int32, (rows, kpad), 1)
    p = _msoftmax(s, (sele > 0.5) & (kpos <= t)).astype(BF16)
    o_slc = _dot_nt(p[:, 0:past], kvt_scr[slot, 128:256, :].astype(BF16)) + _dot(p[:, past:kpad], knew[:, 128:256])

    wpad = kw_scr.shape[0]
    kw_scr[pl.ds(0, wb), :] = cwin_ref[0]
    kw_scr[pl.ds(wb, dseq), :] = kvw_ref[0]
    kw_scr[pl.ds(wb + dseq, wpad - wb - dseq), :] = jnp.zeros((wpad - wb - dseq, 256), F32)
    sw = _dot_nt(qs, kw_scr[:, 0:128].astype(BF16))
    d = t - (past - wb + lax.broadcasted_iota(jnp.int32, (rows, wpad), 1))
    pw = _msoftmax(sw, (d >= 0) & (d < WINDOW))
    o_win = _dot(pw.astype(BF16), kw_scr[:, 128:256].astype(BF16))
    nwin_ref[0, pl.ds(0, wb - dseq), :] = cwin_ref[0, pl.ds(dseq, wb - dseq), :]
    nwin_ref[0, pl.ds(wb - dseq, dseq), :] = kvw_ref[0]

    ngv = ng_ref[0]
    for hh in range(n_heads):
        rs = slice(dseq * hh, dseq * (hh + 1))
        gate = lambda br: ngv[:, br * NSA_HEADS + hh:br * NSA_HEADS + hh + 1]
        o_ref[0, :, LANES * hh:LANES * (hh + 1)] = gate(0) * o_cmp[rs] + gate(1) * o_slc[rs] + gate(2) * o_win[rs]


def _nsa_sample(page_table, q, kvs, kvw, ck, cv, ng, cwin, e_s, pool, past):
    nb, n_pages = page_table.shape
    dseq = q.shape[1]
    wb = cwin.shape[1]
    kpad = past + PAGE_SIZE
    wpad = wb + PAGE_SIZE
    nsp = e_s.shape[0]
    rows = NSA_HEADS * dseq
    per_b = lambda a: pl.BlockSpec((1,) + a.shape[1:], lambda b, pt: (b,) + (0,) * (a.ndim - 1))
    return pl.pallas_call(
        functools.partial(_nsa_sample_kernel, past, n_pages, dseq, wb),
        grid_spec=pltpu.PrefetchScalarGridSpec(
            num_scalar_prefetch=1, grid=(nb,),
            in_specs=[per_b(q), per_b(kvs), per_b(kvw), per_b(ck), per_b(cv), per_b(ng), per_b(cwin),
                      pl.BlockSpec(e_s.shape, lambda b, pt: (0, 0)), pl.BlockSpec(memory_space=pl.ANY)],
            out_specs=[pl.BlockSpec((1, dseq, 1024), lambda b, pt: (b, 0, 0)),
                       pl.BlockSpec((1, wb, 256), lambda b, pt: (b, 0, 0))],
            scratch_shapes=[pltpu.VMEM((2, 256, past), F32), pltpu.VMEM((wpad, 256), F32),
                            pltpu.VMEM((nsp, rows), F32), pltpu.SemaphoreType.DMA((2,))]),
        out_shape=[jax.ShapeDtypeStruct((nb, dseq, 1024), F32), jax.ShapeDtypeStruct((nb, wb, 256), F32)],
        compiler_params=pltpu.CompilerParams(dimension_semantics=("arbitrary",), vmem_limit_bytes=VMEM_LIMIT),
        name="nsa_sample",
    )(page_table, q, kvs, kvw, ck, cv, ng, cwin, e_s, pool)


def _ret_kernel(c_true, rq_ref, rk_ref, rv_ref, rg_ref, s0_ref, gro_ref, o_ref, sout_ref, s_scr):
    c = pl.program_id(1)
    cs = RET_CHUNK

    @pl.when(c == 0)
    def _():
        for tile in range(RET_HEADS // 2):
            s_scr[tile] = jnp.concatenate([s0_ref[0, 2 * tile], s0_ref[0, 2 * tile + 1]], axis=0)

    diff = (lax.broadcasted_iota(jnp.int32, (cs, cs), 0) - lax.broadcasted_iota(jnp.int32, (cs, cs), 1)).astype(F32)
    ic = lax.broadcasted_iota(jnp.int32, (cs, 1), 0).astype(F32)
    lane = lax.broadcasted_iota(jnp.int32, (cs, LANES), 1)
    for hd in range(RET_HEADS):
        lg = math.log(1.0 - 2.0 ** (-5.0 - hd))
        tile, half = hd // 2, hd % 2
        in_half = (lane >= RET_DK * half) & (lane < RET_DK * (half + 1))
        qm = jnp.where(in_half, rq_ref[:, LANES * tile:LANES * (tile + 1)], 0.0).astype(BF16)
        k2 = rk_ref[:, LANES * tile:LANES * (tile + 1)]
        vb = rv_ref[:, LANES * hd:LANES * (hd + 1)].astype(BF16)
        dmat = jnp.where(diff >= 0, jnp.exp(jnp.maximum(diff, 0.0) * lg), 0.0)
        o = _dot((_dot_nt(qm, k2.astype(BF16)) * dmat).astype(BF16), vb)
        st = s_scr[tile]
        o = o + _dot(qm, st.astype(BF16)) * jnp.exp((ic + 1.0) * lg)
        kdec_t = (k2 * jnp.exp((c_true - 1.0 - ic) * lg)).T
        upd = _dot(kdec_t[RET_DK * half:RET_DK * (half + 1)].astype(BF16), vb)
        s_new = st[RET_DK * half:RET_DK * (half + 1)] * math.exp(c_true * lg) + upd
        s_scr[tile, pl.ds(RET_DK * half, RET_DK), :] = s_new
        on = _tilenorm(o, gro_ref[...], RET_DV)
        o_ref[:, LANES * hd:LANES * (hd + 1)] = (rg_ref[:, LANES * hd:LANES * (hd + 1)] * on).astype(BF16)

    @pl.when(c == pl.num_programs(1) - 1)
    def _():
        for hd in range(RET_HEADS):
            sout_ref[0, hd] = s_scr[hd // 2, pl.ds(RET_DK * (hd % 2), RET_DK), :]


def _retention(rq, rk, rv, rg, s0, gro, nb, n_chunks, c_true):
    cs = RET_CHUNK
    n_rows = nb * n_chunks * cs
    row = lambda w: pl.BlockSpec((cs, w), lambda b, c: (b * n_chunks + c, 0))
    return pl.pallas_call(
        functools.partial(_ret_kernel, float(c_true)),
        grid=(nb, n_chunks),
        in_specs=[row(256), row(256), row(512), row(512),
                  pl.BlockSpec((1, RET_HEADS, RET_DK, RET_DV), lambda b, c: (b, 0, 0, 0)),
                  pl.BlockSpec(gro.shape, lambda b, c: (0, 0))],
        out_specs=[row(512), pl.BlockSpec((1, RET_HEADS, RET_DK, RET_DV), lambda b, c: (b, 0, 0, 0))],
        out_shape=[jax.ShapeDtypeStruct((n_rows, 512), BF16),
                   jax.ShapeDtypeStruct((nb, RET_HEADS, RET_DK, RET_DV), F32)],
        scratch_shapes=[pltpu.VMEM((RET_HEADS // 2, LANES, RET_DV), F32)],
        compiler_params=pltpu.CompilerParams(dimension_semantics=("arbitrary", "arbitrary")),
        name="retention",
    )(rq, rk, rv, rg, s0, gro)


def _memkv_kernel(m_ref, g_ref, w_ref, gk_ref, o_ref):
    x = m_ref[...]
    xn = (x * lax.rsqrt(jnp.mean(x * x, axis=-1, keepdims=True) + EPS) * g_ref[...]).astype(BF16)
    hk = _dot(xn, w_ref[...])
    half = MEM_HEADS * MEM_HD
    for hd in range(MEM_HEADS):
        o_ref[:, LANES * hd:LANES * (hd + 1)] = _tilenorm(hk[:, LANES * hd:LANES * (hd + 1)], gk_ref[...], MEM_HD)
    o_ref[:, half:2 * half] = hk[:, half:2 * half]


def _mem_kv(mem2d, g_mem, w_mem, gk_mem, tm):
    n = mem2d.shape[0]
    const = lambda a: pl.BlockSpec(a.shape, lambda i: (0,) * a.ndim)
    return pl.pallas_call(
        _memkv_kernel,
        grid=(n // tm,),
        in_specs=[pl.BlockSpec((tm, D_MODEL), lambda i: (i, 0)), const(g_mem), const(w_mem), const(gk_mem)],
        out_specs=pl.BlockSpec((tm, 1024), lambda i: (i, 0)),
        out_shape=jax.ShapeDtypeStruct((n, 1024), F32),
        compiler_params=pltpu.CompilerParams(dimension_semantics=("arbitrary",)),
        name="mem_kv",
    )(mem2d, g_mem, w_mem, gk_mem)


def _memattn_kernel(q_ref, kv_ref, o_ref):
    half = MEM_HEADS * MEM_HD
    for hd in range(MEM_HEADS):
        cols = slice(LANES * hd, LANES * (hd + 1))
        s = _dot_nt(q_ref[:, cols].astype(BF16), kv_ref[:, cols].astype(BF16))
        p = jnp.exp(s - jnp.max(s, axis=-1, keepdims=True))
        p = p / jnp.sum(p, axis=-1, keepdims=True)
        o = _dot(p.astype(BF16), kv_ref[:, half + LANES * hd:half + LANES * (hd + 1)].astype(BF16))
        o_ref[:, cols] = o.astype(o_ref.dtype)


def _mem_attend(q2d, kv2d, nb, blocks_per_b, tq, mem_len, out_rows, out_dtype):
    return pl.pallas_call(
        _memattn_kernel,
        grid=(nb, blocks_per_b),
        in_specs=[pl.BlockSpec((tq, 512), lambda b, i: (b * blocks_per_b + i, 0)),
                  pl.BlockSpec((mem_len, 1024), lambda b, i: (b, 0))],
        out_specs=pl.BlockSpec((tq, 512), lambda b, i: (b * blocks_per_b + i, 0)),
        out_shape=jax.ShapeDtypeStruct((out_rows, 512), out_dtype),
        compiler_params=pltpu.CompilerParams(dimension_semantics=("arbitrary", "arbitrary")),
        name="mem_attend",
    )(q2d, kv2d)


def _merge_kernel(nbp, xp_ref, xs_ref, onsap_ref, onsas_ref, oretp_ref, orets_ref, omemp_ref, omems_ref, g_ref,
                  wbg_ref, wn_ref, wr_ref, wm_ref, wo_ref, gffn_ref, wrt_ref, brt_ref, h_ref, hn_ref, comb_ref):
    i = pl.program_id(0)
    pick = lambda p_ref, s_ref: jnp.where(i < nbp, p_ref[...], s_ref[...])
    x = pick(xp_ref, xs_ref)
    xn = (x * lax.rsqrt(jnp.mean(x * x, axis=-1, keepdims=True) + EPS) * g_ref[...]).astype(BF16)
    bg = _sigmoid(_dot(xn, wbg_ref[...]))
    mixed = (bg[:, 0:1024] * _dot(pick(onsap_ref, onsas_ref), wn_ref[...])
             + bg[:, 1024:2048] * _dot(pick(oretp_ref, orets_ref), wr_ref[...])
             + bg[:, 2048:3072] * _dot(pick(omemp_ref, omems_ref), wm_ref[...]))
    hres = x + _dot(mixed.astype(BF16), wo_ref[...])
    h_ref[...] = hres
    hn = hres * lax.rsqrt(jnp.mean(hres * hres, axis=-1, keepdims=True) + EPS) * gffn_ref[...]
    hn_ref[...] = hn.astype(BF16)

    a, b, _ = _split3(hn)
    logits = _dot(a, wrt_ref[0]) + _dot(a, wrt_ref[1]) + _dot(b, wrt_ref[0]) + brt_ref[...]
    lane = lax.broadcasted_iota(jnp.int32, logits.shape, 1).astype(F32)
    vals, hots = [], []
    for _k in range(TOP_K):
        mx = jnp.max(logits, axis=-1, keepdims=True)
        idx = jnp.min(jnp.where(logits == mx, lane, float(LANES)), axis=-1, keepdims=True)
        hot = lane == idx
        vals.append(mx)
        hots.append(hot)
        logits = jnp.where(hot, -3.0e38, logits)
    es = [jnp.exp(v - vals[0]) for v in vals]
    den = es[0] + es[1] + es[2] + es[3]
    comb = jnp.zeros(logits.shape, F32)
    for e_k, hot in zip(es, hots):
        comb = comb + jnp.where(hot, e_k / den, 0.0)
    comb_ref[...] = comb


def _merge(xp, xs, onsa, onsa_s, oret, oret_s, omem, omem_s, g_attn, wbg, wn, wr, wm, wo, gffn, wrt, brt, tm):
    n_p, n_s = xp.shape[0], xs.shape[0]
    nbp, nbs = n_p // tm, n_s // tm
    n_all = n_p + n_s
    row = lambda w: pl.BlockSpec((tm, w), lambda i: (i, 0))
    rowp = lambda w: pl.BlockSpec((tm, w), lambda i: (jnp.minimum(i, nbp - 1), 0))
    rows = lambda w: pl.BlockSpec((tm, w), lambda i: (jnp.maximum(i - nbp, 0), 0))
    const = lambda a: pl.BlockSpec(a.shape, lambda i: (0,) * a.ndim)
    return pl.pallas_call(
        functools.partial(_merge_kernel, nbp),
        grid=(nbp + nbs,),
        in_specs=[rowp(1024), rows(1024), rowp(1024), rows(1024), rowp(512), rows(512), rowp(512), rows(512),
                  const(g_attn), const(wbg), const(wn), const(wr), const(wm),
                  const(wo), const(gffn), const(wrt), const(brt)],
        out_specs=[row(1024), row(1024), row(LANES)],
        out_shape=[jax.ShapeDtypeStruct((n_all, 1024), F32), jax.ShapeDtypeStruct((n_all, 1024), BF16),
                   jax.ShapeDtypeStruct((n_all, LANES), F32)],
        compiler_params=pltpu.CompilerParams(dimension_semantics=("arbitrary",), vmem_limit_bytes=VMEM_LIMIT),
        name="merge",
    )(xp, xs, onsa, onsa_s, oret, oret_s, omem, omem_s, g_attn, wbg, wn, wr, wm, wo, gffn, wrt, brt)


MOE_BLOCK_CAP = 1664
MOE_CHUNK = 256
ROUTE_TILE = 128


def _route_kernel(comb_ref, pos_ref, post_ref, cnt_ref):
    tb = comb_ref.shape[0]
    rt = ROUTE_TILE
    r = lax.broadcasted_iota(jnp.int32, (rt, rt), 0)
    c = lax.broadcasted_iota(jnp.int32, (rt, rt), 1)
    ltri = (c < r).astype(BF16)
    eye = (c == r).astype(BF16)
    carry = jnp.zeros((1, LANES), F32)
    for s in range(tb // rt):
        sel = jnp.where(comb_ref[pl.ds(rt * s, rt), :] > 0.0, 1.0, 0.0)
        pos = jnp.where(sel > 0.0, _dot(ltri, sel.astype(BF16)) + carry, -1.0)
        carry = carry + jnp.sum(sel, axis=0, keepdims=True)
        pos_ref[pl.ds(rt * s, rt), :] = pos
        hi = jnp.floor((pos + 1.0) * (1.0 / 256.0))
        lo = (pos + 1.0) - 256.0 * hi
        post_ref[:, rt * s:rt * (s + 1)] = 256.0 * _dot_nt(eye, hi.astype(BF16)) + _dot_nt(eye, lo.astype(BF16)) - 1.0
    cnt_ref[0] = jnp.broadcast_to(carry, (8, LANES))


def _route(comb, tb):
    n_all = comb.shape[0]
    nblk = n_all // tb
    return pl.pallas_call(
        _route_kernel,
        grid=(nblk,),
        in_specs=[pl.BlockSpec((tb, LANES), lambda i: (i, 0))],
        out_specs=[pl.BlockSpec((tb, LANES), lambda i: (i, 0)), pl.BlockSpec((LANES, tb), lambda i: (0, i)),
                   pl.BlockSpec((1, 8, LANES), lambda i: (i, 0, 0))],
        out_shape=[jax.ShapeDtypeStruct((n_all, LANES), F32), jax.ShapeDtypeStruct((LANES, n_all), F32),
                   jax.ShapeDtypeStruct((nblk, 8, LANES), F32)],
        compiler_params=pltpu.CompilerParams(dimension_semantics=("arbitrary",)),
        name="route",
    )(comb)


def _moe_kernel(cnt_ref, hn_ref, h_ref, comb_ref, pos_ref, post_ref, wgu_ref, wd_ref, bgu_ref, bd_ref, y_ref):
    i = pl.program_id(0)
    e = pl.program_id(1)
    tb = hn_ref.shape[0]
    ch = MOE_CHUNK

    @pl.when(e == 0)
    def _():
        y_ref[...] = h_ref[...]

    n_rows = cnt_ref[i * N_EXPERTS + e]
    lane = lax.broadcasted_iota(jnp.int32, (tb, LANES), 1)
    pcol = jnp.sum(jnp.where(lane == e, pos_ref[...], 0.0), axis=-1, keepdims=True)
    ccol = jnp.sum(jnp.where(lane == e, comb_ref[...], 0.0), axis=-1, keepdims=True)
    prow = post_ref[pl.ds(e, 1), :]

    def chunk(c, carry):
        r0 = (c * ch).astype(F32)
        slot_g = r0 + lax.broadcasted_iota(jnp.int32, (ch, tb), 0).astype(F32)
        gather = jnp.where(prow == slot_g, 1.0, 0.0).astype(BF16)
        xg = _dot(gather, hn_ref[...]).astype(BF16)
        gu = _dot(xg, wgu_ref[0]) + bgu_ref[0]
        gate = jnp.minimum(gu[:, 0:D_FF], SWIGLU_LIMIT)
        up = jnp.clip(gu[:, D_FF:2 * D_FF], -SWIGLU_LIMIT, SWIGLU_LIMIT)
        act = (up + 1.0) * gate * _sigmoid(SWIGLU_ALPHA * gate)
        yc = _dot(act.astype(BF16), wd_ref[0]) + bd_ref[0]
        slot_s = r0 + lax.broadcasted_iota(jnp.int32, (tb, ch), 1).astype(F32)
        scatter = jnp.where(pcol == slot_s, ccol, 0.0).astype(BF16)
        y_ref[...] += _dot(scatter, yc.astype(BF16))
        return carry

    lax.fori_loop(0, (n_rows + ch - 1) // ch, chunk, 0)


def _moe(cnt, hn, h, comb, pos, post, wgu, wd, b_gate_up, b_down, tb):
    n_all = hn.shape[0]
    row = lambda w: pl.BlockSpec((tb, w), lambda i, e, cnt: (i, 0))
    return pl.pallas_call(
        _moe_kernel,
        grid_spec=pltpu.PrefetchScalarGridSpec(
            num_scalar_prefetch=1, grid=(n_all // tb, N_EXPERTS),
            in_specs=[row(1024), row(1024), row(LANES), row(LANES),
                      pl.BlockSpec((LANES, tb), lambda i, e, cnt: (0, i)),
                      pl.BlockSpec((1, D_MODEL, 2 * D_FF), lambda i, e, cnt: (e, 0, 0)),
                      pl.BlockSpec((1, D_FF, D_MODEL), lambda i, e, cnt: (e, 0, 0)),
                      pl.BlockSpec((1, 1, 2 * D_FF), lambda i, e, cnt: (e, 0, 0)),
                      pl.BlockSpec((1, 1, D_MODEL), lambda i, e, cnt: (e, 0, 0))],
            out_specs=row(1024)),
        out_shape=jax.ShapeDtypeStruct((n_all, 1024), F32),
        compiler_params=pltpu.CompilerParams(dimension_semantics=("arbitrary",) * 2, vmem_limit_bytes=VMEM_LIMIT),
        name="moe",
    )(cnt, hn, h, comb, pos, post, wgu, wd, b_gate_up, b_down)


def _pick_tile(n, pref):
    t = pref
    while n % t:
        t //= 2
    return t


def kernel(x_prompt, x_sample, cache_cmp, cache_slc, cache_win, state_ret, cache_mem, page_table, mem_prompt,
           g_attn, w_in, g_q_nsa, g_k_cmp, g_k_slc, g_k_win, pe_ck, w_ck1, w_ck2, pe_cv, w_cv1, w_cv2,
           g_ret_out, g_mem, w_mem_kv, g_q_mem, g_k_mem, w_br_nsa, w_br_ret, w_br_mem, w_out,
           g_ffn, w_router, b_router, w_gate_up, b_gate_up, w_down, b_down):
    nb, seq, _ = x_prompt.shape
    db, dseq, _ = x_sample.shape
    n_pages = page_table.shape[1]
    past = n_pages * PAGE_SIZE
    wb = cache_win.shape[1]
    mem_len = mem_prompt.shape[1]
    n_p, n_s = nb * seq, db * dseq
    n_all = n_p + n_s
    tm = _pick_tile(math.gcd(seq, n_s), 256)

    o = 0
    cols = {}
    for name, wdt in (("q", 512), ("kv", 768), ("ng", 24), ("rq", 256), ("rk", 256), ("rv", 512), ("rg", 512),
                      ("mq", 512), ("bg", 3072)):
        cols[name] = w_in[:, o:o + wdt]
        o += wdt
    wq = cols["q"].reshape(D_MODEL, NSA_HEADS, NSA_HD)
    zq = jnp.zeros_like(wq)
    wq_pad = jnp.concatenate(
        [jnp.concatenate([wq[:, hh], zq[:, hh]] if hh < NSA_GROUP else [zq[:, hh], wq[:, hh]], axis=-1)
         for hh in range(NSA_HEADS)], axis=-1)
    w_ng = jnp.pad(cols["ng"], ((0, 0), (0, C_NG - 24)))
    w1 = jnp.concatenate([wq_pad, cols["kv"], w_ng, cols["rq"], cols["rk"], cols["rv"], cols["rg"], cols["mq"]],
                         axis=-1).astype(BF16)
    w_bg = cols["bg"].astype(BF16)
    two = lambda g: jnp.concatenate([g, g]).reshape(1, LANES)
    r1 = lambda g: g.reshape(1, -1)

    half = RET_DK // 2
    inv = ROPE_BASE ** (-jnp.arange(half, dtype=F32) / half)
    pos = jnp.concatenate([jnp.arange(seq, dtype=jnp.int32),
                           jnp.tile(past + jnp.arange(dseq, dtype=jnp.int32), db)]).astype(F32)
    ang = pos[:, None] * inv[None, :]
    cos_t = jnp.tile(jnp.cos(ang), (1, 2 * RET_HEADS))
    sin_t = jnp.tile(jnp.concatenate([-jnp.sin(ang), jnp.sin(ang)], axis=-1), (1, RET_HEADS))

    xp = x_prompt.reshape(n_p, D_MODEL)
    xs = x_sample.reshape(n_s, D_MODEL)
    (q, kvc, kvs, kvw, ng, rq, rk, rv, rg, mq) = _project(
        xp, xs, r1(g_attn), w1, cos_t, sin_t, two(g_q_nsa), two(g_k_slc), two(g_k_win), r1(g_q_mem), seq, tm)

    w1s = jnp.stack([w_ck1, w_ck1, w_cv1, w_cv1]).reshape(4, 2, CMP_STRIDE, NSA_HD, CMP_HIDDEN)
    wbig = jnp.einsum("shpdc,st->psdhtc", w1s, jnp.eye(4, dtype=F32)).reshape(CHUNK_W, 2 * 4 * CMP_HIDDEN).astype(BF16)
    pes = jnp.stack([pe_ck, pe_ck, pe_cv, pe_cv]).reshape(4, 2, CMP_STRIDE, NSA_HD)
    pe2 = jnp.pad(jnp.transpose(pes, (1, 2, 0, 3)).reshape(2, CHUNK_W), ((0, 6), (0, 0)))
    zc = jnp.zeros_like(w_ck2)
    bd2 = lambda w: jnp.concatenate([jnp.concatenate([w, zc], 1), jnp.concatenate([zc, w], 1)], 0).astype(BF16)
    gk2 = two(g_k_cmp)

    pages_p = seq // PAGE_SIZE
    pool_p = (kvc if n_all % PAGE_SIZE == 0 else kvc[:n_p]).reshape(-1, CHUNKS_PER_PAGE, CHUNK_W)
    pt_p = jnp.arange(nb * pages_p, dtype=jnp.int32).reshape(nb, pages_p)
    ck_p, cv_p = _compress(pt_p, pool_p, wbig, pe2, bd2(w_ck2), bd2(w_cv2), gk2)
    pages_t = lambda c: jnp.transpose(c, (0, 2, 3, 4, 1)).reshape(-1, 256, PAGE_SIZE)
    ck_s, cv_s = _compress_t(page_table, pages_t(cache_cmp), wbig, pe2, bd2(w_ck2), bd2(w_cv2), gk2)

    n_cmp_p = seq // CMP_STRIDE - CMP_BLOCK // CMP_STRIDE + 1
    onsa = _nsa_prompt(q, kvs, kvw, ck_p, cv_p, ng, nb, seq, n_cmp_p)

    ns_s = (past + dseq + SLC_BLOCK - 1) // SLC_BLOCK
    nsp = (ns_s + 7) // 8 * 8
    kpad = past + PAGE_SIZE
    e_s = (jnp.arange(kpad)[None, :] // SLC_BLOCK == jnp.arange(nsp)[:, None]).astype(BF16)
    s3 = lambda a: a[n_p:].reshape(db, dseq, a.shape[1])
    onsa_s, new_win_s = _nsa_sample(page_table, s3(q), s3(kvs), s3(kvw), ck_s, cv_s, s3(ng),
                                    cache_win.reshape(db, wb, 256), e_s,
                                    pages_t(cache_slc), past)
    onsa_s = onsa_s.reshape(n_s, 1024).astype(BF16)

    gro = r1(g_ret_out)
    oret, ret_state_p = _retention(rq, rk, rv, rg, jnp.zeros((nb, RET_HEADS, RET_DK, RET_DV), F32), gro,
                                   nb, seq // RET_CHUNK, RET_CHUNK)
    padc = lambda a: jnp.pad(s3(a), ((0, 0), (0, RET_CHUNK - dseq), (0, 0))).reshape(db * RET_CHUNK, a.shape[1])
    oret_s, ret_state_s = _retention(padc(rq), padc(rk), padc(rv), padc(rg), state_ret, gro, db, 1, dseq)
    oret_s = oret_s.reshape(db, RET_CHUNK, 512)[:, :dseq].reshape(n_s, 512)

    mem_kv_p = _mem_kv(mem_prompt.reshape(nb * mem_len, D_MODEL), r1(g_mem), w_mem_kv.astype(BF16), r1(g_k_mem),
                       _pick_tile(nb * mem_len, 256))
    tq = _pick_tile(seq, 512)
    omem = _mem_attend(mq, mem_kv_p, nb, seq // tq, tq, mem_len, n_p, BF16)
    omem_s = _mem_attend(s3(mq).astype(F32).reshape(n_s, 512), cache_mem.reshape(db * mem_len, 1024), db, 1, dseq,
                         mem_len, n_s, F32).astype(BF16)

    wn = w_br_nsa.reshape(NSA_HEADS, NSA_HD, D_MODEL)
    zn = jnp.zeros_like(wn)
    wn_pad = jnp.concatenate(
        [jnp.concatenate([wn[hh], zn[hh]] if hh < NSA_GROUP else [zn[hh], wn[hh]], axis=0) for hh in range(NSA_HEADS)],
        axis=0).astype(BF16)
    wr_pad = jnp.pad(w_router, ((0, 0), (0, LANES - N_EXPERTS)))
    wr_hi = wr_pad.astype(BF16)
    wr_lo = (wr_pad - wr_hi.astype(F32)).astype(BF16)
    brt = jnp.concatenate([b_router, jnp.full((LANES - N_EXPERTS,), NEG, F32)]).reshape(1, LANES)
    h, hn, comb = _merge(xp, xs, onsa, onsa_s, oret, oret_s, omem, omem_s, r1(g_attn), w_bg, wn_pad, w_br_ret.astype(BF16),
                         w_br_mem.astype(BF16), w_out.astype(BF16), r1(g_ffn), jnp.stack([wr_hi, wr_lo]), brt, tm)

    tb = max(c for c in range(LANES, MOE_BLOCK_CAP + 1, LANES) if n_all % c == 0)
    pos, post, cnt = _route(comb, tb)
    cnt = cnt[:, 0, :N_EXPERTS].astype(jnp.int32).reshape(-1)
    y = _moe(cnt, hn, h, comb, pos, post, w_gate_up.astype(BF16), w_down.astype(BF16),
             b_gate_up.reshape(N_EXPERTS, 1, 2 * D_FF), b_down.reshape(N_EXPERTS, 1, D_MODEL), tb)

    kv5 = lambda a, bsz, t: a.reshape(bsz, t, 2, NSA_KV_HEADS, NSA_HD)
    wp = min(WINDOW, seq)
    return (y[:n_p].reshape(nb, seq, D_MODEL), y[n_p:].reshape(db, dseq, D_MODEL),
            kv5(kvc[:n_p], nb, seq), kv5(kvs[:n_p], nb, seq),
            kv5(kvw[:n_p].reshape(nb, seq, 256)[:, seq - wp:], nb, wp),
            ret_state_p, mem_kv_p.reshape(nb, mem_len, 2, MEM_HEADS, MEM_HD),
            kv5(kvc[n_p:], db, dseq), kv5(kvs[n_p:], db, dseq), kv5(new_win_s, db, wb), ret_state_s)
```

```python
import functools
import math

import jax
import jax.numpy as jnp
from jax import lax
from jax.experimental import pallas as pl
from jax.experimental.pallas import tpu as pltpu

F32 = jnp.float32
BF16 = jnp.bfloat16

D_MODEL = 1024
NSA_HEADS = 8
NSA_KV_HEADS = 2
NSA_GROUP = 4
NSA_HD = 64
CMP_BLOCK = 32
CMP_STRIDE = 16
CMP_HIDDEN = 128
SLC_BLOCK = 64
N_SELECT = 16
N_LOCAL_FORCED = 2
FORCE_BONUS = 1.0e4
WINDOW = 512
Q_BLOCK = 128
PAGE_SIZE = 128
RET_HEADS = 4
RET_DK = 64
RET_DV = 128
RET_CHUNK = 128
ROPE_BASE = 10000.0
MEM_HEADS = 4
MEM_HD = 128
N_EXPERTS = 32
TOP_K = 4
D_FF = 1024
SWIGLU_ALPHA = 1.702
SWIGLU_LIMIT = 7.0
EPS = 1e-6
NEG = -1e30

LANES = 128
VMEM_LIMIT = 56 * 1024 * 1024


def _dot(a, b):
    return jnp.dot(a, b, preferred_element_type=F32)


def _dot_nt(a, b):
    return lax.dot_general(a, b, (((1,), (1,)), ((), ())), preferred_element_type=F32)


def _sigmoid(x):
    return 1.0 / (1.0 + jnp.exp(-x))


def _split3(x):
    a = x.astype(BF16)
    r = x - a.astype(F32)
    b = r.astype(BF16)
    c = (r - b.astype(F32)).astype(BF16)
    return a, b, c


def _msoftmax(s, m):
    s = jnp.where(m, s, NEG)
    mx = jnp.max(s, axis=-1, keepdims=True)
    p = jnp.where(m, jnp.exp(s - mx), 0.0)
    return p / jnp.maximum(jnp.sum(p, axis=-1, keepdims=True), 1e-30)


def _halfnorm(t, g2):
    lane = lax.broadcasted_iota(jnp.int32, t.shape, 1)
    lo = lane < NSA_HD
    t2 = t * t
    s0 = jnp.sum(jnp.where(lo, t2, 0.0), axis=-1, keepdims=True)
    s1 = jnp.sum(jnp.where(lo, 0.0, t2), axis=-1, keepdims=True)
    r = jnp.where(lo, lax.rsqrt(s0 * (1.0 / NSA_HD) + EPS), lax.rsqrt(s1 * (1.0 / NSA_HD) + EPS))
    return t * r * g2


def _tilenorm(t, g, width):
    return t * lax.rsqrt(jnp.sum(t * t, axis=-1, keepdims=True) * (1.0 / width) + EPS) * g


def _rot(x, cos, sin):
    lane = lax.broadcasted_iota(jnp.int32, x.shape, 1)
    first = (lane % RET_DK) < (RET_DK // 2)
    n = x.shape[1]
    sw = jnp.where(first, pltpu.roll(x, n - RET_DK // 2, 1), pltpu.roll(x, RET_DK // 2, 1))
    return x * cos + sw * sin


C_Q, C_KV, C_NG, C_RET, C_MQ = 1024, 768, 128, 1536, 512
O_KV = C_Q
O_NG = O_KV + C_KV
O_RET = O_NG + C_NG
O_MQ = O_RET + C_RET
W1_COLS = O_MQ + C_MQ


def _proj_kernel(nbp, xp_ref, xs_ref, g_ref, w_ref, cos_ref, sin_ref, gq_ref, gks_ref, gkw_ref, gqm_ref,
                 q_ref, kvc_ref, kvs_ref, kvw_ref, ng_ref, rq_ref, rk_ref, rv_ref, rg_ref, mq_ref):
    i = pl.program_id(0)
    x = jnp.where(i < nbp, xp_ref[...], xs_ref[...])
    xn = (x * lax.rsqrt(jnp.mean(x * x, axis=-1, keepdims=True) + EPS) * g_ref[...]).astype(BF16)

    hq = _dot(xn, w_ref[:, 0:C_Q])
    for hh in range(NSA_HEADS):
        t = hq[:, LANES * hh:LANES * (hh + 1)]
        q_ref[:, LANES * hh:LANES * (hh + 1)] = (_tilenorm(t, gq_ref[...], NSA_HD) * NSA_HD ** -0.5).astype(BF16)

    hkv = _dot(xn, w_ref[:, O_KV:O_KV + C_KV])
    kvc_ref[...] = hkv[:, 0:256]
    kvs_ref[:, 0:128] = _halfnorm(hkv[:, 256:384], gks_ref[...])
    kvs_ref[:, 128:256] = hkv[:, 384:512]
    kvw_ref[:, 0:128] = _halfnorm(hkv[:, 512:640], gkw_ref[...])
    kvw_ref[:, 128:256] = hkv[:, 640:768]

    ng_ref[...] = _sigmoid(_dot(xn, w_ref[:, O_NG:O_NG + C_NG]))

    hr = _dot(xn, w_ref[:, O_RET:O_RET + C_RET])
    cos = cos_ref[...]
    sin = sin_ref[...]
    rq_ref[...] = _rot(hr[:, 0:256], cos, sin)
    rk_ref[...] = _rot(hr[:, 256:512], cos, sin) * RET_DK ** -0.5
    rv_ref[...] = hr[:, 512:1024]
    rg = hr[:, 1024:1536]
    rg_ref[...] = rg * _sigmoid(rg)

    hm = _dot(xn, w_ref[:, O_MQ:O_MQ + C_MQ])
    for hd in range(MEM_HEADS):
        t = hm[:, LANES * hd:LANES * (hd + 1)]
        mq_ref[:, LANES * hd:LANES * (hd + 1)] = (_tilenorm(t, gqm_ref[...], MEM_HD) * MEM_HD ** -0.5).astype(BF16)


def _project(xp, xs, g_attn, w1, cos_t, sin_t, gq2, gks2, gkw2, gqm, seq, tm):
    n_p, n_s = xp.shape[0], xs.shape[0]
    nbp, nbs = n_p // tm, n_s // tm
    n_all = n_p + n_s
    bps = seq // tm
    row = lambda w: pl.BlockSpec((tm, w), lambda i: (i, 0))
    const = lambda a: pl.BlockSpec(a.shape, lambda i: (0,) * a.ndim)
    rope = pl.BlockSpec((tm, 256), lambda i: (jnp.where(i < nbp, i % bps, bps + i - nbp), 0))
    widths = (1024, 256, 256, 256, 128, 256, 256, 512, 512, 512)
    dtypes = (BF16, F32, F32, F32, F32, F32, F32, F32, F32, BF16)
    return pl.pallas_call(
        functools.partial(_proj_kernel, nbp),
        grid=(nbp + nbs,),
        in_specs=[pl.BlockSpec((tm, D_MODEL), lambda i: (jnp.minimum(i, nbp - 1), 0)),
                  pl.BlockSpec((tm, D_MODEL), lambda i: (jnp.maximum(i - nbp, 0), 0)),
                  const(g_attn), const(w1), rope, rope, const(gq2), const(gks2), const(gkw2), const(gqm)],
        out_specs=[row(w) for w in widths],
        out_shape=[jax.ShapeDtypeStruct((n_all, w), d) for w, d in zip(widths, dtypes)],
        compiler_params=pltpu.CompilerParams(dimension_semantics=("arbitrary",), vmem_limit_bytes=VMEM_LIMIT),
        name="proj",
    )(xp, xs, g_attn, w1, cos_t, sin_t, gq2, gks2, gkw2, gqm)


CHUNK_W = CMP_STRIDE * 256
CHUNKS_PER_PAGE = PAGE_SIZE // CMP_STRIDE


def _compress_kernel(n_pages, pt_ref, pool_ref, wbig_ref, pe_ref, w2k_ref, w2v_ref, gk_ref,
                     ck_ref, cv_ref, x_scr, r_scr, sem):
    b = pl.program_id(0)
    n = n_pages * CHUNKS_PER_PAGE

    def page_copy(j):
        return pltpu.make_async_copy(pool_ref.at[pt_ref[b, j]],
                                     x_scr.at[pl.ds(CHUNKS_PER_PAGE * j, CHUNKS_PER_PAGE)], sem)

    for j in range(n_pages):
        page_copy(j).start()
    x_scr[pl.ds(n, 8), :] = pe_ref[...]
    for j in range(n_pages):
        page_copy(j).wait()

    r_scr[...] = _dot(x_scr[...].astype(BF16), wbig_ref[...])
    cvec = r_scr[n:n + 1, 0:512] + r_scr[n + 1:n + 2, 512:1024]
    hid = r_scr[0:n, 0:512] + r_scr[pl.ds(1, n), 512:1024] + cvec
    hb = (hid * _sigmoid(hid)).astype(BF16)
    ck_ref[0] = _halfnorm(_dot(hb[:, 0:256], w2k_ref[...]), gk_ref[...])
    cv_ref[0] = _dot(hb[:, 256:512], w2v_ref[...])


def _compress(page_table, pool, wbig, pe2, w2k, w2v, gk2):
    nb, n_pages = page_table.shape
    n = n_pages * CHUNKS_PER_PAGE
    const = lambda a: pl.BlockSpec(a.shape, lambda b, pt: (0,) * a.ndim)
    out = pl.BlockSpec((1, n, LANES), lambda b, pt: (b, 0, 0))
    return pl.pallas_call(
        functools.partial(_compress_kernel, n_pages),
        grid_spec=pltpu.PrefetchScalarGridSpec(
            num_scalar_prefetch=1, grid=(nb,),
            in_specs=[pl.BlockSpec(memory_space=pl.ANY), const(wbig), const(pe2), const(w2k), const(w2v), const(gk2)],
            out_specs=[out, out],
            scratch_shapes=[pltpu.VMEM((n + 8, CHUNK_W), F32), pltpu.VMEM((n + 8, 1024), F32),
                            pltpu.SemaphoreType.DMA(())]),
        out_shape=[jax.ShapeDtypeStruct((nb, n, LANES), F32)] * 2,
        compiler_params=pltpu.CompilerParams(dimension_semantics=("arbitrary",), vmem_limit_bytes=VMEM_LIMIT),
        name="compress",
    )(page_table, pool, wbig, pe2, w2k, w2v, gk2)


def _compress_t_kernel(n_pages, pt_ref, pool_ref, wbig_ref, pe_ref, w2k_ref, w2v_ref, gk_ref,
                       ck_ref, cv_ref, pg_scr, t_scr, r_scr, cvec_scr, sem):
    b = pl.program_id(0)
    slot = b % 2
    n = n_pages * CHUNKS_PER_PAGE

    def page_copy(bb, sl, j):
        return pltpu.make_async_copy(pool_ref.at[pt_ref[bb, j]], pg_scr.at[sl, j], sem.at[sl])

    @pl.when(b == 0)
    def _():
        for j in range(n_pages):
            page_copy(0, 0, j).start()
        r = _dot(pe_ref[...].astype(BF16), wbig_ref[...])
        cvec_scr[...] = jnp.broadcast_to(r[0:1, 0:512] + r[1:2, 512:1024], cvec_scr.shape)
        r_scr[pl.ds(n, 8), :] = jnp.zeros((8, 1024), F32)

    @pl.when(b + 1 < pl.num_programs(0))
    def _():
        for j in range(n_pages):
            page_copy(b + 1, 1 - slot, j).start()

    for j in range(n_pages):
        page_copy(b, slot, j).wait()

    src = lax.broadcasted_iota(jnp.int32, (PAGE_SIZE, PAGE_SIZE), 0)
    dst = lax.broadcasted_iota(jnp.int32, (PAGE_SIZE, PAGE_SIZE), 1)
    perm = (dst == CHUNKS_PER_PAGE * (src % CMP_STRIDE) + src // CMP_STRIDE).astype(BF16)
    for j in range(n_pages):
        xp = _dot(pg_scr[slot, j].astype(BF16), perm)
        tk = xp[0:LANES, :].T
        tv = xp[LANES:2 * LANES, :].T
        for p in range(CMP_STRIDE):
            rows = pl.ds(CHUNKS_PER_PAGE * j, CHUNKS_PER_PAGE)
            t_scr[p, rows, 0:LANES] = tk[CHUNKS_PER_PAGE * p:CHUNKS_PER_PAGE * (p + 1)]
            t_scr[p, rows, LANES:2 * LANES] = tv[CHUNKS_PER_PAGE * p:CHUNKS_PER_PAGE * (p + 1)]

    acc = jnp.zeros((n, 1024), F32)
    for p in range(CMP_STRIDE):
        acc = acc + _dot(t_scr[p].astype(BF16), wbig_ref[256 * p:256 * (p + 1), :])
    r_scr[pl.ds(0, n), :] = acc
    hid = acc[:, 0:512] + r_scr[pl.ds(1, n), 512:1024] + cvec_scr[0:1, :]
    hb = (hid * _sigmoid(hid)).astype(BF16)
    ck_ref[0] = _halfnorm(_dot(hb[:, 0:256], w2k_ref[...]), gk_ref[...])
    cv_ref[0] = _dot(hb[:, 256:512], w2v_ref[...])


def _compress_t(page_table, pool_t, wbig, pe2, w2k, w2v, gk2):
    nb, n_pages = page_table.shape
    n = n_pages * CHUNKS_PER_PAGE
    const = lambda a: pl.BlockSpec(a.shape, lambda b, pt: (0,) * a.ndim)
    out = pl.BlockSpec((1, n, LANES), lambda b, pt: (b, 0, 0))
    return pl.pallas_call(
        functools.partial(_compress_t_kernel, n_pages),
        grid_spec=pltpu.PrefetchScalarGridSpec(
            num_scalar_prefetch=1, grid=(nb,),
            in_specs=[pl.BlockSpec(memory_space=pl.ANY), const(wbig), const(pe2), const(w2k), const(w2v), const(gk2)],
            out_specs=[out, out],
            scratch_shapes=[pltpu.VMEM((2, n_pages, 256, PAGE_SIZE), F32), pltpu.VMEM((CMP_STRIDE, n, 2 * LANES), F32),
                            pltpu.VMEM((n + 8, 1024), F32), pltpu.VMEM((8, 512), F32),
                            pltpu.SemaphoreType.DMA((2,))]),
        out_shape=[jax.ShapeDtypeStruct((nb, n, LANES), F32)] * 2,
        compiler_params=pltpu.CompilerParams(dimension_semantics=("arbitrary",), vmem_limit_bytes=VMEM_LIMIT),
        name="compress_t",
    )(page_table, pool_t, wbig, pe2, w2k, w2v, gk2)


def _overlap_t(ns_rows, n_cmp_cols, n_cmp):
    s = lax.broadcasted_iota(jnp.int32, (ns_rows, n_cmp_cols), 0)
    n = lax.broadcasted_iota(jnp.int32, (ns_rows, n_cmp_cols), 1)
    ov = (n * CMP_STRIDE < s * SLC_BLOCK + SLC_BLOCK) & (n * CMP_STRIDE + CMP_BLOCK > s * SLC_BLOCK) & (n < n_cmp)
    return ov.astype(BF16)


def _importance_t(ov_t, pcs):
    a, b, c = _split3(pcs)
    return _dot_nt(ov_t, a) + _dot_nt(ov_t, b) + _dot_nt(ov_t, c)


def _block_scores_t(imp_t, tl):
    blk = lax.broadcasted_iota(jnp.int32, imp_t.shape, 0)
    cur = tl // SLC_BLOCK
    valid = blk <= cur
    forced = (blk == 0) | (valid & (blk > cur - N_LOCAL_FORCED))
    return jnp.where(valid, imp_t + jnp.where(forced, FORCE_BONUS, 0.0), NEG), blk


SLC_CHUNK = 512
RANK_ACCS = 4


def _nsa_prompt_kernel(seq, n_cmp, q_ref, kvs_ref, kvw_ref, ck_ref, cv_ref, ng_ref, et_ref, eg_ref, o_ref):
    i = pl.program_id(1)
    ns = seq // SLC_BLOCK
    nh = NSA_HEADS
    q8 = q_ref[...]
    qs = jnp.concatenate([q8[:, LANES * hh:LANES * (hh + 1)] for hh in range(nh)], axis=0)
    tq = i * Q_BLOCK + lax.broadcasted_iota(jnp.int32, (Q_BLOCK, 1), 0)
    rep = lambda a: jnp.concatenate([a] * nh, axis=0)
    t8 = rep(tq)

    ckb = ck_ref[0].astype(BF16)
    sc = _dot_nt(qs, ckb)
    jn = lax.broadcasted_iota(jnp.int32, sc.shape, 1)
    pc = _msoftmax(sc, (jn * CMP_STRIDE + CMP_BLOCK - 1) <= t8)
    o_cmp = _dot(pc.astype(BF16), cv_ref[0].astype(BF16))

    ov_t = _overlap_t(ns, pc.shape[1], n_cmp)
    tl = i * Q_BLOCK + lax.broadcasted_iota(jnp.int32, (ns, Q_BLOCK), 1)
    eye = (lax.broadcasted_iota(jnp.int32, (Q_BLOCK, Q_BLOCK), 0)
           == lax.broadcasted_iota(jnp.int32, (Q_BLOCK, Q_BLOCK), 1)).astype(BF16)
    pens = []
    for h in range(NSA_KV_HEADS):
        r0 = h * NSA_GROUP * Q_BLOCK
        pcs = pc[r0:r0 + 128] + pc[r0 + 128:r0 + 256] + pc[r0 + 256:r0 + 384] + pc[r0 + 384:r0 + 512]
        score, blk = _block_scores_t(_importance_t(ov_t, pcs), tl)
        ranks = [jnp.zeros(score.shape, F32) for _ in range(RANK_ACCS)]
        for k in range(ns):
            sk = score[k:k + 1, :]
            ranks[k % RANK_ACCS] += jnp.where((sk > score) | ((sk == score) & (blk > k)), 1.0, 0.0)
        rank = functools.reduce(lambda x, y: x + y, ranks)
        sel_t = jnp.where(rank < min(N_SELECT, ns), 1.0, 0.0).astype(BF16)
        if ns < LANES:
            sel_t = jnp.concatenate([sel_t, jnp.zeros((LANES - ns, Q_BLOCK), BF16)], axis=0)
        pen = ((_dot_nt(eye, sel_t) - 1.0) * -NEG).astype(BF16)
        pens += [pen] * NSA_GROUP

    qx = jnp.concatenate([qs, jnp.concatenate(pens, axis=0)], axis=1)
    kc = min(SLC_CHUNK, seq)

    def chunk(c, carry, bias):
        m, l, acc = carry
        k0 = pl.multiple_of(c * kc, kc)
        kx = jnp.concatenate([kvs_ref[pl.ds(k0, kc), 0:128].astype(BF16), et_ref[pl.ds(k0, kc), :]], axis=1)
        vv = kvs_ref[pl.ds(k0, kc), 128:256].astype(BF16)
        s = _dot_nt(qx, kx)
        if bias is not None:
            s = s + bias
        m_new = jnp.maximum(m, jnp.max(s, axis=-1, keepdims=True))
        alpha = jnp.exp(m - m_new)
        p = jnp.exp(s - m_new)
        l = alpha * l + jnp.sum(p, axis=-1, keepdims=True)
        acc = alpha * acc + _dot(p.astype(BF16), vv)
        return m_new, l, acc

    rows = nh * Q_BLOCK
    init = (jnp.full((rows, 1), NEG, F32), jnp.zeros((rows, 1), F32), jnp.zeros((rows, LANES), F32))
    c_last = (i * Q_BLOCK) // kc
    carry = lax.fori_loop(0, c_last, lambda c, cr: chunk(c, cr, None), init)
    kpos = c_last * kc + lax.broadcasted_iota(jnp.int32, (Q_BLOCK, kc), 1)
    _, l, acc = chunk(c_last, carry, rep(jnp.where(kpos <= tq, 0.0, NEG)))
    o_slc = acc / l

    wk = WINDOW + Q_BLOCK
    start = pl.multiple_of(Q_BLOCK * jnp.maximum(i - WINDOW // Q_BLOCK, 0), Q_BLOCK)
    kk = kvw_ref[pl.ds(start, wk), 0:128].astype(BF16)
    vx = jnp.concatenate([kvw_ref[pl.ds(start, wk), 128:256].astype(BF16), jnp.ones((wk, LANES), BF16)], axis=1)
    d = tq - (start + lax.broadcasted_iota(jnp.int32, (Q_BLOCK, wk), 1))
    sw = _dot_nt(qs, kk) + rep(jnp.where((d >= 0) & (d < WINDOW), 0.0, NEG))
    aw = _dot(jnp.exp(sw - jnp.max(sw, axis=-1, keepdims=True)).astype(BF16), vx)
    o_win = aw[:, 0:LANES] / aw[:, LANES:LANES + 1]

    a, b, c = _split3(ng_ref[...])
    for h in range(NSA_KV_HEADS):
        eg = eg_ref[h]
        gexp = _dot(a, eg) + _dot(b, eg) + _dot(c, eg)
        for g in range(NSA_GROUP):
            hh = NSA_GROUP * h + g
            rs = slice(Q_BLOCK * hh, Q_BLOCK * (hh + 1))
            gate = lambda br: gexp[:, LANES * (NSA_GROUP * br + g):LANES * (NSA_GROUP * br + g + 1)]
            o = gate(0) * o_cmp[rs] + gate(1) * o_slc[rs] + gate(2) * o_win[rs]
            o_ref[:, LANES * hh:LANES * (hh + 1)] = o.astype(BF16)


def _nsa_prompt(q, kvs, kvw, ck, cv, ng, nb, seq, n_cmp):
    nq = seq // Q_BLOCK
    ncp = ck.shape[1]
    ns = seq // SLC_BLOCK
    assert ns <= LANES and seq >= WINDOW + Q_BLOCK
    et = (jnp.arange(seq)[:, None] // SLC_BLOCK == jnp.arange(LANES)[None, :]).astype(BF16)
    n_g = 3 * NSA_GROUP
    col = jnp.arange(n_g * LANES)[None, None, :] // LANES
    src = (col // NSA_GROUP) * NSA_HEADS + NSA_GROUP * jnp.arange(NSA_KV_HEADS)[:, None, None] + col % NSA_GROUP
    eg = (jnp.arange(LANES)[None, :, None] == src).astype(BF16)
    return pl.pallas_call(
        functools.partial(_nsa_prompt_kernel, seq, n_cmp),
        grid=(nb, nq),
        in_specs=[pl.BlockSpec((Q_BLOCK, 1024), lambda b, i: (b * nq + i, 0)),
                  pl.BlockSpec((seq, 256), lambda b, i: (b, 0)),
                  pl.BlockSpec((seq, 256), lambda b, i: (b, 0)),
                  pl.BlockSpec((1, ncp, LANES), lambda b, i: (b, 0, 0)),
                  pl.BlockSpec((1, ncp, LANES), lambda b, i: (b, 0, 0)),
                  pl.BlockSpec((Q_BLOCK, LANES), lambda b, i: (b * nq + i, 0)),
                  pl.BlockSpec((seq, LANES), lambda b, i: (0, 0)),
                  pl.BlockSpec((NSA_KV_HEADS, LANES, n_g * LANES), lambda b, i: (0, 0, 0))],
        out_specs=pl.BlockSpec((Q_BLOCK, 1024), lambda b, i: (b * nq + i, 0)),
        out_shape=jax.ShapeDtypeStruct((nb * seq, 1024), BF16),
        compiler_params=pltpu.CompilerParams(dimension_semantics=("arbitrary",) * 2, vmem_limit_bytes=VMEM_LIMIT),
        name="nsa_prompt",
    )(q, kvs, kvw, ck, cv, ng, et, eg)


def _nsa_sample_kernel(past, n_pages, dseq, wb, pt_ref, q_ref, kvs_ref, kvw_ref, ck_ref, cv_ref, ng_ref, cwin_ref,
                       e_ref, pool_ref, o_ref, nwin_ref, kvt_scr, kw_scr, sc_scr, sem):
    b = pl.program_id(0)
    slot = b % 2

    def page_copy(bb, sl, j):
        return pltpu.make_async_copy(pool_ref.at[pt_ref[bb, j]], kvt_scr.at[sl, :, pl.ds(PAGE_SIZE * j, PAGE_SIZE)],
                                     sem.at[sl])

    @pl.when(b == 0)
    def _():
        for j in range(n_pages):
            page_copy(0, 0, j).start()

    @pl.when(b + 1 < pl.num_programs(0))
    def _():
        for j in range(n_pages):
            page_copy(b + 1, 1 - slot, j).start()

    n_heads = NSA_KV_HEADS * NSA_GROUP
    rows = n_heads * dseq
    kpad = past + PAGE_SIZE
    ns = (past + dseq + SLC_BLOCK - 1) // SLC_BLOCK
    nsp = e_ref.shape[0]
    qf = q_ref[0].astype(F32)
    qs = jnp.concatenate([qf[:, LANES * hh:LANES * (hh + 1)] for hh in range(n_heads)], axis=0).astype(BF16)
    t1 = past + lax.broadcasted_iota(jnp.int32, (dseq, 1), 0)
    t = jnp.concatenate([t1] * n_heads, axis=0)

    sc = _dot_nt(qs, ck_ref[0].astype(BF16))
    jn = lax.broadcasted_iota(jnp.int32, sc.shape, 1)
    pc = _msoftmax(sc, (jn * CMP_STRIDE + CMP_BLOCK - 1) <= t)
    o_cmp = _dot(pc.astype(BF16), cv_ref[0].astype(BF16))

    per_kv = []
    for kvh in range(NSA_KV_HEADS):
        base = kvh * NSA_GROUP * dseq
        s = pc[base:base + dseq]
        for g in range(1, NSA_GROUP):
            s = s + pc[base + g * dseq:base + (g + 1) * dseq]
        per_kv += [s] * NSA_GROUP
    pcs = jnp.concatenate(per_kv, axis=0)
    n_cmp = (ns * SLC_BLOCK) // CMP_STRIDE - CMP_BLOCK // CMP_STRIDE + 1
    imp_t = _importance_t(_overlap_t(nsp, pcs.shape[1], n_cmp), pcs)
    tl = past + lax.broadcasted_iota(jnp.int32, (nsp, rows), 1) % dseq
    score, blk = _block_scores_t(imp_t, tl)
    sc_scr[...] = score

    def rank_body(k, rank):
        sk = sc_scr[pl.ds(k, 1), :]
        return rank + jnp.where((sk > score) | ((sk == score) & (blk > k)), 1.0, 0.0)

    rank = lax.fori_loop(0, ns, rank_body, jnp.zeros(score.shape, F32))
    sel_t = jnp.where(rank < min(N_SELECT, ns), 1.0, 0.0).astype(BF16)
    eye = (lax.broadcasted_iota(jnp.int32, (rows, rows), 0)
           == lax.broadcasted_iota(jnp.int32, (rows, rows), 1)).astype(BF16)
    sel = _dot_nt(eye, sel_t)
    sele = _dot(sel.astype(BF16), e_ref[...])

    knew = jnp.concatenate([kvs_ref[0], jnp.zeros((PAGE_SIZE - dseq, 256), F32)], axis=0).astype(BF16)
    for j in range(n_pages):
        page_copy(b, slot, j).wait()
    s = jnp.concatenate([_dot(qs, kvt_scr[slot, 0:128, :].astype(BF16)), _dot_nt(qs, knew[:, 0:128])], axis=1)
    kpos = lax.broadcasted_iota(jnp.int32, (rows, kpad), 1)
    p = _msoftmax(s, (sele > 0.5) & (kpos <= t)).astype(BF16)
    o_slc = _dot_nt(p[:, 0:past], kvt_scr[slot, 128:256, :].astype(BF16)) + _dot(p[:, past:kpad], knew[:, 128:256])

    wpad = kw_scr.shape[0]
    kw_scr[pl.ds(0, wb), :] = cwin_ref[0]
    kw_scr[pl.ds(wb, dseq), :] = kvw_ref[0]
    kw_scr[pl.ds(wb + dseq, wpad - wb - dseq), :] = jnp.zeros((wpad - wb - dseq, 256), F32)
    sw = _dot_nt(qs, kw_scr[:, 0:128].astype(BF16))
    d = t - (past - wb + lax.broadcasted_iota(jnp.int32, (rows, wpad), 1))
    pw = _msoftmax(sw, (d >= 0) & (d < WINDOW))
    o_win = _dot(pw.astype(BF16), kw_scr[:, 128:256].astype(BF16))
    nwin_ref[0, pl.ds(0, wb - dseq), :] = cwin_ref[0, pl.ds(dseq, wb - dseq), :]
    nwin_ref[0, pl.ds(wb - dseq, dseq), :] = kvw_ref[0]

    ngv = ng_ref[0]
    for hh in range(n_heads):
        rs = slice(dseq * hh, dseq * (hh + 1))
        gate = lambda br: ngv[:, br * NSA_HEADS + hh:br * NSA_HEADS + hh + 1]
        o_ref[0, :, LANES * hh:LANES * (hh + 1)] = gate(0) * o_cmp[rs] + gate(1) * o_slc[rs] + gate(2) * o_win[rs]


def _nsa_sample(page_table, q, kvs, kvw, ck, cv, ng, cwin, e_s, pool, past):
    nb, n_pages = page_table.shape
    dseq = q.shape[1]
    wb = cwin.shape[1]
    kpad = past + PAGE_SIZE
    wpad = wb + PAGE_SIZE
    nsp = e_s.shape[0]
    rows = NSA_HEADS * dseq
    per_b = lambda a: pl.BlockSpec((1,) + a.shape[1:], lambda b, pt: (b,) + (0,) * (a.ndim - 1))
    return pl.pallas_call(
        functools.partial(_nsa_sample_kernel, past, n_pages, dseq, wb),
        grid_spec=pltpu.PrefetchScalarGridSpec(
            num_scalar_prefetch=1, grid=(nb,),
            in_specs=[per_b(q), per_b(kvs), per_b(kvw), per_b(ck), per_b(cv), per_b(ng), per_b(cwin),
                      pl.BlockSpec(e_s.shape, lambda b, pt: (0, 0)), pl.BlockSpec(memory_space=pl.ANY)],
            out_specs=[pl.BlockSpec((1, dseq, 1024), lambda b, pt: (b, 0, 0)),
                       pl.BlockSpec((1, wb, 256), lambda b, pt: (b, 0, 0))],
            scratch_shapes=[pltpu.VMEM((2, 256, past), F32), pltpu.VMEM((wpad, 256), F32),
                            pltpu.VMEM((nsp, rows), F32), pltpu.SemaphoreType.DMA((2,))]),
        out_shape=[jax.ShapeDtypeStruct((nb, dseq, 1024), F32), jax.ShapeDtypeStruct((nb, wb, 256), F32)],
        compiler_params=pltpu.CompilerParams(dimension_semantics=("arbitrary",), vmem_limit_bytes=VMEM_LIMIT),
        name="nsa_sample",
    )(page_table, q, kvs, kvw, ck, cv, ng, cwin, e_s, pool)


def _ret_kernel(c_true, rq_ref, rk_ref, rv_ref, rg_ref, s0_ref, gro_ref, o_ref, sout_ref, s_scr):
    c = pl.program_id(1)
    cs = RET_CHUNK

    @pl.when(c == 0)
    def _():
        for tile in range(RET_HEADS // 2):
            s_scr[tile] = jnp.concatenate([s0_ref[0, 2 * tile], s0_ref[0, 2 * tile + 1]], axis=0)

    diff = (lax.broadcasted_iota(jnp.int32, (cs, cs), 0) - lax.broadcasted_iota(jnp.int32, (cs, cs), 1)).astype(F32)
    ic = lax.broadcasted_iota(jnp.int32, (cs, 1), 0).astype(F32)
    lane = lax.broadcasted_iota(jnp.int32, (cs, LANES), 1)
    for hd in range(RET_HEADS):
        lg = math.log(1.0 - 2.0 ** (-5.0 - hd))
        tile, half = hd // 2, hd % 2
        in_half = (lane >= RET_DK * half) & (lane < RET_DK * (half + 1))
        qm = jnp.where(in_half, rq_ref[:, LANES * tile:LANES * (tile + 1)], 0.0).astype(BF16)
        k2 = rk_ref[:, LANES * tile:LANES * (tile + 1)]
        vb = rv_ref[:, LANES * hd:LANES * (hd + 1)].astype(BF16)
        dmat = jnp.where(diff >= 0, jnp.exp(jnp.maximum(diff, 0.0) * lg), 0.0)
        o = _dot((_dot_nt(qm, k2.astype(BF16)) * dmat).astype(BF16), vb)
        st = s_scr[tile]
        o = o + _dot(qm, st.astype(BF16)) * jnp.exp((ic + 1.0) * lg)
        kdec_t = (k2 * jnp.exp((c_true - 1.0 - ic) * lg)).T
        upd = _dot(kdec_t[RET_DK * half:RET_DK * (half + 1)].astype(BF16), vb)
        s_new = st[RET_DK * half:RET_DK * (half + 1)] * math.exp(c_true * lg) + upd
        s_scr[tile, pl.ds(RET_DK * half, RET_DK), :] = s_new
        on = _tilenorm(o, gro_ref[...], RET_DV)
        o_ref[:, LANES * hd:LANES * (hd + 1)] = (rg_ref[:, LANES * hd:LANES * (hd + 1)] * on).astype(BF16)

    @pl.when(c == pl.num_programs(1) - 1)
    def _():
        for hd in range(RET_HEADS):
            sout_ref[0, hd] = s_scr[hd // 2, pl.ds(RET_DK * (hd % 2), RET_DK), :]


def _retention(rq, rk, rv, rg, s0, gro, nb, n_chunks, c_true):
    cs = RET_CHUNK
    n_rows = nb * n_chunks * cs
    row = lambda w: pl.BlockSpec((cs, w), lambda b, c: (b * n_chunks + c, 0))
    return pl.pallas_call(
        functools.partial(_ret_kernel, float(c_true)),
        grid=(nb, n_chunks),
        in_specs=[row(256), row(256), row(512), row(512),
                  pl.BlockSpec((1, RET_HEADS, RET_DK, RET_DV), lambda b, c: (b, 0, 0, 0)),
                  pl.BlockSpec(gro.shape, lambda b, c: (0, 0))],
        out_specs=[row(512), pl.BlockSpec((1, RET_HEADS, RET_DK, RET_DV), lambda b, c: (b, 0, 0, 0))],
        out_shape=[jax.ShapeDtypeStruct((n_rows, 512), BF16),
                   jax.ShapeDtypeStruct((nb, RET_HEADS, RET_DK, RET_DV), F32)],
        scratch_shapes=[pltpu.VMEM((RET_HEADS // 2, LANES, RET_DV), F32)],
        compiler_params=pltpu.CompilerParams(dimension_semantics=("arbitrary", "arbitrary")),
        name="retention",
    )(rq, rk, rv, rg, s0, gro)


def _memkv_kernel(m_ref, g_ref, w_ref, gk_ref, o_ref):
    x = m_ref[...]
    xn = (x * lax.rsqrt(jnp.mean(x * x, axis=-1, keepdims=True) + EPS) * g_ref[...]).astype(BF16)
    hk = _dot(xn, w_ref[...])
    half = MEM_HEADS * MEM_HD
    for hd in range(MEM_HEADS):
        o_ref[:, LANES * hd:LANES * (hd + 1)] = _tilenorm(hk[:, LANES * hd:LANES * (hd + 1)], gk_ref[...], MEM_HD)
    o_ref[:, half:2 * half] = hk[:, half:2 * half]


def _mem_kv(mem2d, g_mem, w_mem, gk_mem, tm):
    n = mem2d.shape[0]
    const = lambda a: pl.BlockSpec(a.shape, lambda i: (0,) * a.ndim)
    return pl.pallas_call(
        _memkv_kernel,
        grid=(n // tm,),
        in_specs=[pl.BlockSpec((tm, D_MODEL), lambda i: (i, 0)), const(g_mem), const(w_mem), const(gk_mem)],
        out_specs=pl.BlockSpec((tm, 1024), lambda i: (i, 0)),
        out_shape=jax.ShapeDtypeStruct((n, 1024), F32),
        compiler_params=pltpu.CompilerParams(dimension_semantics=("arbitrary",)),
        name="mem_kv",
    )(mem2d, g_mem, w_mem, gk_mem)


def _memattn_kernel(q_ref, kv_ref, o_ref):
    half = MEM_HEADS * MEM_HD
    for hd in range(MEM_HEADS):
        cols = slice(LANES * hd, LANES * (hd + 1))
        s = _dot_nt(q_ref[:, cols].astype(BF16), kv_ref[:, cols].astype(BF16))
        p = jnp.exp(s - jnp.max(s, axis=-1, keepdims=True))
        p = p / jnp.sum(p, axis=-1, keepdims=True)
        o = _dot(p.astype(BF16), kv_ref[:, half + LANES * hd:half + LANES * (hd + 1)].astype(BF16))
        o_ref[:, cols] = o.astype(o_ref.dtype)


def _mem_attend(q2d, kv2d, nb, blocks_per_b, tq, mem_len, out_rows, out_dtype):
    return pl.pallas_call(
        _memattn_kernel,
        grid=(nb, blocks_per_b),
        in_specs=[pl.BlockSpec((tq, 512), lambda b, i: (b * blocks_per_b + i, 0)),
                  pl.BlockSpec((mem_len, 1024), lambda b, i: (b, 0))],
        out_specs=pl.BlockSpec((tq, 512), lambda b, i: (b * blocks_per_b + i, 0)),
        out_shape=jax.ShapeDtypeStruct((out_rows, 512), out_dtype),
        compiler_params=pltpu.CompilerParams(dimension_semantics=("arbitrary", "arbitrary")),
        name="mem_attend",
    )(q2d, kv2d)


def _merge_kernel(nbp, xp_ref, xs_ref, onsap_ref, onsas_ref, oretp_ref, orets_ref, omemp_ref, omems_ref, g_ref,
                  wbg_ref, wn_ref, wr_ref, wm_ref, wo_ref, gffn_ref, wrt_ref, brt_ref, h_ref, hn_ref, comb_ref):
    i = pl.program_id(0)
    pick = lambda p_ref, s_ref: jnp.where(i < nbp, p_ref[...], s_ref[...])
    x = pick(xp_ref, xs_ref)
    xn = (x * lax.rsqrt(jnp.mean(x * x, axis=-1, keepdims=True) + EPS) * g_ref[...]).astype(BF16)
    bg = _sigmoid(_dot(xn, wbg_ref[...]))
    mixed = (bg[:, 0:1024] * _dot(pick(onsap_ref, onsas_ref), wn_ref[...])
             + bg[:, 1024:2048] * _dot(pick(oretp_ref, orets_ref), wr_ref[...])
             + bg[:, 2048:3072] * _dot(pick(omemp_ref, omems_ref), wm_ref[...]))
    hres = x + _dot(mixed.astype(BF16), wo_ref[...])
    h_ref[...] = hres
    hn = hres * lax.rsqrt(jnp.mean(hres * hres, axis=-1, keepdims=True) + EPS) * gffn_ref[...]
    hn_ref[...] = hn.astype(BF16)

    a, b, _ = _split3(hn)
    logits = _dot(a, wrt_ref[0]) + _dot(a, wrt_ref[1]) + _dot(b, wrt_ref[0]) + brt_ref[...]
    lane = lax.broadcasted_iota(jnp.int32, logits.shape, 1).astype(F32)
    vals, hots = [], []
    for _k in range(TOP_K):
        mx = jnp.max(logits, axis=-1, keepdims=True)
        idx = jnp.min(jnp.where(logits == mx, lane, float(LANES)), axis=-1, keepdims=True)
        hot = lane == idx
        vals.append(mx)
        hots.append(hot)
        logits = jnp.where(hot, -3.0e38, logits)
    es = [jnp.exp(v - vals[0]) for v in vals]
    den = es[0] + es[1] + es[2] + es[3]
    comb = jnp.zeros(logits.shape, F32)
    for e_k, hot in zip(es, hots):
        comb = comb + jnp.where(hot, e_k / den, 0.0)
    comb_ref[...] = comb


def _merge(xp, xs, onsa, onsa_s, oret, oret_s, omem, omem_s, g_attn, wbg, wn, wr, wm, wo, gffn, wrt, brt, tm):
    n_p, n_s = xp.shape[0], xs.shape[0]
    nbp, nbs = n_p // tm, n_s // tm
    n_all = n_p + n_s
    row = lambda w: pl.BlockSpec((tm, w), lambda i: (i, 0))
    rowp = lambda w: pl.BlockSpec((tm, w), lambda i: (jnp.minimum(i, nbp - 1), 0))
    rows = lambda w: pl.BlockSpec((tm, w), lambda i: (jnp.maximum(i - nbp, 0), 0))
    const = lambda a: pl.BlockSpec(a.shape, lambda i: (0,) * a.ndim)
    return pl.pallas_call(
        functools.partial(_merge_kernel, nbp),
        grid=(nbp + nbs,),
        in_specs=[rowp(1024), rows(1024), rowp(1024), rows(1024), rowp(512), rows(512), rowp(512), rows(512),
                  const(g_attn), const(wbg), const(wn), const(wr), const(wm),
                  const(wo), const(gffn), const(wrt), const(brt)],
        out_specs=[row(1024), row(1024), row(LANES)],
        out_shape=[jax.ShapeDtypeStruct((n_all, 1024), F32), jax.ShapeDtypeStruct((n_all, 1024), BF16),
                   jax.ShapeDtypeStruct((n_all, LANES), F32)],
        compiler_params=pltpu.CompilerParams(dimension_semantics=("arbitrary",), vmem_limit_bytes=VMEM_LIMIT),
        name="merge",
    )(xp, xs, onsa, onsa_s, oret, oret_s, omem, omem_s, g_attn, wbg, wn, wr, wm, wo, gffn, wrt, brt)


MOE_BLOCK_CAP = 1664
MOE_CHUNK = 256
ROUTE_TILE = 128


def _route_kernel(comb_ref, pos_ref, post_ref, cnt_ref):
    tb = comb_ref.shape[0]
    rt = ROUTE_TILE
    r = lax.broadcasted_iota(jnp.int32, (rt, rt), 0)
    c = lax.broadcasted_iota(jnp.int32, (rt, rt), 1)
    ltri = (c < r).astype(BF16)
    eye = (c == r).astype(BF16)
    carry = jnp.zeros((1, LANES), F32)
    for s in range(tb // rt):
        sel = jnp.where(comb_ref[pl.ds(rt * s, rt), :] > 0.0, 1.0, 0.0)
        pos = jnp.where(sel > 0.0, _dot(ltri, sel.astype(BF16)) + carry, -1.0)
        carry = carry + jnp.sum(sel, axis=0, keepdims=True)
        pos_ref[pl.ds(rt * s, rt), :] = pos
        hi = jnp.floor((pos + 1.0) * (1.0 / 256.0))
        lo = (pos + 1.0) - 256.0 * hi
        post_ref[:, rt * s:rt * (s + 1)] = 256.0 * _dot_nt(eye, hi.astype(BF16)) + _dot_nt(eye, lo.astype(BF16)) - 1.0
    cnt_ref[0] = jnp.broadcast_to(carry, (8, LANES))


def _route(comb, tb):
    n_all = comb.shape[0]
    nblk = n_all // tb
    return pl.pallas_call(
        _route_kernel,
        grid=(nblk,),
        in_specs=[pl.BlockSpec((tb, LANES), lambda i: (i, 0))],
        out_specs=[pl.BlockSpec((tb, LANES), lambda i: (i, 0)), pl.BlockSpec((LANES, tb), lambda i: (0, i)),
                   pl.BlockSpec((1, 8, LANES), lambda i: (i, 0, 0))],
        out_shape=[jax.ShapeDtypeStruct((n_all, LANES), F32), jax.ShapeDtypeStruct((LANES, n_all), F32),
                   jax.ShapeDtypeStruct((nblk, 8, LANES), F32)],
        compiler_params=pltpu.CompilerParams(dimension_semantics=("arbitrary",)),
        name="route",
    )(comb)


def _moe_kernel(cnt_ref, hn_ref, h_ref, comb_ref, pos_ref, post_ref, wgu_ref, wd_ref, bgu_ref, bd_ref, y_ref):
    i = pl.program_id(0)
    e = pl.program_id(1)
    tb = hn_ref.shape[0]
    ch = MOE_CHUNK

    @pl.when(e == 0)
    def _():
        y_ref[...] = h_ref[...]

    n_rows = cnt_ref[i * N_EXPERTS + e]
    lane = lax.broadcasted_iota(jnp.int32, (tb, LANES), 1)
    pcol = jnp.sum(jnp.where(lane == e, pos_ref[...], 0.0), axis=-1, keepdims=True)
    ccol = jnp.sum(jnp.where(lane == e, comb_ref[...], 0.0), axis=-1, keepdims=True)
    prow = post_ref[pl.ds(e, 1), :]

    def chunk(c, carry):
        r0 = (c * ch).astype(F32)
        slot_g = r0 + lax.broadcasted_iota(jnp.int32, (ch, tb), 0).astype(F32)
        gather = jnp.where(prow == slot_g, 1.0, 0.0).astype(BF16)
        xg = _dot(gather, hn_ref[...]).astype(BF16)
        gu = _dot(xg, wgu_ref[0]) + bgu_ref[0]
        gate = jnp.minimum(gu[:, 0:D_FF], SWIGLU_LIMIT)
        up = jnp.clip(gu[:, D_FF:2 * D_FF], -SWIGLU_LIMIT, SWIGLU_LIMIT)
        act = (up + 1.0) * gate * _sigmoid(SWIGLU_ALPHA * gate)
        yc = _dot(act.astype(BF16), wd_ref[0]) + bd_ref[0]
        slot_s = r0 + lax.broadcasted_iota(jnp.int32, (tb, ch), 1).astype(F32)
        scatter = jnp.where(pcol == slot_s, ccol, 0.0).astype(BF16)
        y_ref[...] += _dot(scatter, yc.astype(BF16))
        return carry

    lax.fori_loop(0, (n_rows + ch - 1) // ch, chunk, 0)


def _moe(cnt, hn, h, comb, pos, post, wgu, wd, b_gate_up, b_down, tb):
    n_all = hn.shape[0]
    row = lambda w: pl.BlockSpec((tb, w), lambda i, e, cnt: (i, 0))
    return pl.pallas_call(
        _moe_kernel,
        grid_spec=pltpu.PrefetchScalarGridSpec(
            num_scalar_prefetch=1, grid=(n_all // tb, N_EXPERTS),
            in_specs=[row(1024), row(1024), row(LANES), row(LANES),
                      pl.BlockSpec((LANES, tb), lambda i, e, cnt: (0, i)),
                      pl.BlockSpec((1, D_MODEL, 2 * D_FF), lambda i, e, cnt: (e, 0, 0)),
                      pl.BlockSpec((1, D_FF, D_MODEL), lambda i, e, cnt: (e, 0, 0)),
                      pl.BlockSpec((1, 1, 2 * D_FF), lambda i, e, cnt: (e, 0, 0)),
                      pl.BlockSpec((1, 1, D_MODEL), lambda i, e, cnt: (e, 0, 0))],
            out_specs=row(1024)),
        out_shape=jax.ShapeDtypeStruct((n_all, 1024), F32),
        compiler_params=pltpu.CompilerParams(dimension_semantics=("arbitrary",) * 2, vmem_limit_bytes=VMEM_LIMIT),
        name="moe",
    )(cnt, hn, h, comb, pos, post, wgu, wd, b_gate_up, b_down)


def _pick_tile(n, pref):
    t = pref
    while n % t:
        t //= 2
    return t


def kernel(x_prompt, x_sample, cache_cmp, cache_slc, cache_win, state_ret, cache_mem, page_table, mem_prompt,
           g_attn, w_in, g_q_nsa, g_k_cmp, g_k_slc, g_k_win, pe_ck, w_ck1, w_ck2, pe_cv, w_cv1, w_cv2,
           g_ret_out, g_mem, w_mem_kv, g_q_mem, g_k_mem, w_br_nsa, w_br_ret, w_br_mem, w_out,
           g_ffn, w_router, b_router, w_gate_up, b_gate_up, w_down, b_down):
    nb, seq, _ = x_prompt.shape
    db, dseq, _ = x_sample.shape
    n_pages = page_table.shape[1]
    past = n_pages * PAGE_SIZE
    wb = cache_win.shape[1]
    mem_len = mem_prompt.shape[1]
    n_p, n_s = nb * seq, db * dseq
    n_all = n_p + n_s
    tm = _pick_tile(math.gcd(seq, n_s), 256)

    o = 0
    cols = {}
    for name, wdt in (("q", 512), ("kv", 768), ("ng", 24), ("rq", 256), ("rk", 256), ("rv", 512), ("rg", 512),
                      ("mq", 512), ("bg", 3072)):
        cols[name] = w_in[:, o:o + wdt]
        o += wdt
    wq = cols["q"].reshape(D_MODEL, NSA_HEADS, NSA_HD)
    zq = jnp.zeros_like(wq)
    wq_pad = jnp.concatenate(
        [jnp.concatenate([wq[:, hh], zq[:, hh]] if hh < NSA_GROUP else [zq[:, hh], wq[:, hh]], axis=-1)
         for hh in range(NSA_HEADS)], axis=-1)
    w_ng = jnp.pad(cols["ng"], ((0, 0), (0, C_NG - 24)))
    w1 = jnp.concatenate([wq_pad, cols["kv"], w_ng, cols["rq"], cols["rk"], cols["rv"], cols["rg"], cols["mq"]],
                         axis=-1).astype(BF16)
    w_bg = cols["bg"].astype(BF16)
    two = lambda g: jnp.concatenate([g, g]).reshape(1, LANES)
    r1 = lambda g: g.reshape(1, -1)

    half = RET_DK // 2
    inv = ROPE_BASE ** (-jnp.arange(half, dtype=F32) / half)
    pos = jnp.concatenate([jnp.arange(seq, dtype=jnp.int32),
                           jnp.tile(past + jnp.arange(dseq, dtype=jnp.int32), db)]).astype(F32)
    ang = pos[:, None] * inv[None, :]
    cos_t = jnp.tile(jnp.cos(ang), (1, 2 * RET_HEADS))
    sin_t = jnp.tile(jnp.concatenate([-jnp.sin(ang), jnp.sin(ang)], axis=-1), (1, RET_HEADS))

    xp = x_prompt.reshape(n_p, D_MODEL)
    xs = x_sample.reshape(n_s, D_MODEL)
    (q, kvc, kvs, kvw, ng, rq, rk, rv, rg, mq) = _project(
        xp, xs, r1(g_attn), w1, cos_t, sin_t, two(g_q_nsa), two(g_k_slc), two(g_k_win), r1(g_q_mem), seq, tm)

    w1s = jnp.stack([w_ck1, w_ck1, w_cv1, w_cv1]).reshape(4, 2, CMP_STRIDE, NSA_HD, CMP_HIDDEN)
    wbig = jnp.einsum("shpdc,st->psdhtc", w1s, jnp.eye(4, dtype=F32)).reshape(CHUNK_W, 2 * 4 * CMP_HIDDEN).astype(BF16)
    pes = jnp.stack([pe_ck, pe_ck, pe_cv, pe_cv]).reshape(4, 2, CMP_STRIDE, NSA_HD)
    pe2 = jnp.pad(jnp.transpose(pes, (1, 2, 0, 3)).reshape(2, CHUNK_W), ((0, 6), (0, 0)))
    zc = jnp.zeros_like(w_ck2)
    bd2 = lambda w: jnp.concatenate([jnp.concatenate([w, zc], 1), jnp.concatenate([zc, w], 1)], 0).astype(BF16)
    gk2 = two(g_k_cmp)

    pages_p = seq // PAGE_SIZE
    pool_p = (kvc if n_all % PAGE_SIZE == 0 else kvc[:n_p]).reshape(-1, CHUNKS_PER_PAGE, CHUNK_W)
    pt_p = jnp.arange(nb * pages_p, dtype=jnp.int32).reshape(nb, pages_p)
    ck_p, cv_p = _compress(pt_p, pool_p, wbig, pe2, bd2(w_ck2), bd2(w_cv2), gk2)
    pages_t = lambda c: jnp.transpose(c, (0, 2, 3, 4, 1)).reshape(-1, 256, PAGE_SIZE)
    ck_s, cv_s = _compress_t(page_table, pages_t(cache_cmp), wbig, pe2, bd2(w_ck2), bd2(w_cv2), gk2)

    n_cmp_p = seq // CMP_STRIDE - CMP_BLOCK // CMP_STRIDE + 1
    onsa = _nsa_prompt(q, kvs, kvw, ck_p, cv_p, ng, nb, seq, n_cmp_p)

    ns_s = (past + dseq + SLC_BLOCK - 1) // SLC_BLOCK
    nsp = (ns_s + 7) // 8 * 8
    kpad = past + PAGE_SIZE
    e_s = (jnp.arange(kpad)[None, :] // SLC_BLOCK == jnp.arange(nsp)[:, None]).astype(BF16)
    s3 = lambda a: a[n_p:].reshape(db, dseq, a.shape[1])
    onsa_s, new_win_s = _nsa_sample(page_table, s3(q), s3(kvs), s3(kvw), ck_s, cv_s, s3(ng),
                                    cache_win.reshape(db, wb, 256), e_s,
                                    pages_t(cache_slc), past)
    onsa_s = onsa_s.reshape(n_s, 1024).astype(BF16)

    gro = r1(g_ret_out)
    oret, ret_state_p = _retention(rq, rk, rv, rg, jnp.zeros((nb, RET_HEADS, RET_DK, RET_DV), F32), gro,
                                   nb, seq // RET_CHUNK, RET_CHUNK)
    padc = lambda a: jnp.pad(s3(a), ((0, 0), (0, RET_CHUNK - dseq), (0, 0))).reshape(db * RET_CHUNK, a.shape[1])
    oret_s, ret_state_s = _retention(padc(rq), padc(rk), padc(rv), padc(rg), state_ret, gro, db, 1, dseq)
    oret_s = oret_s.reshape(db, RET_CHUNK, 512)[:, :dseq].reshape(n_s, 512)

    mem_kv_p = _mem_kv(mem_prompt.reshape(nb * mem_len, D_MODEL), r1(g_mem), w_mem_kv.astype(BF16), r1(g_k_mem),
                       _pick_tile(nb * mem_len, 256))
    tq = _pick_tile(seq, 512)
    omem = _mem_attend(mq, mem_kv_p, nb, seq // tq, tq, mem_len, n_p, BF16)
    omem_s = _mem_attend(s3(mq).astype(F32).reshape(n_s, 512), cache_mem.reshape(db * mem_len, 1024), db, 1, dseq,
                         mem_len, n_s, F32).astype(BF16)

    wn = w_br_nsa.reshape(NSA_HEADS, NSA_HD, D_MODEL)
    zn = jnp.zeros_like(wn)
    wn_pad = jnp.concatenate(
        [jnp.concatenate([wn[hh], zn[hh]] if hh < NSA_GROUP else [zn[hh], wn[hh]], axis=0) for hh in range(NSA_HEADS)],
        axis=0).astype(BF16)
    wr_pad = jnp.pad(w_router, ((0, 0), (0, LANES - N_EXPERTS)))
    wr_hi = wr_pad.astype(BF16)
    wr_lo = (wr_pad - wr_hi.astype(F32)).astype(BF16)
    brt = jnp.concatenate([b_router, jnp.full((LANES - N_EXPERTS,), NEG, F32)]).reshape(1, LANES)
    h, hn, comb = _merge(xp, xs, onsa, onsa_s, oret, oret_s, omem, omem_s, r1(g_attn), w_bg, wn_pad, w_br_ret.astype(BF16),
                         w_br_mem.astype(BF16), w_out.astype(BF16), r1(g_ffn), jnp.stack([wr_hi, wr_lo]), brt, tm)

    tb = max(c for c in range(LANES, MOE_BLOCK_CAP + 1, LANES) if n_all % c == 0)
    pos, post, cnt = _route(comb, tb)
    cnt = cnt[:, 0, :N_EXPERTS].astype(jnp.int32).reshape(-1)
    y = _moe(cnt, hn, h, comb, pos, post, w_gate_up.astype(BF16), w_down.astype(BF16),
             b_gate_up.reshape(N_EXPERTS, 1, 2 * D_FF), b_down.reshape(N_EXPERTS, 1, D_MODEL), tb)

    kv5 = lambda a, bsz, t: a.reshape(bsz, t, 2, NSA_KV_HEADS, NSA_HD)
    wp = min(WINDOW, seq)
    return (y[:n_p].reshape(nb, seq, D_MODEL), y[n_p:].reshape(db, dseq, D_MODEL),
            kv5(kvc[:n_p], nb, seq), kv5(kvs[:n_p], nb, seq),
            kv5(kvw[:n_p].reshape(nb, seq, 256)[:, seq - wp:], nb, wp),
            ret_state_p, mem_kv_p.reshape(nb, mem_len, 2, MEM_HEADS, MEM_HD),
            kv5(kvc[n_p:], db, dseq), kv5(kvs[n_p:], db, dseq), kv5(new_win_s, db, wb), ret_state_s)
```

```python
import functools
import math

import jax
import jax.numpy as jnp
from jax import lax
from jax.experimental import pallas as pl
from jax.experimental.pallas import tpu as pltpu

F32 = jnp.float32
BF16 = jnp.bfloat16

D_MODEL = 1024
NSA_HEADS = 8
NSA_KV_HEADS = 2
NSA_GROUP = 4
NSA_HD = 64
CMP_BLOCK = 32
CMP_STRIDE = 16
CMP_HIDDEN = 128
SLC_BLOCK = 64
N_SELECT = 16
N_LOCAL_FORCED = 2
FORCE_BONUS = 1.0e4
WINDOW = 512
Q_BLOCK = 128
PAGE_SIZE = 128
RET_HEADS = 4
RET_DK = 64
RET_DV = 128
RET_CHUNK = 128
ROPE_BASE = 10000.0
MEM_HEADS = 4
MEM_HD = 128
N_EXPERTS = 32
TOP_K = 4
D_FF = 1024
SWIGLU_ALPHA = 1.702
SWIGLU_LIMIT = 7.0
EPS = 1e-6
NEG = -1e30

LANES = 128
VMEM_LIMIT = 56 * 1024 * 1024


def _dot(a, b):
    return jnp.dot(a, b, preferred_element_type=F32)


def _dot_nt(a, b):
    return lax.dot_general(a, b, (((1,), (1,)), ((), ())), preferred_element_type=F32)


def _sigmoid(x):
    return 1.0 / (1.0 + jnp.exp(-x))


def _split3(x):
    a = x.astype(BF16)
    r = x - a.astype(F32)
    b = r.astype(BF16)
    c = (r - b.astype(F32)).astype(BF16)
    return a, b, c


def _msoftmax(s, m):
    s = jnp.where(m, s, NEG)
    mx = jnp.max(s, axis=-1, keepdims=True)
    p = jnp.where(m, jnp.exp(s - mx), 0.0)
    return p / jnp.maximum(jnp.sum(p, axis=-1, keepdims=True), 1e-30)


def _halfnorm(t, g2):
    lane = lax.broadcasted_iota(jnp.int32, t.shape, 1)
    lo = lane < NSA_HD
    t2 = t * t
    s0 = jnp.sum(jnp.where(lo, t2, 0.0), axis=-1, keepdims=True)
    s1 = jnp.sum(jnp.where(lo, 0.0, t2), axis=-1, keepdims=True)
    r = jnp.where(lo, lax.rsqrt(s0 * (1.0 / NSA_HD) + EPS), lax.rsqrt(s1 * (1.0 / NSA_HD) + EPS))
    return t * r * g2


def _tilenorm(t, g, width):
    return t * lax.rsqrt(jnp.sum(t * t, axis=-1, keepdims=True) * (1.0 / width) + EPS) * g


def _rot(x, cos, sin):
    lane = lax.broadcasted_iota(jnp.int32, x.shape, 1)
    first = (lane % RET_DK) < (RET_DK // 2)
    n = x.shape[1]
    sw = jnp.where(first, pltpu.roll(x, n - RET_DK // 2, 1), pltpu.roll(x, RET_DK // 2, 1))
    return x * cos + sw * sin


C_Q, C_KV, C_NG, C_RET, C_MQ = 1024, 768, 128, 1536, 512
O_KV = C_Q
O_NG = O_KV + C_KV
O_RET = O_NG + C_NG
O_MQ = O_RET + C_RET
W1_COLS = O_MQ + C_MQ


def _proj_kernel(nbp, xp_ref, xs_ref, g_ref, w_ref, cos_ref, sin_ref, gq_ref, gks_ref, gkw_ref, gqm_ref,
                 q_ref, kvc_ref, kvs_ref, kvw_ref, ng_ref, rq_ref, rk_ref, rv_ref, rg_ref, mq_ref,
                 kvct_ref, kvst_ref, kvwt_ref):
    i = pl.program_id(0)
    x = jnp.where(i < nbp, xp_ref[...], xs_ref[...])
    xn = (x * lax.rsqrt(jnp.mean(x * x, axis=-1, keepdims=True) + EPS) * g_ref[...]).astype(BF16)

    hq = _dot(xn, w_ref[:, 0:C_Q])
    for hh in range(NSA_HEADS):
        t = hq[:, LANES * hh:LANES * (hh + 1)]
        q_ref[:, LANES * hh:LANES * (hh + 1)] = (_tilenorm(t, gq_ref[...], NSA_HD) * NSA_HD ** -0.5).astype(BF16)

    hkv = _dot(xn, w_ref[:, O_KV:O_KV + C_KV])
    kvc = hkv[:, 0:256]
    kvs = jnp.concatenate([_halfnorm(hkv[:, 256:384], gks_ref[...]), hkv[:, 384:512]], axis=1)
    kvw = jnp.concatenate([_halfnorm(hkv[:, 512:640], gkw_ref[...]), hkv[:, 640:768]], axis=1)
    kvc_ref[...] = kvc
    kvs_ref[...] = kvs
    kvw_ref[...] = kvw

    @pl.when(i < nbp)
    def _():
        kvct_ref[0] = kvc.T
        kvst_ref[0] = kvs.T
        kvwt_ref[0] = kvw.T

    ng_ref[...] = _sigmoid(_dot(xn, w_ref[:, O_NG:O_NG + C_NG]))

    hr = _dot(xn, w_ref[:, O_RET:O_RET + C_RET])
    cos = cos_ref[...]
    sin = sin_ref[...]
    rq_ref[...] = _rot(hr[:, 0:256], cos, sin)
    rk_ref[...] = _rot(hr[:, 256:512], cos, sin) * RET_DK ** -0.5
    rv_ref[...] = hr[:, 512:1024]
    rg = hr[:, 1024:1536]
    rg_ref[...] = rg * _sigmoid(rg)

    hm = _dot(xn, w_ref[:, O_MQ:O_MQ + C_MQ])
    for hd in range(MEM_HEADS):
        t = hm[:, LANES * hd:LANES * (hd + 1)]
        mq_ref[:, LANES * hd:LANES * (hd + 1)] = (_tilenorm(t, gqm_ref[...], MEM_HD) * MEM_HD ** -0.5).astype(BF16)


def _project(xp, xs, g_attn, w1, cos_t, sin_t, gq2, gks2, gkw2, gqm, seq, tm):
    n_p, n_s = xp.shape[0], xs.shape[0]
    nbp, nbs = n_p // tm, n_s // tm
    n_all = n_p + n_s
    bps = seq // tm
    row = lambda w: pl.BlockSpec((tm, w), lambda i: (i, 0))
    const = lambda a: pl.BlockSpec(a.shape, lambda i: (0,) * a.ndim)
    rope = pl.BlockSpec((tm, 256), lambda i: (jnp.where(i < nbp, i % bps, bps + i - nbp), 0))
    widths = (1024, 256, 256, 256, 128, 256, 256, 512, 512, 512)
    dtypes = (BF16, F32, F32, F32, F32, F32, F32, F32, F32, BF16)
    kvt = pl.BlockSpec((1, 256, tm), lambda i: (jnp.minimum(i, nbp - 1) // bps, 0, jnp.minimum(i, nbp - 1) % bps))
    return pl.pallas_call(
        functools.partial(_proj_kernel, nbp),
        grid=(nbp + nbs,),
        in_specs=[pl.BlockSpec((tm, D_MODEL), lambda i: (jnp.minimum(i, nbp - 1), 0)),
                  pl.BlockSpec((tm, D_MODEL), lambda i: (jnp.maximum(i - nbp, 0), 0)),
                  const(g_attn), const(w1), rope, rope, const(gq2), const(gks2), const(gkw2), const(gqm)],
        out_specs=[row(w) for w in widths] + [kvt] * 3,
        out_shape=[jax.ShapeDtypeStruct((n_all, w), d) for w, d in zip(widths, dtypes)]
        + [jax.ShapeDtypeStruct((n_p // seq, 256, seq), F32)] * 3,
        compiler_params=pltpu.CompilerParams(dimension_semantics=("arbitrary",), vmem_limit_bytes=VMEM_LIMIT),
        name="proj",
    )(xp, xs, g_attn, w1, cos_t, sin_t, gq2, gks2, gkw2, gqm)


CHUNK_W = CMP_STRIDE * 256
CHUNKS_PER_PAGE = PAGE_SIZE // CMP_STRIDE


def _compress_kernel(n_pages, pt_ref, pool_ref, wbig_ref, pe_ref, w2k_ref, w2v_ref, gk_ref,
                     ck_ref, cv_ref, x_scr, r_scr, sem):
    b = pl.program_id(0)
    n = n_pages * CHUNKS_PER_PAGE

    def page_copy(j):
        return pltpu.make_async_copy(pool_ref.at[pt_ref[b, j]],
                                     x_scr.at[pl.ds(CHUNKS_PER_PAGE * j, CHUNKS_PER_PAGE)], sem)

    for j in range(n_pages):
        page_copy(j).start()
    x_scr[pl.ds(n, 8), :] = pe_ref[...]
    for j in range(n_pages):
        page_copy(j).wait()

    r_scr[...] = _dot(x_scr[...].astype(BF16), wbig_ref[...])
    cvec = r_scr[n:n + 1, 0:512] + r_scr[n + 1:n + 2, 512:1024]
    hid = r_scr[0:n, 0:512] + r_scr[pl.ds(1, n), 512:1024] + cvec
    hb = (hid * _sigmoid(hid)).astype(BF16)
    ck_ref[0] = _halfnorm(_dot(hb[:, 0:256], w2k_ref[...]), gk_ref[...])
    cv_ref[0] = _dot(hb[:, 256:512], w2v_ref[...])


def _compress(page_table, pool, wbig, pe2, w2k, w2v, gk2):
    nb, n_pages = page_table.shape
    n = n_pages * CHUNKS_PER_PAGE
    const = lambda a: pl.BlockSpec(a.shape, lambda b, pt: (0,) * a.ndim)
    out = pl.BlockSpec((1, n, LANES), lambda b, pt: (b, 0, 0))
    return pl.pallas_call(
        functools.partial(_compress_kernel, n_pages),
        grid_spec=pltpu.PrefetchScalarGridSpec(
            num_scalar_prefetch=1, grid=(nb,),
            in_specs=[pl.BlockSpec(memory_space=pl.ANY), const(wbig), const(pe2), const(w2k), const(w2v), const(gk2)],
            out_specs=[out, out],
            scratch_shapes=[pltpu.VMEM((n + 8, CHUNK_W), F32), pltpu.VMEM((n + 8, 1024), F32),
                            pltpu.SemaphoreType.DMA(())]),
        out_shape=[jax.ShapeDtypeStruct((nb, n, LANES), F32)] * 2,
        compiler_params=pltpu.CompilerParams(dimension_semantics=("arbitrary",), vmem_limit_bytes=VMEM_LIMIT),
        name="compress",
    )(page_table, pool, wbig, pe2, w2k, w2v, gk2)


def _compress_t_kernel(n_pages, pt_ref, pool_ref, wbig_ref, pe_ref, w2k_ref, w2v_ref, gk_ref,
                       ck_ref, cv_ref, pg_scr, t_scr, r_scr, cvec_scr, sem):
    b = pl.program_id(0)
    slot = b % 2
    n = n_pages * CHUNKS_PER_PAGE

    def page_copy(bb, sl, j):
        return pltpu.make_async_copy(pool_ref.at[pt_ref[bb, j]], pg_scr.at[sl, j], sem.at[sl])

    @pl.when(b == 0)
    def _():
        for j in range(n_pages):
            page_copy(0, 0, j).start()
        r = _dot(pe_ref[...].astype(BF16), wbig_ref[...])
        cvec_scr[...] = jnp.broadcast_to(r[0:1, 0:512] + r[1:2, 512:1024], cvec_scr.shape)
        r_scr[pl.ds(n, 8), :] = jnp.zeros((8, 1024), F32)

    @pl.when(b + 1 < pl.num_programs(0))
    def _():
        for j in range(n_pages):
            page_copy(b + 1, 1 - slot, j).start()

    for j in range(n_pages):
        page_copy(b, slot, j).wait()

    src = lax.broadcasted_iota(jnp.int32, (PAGE_SIZE, PAGE_SIZE), 0)
    dst = lax.broadcasted_iota(jnp.int32, (PAGE_SIZE, PAGE_SIZE), 1)
    perm = (dst == CHUNKS_PER_PAGE * (src % CMP_STRIDE) + src // CMP_STRIDE).astype(BF16)
    for j in range(n_pages):
        xp = _dot(pg_scr[slot, j].astype(BF16), perm)
        tk = xp[0:LANES, :].T
        tv = xp[LANES:2 * LANES, :].T
        for p in range(CMP_STRIDE):
            rows = pl.ds(CHUNKS_PER_PAGE * j, CHUNKS_PER_PAGE)
            t_scr[p, rows, 0:LANES] = tk[CHUNKS_PER_PAGE * p:CHUNKS_PER_PAGE * (p + 1)]
            t_scr[p, rows, LANES:2 * LANES] = tv[CHUNKS_PER_PAGE * p:CHUNKS_PER_PAGE * (p + 1)]

    acc = jnp.zeros((n, 1024), F32)
    for p in range(CMP_STRIDE):
        acc = acc + _dot(t_scr[p].astype(BF16), wbig_ref[256 * p:256 * (p + 1), :])
    r_scr[pl.ds(0, n), :] = acc
    hid = acc[:, 0:512] + r_scr[pl.ds(1, n), 512:1024] + cvec_scr[0:1, :]
    hb = (hid * _sigmoid(hid)).astype(BF16)
    ck_ref[0] = _halfnorm(_dot(hb[:, 0:256], w2k_ref[...]), gk_ref[...])
    cv_ref[0] = _dot(hb[:, 256:512], w2v_ref[...])


def _compress_t(page_table, pool_t, wbig, pe2, w2k, w2v, gk2):
    nb, n_pages = page_table.shape
    n = n_pages * CHUNKS_PER_PAGE
    const = lambda a: pl.BlockSpec(a.shape, lambda b, pt: (0,) * a.ndim)
    out = pl.BlockSpec((1, n, LANES), lambda b, pt: (b, 0, 0))
    return pl.pallas_call(
        functools.partial(_compress_t_kernel, n_pages),
        grid_spec=pltpu.PrefetchScalarGridSpec(
            num_scalar_prefetch=1, grid=(nb,),
            in_specs=[pl.BlockSpec(memory_space=pl.ANY), const(wbig), const(pe2), const(w2k), const(w2v), const(gk2)],
            out_specs=[out, out],
            scratch_shapes=[pltpu.VMEM((2, n_pages, 256, PAGE_SIZE), F32), pltpu.VMEM((CMP_STRIDE, n, 2 * LANES), F32),
                            pltpu.VMEM((n + 8, 1024), F32), pltpu.VMEM((8, 512), F32),
                            pltpu.SemaphoreType.DMA((2,))]),
        out_shape=[jax.ShapeDtypeStruct((nb, n, LANES), F32)] * 2,
        compiler_params=pltpu.CompilerParams(dimension_semantics=("arbitrary",), vmem_limit_bytes=VMEM_LIMIT),
        name="compress_t",
    )(page_table, pool_t, wbig, pe2, w2k, w2v, gk2)


def _overlap_t(ns_rows, n_cmp_cols, n_cmp):
    s = lax.broadcasted_iota(jnp.int32, (ns_rows, n_cmp_cols), 0)
    n = lax.broadcasted_iota(jnp.int32, (ns_rows, n_cmp_cols), 1)
    ov = (n * CMP_STRIDE < s * SLC_BLOCK + SLC_BLOCK) & (n * CMP_STRIDE + CMP_BLOCK > s * SLC_BLOCK) & (n < n_cmp)
    return ov.astype(BF16)


def _importance_t(ov_t, pcs):
    a, b, c = _split3(pcs)
    return _dot_nt(ov_t, a) + _dot_nt(ov_t, b) + _dot_nt(ov_t, c)


def _block_scores_t(imp_t, tl):
    blk = lax.broadcasted_iota(jnp.int32, imp_t.shape, 0)
    cur = tl // SLC_BLOCK
    valid = blk <= cur
    forced = (blk == 0) | (valid & (blk > cur - N_LOCAL_FORCED))
    return jnp.where(valid, imp_t + jnp.where(forced, FORCE_BONUS, 0.0), NEG), blk


SLC_CHUNK = 512
RANK_ACCS = 4


def _nsa_prompt_kernel(seq, n_cmp, q_ref, kvs_ref, kvw_ref, ck_ref, cv_ref, ng_ref, et_ref, eg_ref, o_ref):
    i = pl.program_id(1)
    ns = seq // SLC_BLOCK
    nh = NSA_HEADS
    q8 = q_ref[...]
    qs = jnp.concatenate([q8[:, LANES * hh:LANES * (hh + 1)] for hh in range(nh)], axis=0)
    tq = i * Q_BLOCK + lax.broadcasted_iota(jnp.int32, (Q_BLOCK, 1), 0)
    rep = lambda a: jnp.concatenate([a] * nh, axis=0)
    t8 = rep(tq)

    ckb = ck_ref[0].astype(BF16)
    sc = _dot_nt(qs, ckb)
    jn = lax.broadcasted_iota(jnp.int32, sc.shape, 1)
    pc = _msoftmax(sc, (jn * CMP_STRIDE + CMP_BLOCK - 1) <= t8)
    o_cmp = _dot(pc.astype(BF16), cv_ref[0].astype(BF16))

    ov_t = _overlap_t(ns, pc.shape[1], n_cmp)
    tl = i * Q_BLOCK + lax.broadcasted_iota(jnp.int32, (ns, Q_BLOCK), 1)
    eye = (lax.broadcasted_iota(jnp.int32, (Q_BLOCK, Q_BLOCK), 0)
           == lax.broadcasted_iota(jnp.int32, (Q_BLOCK, Q_BLOCK), 1)).astype(BF16)
    pens = []
    for h in range(NSA_KV_HEADS):
        r0 = h * NSA_GROUP * Q_BLOCK
        pcs = pc[r0:r0 + 128] + pc[r0 + 128:r0 + 256] + pc[r0 + 256:r0 + 384] + pc[r0 + 384:r0 + 512]
        score, blk = _block_scores_t(_importance_t(ov_t, pcs), tl)
        ranks = [jnp.zeros(score.shape, F32) for _ in range(RANK_ACCS)]
        for k in range(ns):
            sk = score[k:k + 1, :]
            ranks[k % RANK_ACCS] += jnp.where((sk > score) | ((sk == score) & (blk > k)), 1.0, 0.0)
        rank = functools.reduce(lambda x, y: x + y, ranks)
        sel_t = jnp.where(rank < min(N_SELECT, ns), 1.0, 0.0).astype(BF16)
        if ns < LANES:
            sel_t = jnp.concatenate([sel_t, jnp.zeros((LANES - ns, Q_BLOCK), BF16)], axis=0)
        pen = ((_dot_nt(eye, sel_t) - 1.0) * -NEG).astype(BF16)
        pens += [pen] * NSA_GROUP

    qx = jnp.concatenate([qs, jnp.concatenate(pens, axis=0)], axis=1)
    kc = min(SLC_CHUNK, seq)

    def chunk(c, carry, bias):
        m, l, acc = carry
        k0 = pl.multiple_of(c * kc, kc)
        kx = jnp.concatenate([kvs_ref[pl.ds(k0, kc), 0:128].astype(BF16), et_ref[pl.ds(k0, kc), :]], axis=1)
        vv = kvs_ref[pl.ds(k0, kc), 128:256].astype(BF16)
        s = _dot_nt(qx, kx)
        if bias is not None:
            s = s + bias
        m_new = jnp.maximum(m, jnp.max(s, axis=-1, keepdims=True))
        alpha = jnp.exp(m - m_new)
        p = jnp.exp(s - m_new)
        l = alpha * l + jnp.sum(p, axis=-1, keepdims=True)
        acc = alpha * acc + _dot(p.astype(BF16), vv)
        return m_new, l, acc

    rows = nh * Q_BLOCK
    init = (jnp.full((rows, 1), NEG, F32), jnp.zeros((rows, 1), F32), jnp.zeros((rows, LANES), F32))
    c_last = (i * Q_BLOCK) // kc
    carry = lax.fori_loop(0, c_last, lambda c, cr: chunk(c, cr, None), init)
    kpos = c_last * kc + lax.broadcasted_iota(jnp.int32, (Q_BLOCK, kc), 1)
    _, l, acc = chunk(c_last, carry, rep(jnp.where(kpos <= tq, 0.0, NEG)))
    o_slc = acc / l

    wk = WINDOW + Q_BLOCK
    start = pl.multiple_of(Q_BLOCK * jnp.maximum(i - WINDOW // Q_BLOCK, 0), Q_BLOCK)
    kk = kvw_ref[pl.ds(start, wk), 0:128].astype(BF16)
    vx = jnp.concatenate([kvw_ref[pl.ds(start, wk), 128:256].astype(BF16), jnp.ones((wk, LANES), BF16)], axis=1)
    d = tq - (start + lax.broadcasted_iota(jnp.int32, (Q_BLOCK, wk), 1))
    sw = _dot_nt(qs, kk) + rep(jnp.where((d >= 0) & (d < WINDOW), 0.0, NEG))
    aw = _dot(jnp.exp(sw - jnp.max(sw, axis=-1, keepdims=True)).astype(BF16), vx)
    o_win = aw[:, 0:LANES] / aw[:, LANES:LANES + 1]

    a, b, c = _split3(ng_ref[...])
    for h in range(NSA_KV_HEADS):
        eg = eg_ref[h]
        gexp = _dot(a, eg) + _dot(b, eg) + _dot(c, eg)
        for g in range(NSA_GROUP):
            hh = NSA_GROUP * h + g
            rs = slice(Q_BLOCK * hh, Q_BLOCK * (hh + 1))
            gate = lambda br: gexp[:, LANES * (NSA_GROUP * br + g):LANES * (NSA_GROUP * br + g + 1)]
            o = gate(0) * o_cmp[rs] + gate(1) * o_slc[rs] + gate(2) * o_win[rs]
            o_ref[:, LANES * hh:LANES * (hh + 1)] = o.astype(BF16)


def _nsa_prompt(q, kvs, kvw, ck, cv, ng, nb, seq, n_cmp):
    nq = seq // Q_BLOCK
    ncp = ck.shape[1]
    ns = seq // SLC_BLOCK
    assert ns <= LANES and seq >= WINDOW + Q_BLOCK
    et = (jnp.arange(seq)[:, None] // SLC_BLOCK == jnp.arange(LANES)[None, :]).astype(BF16)
    n_g = 3 * NSA_GROUP
    col = jnp.arange(n_g * LANES)[None, None, :] // LANES
    src = (col // NSA_GROUP) * NSA_HEADS + NSA_GROUP * jnp.arange(NSA_KV_HEADS)[:, None, None] + col % NSA_GROUP
    eg = (jnp.arange(LANES)[None, :, None] == src).astype(BF16)
    return pl.pallas_call(
        functools.partial(_nsa_prompt_kernel, seq, n_cmp),
        grid=(nb, nq),
        in_specs=[pl.BlockSpec((Q_BLOCK, 1024), lambda b, i: (b * nq + i, 0)),
                  pl.BlockSpec((seq, 256), lambda b, i: (b, 0)),
                  pl.BlockSpec((seq, 256), lambda b, i: (b, 0)),
                  pl.BlockSpec((1, ncp, LANES), lambda b, i: (b, 0, 0)),
                  pl.BlockSpec((1, ncp, LANES), lambda b, i: (b, 0, 0)),
                  pl.BlockSpec((Q_BLOCK, LANES), lambda b, i: (b * nq + i, 0)),
                  pl.BlockSpec((seq, LANES), lambda b, i: (0, 0)),
                  pl.BlockSpec((NSA_KV_HEADS, LANES, n_g * LANES), lambda b, i: (0, 0, 0))],
        out_specs=pl.BlockSpec((Q_BLOCK, 1024), lambda b, i: (b * nq + i, 0)),
        out_shape=jax.ShapeDtypeStruct((nb * seq, 1024), BF16),
        compiler_params=pltpu.CompilerParams(dimension_semantics=("arbitrary",) * 2, vmem_limit_bytes=VMEM_LIMIT),
        name="nsa_prompt",
    )(q, kvs, kvw, ck, cv, ng, et, eg)


def _nsa_sample_kernel(past, n_pages, dseq, wb, pt_ref, q_ref, kvs_ref, kvw_ref, ck_ref, cv_ref, ng_ref, cwin_ref,
                       e_ref, pool_ref, o_ref, nwin_ref, kvt_scr, kw_scr, sem):
    b = pl.program_id(0)
    slot = b % 2

    def page_copy(bb, sl, j):
        return pltpu.make_async_copy(pool_ref.at[pt_ref[bb, j]], kvt_scr.at[sl, :, pl.ds(PAGE_SIZE * j, PAGE_SIZE)],
                                     sem.at[sl])

    @pl.when(b == 0)
    def _():
        for j in range(n_pages):
            page_copy(0, 0, j).start()

    @pl.when(b + 1 < pl.num_programs(0))
    def _():
        for j in range(n_pages):
            page_copy(b + 1, 1 - slot, j).start()

    n_heads = NSA_KV_HEADS * NSA_GROUP
    rows = n_heads * dseq
    kpad = past + PAGE_SIZE
    ns = (past + dseq + SLC_BLOCK - 1) // SLC_BLOCK
    nsp = e_ref.shape[0]
    qf = q_ref[0].astype(F32)
    qs = jnp.concatenate([qf[:, LANES * hh:LANES * (hh + 1)] for hh in range(n_heads)], axis=0).astype(BF16)
    t1 = past + lax.broadcasted_iota(jnp.int32, (dseq, 1), 0)
    t = jnp.concatenate([t1] * n_heads, axis=0)

    sc = _dot_nt(qs, ck_ref[0].astype(BF16))
    jn = lax.broadcasted_iota(jnp.int32, sc.shape, 1)
    pc = _msoftmax(sc, (jn * CMP_STRIDE + CMP_BLOCK - 1) <= t)
    o_cmp = _dot(pc.astype(BF16), cv_ref[0].astype(BF16))

    per_kv = []
    for kvh in range(NSA_KV_HEADS):
        base = kvh * NSA_GROUP * dseq
        s = pc[base:base + dseq]
        for g in range(1, NSA_GROUP):
            s = s + pc[base + g * dseq:base + (g + 1) * dseq]
        per_kv += [s] * NSA_GROUP
    pcs = jnp.concatenate(per_kv, axis=0)
    n_cmp = (ns * SLC_BLOCK) // CMP_STRIDE - CMP_BLOCK // CMP_STRIDE + 1
    imp_t = _importance_t(_overlap_t(nsp, pcs.shape[1], n_cmp), pcs)
    tl = past + lax.broadcasted_iota(jnp.int32, (nsp, rows), 1) % dseq
    score, blk = _block_scores_t(imp_t, tl)
    ranks = [jnp.zeros(score.shape, F32) for _ in range(RANK_ACCS)]
    for k in range(ns):
        sk = score[k:k + 1, :]
        ranks[k % RANK_ACCS] += jnp.where((sk > score) | ((sk == score) & (blk > k)), 1.0, 0.0)
    rank = functools.reduce(lambda x, y: x + y, ranks)
    sel_t = jnp.where(rank < min(N_SELECT, ns), 1.0, 0.0).astype(BF16)
    eye = (lax.broadcasted_iota(jnp.int32, (rows, rows), 0)
           == lax.broadcasted_iota(jnp.int32, (rows, rows), 1)).astype(BF16)
    sel = _dot_nt(eye, sel_t)
    sele = _dot(sel.astype(BF16), e_ref[...])

    knew = jnp.concatenate([kvs_ref[0], jnp.zeros((PAGE_SIZE - dseq, 256), F32)], axis=0).astype(BF16)
    for j in range(n_pages):
        page_copy(b, slot, j).wait()
    s = jnp.concatenate([_dot(qs, kvt_scr[slot, 0:128, :].astype(BF16)), _dot_nt(qs, knew[:, 0:128])], axis=1)
    kpos = lax.broadcasted_iota(jnp.int32, (rows, kpad), 1)
    p = _msoftmax(s, (sele > 0.5) & (kpos <= t)).astype(BF16)
    o_slc = _dot_nt(p[:, 0:past], kvt_scr[slot, 128:256, :].astype(BF16)) + _dot(p[:, past:kpad], knew[:, 128:256])

    wpad = kw_scr.shape[0]
    kw_scr[pl.ds(0, wb), :] = cwin_ref[0]
    kw_scr[pl.ds(wb, dseq), :] = kvw_ref[0]
    kw_scr[pl.ds(wb + dseq, wpad - wb - dseq), :] = jnp.zeros((wpad - wb - dseq, 256), F32)
    sw = _dot_nt(qs, kw_scr[:, 0:128].astype(BF16))
    d = t - (past - wb + lax.broadcasted_iota(jnp.int32, (rows, wpad), 1))
    pw = _msoftmax(sw, (d >= 0) & (d < WINDOW))
    o_win = _dot(pw.astype(BF16), kw_scr[:, 128:256].astype(BF16))
    nwin_ref[0, pl.ds(0, wb - dseq), :] = cwin_ref[0, pl.ds(dseq, wb - dseq), :]
    nwin_ref[0, pl.ds(wb - dseq, dseq), :] = kvw_ref[0]

    ngv = ng_ref[0]
    for hh in range(n_heads):
        rs = slice(dseq * hh, dseq * (hh + 1))
        gate = lambda br: ngv[:, br * NSA_HEADS + hh:br * NSA_HEADS + hh + 1]
        o_ref[0, :, LANES * hh:LANES * (hh + 1)] = gate(0) * o_cmp[rs] + gate(1) * o_slc[rs] + gate(2) * o_win[rs]


def _nsa_sample(page_table, q, kvs, kvw, ck, cv, ng, cwin, e_s, pool, past):
    nb, n_pages = page_table.shape
    dseq = q.shape[1]
    wb = cwin.shape[1]
    kpad = past + PAGE_SIZE
    wpad = wb + PAGE_SIZE
    nsp = e_s.shape[0]
    rows = NSA_HEADS * dseq
    per_b = lambda a: pl.BlockSpec((1,) + a.shape[1:], lambda b, pt: (b,) + (0,) * (a.ndim - 1))
    return pl.pallas_call(
        functools.partial(_nsa_sample_kernel, past, n_pages, dseq, wb),
        grid_spec=pltpu.PrefetchScalarGridSpec(
            num_scalar_prefetch=1, grid=(nb,),
            in_specs=[per_b(q), per_b(kvs), per_b(kvw), per_b(ck), per_b(cv), per_b(ng), per_b(cwin),
                      pl.BlockSpec(e_s.shape, lambda b, pt: (0, 0)), pl.BlockSpec(memory_space=pl.ANY)],
            out_specs=[pl.BlockSpec((1, dseq, 1024), lambda b, pt: (b, 0, 0)),
                       pl.BlockSpec((1, wb, 256), lambda b, pt: (b, 0, 0))],
            scratch_shapes=[pltpu.VMEM((2, 256, past), F32), pltpu.VMEM((wpad, 256), F32),
                            pltpu.SemaphoreType.DMA((2,))]),
        out_shape=[jax.ShapeDtypeStruct((nb, dseq, 1024), F32), jax.ShapeDtypeStruct((nb, wb, 256), F32)],
        compiler_params=pltpu.CompilerParams(dimension_semantics=("arbitrary",), vmem_limit_bytes=VMEM_LIMIT),
        name="nsa_sample",
    )(page_table, q, kvs, kvw, ck, cv, ng, cwin, e_s, pool)


def _ret_kernel(c_true, rq_ref, rk_ref, rv_ref, rg_ref, s0_ref, gro_ref, o_ref, sout_ref, s_scr):
    c = pl.program_id(1)
    cs = RET_CHUNK

    @pl.when(c == 0)
    def _():
        for tile in range(RET_HEADS // 2):
            s_scr[tile] = jnp.concatenate([s0_ref[0, 2 * tile], s0_ref[0, 2 * tile + 1]], axis=0)

    diff = (lax.broadcasted_iota(jnp.int32, (cs, cs), 0) - lax.broadcasted_iota(jnp.int32, (cs, cs), 1)).astype(F32)
    ic = lax.broadcasted_iota(jnp.int32, (cs, 1), 0).astype(F32)
    lane = lax.broadcasted_iota(jnp.int32, (cs, LANES), 1)
    for hd in range(RET_HEADS):
        lg = math.log(1.0 - 2.0 ** (-5.0 - hd))
        tile, half = hd // 2, hd % 2
        in_half = (lane >= RET_DK * half) & (lane < RET_DK * (half + 1))
        qm = jnp.where(in_half, rq_ref[:, LANES * tile:LANES * (tile + 1)], 0.0).astype(BF16)
        k2 = rk_ref[:, LANES * tile:LANES * (tile + 1)]
        vb = rv_ref[:, LANES * hd:LANES * (hd + 1)].astype(BF16)
        dmat = jnp.where(diff >= 0, jnp.exp(jnp.maximum(diff, 0.0) * lg), 0.0)
        o = _dot((_dot_nt(qm, k2.astype(BF16)) * dmat).astype(BF16), vb)
        st = s_scr[tile]
        o = o + _dot(qm, st.astype(BF16)) * jnp.exp((ic + 1.0) * lg)
        kdec_t = (k2 * jnp.exp((c_true - 1.0 - ic) * lg)).T
        upd = _dot(kdec_t[RET_DK * half:RET_DK * (half + 1)].astype(BF16), vb)
        s_new = st[RET_DK * half:RET_DK * (half + 1)] * math.exp(c_true * lg) + upd
        s_scr[tile, pl.ds(RET_DK * half, RET_DK), :] = s_new
        on = _tilenorm(o, gro_ref[...], RET_DV)
        o_ref[:, LANES * hd:LANES * (hd + 1)] = (rg_ref[:, LANES * hd:LANES * (hd + 1)] * on).astype(BF16)

    @pl.when(c == pl.num_programs(1) - 1)
    def _():
        for hd in range(RET_HEADS):
            sout_ref[0, hd] = s_scr[hd // 2, pl.ds(RET_DK * (hd % 2), RET_DK), :]


def _retention(rq, rk, rv, rg, s0, gro, nb, n_chunks, c_true):
    cs = RET_CHUNK
    n_rows = nb * n_chunks * cs
    row = lambda w: pl.BlockSpec((cs, w), lambda b, c: (b * n_chunks + c, 0))
    return pl.pallas_call(
        functools.partial(_ret_kernel, float(c_true)),
        grid=(nb, n_chunks),
        in_specs=[row(256), row(256), row(512), row(512),
                  pl.BlockSpec((1, RET_HEADS, RET_DK, RET_DV), lambda b, c: (b, 0, 0, 0)),
                  pl.BlockSpec(gro.shape, lambda b, c: (0, 0))],
        out_specs=[row(512), pl.BlockSpec((1, RET_HEADS, RET_DK, RET_DV), lambda b, c: (b, 0, 0, 0))],
        out_shape=[jax.ShapeDtypeStruct((n_rows, 512), BF16),
                   jax.ShapeDtypeStruct((nb, RET_HEADS, RET_DK, RET_DV), F32)],
        scratch_shapes=[pltpu.VMEM((RET_HEADS // 2, LANES, RET_DV), F32)],
        compiler_params=pltpu.CompilerParams(dimension_semantics=("arbitrary", "arbitrary")),
        name="retention",
    )(rq, rk, rv, rg, s0, gro)


def _memkv_kernel(m_ref, g_ref, w_ref, gk_ref, o_ref):
    x = m_ref[...]
    xn = (x * lax.rsqrt(jnp.mean(x * x, axis=-1, keepdims=True) + EPS) * g_ref[...]).astype(BF16)
    hk = _dot(xn, w_ref[...])
    half = MEM_HEADS * MEM_HD
    for hd in range(MEM_HEADS):
        o_ref[:, LANES * hd:LANES * (hd + 1)] = _tilenorm(hk[:, LANES * hd:LANES * (hd + 1)], gk_ref[...], MEM_HD)
    o_ref[:, half:2 * half] = hk[:, half:2 * half]


def _mem_kv(mem2d, g_mem, w_mem, gk_mem, tm):
    n = mem2d.shape[0]
    const = lambda a: pl.BlockSpec(a.shape, lambda i: (0,) * a.ndim)
    return pl.pallas_call(
        _memkv_kernel,
        grid=(n // tm,),
        in_specs=[pl.BlockSpec((tm, D_MODEL), lambda i: (i, 0)), const(g_mem), const(w_mem), const(gk_mem)],
        out_specs=pl.BlockSpec((tm, 1024), lambda i: (i, 0)),
        out_shape=jax.ShapeDtypeStruct((n, 1024), F32),
        compiler_params=pltpu.CompilerParams(dimension_semantics=("arbitrary",)),
        name="mem_kv",
    )(mem2d, g_mem, w_mem, gk_mem)


def _memattn_kernel(q_ref, kv_ref, o_ref):
    half = MEM_HEADS * MEM_HD
    for hd in range(MEM_HEADS):
        cols = slice(LANES * hd, LANES * (hd + 1))
        s = _dot_nt(q_ref[:, cols].astype(BF16), kv_ref[:, cols].astype(BF16))
        p = jnp.exp(s - jnp.max(s, axis=-1, keepdims=True))
        p = p / jnp.sum(p, axis=-1, keepdims=True)
        o = _dot(p.astype(BF16), kv_ref[:, half + LANES * hd:half + LANES * (hd + 1)].astype(BF16))
        o_ref[:, cols] = o.astype(o_ref.dtype)


def _memattn_cache_kernel(q_ref, kv_ref, o_ref):
    stride = 2 * MEM_HEADS
    mem_len = kv_ref.shape[1] // stride
    for hd in range(MEM_HEADS):
        cols = slice(LANES * hd, LANES * (hd + 1))
        k = kv_ref[0, pl.ds(hd, mem_len, stride=stride), :].astype(BF16)
        v = kv_ref[0, pl.ds(MEM_HEADS + hd, mem_len, stride=stride), :].astype(BF16)
        s = _dot_nt(q_ref[:, cols].astype(BF16), k)
        p = jnp.exp(s - jnp.max(s, axis=-1, keepdims=True))
        p = p / jnp.sum(p, axis=-1, keepdims=True)
        o_ref[:, cols] = _dot(p.astype(BF16), v).astype(o_ref.dtype)


def _mem_attend_cache(q2d, cache_rows, dseq):
    nb, n_rows, _ = cache_rows.shape
    return pl.pallas_call(
        _memattn_cache_kernel,
        grid=(nb,),
        in_specs=[pl.BlockSpec((dseq, 512), lambda b: (b, 0)), pl.BlockSpec((1, n_rows, LANES), lambda b: (b, 0, 0))],
        out_specs=pl.BlockSpec((dseq, 512), lambda b: (b, 0)),
        out_shape=jax.ShapeDtypeStruct((nb * dseq, 512), F32),
        compiler_params=pltpu.CompilerParams(dimension_semantics=("arbitrary",)),
        name="mem_attend_cache",
    )(q2d, cache_rows)


def _mem_attend(q2d, kv2d, nb, blocks_per_b, tq, mem_len, out_rows, out_dtype):
    return pl.pallas_call(
        _memattn_kernel,
        grid=(nb, blocks_per_b),
        in_specs=[pl.BlockSpec((tq, 512), lambda b, i: (b * blocks_per_b + i, 0)),
                  pl.BlockSpec((mem_len, 1024), lambda b, i: (b, 0))],
        out_specs=pl.BlockSpec((tq, 512), lambda b, i: (b * blocks_per_b + i, 0)),
        out_shape=jax.ShapeDtypeStruct((out_rows, 512), out_dtype),
        compiler_params=pltpu.CompilerParams(dimension_semantics=("arbitrary", "arbitrary")),
        name="mem_attend",
    )(q2d, kv2d)


def _merge_kernel(nbp, xp_ref, xs_ref, onsap_ref, onsas_ref, oretp_ref, orets_ref, omemp_ref, omems_ref, g_ref,
                  wbg_ref, wn_ref, wr_ref, wm_ref, wo_ref, gffn_ref, wrt_ref, brt_ref, h_ref, hn_ref, comb_ref):
    i = pl.program_id(0)
    pick = lambda p_ref, s_ref: jnp.where(i < nbp, p_ref[...], s_ref[...])
    x = pick(xp_ref, xs_ref)
    xn = (x * lax.rsqrt(jnp.mean(x * x, axis=-1, keepdims=True) + EPS) * g_ref[...]).astype(BF16)
    bg = _sigmoid(_dot(xn, wbg_ref[...]))
    mixed = (bg[:, 0:1024] * _dot(pick(onsap_ref, onsas_ref), wn_ref[...])
             + bg[:, 1024:2048] * _dot(pick(oretp_ref, orets_ref), wr_ref[...])
             + bg[:, 2048:3072] * _dot(pick(omemp_ref, omems_ref), wm_ref[...]))
    hres = x + _dot(mixed.astype(BF16), wo_ref[...])
    h_ref[...] = hres
    hn = hres * lax.rsqrt(jnp.mean(hres * hres, axis=-1, keepdims=True) + EPS) * gffn_ref[...]
    hn_ref[...] = hn.astype(BF16)

    a, b, _ = _split3(hn)
    logits = _dot(a, wrt_ref[0]) + _dot(a, wrt_ref[1]) + _dot(b, wrt_ref[0]) + brt_ref[...]
    lane = lax.broadcasted_iota(jnp.int32, logits.shape, 1).astype(F32)
    vals, hots = [], []
    for _k in range(TOP_K):
        mx = jnp.max(logits, axis=-1, keepdims=True)
        idx = jnp.min(jnp.where(logits == mx, lane, float(LANES)), axis=-1, keepdims=True)
        hot = lane == idx
        vals.append(mx)
        hots.append(hot)
        logits = jnp.where(hot, -3.0e38, logits)
    es = [jnp.exp(v - vals[0]) for v in vals]
    den = es[0] + es[1] + es[2] + es[3]
    comb = jnp.zeros(logits.shape, F32)
    for e_k, hot in zip(es, hots):
        comb = comb + jnp.where(hot, e_k / den, 0.0)
    comb_ref[...] = comb


def _merge(xp, xs, onsa, onsa_s, oret, oret_s, omem, omem_s, g_attn, wbg, wn, wr, wm, wo, gffn, wrt, brt, tm):
    n_p, n_s = xp.shape[0], xs.shape[0]
    nbp, nbs = n_p // tm, n_s // tm
    n_all = n_p + n_s
    row = lambda w: pl.BlockSpec((tm, w), lambda i: (i, 0))
    rowp = lambda w: pl.BlockSpec((tm, w), lambda i: (jnp.minimum(i, nbp - 1), 0))
    rows = lambda w: pl.BlockSpec((tm, w), lambda i: (jnp.maximum(i - nbp, 0), 0))
    const = lambda a: pl.BlockSpec(a.shape, lambda i: (0,) * a.ndim)
    return pl.pallas_call(
        functools.partial(_merge_kernel, nbp),
        grid=(nbp + nbs,),
        in_specs=[rowp(1024), rows(1024), rowp(1024), rows(1024), rowp(512), rows(512), rowp(512), rows(512),
                  const(g_attn), const(wbg), const(wn), const(wr), const(wm),
                  const(wo), const(gffn), const(wrt), const(brt)],
        out_specs=[row(1024), row(1024), row(LANES)],
        out_shape=[jax.ShapeDtypeStruct((n_all, 1024), F32), jax.ShapeDtypeStruct((n_all, 1024), BF16),
                   jax.ShapeDtypeStruct((n_all, LANES), F32)],
        compiler_params=pltpu.CompilerParams(dimension_semantics=("arbitrary",), vmem_limit_bytes=VMEM_LIMIT),
        name="merge",
    )(xp, xs, onsa, onsa_s, oret, oret_s, omem, omem_s, g_attn, wbg, wn, wr, wm, wo, gffn, wrt, brt)


MOE_BLOCK_CAP = 1664
MOE_CHUNK = 256
ROUTE_TILE = 128


def _route_kernel(comb_ref, pos_ref, post_ref, cnt_ref):
    tb = comb_ref.shape[0]
    rt = ROUTE_TILE
    r = lax.broadcasted_iota(jnp.int32, (rt, rt), 0)
    c = lax.broadcasted_iota(jnp.int32, (rt, rt), 1)
    ltri = (c < r).astype(BF16)
    eye = (c == r).astype(BF16)
    carry = jnp.zeros((1, LANES), F32)
    for s in range(tb // rt):
        sel = jnp.where(comb_ref[pl.ds(rt * s, rt), :] > 0.0, 1.0, 0.0)
        pos = jnp.where(sel > 0.0, _dot(ltri, sel.astype(BF16)) + carry, -1.0)
        carry = carry + jnp.sum(sel, axis=0, keepdims=True)
        pos_ref[pl.ds(rt * s, rt), :] = pos
        hi = jnp.floor((pos + 1.0) * (1.0 / 256.0))
        lo = (pos + 1.0) - 256.0 * hi
        post_ref[:, rt * s:rt * (s + 1)] = 256.0 * _dot_nt(eye, hi.astype(BF16)) + _dot_nt(eye, lo.astype(BF16)) - 1.0
    cnt_ref[0] = jnp.broadcast_to(carry, (8, LANES))


def _route(comb, tb):
    n_all = comb.shape[0]
    nblk = n_all // tb
    return pl.pallas_call(
        _route_kernel,
        grid=(nblk,),
        in_specs=[pl.BlockSpec((tb, LANES), lambda i: (i, 0))],
        out_specs=[pl.BlockSpec((tb, LANES), lambda i: (i, 0)), pl.BlockSpec((LANES, tb), lambda i: (0, i)),
                   pl.BlockSpec((1, 8, LANES), lambda i: (i, 0, 0))],
        out_shape=[jax.ShapeDtypeStruct((n_all, LANES), F32), jax.ShapeDtypeStruct((LANES, n_all), F32),
                   jax.ShapeDtypeStruct((nblk, 8, LANES), F32)],
        compiler_params=pltpu.CompilerParams(dimension_semantics=("arbitrary",)),
        name="route",
    )(comb)


def _moe_kernel(cnt_ref, hn_ref, h_ref, comb_ref, pos_ref, post_ref, wgu_ref, wd_ref, bgu_ref, bd_ref, y_ref):
    i = pl.program_id(0)
    e = pl.program_id(1)
    tb = hn_ref.shape[0]
    ch = MOE_CHUNK

    @pl.when(e == 0)
    def _():
        y_ref[...] = h_ref[...]

    n_rows = cnt_ref[i * N_EXPERTS + e]
    lane = lax.broadcasted_iota(jnp.int32, (tb, LANES), 1)
    pcol = jnp.sum(jnp.where(lane == e, pos_ref[...], 0.0), axis=-1, keepdims=True)
    ccol = jnp.sum(jnp.where(lane == e, comb_ref[...], 0.0), axis=-1, keepdims=True)
    prow = post_ref[pl.ds(e, 1), :]

    def chunk(c, carry):
        r0 = (c * ch).astype(F32)
        slot_g = r0 + lax.broadcasted_iota(jnp.int32, (ch, tb), 0).astype(F32)
        gather = jnp.where(prow == slot_g, 1.0, 0.0).astype(BF16)
        xg = _dot(gather, hn_ref[...]).astype(BF16)
        gu = _dot(xg, wgu_ref[0]) + bgu_ref[0]
        gate = jnp.minimum(gu[:, 0:D_FF], SWIGLU_LIMIT)
        up = jnp.clip(gu[:, D_FF:2 * D_FF], -SWIGLU_LIMIT, SWIGLU_LIMIT)
        act = (up + 1.0) * gate * _sigmoid(SWIGLU_ALPHA * gate)
        yc = _dot(act.astype(BF16), wd_ref[0]) + bd_ref[0]
        slot_s = r0 + lax.broadcasted_iota(jnp.int32, (tb, ch), 1).astype(F32)
        scatter = jnp.where(pcol == slot_s, ccol, 0.0).astype(BF16)
        y_ref[...] += _dot(scatter, yc.astype(BF16))
        return carry

    lax.fori_loop(0, (n_rows + ch - 1) // ch, chunk, 0)


def _moe(cnt, hn, h, comb, pos, post, wgu, wd, b_gate_up, b_down, tb):
    n_all = hn.shape[0]
    row = lambda w: pl.BlockSpec((tb, w), lambda i, e, cnt: (i, 0))
    return pl.pallas_call(
        _moe_kernel,
        grid_spec=pltpu.PrefetchScalarGridSpec(
            num_scalar_prefetch=1, grid=(n_all // tb, N_EXPERTS),
            in_specs=[row(1024), row(1024), row(LANES), row(LANES),
                      pl.BlockSpec((LANES, tb), lambda i, e, cnt: (0, i)),
                      pl.BlockSpec((1, D_MODEL, 2 * D_FF), lambda i, e, cnt: (e, 0, 0)),
                      pl.BlockSpec((1, D_FF, D_MODEL), lambda i, e, cnt: (e, 0, 0)),
                      pl.BlockSpec((1, 1, 2 * D_FF), lambda i, e, cnt: (e, 0, 0)),
                      pl.BlockSpec((1, 1, D_MODEL), lambda i, e, cnt: (e, 0, 0))],
            out_specs=row(1024)),
        out_shape=jax.ShapeDtypeStruct((n_all, 1024), F32),
        compiler_params=pltpu.CompilerParams(dimension_semantics=("arbitrary",) * 2, vmem_limit_bytes=VMEM_LIMIT),
        name="moe",
    )(cnt, hn, h, comb, pos, post, wgu, wd, b_gate_up, b_down)


def _pick_tile(n, pref):
    t = pref
    while n % t:
        t //= 2
    return t


def kernel(x_prompt, x_sample, cache_cmp, cache_slc, cache_win, state_ret, cache_mem, page_table, mem_prompt,
           g_attn, w_in, g_q_nsa, g_k_cmp, g_k_slc, g_k_win, pe_ck, w_ck1, w_ck2, pe_cv, w_cv1, w_cv2,
           g_ret_out, g_mem, w_mem_kv, g_q_mem, g_k_mem, w_br_nsa, w_br_ret, w_br_mem, w_out,
           g_ffn, w_router, b_router, w_gate_up, b_gate_up, w_down, b_down):
    nb, seq, _ = x_prompt.shape
    db, dseq, _ = x_sample.shape
    n_pages = page_table.shape[1]
    past = n_pages * PAGE_SIZE
    wb = cache_win.shape[1]
    mem_len = mem_prompt.shape[1]
    n_p, n_s = nb * seq, db * dseq
    n_all = n_p + n_s
    tm = _pick_tile(math.gcd(seq, n_s), 256)

    o = 0
    cols = {}
    for name, wdt in (("q", 512), ("kv", 768), ("ng", 24), ("rq", 256), ("rk", 256), ("rv", 512), ("rg", 512),
                      ("mq", 512), ("bg", 3072)):
        cols[name] = w_in[:, o:o + wdt]
        o += wdt
    wq = cols["q"].reshape(D_MODEL, NSA_HEADS, NSA_HD)
    zq = jnp.zeros_like(wq)
    wq_pad = jnp.concatenate(
        [jnp.concatenate([wq[:, hh], zq[:, hh]] if hh < NSA_GROUP else [zq[:, hh], wq[:, hh]], axis=-1)
         for hh in range(NSA_HEADS)], axis=-1)
    w_ng = jnp.pad(cols["ng"], ((0, 0), (0, C_NG - 24)))
    w1 = jnp.concatenate([wq_pad, cols["kv"], w_ng, cols["rq"], cols["rk"], cols["rv"], cols["rg"], cols["mq"]],
                         axis=-1).astype(BF16)
    w_bg = cols["bg"].astype(BF16)
    two = lambda g: jnp.concatenate([g, g]).reshape(1, LANES)
    r1 = lambda g: g.reshape(1, -1)

    half = RET_DK // 2
    inv = ROPE_BASE ** (-jnp.arange(half, dtype=F32) / half)
    pos = jnp.concatenate([jnp.arange(seq, dtype=jnp.int32),
                           jnp.tile(past + jnp.arange(dseq, dtype=jnp.int32), db)]).astype(F32)
    ang = pos[:, None] * inv[None, :]
    cos_t = jnp.tile(jnp.cos(ang), (1, 2 * RET_HEADS))
    sin_t = jnp.tile(jnp.concatenate([-jnp.sin(ang), jnp.sin(ang)], axis=-1), (1, RET_HEADS))

    xp = x_prompt.reshape(n_p, D_MODEL)
    xs = x_sample.reshape(n_s, D_MODEL)
    (q, kvc, kvs, kvw, ng, rq, rk, rv, rg, mq, kvct, kvst, kvwt) = _project(
        xp, xs, r1(g_attn), w1, cos_t, sin_t, two(g_q_nsa), two(g_k_slc), two(g_k_win), r1(g_q_mem), seq, tm)

    w1s = jnp.stack([w_ck1, w_ck1, w_cv1, w_cv1]).reshape(4, 2, CMP_STRIDE, NSA_HD, CMP_HIDDEN)
    wbig = jnp.einsum("shpdc,st->psdhtc", w1s, jnp.eye(4, dtype=F32)).reshape(CHUNK_W, 2 * 4 * CMP_HIDDEN).astype(BF16)
    pes = jnp.stack([pe_ck, pe_ck, pe_cv, pe_cv]).reshape(4, 2, CMP_STRIDE, NSA_HD)
    pe2 = jnp.pad(jnp.transpose(pes, (1, 2, 0, 3)).reshape(2, CHUNK_W), ((0, 6), (0, 0)))
    zc = jnp.zeros_like(w_ck2)
    bd2 = lambda w: jnp.concatenate([jnp.concatenate([w, zc], 1), jnp.concatenate([zc, w], 1)], 0).astype(BF16)
    gk2 = two(g_k_cmp)

    pages_p = seq // PAGE_SIZE
    pool_p = (kvc if n_all % PAGE_SIZE == 0 else kvc[:n_p]).reshape(-1, CHUNKS_PER_PAGE, CHUNK_W)
    pt_p = jnp.arange(nb * pages_p, dtype=jnp.int32).reshape(nb, pages_p)
    ck_p, cv_p = _compress(pt_p, pool_p, wbig, pe2, bd2(w_ck2), bd2(w_cv2), gk2)
    pages_t = lambda c: jnp.transpose(c, (0, 2, 3, 4, 1)).reshape(-1, 256, PAGE_SIZE)
    ck_s, cv_s = _compress_t(page_table, pages_t(cache_cmp), wbig, pe2, bd2(w_ck2), bd2(w_cv2), gk2)

    n_cmp_p = seq // CMP_STRIDE - CMP_BLOCK // CMP_STRIDE + 1
    onsa = _nsa_prompt(q, kvs, kvw, ck_p, cv_p, ng, nb, seq, n_cmp_p)

    ns_s = (past + dseq + SLC_BLOCK - 1) // SLC_BLOCK
    nsp = (ns_s + 7) // 8 * 8
    kpad = past + PAGE_SIZE
    e_s = (jnp.arange(kpad)[None, :] // SLC_BLOCK == jnp.arange(nsp)[:, None]).astype(BF16)
    s3 = lambda a: a[n_p:].reshape(db, dseq, a.shape[1])
    onsa_s, new_win_s = _nsa_sample(page_table, s3(q), s3(kvs), s3(kvw), ck_s, cv_s, s3(ng),
                                    cache_win.reshape(db, wb, 256), e_s,
                                    pages_t(cache_slc), past)
    onsa_s = onsa_s.reshape(n_s, 1024).astype(BF16)

    gro = r1(g_ret_out)
    oret, ret_state_p = _retention(rq, rk, rv, rg, jnp.zeros((nb, RET_HEADS, RET_DK, RET_DV), F32), gro,
                                   nb, seq // RET_CHUNK, RET_CHUNK)
    padc = lambda a: jnp.pad(s3(a), ((0, 0), (0, RET_CHUNK - dseq), (0, 0))).reshape(db * RET_CHUNK, a.shape[1])
    oret_s, ret_state_s = _retention(padc(rq), padc(rk), padc(rv), padc(rg), state_ret, gro, db, 1, dseq)
    oret_s = oret_s.reshape(db, RET_CHUNK, 512)[:, :dseq].reshape(n_s, 512)

    mem_kv_p = _mem_kv(mem_prompt.reshape(nb * mem_len, D_MODEL), r1(g_mem), w_mem_kv.astype(BF16), r1(g_k_mem),
                       _pick_tile(nb * mem_len, 256))
    tq = _pick_tile(seq, 512)
    omem = _mem_attend(mq, mem_kv_p, nb, seq // tq, tq, mem_len, n_p, BF16)
    omem_s = _mem_attend_cache(mq[n_p:].astype(F32), cache_mem.reshape(db, mem_len * 2 * MEM_HEADS, MEM_HD),
                               dseq).astype(BF16)

    wn = w_br_nsa.reshape(NSA_HEADS, NSA_HD, D_MODEL)
    zn = jnp.zeros_like(wn)
    wn_pad = jnp.concatenate(
        [jnp.concatenate([wn[hh], zn[hh]] if hh < NSA_GROUP else [zn[hh], wn[hh]], axis=0) for hh in range(NSA_HEADS)],
        axis=0).astype(BF16)
    wr_pad = jnp.pad(w_router, ((0, 0), (0, LANES - N_EXPERTS)))
    wr_hi = wr_pad.astype(BF16)
    wr_lo = (wr_pad - wr_hi.astype(F32)).astype(BF16)
    brt = jnp.concatenate([b_router, jnp.full((LANES - N_EXPERTS,), NEG, F32)]).reshape(1, LANES)
    h, hn, comb = _merge(xp, xs, onsa, onsa_s, oret, oret_s, omem, omem_s, r1(g_attn), w_bg, wn_pad, w_br_ret.astype(BF16),
                         w_br_mem.astype(BF16), w_out.astype(BF16), r1(g_ffn), jnp.stack([wr_hi, wr_lo]), brt, tm)

    tb = max(c for c in range(LANES, MOE_BLOCK_CAP + 1, LANES) if n_all % c == 0)
    pos, post, cnt = _route(comb, tb)
    cnt = cnt[:, 0, :N_EXPERTS].astype(jnp.int32).reshape(-1)
    y = _moe(cnt, hn, h, comb, pos, post, w_gate_up.astype(BF16), w_down.astype(BF16),
             b_gate_up.reshape(N_EXPERTS, 1, 2 * D_FF), b_down.reshape(N_EXPERTS, 1, D_MODEL), tb)

    kv5 = lambda a, bsz, t: a.reshape(bsz, t, 2, NSA_KV_HEADS, NSA_HD)
    wp = min(WINDOW, seq)
    kv5t = lambda a: jnp.transpose(a.reshape(nb, 2, NSA_KV_HEADS, NSA_HD, a.shape[2]), (0, 4, 1, 2, 3))
    return (y[:n_p].reshape(nb, seq, D_MODEL), y[n_p:].reshape(db, dseq, D_MODEL),
            kv5t(kvct), kv5t(kvst), kv5t(kvwt[:, :, seq - wp:]),
            ret_state_p, mem_kv_p.reshape(nb, mem_len, 2, MEM_HEADS, MEM_HD),
            kv5(kvc[n_p:], db, dseq), kv5(kvs[n_p:], db, dseq), kv5(new_win_s, db, wb), ret_state_s)
```

```python
import functools
import math

import jax
import jax.numpy as jnp
from jax import lax
from jax.experimental import pallas as pl
from jax.experimental.pallas import tpu as pltpu

F32 = jnp.float32
BF16 = jnp.bfloat16

D_MODEL = 1024
NSA_HEADS = 8
NSA_KV_HEADS = 2
NSA_GROUP = 4
NSA_HD = 64
CMP_BLOCK = 32
CMP_STRIDE = 16
CMP_HIDDEN = 128
SLC_BLOCK = 64
N_SELECT = 16
N_LOCAL_FORCED = 2
FORCE_BONUS = 1.0e4
WINDOW = 512
Q_BLOCK = 128
PAGE_SIZE = 128
RET_HEADS = 4
RET_DK = 64
RET_DV = 128
RET_CHUNK = 128
ROPE_BASE = 10000.0
MEM_HEADS = 4
MEM_HD = 128
N_EXPERTS = 32
TOP_K = 4
D_FF = 1024
SWIGLU_ALPHA = 1.702
SWIGLU_LIMIT = 7.0
EPS = 1e-6
NEG = -1e30

LANES = 128
VMEM_LIMIT = 56 * 1024 * 1024


def _dot(a, b):
    return jnp.dot(a, b, preferred_element_type=F32)


def _dot_nt(a, b):
    return lax.dot_general(a, b, (((1,), (1,)), ((), ())), preferred_element_type=F32)


def _sigmoid(x):
    return 1.0 / (1.0 + jnp.exp(-x))


def _split3(x):
    a = x.astype(BF16)
    r = x - a.astype(F32)
    b = r.astype(BF16)
    c = (r - b.astype(F32)).astype(BF16)
    return a, b, c


def _msoftmax(s, m):
    s = jnp.where(m, s, NEG)
    mx = jnp.max(s, axis=-1, keepdims=True)
    p = jnp.where(m, jnp.exp(s - mx), 0.0)
    return p / jnp.maximum(jnp.sum(p, axis=-1, keepdims=True), 1e-30)


def _halfnorm(t, g2):
    lane = lax.broadcasted_iota(jnp.int32, t.shape, 1)
    lo = lane < NSA_HD
    t2 = t * t
    s0 = jnp.sum(jnp.where(lo, t2, 0.0), axis=-1, keepdims=True)
    s1 = jnp.sum(jnp.where(lo, 0.0, t2), axis=-1, keepdims=True)
    r = jnp.where(lo, lax.rsqrt(s0 * (1.0 / NSA_HD) + EPS), lax.rsqrt(s1 * (1.0 / NSA_HD) + EPS))
    return t * r * g2


def _tilenorm(t, g, width):
    return t * lax.rsqrt(jnp.sum(t * t, axis=-1, keepdims=True) * (1.0 / width) + EPS) * g


def _rot(x, cos, sin):
    lane = lax.broadcasted_iota(jnp.int32, x.shape, 1)
    first = (lane % RET_DK) < (RET_DK // 2)
    n = x.shape[1]
    sw = jnp.where(first, pltpu.roll(x, n - RET_DK // 2, 1), pltpu.roll(x, RET_DK // 2, 1))
    return x * cos + sw * sin


C_Q, C_KV, C_NG, C_RET, C_MQ = 1024, 768, 128, 1536, 512
O_KV = C_Q
O_NG = O_KV + C_KV
O_RET = O_NG + C_NG
O_MQ = O_RET + C_RET
W1_COLS = O_MQ + C_MQ


def _proj_kernel(nbp, xp_ref, xs_ref, g_ref, w_ref, cos_ref, sin_ref, gq_ref, gks_ref, gkw_ref, gqm_ref,
                 q_ref, kvc_ref, kvs_ref, kvw_ref, ng_ref, rq_ref, rk_ref, rv_ref, rg_ref, mq_ref,
                 kvct_ref, kvst_ref, kvwt_ref):
    i = pl.program_id(0)
    x = jnp.where(i < nbp, xp_ref[...], xs_ref[...])
    xn = (x * lax.rsqrt(jnp.mean(x * x, axis=-1, keepdims=True) + EPS) * g_ref[...]).astype(BF16)

    hq = _dot(xn, w_ref[:, 0:C_Q])
    for hh in range(NSA_HEADS):
        t = hq[:, LANES * hh:LANES * (hh + 1)]
        q_ref[:, LANES * hh:LANES * (hh + 1)] = (_tilenorm(t, gq_ref[...], NSA_HD) * NSA_HD ** -0.5).astype(BF16)

    hkv = _dot(xn, w_ref[:, O_KV:O_KV + C_KV])
    kvc = hkv[:, 0:256]
    kvs = jnp.concatenate([_halfnorm(hkv[:, 256:384], gks_ref[...]), hkv[:, 384:512]], axis=1)
    kvw = jnp.concatenate([_halfnorm(hkv[:, 512:640], gkw_ref[...]), hkv[:, 640:768]], axis=1)
    kvc_ref[...] = kvc
    kvs_ref[...] = kvs
    kvw_ref[...] = kvw

    @pl.when(i < nbp)
    def _():
        kvct_ref[0] = kvc.T
        kvst_ref[0] = kvs.T
        kvwt_ref[0] = kvw.T

    ng_ref[...] = _sigmoid(_dot(xn, w_ref[:, O_NG:O_NG + C_NG]))

    hr = _dot(xn, w_ref[:, O_RET:O_RET + C_RET])
    cos = cos_ref[...]
    sin = sin_ref[...]
    rq_ref[...] = _rot(hr[:, 0:256], cos, sin)
    rk_ref[...] = _rot(hr[:, 256:512], cos, sin) * RET_DK ** -0.5
    rv_ref[...] = hr[:, 512:1024]
    rg = hr[:, 1024:1536]
    rg_ref[...] = rg * _sigmoid(rg)

    hm = _dot(xn, w_ref[:, O_MQ:O_MQ + C_MQ])
    for hd in range(MEM_HEADS):
        t = hm[:, LANES * hd:LANES * (hd + 1)]
        mq_ref[:, LANES * hd:LANES * (hd + 1)] = (_tilenorm(t, gqm_ref[...], MEM_HD) * MEM_HD ** -0.5).astype(BF16)


def _project(xp, xs, g_attn, w1, cos_t, sin_t, gq2, gks2, gkw2, gqm, seq, tm):
    n_p, n_s = xp.shape[0], xs.shape[0]
    nbp, nbs = n_p // tm, n_s // tm
    n_all = n_p + n_s
    bps = seq // tm
    row = lambda w: pl.BlockSpec((tm, w), lambda i: (i, 0))
    const = lambda a: pl.BlockSpec(a.shape, lambda i: (0,) * a.ndim)
    rope = pl.BlockSpec((tm, 256), lambda i: (jnp.where(i < nbp, i % bps, bps + i - nbp), 0))
    widths = (1024, 256, 256, 256, 128, 256, 256, 512, 512, 512)
    dtypes = (BF16, F32, F32, F32, F32, F32, F32, F32, F32, BF16)
    kvt = pl.BlockSpec((1, 256, tm), lambda i: (jnp.minimum(i, nbp - 1) // bps, 0, jnp.minimum(i, nbp - 1) % bps))
    return pl.pallas_call(
        functools.partial(_proj_kernel, nbp),
        grid=(nbp + nbs,),
        in_specs=[pl.BlockSpec((tm, D_MODEL), lambda i: (jnp.minimum(i, nbp - 1), 0)),
                  pl.BlockSpec((tm, D_MODEL), lambda i: (jnp.maximum(i - nbp, 0), 0)),
                  const(g_attn), const(w1), rope, rope, const(gq2), const(gks2), const(gkw2), const(gqm)],
        out_specs=[row(w) for w in widths] + [kvt] * 3,
        out_shape=[jax.ShapeDtypeStruct((n_all, w), d) for w, d in zip(widths, dtypes)]
        + [jax.ShapeDtypeStruct((n_p // seq, 256, seq), F32)] * 3,
        compiler_params=pltpu.CompilerParams(dimension_semantics=("arbitrary",), vmem_limit_bytes=VMEM_LIMIT),
        name="proj",
    )(xp, xs, g_attn, w1, cos_t, sin_t, gq2, gks2, gkw2, gqm)


CHUNK_W = CMP_STRIDE * 256
CHUNKS_PER_PAGE = PAGE_SIZE // CMP_STRIDE


def _compress_kernel(n_pages, pt_ref, pool_ref, wbig_ref, pe_ref, w2k_ref, w2v_ref, gk_ref,
                     ck_ref, cv_ref, x_scr, r_scr, sem):
    b = pl.program_id(0)
    n = n_pages * CHUNKS_PER_PAGE

    def page_copy(j):
        return pltpu.make_async_copy(pool_ref.at[pt_ref[b, j]],
                                     x_scr.at[pl.ds(CHUNKS_PER_PAGE * j, CHUNKS_PER_PAGE)], sem)

    for j in range(n_pages):
        page_copy(j).start()
    x_scr[pl.ds(n, 8), :] = pe_ref[...]
    for j in range(n_pages):
        page_copy(j).wait()

    r_scr[...] = _dot(x_scr[...].astype(BF16), wbig_ref[...])
    cvec = r_scr[n:n + 1, 0:512] + r_scr[n + 1:n + 2, 512:1024]
    hid = r_scr[0:n, 0:512] + r_scr[pl.ds(1, n), 512:1024] + cvec
    hb = (hid * _sigmoid(hid)).astype(BF16)
    ck_ref[0] = _halfnorm(_dot(hb[:, 0:256], w2k_ref[...]), gk_ref[...])
    cv_ref[0] = _dot(hb[:, 256:512], w2v_ref[...])


def _compress(page_table, pool, wbig, pe2, w2k, w2v, gk2):
    nb, n_pages = page_table.shape
    n = n_pages * CHUNKS_PER_PAGE
    const = lambda a: pl.BlockSpec(a.shape, lambda b, pt: (0,) * a.ndim)
    out = pl.BlockSpec((1, n, LANES), lambda b, pt: (b, 0, 0))
    return pl.pallas_call(
        functools.partial(_compress_kernel, n_pages),
        grid_spec=pltpu.PrefetchScalarGridSpec(
            num_scalar_prefetch=1, grid=(nb,),
            in_specs=[pl.BlockSpec(memory_space=pl.ANY), const(wbig), const(pe2), const(w2k), const(w2v), const(gk2)],
            out_specs=[out, out],
            scratch_shapes=[pltpu.VMEM((n + 8, CHUNK_W), F32), pltpu.VMEM((n + 8, 1024), F32),
                            pltpu.SemaphoreType.DMA(())]),
        out_shape=[jax.ShapeDtypeStruct((nb, n, LANES), F32)] * 2,
        compiler_params=pltpu.CompilerParams(dimension_semantics=("arbitrary",), vmem_limit_bytes=VMEM_LIMIT),
        name="compress",
    )(page_table, pool, wbig, pe2, w2k, w2v, gk2)


def _compress_t_kernel(n_pages, pt_ref, pool_ref, wbig_ref, pe_ref, w2k_ref, w2v_ref, gk_ref,
                       ck_ref, cv_ref, pg_scr, t_scr, r_scr, cvec_scr, sem):
    b = pl.program_id(0)
    slot = b % 2
    n = n_pages * CHUNKS_PER_PAGE

    def page_copy(bb, sl, j):
        return pltpu.make_async_copy(pool_ref.at[pt_ref[bb, j]], pg_scr.at[sl, j], sem.at[sl])

    @pl.when(b == 0)
    def _():
        for j in range(n_pages):
            page_copy(0, 0, j).start()
        r = _dot(pe_ref[...].astype(BF16), wbig_ref[...])
        cvec_scr[...] = jnp.broadcast_to(r[0:1, 0:512] + r[1:2, 512:1024], cvec_scr.shape)
        r_scr[pl.ds(n, 8), :] = jnp.zeros((8, 1024), F32)

    @pl.when(b + 1 < pl.num_programs(0))
    def _():
        for j in range(n_pages):
            page_copy(b + 1, 1 - slot, j).start()

    for j in range(n_pages):
        page_copy(b, slot, j).wait()

    src = lax.broadcasted_iota(jnp.int32, (PAGE_SIZE, PAGE_SIZE), 0)
    dst = lax.broadcasted_iota(jnp.int32, (PAGE_SIZE, PAGE_SIZE), 1)
    perm = (dst == CHUNKS_PER_PAGE * (src % CMP_STRIDE) + src // CMP_STRIDE).astype(BF16)
    for j in range(n_pages):
        xp = _dot(pg_scr[slot, j].astype(BF16), perm)
        tk = xp[0:LANES, :].T
        tv = xp[LANES:2 * LANES, :].T
        for p in range(CMP_STRIDE):
            rows = pl.ds(CHUNKS_PER_PAGE * j, CHUNKS_PER_PAGE)
            t_scr[p, rows, 0:LANES] = tk[CHUNKS_PER_PAGE * p:CHUNKS_PER_PAGE * (p + 1)]
            t_scr[p, rows, LANES:2 * LANES] = tv[CHUNKS_PER_PAGE * p:CHUNKS_PER_PAGE * (p + 1)]

    acc = jnp.zeros((n, 1024), F32)
    for p in range(CMP_STRIDE):
        acc = acc + _dot(t_scr[p].astype(BF16), wbig_ref[256 * p:256 * (p + 1), :])
    r_scr[pl.ds(0, n), :] = acc
    hid = acc[:, 0:512] + r_scr[pl.ds(1, n), 512:1024] + cvec_scr[0:1, :]
    hb = (hid * _sigmoid(hid)).astype(BF16)
    ck_ref[0] = _halfnorm(_dot(hb[:, 0:256], w2k_ref[...]), gk_ref[...])
    cv_ref[0] = _dot(hb[:, 256:512], w2v_ref[...])


def _compress_t(page_table, pool_t, wbig, pe2, w2k, w2v, gk2):
    nb, n_pages = page_table.shape
    n = n_pages * CHUNKS_PER_PAGE
    const = lambda a: pl.BlockSpec(a.shape, lambda b, pt: (0,) * a.ndim)
    out = pl.BlockSpec((1, n, LANES), lambda b, pt: (b, 0, 0))
    return pl.pallas_call(
        functools.partial(_compress_t_kernel, n_pages),
        grid_spec=pltpu.PrefetchScalarGridSpec(
            num_scalar_prefetch=1, grid=(nb,),
            in_specs=[pl.BlockSpec(memory_space=pl.ANY), const(wbig), const(pe2), const(w2k), const(w2v), const(gk2)],
            out_specs=[out, out],
            scratch_shapes=[pltpu.VMEM((2, n_pages, 256, PAGE_SIZE), F32), pltpu.VMEM((CMP_STRIDE, n, 2 * LANES), F32),
                            pltpu.VMEM((n + 8, 1024), F32), pltpu.VMEM((8, 512), F32),
                            pltpu.SemaphoreType.DMA((2,))]),
        out_shape=[jax.ShapeDtypeStruct((nb, n, LANES), F32)] * 2,
        compiler_params=pltpu.CompilerParams(dimension_semantics=("arbitrary",), vmem_limit_bytes=VMEM_LIMIT),
        name="compress_t",
    )(page_table, pool_t, wbig, pe2, w2k, w2v, gk2)


def _overlap_t(ns_rows, n_cmp_cols, n_cmp):
    s = lax.broadcasted_iota(jnp.int32, (ns_rows, n_cmp_cols), 0)
    n = lax.broadcasted_iota(jnp.int32, (ns_rows, n_cmp_cols), 1)
    ov = (n * CMP_STRIDE < s * SLC_BLOCK + SLC_BLOCK) & (n * CMP_STRIDE + CMP_BLOCK > s * SLC_BLOCK) & (n < n_cmp)
    return ov.astype(BF16)


def _importance_t(ov_t, pcs):
    a, b, c = _split3(pcs)
    return _dot_nt(ov_t, a) + _dot_nt(ov_t, b) + _dot_nt(ov_t, c)


def _block_scores_t(imp_t, tl):
    blk = lax.broadcasted_iota(jnp.int32, imp_t.shape, 0)
    cur = tl // SLC_BLOCK
    valid = blk <= cur
    forced = (blk == 0) | (valid & (blk > cur - N_LOCAL_FORCED))
    return jnp.where(valid, imp_t + jnp.where(forced, FORCE_BONUS, 0.0), NEG), blk


SLC_CHUNK = 1024
RANK_ACCS = 4


def _nsa_prompt_kernel(seq, n_cmp, q_ref, kvs_ref, kvw_ref, ck_ref, cv_ref, ng_ref, et_ref, eg_ref, o_ref):
    i = pl.program_id(1)
    ns = seq // SLC_BLOCK
    nh = NSA_HEADS
    q8 = q_ref[...]
    qs = jnp.concatenate([q8[:, LANES * hh:LANES * (hh + 1)] for hh in range(nh)], axis=0)
    tq = i * Q_BLOCK + lax.broadcasted_iota(jnp.int32, (Q_BLOCK, 1), 0)
    rep = lambda a: jnp.concatenate([a] * nh, axis=0)
    t8 = rep(tq)

    ckb = ck_ref[0].astype(BF16)
    sc = _dot_nt(qs, ckb)
    jn = lax.broadcasted_iota(jnp.int32, sc.shape, 1)
    pc = _msoftmax(sc, (jn * CMP_STRIDE + CMP_BLOCK - 1) <= t8)
    o_cmp = _dot(pc.astype(BF16), cv_ref[0].astype(BF16))

    ov_t = _overlap_t(ns, pc.shape[1], n_cmp)
    tl = i * Q_BLOCK + lax.broadcasted_iota(jnp.int32, (ns, Q_BLOCK), 1)
    eye = (lax.broadcasted_iota(jnp.int32, (Q_BLOCK, Q_BLOCK), 0)
           == lax.broadcasted_iota(jnp.int32, (Q_BLOCK, Q_BLOCK), 1)).astype(BF16)
    pens = []
    for h in range(NSA_KV_HEADS):
        r0 = h * NSA_GROUP * Q_BLOCK
        pcs = pc[r0:r0 + 128] + pc[r0 + 128:r0 + 256] + pc[r0 + 256:r0 + 384] + pc[r0 + 384:r0 + 512]
        score, blk = _block_scores_t(_importance_t(ov_t, pcs), tl)
        ranks = [jnp.zeros(score.shape, F32) for _ in range(RANK_ACCS)]
        for k in range(ns):
            sk = score[k:k + 1, :]
            ranks[k % RANK_ACCS] += jnp.where((sk > score) | ((sk == score) & (blk > k)), 1.0, 0.0)
        rank = functools.reduce(lambda x, y: x + y, ranks)
        sel_t = jnp.where(rank < min(N_SELECT, ns), 1.0, 0.0).astype(BF16)
        if ns < LANES:
            sel_t = jnp.concatenate([sel_t, jnp.zeros((LANES - ns, Q_BLOCK), BF16)], axis=0)
        pen = ((_dot_nt(eye, sel_t) - 1.0) * -NEG).astype(BF16)
        pens += [pen] * NSA_GROUP

    qx = jnp.concatenate([qs, jnp.concatenate(pens, axis=0)], axis=1)
    kc = min(SLC_CHUNK, seq)

    def chunk(c, carry, bias):
        m, l, acc = carry
        k0 = pl.multiple_of(c * kc, kc)
        kx = jnp.concatenate([kvs_ref[pl.ds(k0, kc), 0:128].astype(BF16), et_ref[pl.ds(k0, kc), :]], axis=1)
        vv = kvs_ref[pl.ds(k0, kc), 128:256].astype(BF16)
        s = _dot_nt(qx, kx)
        if bias is not None:
            s = s + bias
        m_new = jnp.maximum(m, jnp.max(s, axis=-1, keepdims=True))
        alpha = jnp.exp(m - m_new)
        p = jnp.exp(s - m_new)
        l = alpha * l + jnp.sum(p, axis=-1, keepdims=True)
        acc = alpha * acc + _dot(p.astype(BF16), vv)
        return m_new, l, acc

    rows = nh * Q_BLOCK
    init = (jnp.full((rows, 1), NEG, F32), jnp.zeros((rows, 1), F32), jnp.zeros((rows, LANES), F32))
    c_last = (i * Q_BLOCK) // kc
    carry = lax.fori_loop(0, c_last, lambda c, cr: chunk(c, cr, None), init)
    kpos = c_last * kc + lax.broadcasted_iota(jnp.int32, (Q_BLOCK, kc), 1)
    _, l, acc = chunk(c_last, carry, rep(jnp.where(kpos <= tq, 0.0, NEG)))
    o_slc = acc / l

    wk = WINDOW + Q_BLOCK
    start = pl.multiple_of(Q_BLOCK * jnp.maximum(i - WINDOW // Q_BLOCK, 0), Q_BLOCK)
    kk = kvw_ref[pl.ds(start, wk), 0:128].astype(BF16)
    vx = jnp.concatenate([kvw_ref[pl.ds(start, wk), 128:256].astype(BF16), jnp.ones((wk, LANES), BF16)], axis=1)
    d = tq - (start + lax.broadcasted_iota(jnp.int32, (Q_BLOCK, wk), 1))
    sw = _dot_nt(qs, kk) + rep(jnp.where((d >= 0) & (d < WINDOW), 0.0, NEG))
    aw = _dot(jnp.exp(sw - jnp.max(sw, axis=-1, keepdims=True)).astype(BF16), vx)
    o_win = aw[:, 0:LANES] / aw[:, LANES:LANES + 1]

    a, b, c = _split3(ng_ref[...])
    for h in range(NSA_KV_HEADS):
        eg = eg_ref[h]
        gexp = _dot(a, eg) + _dot(b, eg) + _dot(c, eg)
        for g in range(NSA_GROUP):
            hh = NSA_GROUP * h + g
            rs = slice(Q_BLOCK * hh, Q_BLOCK * (hh + 1))
            gate = lambda br: gexp[:, LANES * (NSA_GROUP * br + g):LANES * (NSA_GROUP * br + g + 1)]
            o = gate(0) * o_cmp[rs] + gate(1) * o_slc[rs] + gate(2) * o_win[rs]
            o_ref[:, LANES * hh:LANES * (hh + 1)] = o.astype(BF16)


def _nsa_prompt(q, kvs, kvw, ck, cv, ng, nb, seq, n_cmp):
    nq = seq // Q_BLOCK
    ncp = ck.shape[1]
    ns = seq // SLC_BLOCK
    assert ns <= LANES and seq >= WINDOW + Q_BLOCK
    et = (jnp.arange(seq)[:, None] // SLC_BLOCK == jnp.arange(LANES)[None, :]).astype(BF16)
    n_g = 3 * NSA_GROUP
    col = jnp.arange(n_g * LANES)[None, None, :] // LANES
    src = (col // NSA_GROUP) * NSA_HEADS + NSA_GROUP * jnp.arange(NSA_KV_HEADS)[:, None, None] + col % NSA_GROUP
    eg = (jnp.arange(LANES)[None, :, None] == src).astype(BF16)
    return pl.pallas_call(
        functools.partial(_nsa_prompt_kernel, seq, n_cmp),
        grid=(nb, nq),
        in_specs=[pl.BlockSpec((Q_BLOCK, 1024), lambda b, i: (b * nq + i, 0)),
                  pl.BlockSpec((seq, 256), lambda b, i: (b, 0)),
                  pl.BlockSpec((seq, 256), lambda b, i: (b, 0)),
                  pl.BlockSpec((1, ncp, LANES), lambda b, i: (b, 0, 0)),
                  pl.BlockSpec((1, ncp, LANES), lambda b, i: (b, 0, 0)),
                  pl.BlockSpec((Q_BLOCK, LANES), lambda b, i: (b * nq + i, 0)),
                  pl.BlockSpec((seq, LANES), lambda b, i: (0, 0)),
                  pl.BlockSpec((NSA_KV_HEADS, LANES, n_g * LANES), lambda b, i: (0, 0, 0))],
        out_specs=pl.BlockSpec((Q_BLOCK, 1024), lambda b, i: (b * nq + i, 0)),
        out_shape=jax.ShapeDtypeStruct((nb * seq, 1024), BF16),
        compiler_params=pltpu.CompilerParams(dimension_semantics=("arbitrary",) * 2, vmem_limit_bytes=VMEM_LIMIT),
        name="nsa_prompt",
    )(q, kvs, kvw, ck, cv, ng, et, eg)


def _nsa_sample_kernel(past, n_pages, dseq, wb, pt_ref, q_ref, kvs_ref, kvw_ref, ck_ref, cv_ref, ng_ref, cwin_ref,
                       e_ref, pool_ref, o_ref, nwin_ref, kvt_scr, kw_scr, sem):
    b = pl.program_id(0)
    slot = b % 2

    def page_copy(bb, sl, j):
        return pltpu.make_async_copy(pool_ref.at[pt_ref[bb, j]], kvt_scr.at[sl, :, pl.ds(PAGE_SIZE * j, PAGE_SIZE)],
                                     sem.at[sl])

    @pl.when(b == 0)
    def _():
        for j in range(n_pages):
            page_copy(0, 0, j).start()

    @pl.when(b + 1 < pl.num_programs(0))
    def _():
        for j in range(n_pages):
            page_copy(b + 1, 1 - slot, j).start()

    n_heads = NSA_KV_HEADS * NSA_GROUP
    rows = n_heads * dseq
    kpad = past + PAGE_SIZE
    ns = (past + dseq + SLC_BLOCK - 1) // SLC_BLOCK
    nsp = e_ref.shape[0]
    qf = q_ref[0].astype(F32)
    qs = jnp.concatenate([qf[:, LANES * hh:LANES * (hh + 1)] for hh in range(n_heads)], axis=0).astype(BF16)
    t1 = past + lax.broadcasted_iota(jnp.int32, (dseq, 1), 0)
    t = jnp.concatenate([t1] * n_heads, axis=0)

    sc = _dot_nt(qs, ck_ref[0].astype(BF16))
    jn = lax.broadcasted_iota(jnp.int32, sc.shape, 1)
    pc = _msoftmax(sc, (jn * CMP_STRIDE + CMP_BLOCK - 1) <= t)
    o_cmp = _dot(pc.astype(BF16), cv_ref[0].astype(BF16))

    per_kv = []
    for kvh in range(NSA_KV_HEADS):
        base = kvh * NSA_GROUP * dseq
        s = pc[base:base + dseq]
        for g in range(1, NSA_GROUP):
            s = s + pc[base + g * dseq:base + (g + 1) * dseq]
        per_kv += [s] * NSA_GROUP
    pcs = jnp.concatenate(per_kv, axis=0)
    n_cmp = (ns * SLC_BLOCK) // CMP_STRIDE - CMP_BLOCK // CMP_STRIDE + 1
    imp_t = _importance_t(_overlap_t(nsp, pcs.shape[1], n_cmp), pcs)
    tl = past + lax.broadcasted_iota(jnp.int32, (nsp, rows), 1) % dseq
    score, blk = _block_scores_t(imp_t, tl)
    ranks = [jnp.zeros(score.shape, F32) for _ in range(RANK_ACCS)]
    for k in range(ns):
        sk = score[k:k + 1, :]
        ranks[k % RANK_ACCS] += jnp.where((sk > score) | ((sk == score) & (blk > k)), 1.0, 0.0)
    rank = functools.reduce(lambda x, y: x + y, ranks)
    sel_t = jnp.where(rank < min(N_SELECT, ns), 1.0, 0.0).astype(BF16)
    eye = (lax.broadcasted_iota(jnp.int32, (rows, rows), 0)
           == lax.broadcasted_iota(jnp.int32, (rows, rows), 1)).astype(BF16)
    sel = _dot_nt(eye, sel_t)
    sele = _dot(sel.astype(BF16), e_ref[...])

    knew = jnp.concatenate([kvs_ref[0], jnp.zeros((PAGE_SIZE - dseq, 256), F32)], axis=0).astype(BF16)
    for j in range(n_pages):
        page_copy(b, slot, j).wait()
    s = jnp.concatenate([_dot(qs, kvt_scr[slot, 0:128, :].astype(BF16)), _dot_nt(qs, knew[:, 0:128])], axis=1)
    kpos = lax.broadcasted_iota(jnp.int32, (rows, kpad), 1)
    p = _msoftmax(s, (sele > 0.5) & (kpos <= t)).astype(BF16)
    o_slc = _dot_nt(p[:, 0:past], kvt_scr[slot, 128:256, :].astype(BF16)) + _dot(p[:, past:kpad], knew[:, 128:256])

    wpad = kw_scr.shape[0]
    kw_scr[pl.ds(0, wb), :] = cwin_ref[0]
    kw_scr[pl.ds(wb, dseq), :] = kvw_ref[0]
    kw_scr[pl.ds(wb + dseq, wpad - wb - dseq), :] = jnp.zeros((wpad - wb - dseq, 256), F32)
    sw = _dot_nt(qs, kw_scr[:, 0:128].astype(BF16))
    d = t - (past - wb + lax.broadcasted_iota(jnp.int32, (rows, wpad), 1))
    pw = _msoftmax(sw, (d >= 0) & (d < WINDOW))
    o_win = _dot(pw.astype(BF16), kw_scr[:, 128:256].astype(BF16))
    nwin_ref[0, pl.ds(0, wb - dseq), :] = cwin_ref[0, pl.ds(dseq, wb - dseq), :]
    nwin_ref[0, pl.ds(wb - dseq, dseq), :] = kvw_ref[0]

    ngv = ng_ref[0]
    for hh in range(n_heads):
        rs = slice(dseq * hh, dseq * (hh + 1))
        gate = lambda br: ngv[:, br * NSA_HEADS + hh:br * NSA_HEADS + hh + 1]
        o_ref[0, :, LANES * hh:LANES * (hh + 1)] = gate(0) * o_cmp[rs] + gate(1) * o_slc[rs] + gate(2) * o_win[rs]


def _nsa_sample(page_table, q, kvs, kvw, ck, cv, ng, cwin, e_s, pool, past):
    nb, n_pages = page_table.shape
    dseq = q.shape[1]
    wb = cwin.shape[1]
    kpad = past + PAGE_SIZE
    wpad = wb + PAGE_SIZE
    nsp = e_s.shape[0]
    rows = NSA_HEADS * dseq
    per_b = lambda a: pl.BlockSpec((1,) + a.shape[1:], lambda b, pt: (b,) + (0,) * (a.ndim - 1))
    return pl.pallas_call(
        functools.partial(_nsa_sample_kernel, past, n_pages, dseq, wb),
        grid_spec=pltpu.PrefetchScalarGridSpec(
            num_scalar_prefetch=1, grid=(nb,),
            in_specs=[per_b(q), per_b(kvs), per_b(kvw), per_b(ck), per_b(cv), per_b(ng), per_b(cwin),
                      pl.BlockSpec(e_s.shape, lambda b, pt: (0, 0)), pl.BlockSpec(memory_space=pl.ANY)],
            out_specs=[pl.BlockSpec((1, dseq, 1024), lambda b, pt: (b, 0, 0)),
                       pl.BlockSpec((1, wb, 256), lambda b, pt: (b, 0, 0))],
            scratch_shapes=[pltpu.VMEM((2, 256, past), F32), pltpu.VMEM((wpad, 256), F32),
                            pltpu.SemaphoreType.DMA((2,))]),
        out_shape=[jax.ShapeDtypeStruct((nb, dseq, 1024), F32), jax.ShapeDtypeStruct((nb, wb, 256), F32)],
        compiler_params=pltpu.CompilerParams(dimension_semantics=("arbitrary",), vmem_limit_bytes=VMEM_LIMIT),
        name="nsa_sample",
    )(page_table, q, kvs, kvw, ck, cv, ng, cwin, e_s, pool)


RET_GROUP = 4


def _ret_kernel(bg, c_true, *refs):
    rq_refs, rk_refs, rv_refs, rg_refs = (refs[k * bg:(k + 1) * bg] for k in range(4))
    s0_ref, gro_ref, o_ref, sout_ref, s_scr = refs[4 * bg:]
    c = pl.program_id(1)
    cs = RET_CHUNK

    @pl.when(c == 0)
    def _():
        for b in range(bg):
            for tile in range(RET_HEADS // 2):
                s_scr[b, tile] = jnp.concatenate([s0_ref[b, 2 * tile], s0_ref[b, 2 * tile + 1]], axis=0)

    diff = (lax.broadcasted_iota(jnp.int32, (cs, cs), 0) - lax.broadcasted_iota(jnp.int32, (cs, cs), 1)).astype(F32)
    ic = lax.broadcasted_iota(jnp.int32, (cs, 1), 0).astype(F32)
    lane = lax.broadcasted_iota(jnp.int32, (cs, LANES), 1)
    for hd in range(RET_HEADS):
        lg = math.log(1.0 - 2.0 ** (-5.0 - hd))
        tile, half = hd // 2, hd % 2
        in_half = (lane >= RET_DK * half) & (lane < RET_DK * (half + 1))
        dmat = jnp.where(diff >= 0, jnp.exp(jnp.maximum(diff, 0.0) * lg), 0.0)
        dec_q = jnp.exp((ic + 1.0) * lg)
        dec_k = jnp.exp((c_true - 1.0 - ic) * lg)
        for b in range(bg):
            qm = jnp.where(in_half, rq_refs[b][:, LANES * tile:LANES * (tile + 1)], 0.0).astype(BF16)
            k2 = rk_refs[b][:, LANES * tile:LANES * (tile + 1)]
            vb = rv_refs[b][:, LANES * hd:LANES * (hd + 1)].astype(BF16)
            o = _dot((_dot_nt(qm, k2.astype(BF16)) * dmat).astype(BF16), vb)
            st = s_scr[b, tile]
            o = o + _dot(qm, st.astype(BF16)) * dec_q
            kdec_t = (k2 * dec_k).T
            upd = _dot(kdec_t[RET_DK * half:RET_DK * (half + 1)].astype(BF16), vb)
            s_new = st[RET_DK * half:RET_DK * (half + 1)] * math.exp(c_true * lg) + upd
            s_scr[b, tile, pl.ds(RET_DK * half, RET_DK), :] = s_new
            on = _tilenorm(o, gro_ref[...], RET_DV)
            o_ref[b, :, LANES * hd:LANES * (hd + 1)] = (rg_refs[b][:, LANES * hd:LANES * (hd + 1)] * on).astype(BF16)

    @pl.when(c == pl.num_programs(1) - 1)
    def _():
        for b in range(bg):
            for hd in range(RET_HEADS):
                sout_ref[b, hd] = s_scr[b, hd // 2, pl.ds(RET_DK * (hd % 2), RET_DK), :]


def _retention(rq, rk, rv, rg, s0, gro, nb, n_chunks, c_true):
    cs = RET_CHUNK
    bg = math.gcd(nb, RET_GROUP)
    rows = lambda w: [pl.BlockSpec((cs, w), lambda g, c, b=b: ((g * bg + b) * n_chunks + c, 0)) for b in range(bg)]
    state = pl.BlockSpec((bg, RET_HEADS, RET_DK, RET_DV), lambda g, c: (g, 0, 0, 0))
    return pl.pallas_call(
        functools.partial(_ret_kernel, bg, float(c_true)),
        grid=(nb // bg, n_chunks),
        in_specs=rows(256) + rows(256) + rows(512) + rows(512) + [state, pl.BlockSpec(gro.shape, lambda g, c: (0, 0))],
        out_specs=[pl.BlockSpec((bg, cs, 512), lambda g, c: (g, c, 0)), state],
        out_shape=[jax.ShapeDtypeStruct((nb, n_chunks * cs, 512), BF16),
                   jax.ShapeDtypeStruct((nb, RET_HEADS, RET_DK, RET_DV), F32)],
        scratch_shapes=[pltpu.VMEM((bg, RET_HEADS // 2, LANES, RET_DV), F32)],
        compiler_params=pltpu.CompilerParams(dimension_semantics=("arbitrary", "arbitrary")),
        name="retention",
    )(*([rq] * bg + [rk] * bg + [rv] * bg + [rg] * bg), s0, gro)


def _memkv_kernel(m_ref, g_ref, w_ref, gk_ref, o_ref):
    x = m_ref[...]
    xn = (x * lax.rsqrt(jnp.mean(x * x, axis=-1, keepdims=True) + EPS) * g_ref[...]).astype(BF16)
    hk = _dot(xn, w_ref[...])
    half = MEM_HEADS * MEM_HD
    for hd in range(MEM_HEADS):
        o_ref[:, LANES * hd:LANES * (hd + 1)] = _tilenorm(hk[:, LANES * hd:LANES * (hd + 1)], gk_ref[...], MEM_HD)
    o_ref[:, half:2 * half] = hk[:, half:2 * half]


def _mem_kv(mem2d, g_mem, w_mem, gk_mem, tm):
    n = mem2d.shape[0]
    const = lambda a: pl.BlockSpec(a.shape, lambda i: (0,) * a.ndim)
    return pl.pallas_call(
        _memkv_kernel,
        grid=(n // tm,),
        in_specs=[pl.BlockSpec((tm, D_MODEL), lambda i: (i, 0)), const(g_mem), const(w_mem), const(gk_mem)],
        out_specs=pl.BlockSpec((tm, 1024), lambda i: (i, 0)),
        out_shape=jax.ShapeDtypeStruct((n, 1024), F32),
        compiler_params=pltpu.CompilerParams(dimension_semantics=("arbitrary",)),
        name="mem_kv",
    )(mem2d, g_mem, w_mem, gk_mem)


def _memattn_kernel(q_ref, kv_ref, o_ref):
    half = MEM_HEADS * MEM_HD
    for hd in range(MEM_HEADS):
        cols = slice(LANES * hd, LANES * (hd + 1))
        s = _dot_nt(q_ref[:, cols].astype(BF16), kv_ref[:, cols].astype(BF16))
        p = jnp.exp(s - jnp.max(s, axis=-1, keepdims=True))
        p = p / jnp.sum(p, axis=-1, keepdims=True)
        o = _dot(p.astype(BF16), kv_ref[:, half + LANES * hd:half + LANES * (hd + 1)].astype(BF16))
        o_ref[:, cols] = o.astype(o_ref.dtype)


def _memattn_cache_kernel(q_ref, kv_ref, o_ref):
    stride = 2 * MEM_HEADS
    mem_len = kv_ref.shape[1] // stride
    for hd in range(MEM_HEADS):
        cols = slice(LANES * hd, LANES * (hd + 1))
        k = kv_ref[0, pl.ds(hd, mem_len, stride=stride), :].astype(BF16)
        v = kv_ref[0, pl.ds(MEM_HEADS + hd, mem_len, stride=stride), :].astype(BF16)
        s = _dot_nt(q_ref[:, cols].astype(BF16), k)
        p = jnp.exp(s - jnp.max(s, axis=-1, keepdims=True))
        p = p / jnp.sum(p, axis=-1, keepdims=True)
        o_ref[:, cols] = _dot(p.astype(BF16), v).astype(o_ref.dtype)


def _mem_attend_cache(q2d, cache_rows, dseq):
    nb, n_rows, _ = cache_rows.shape
    return pl.pallas_call(
        _memattn_cache_kernel,
        grid=(nb,),
        in_specs=[pl.BlockSpec((dseq, 512), lambda b: (b, 0)), pl.BlockSpec((1, n_rows, LANES), lambda b: (b, 0, 0))],
        out_specs=pl.BlockSpec((dseq, 512), lambda b: (b, 0)),
        out_shape=jax.ShapeDtypeStruct((nb * dseq, 512), F32),
        compiler_params=pltpu.CompilerParams(dimension_semantics=("arbitrary",)),
        name="mem_attend_cache",
    )(q2d, cache_rows)


def _mem_attend(q2d, kv2d, nb, blocks_per_b, tq, mem_len, out_rows, out_dtype):
    return pl.pallas_call(
        _memattn_kernel,
        grid=(nb, blocks_per_b),
        in_specs=[pl.BlockSpec((tq, 512), lambda b, i: (b * blocks_per_b + i, 0)),
                  pl.BlockSpec((mem_len, 1024), lambda b, i: (b, 0))],
        out_specs=pl.BlockSpec((tq, 512), lambda b, i: (b * blocks_per_b + i, 0)),
        out_shape=jax.ShapeDtypeStruct((out_rows, 512), out_dtype),
        compiler_params=pltpu.CompilerParams(dimension_semantics=("arbitrary", "arbitrary")),
        name="mem_attend",
    )(q2d, kv2d)


def _merge_kernel(nbp, xp_ref, xs_ref, onsap_ref, onsas_ref, oretp_ref, orets_ref, omemp_ref, omems_ref, g_ref,
                  wbg_ref, wn_ref, wr_ref, wm_ref, wo_ref, gffn_ref, wrt_ref, brt_ref, h_ref, hn_ref, comb_ref):
    i = pl.program_id(0)
    pick = lambda p_ref, s_ref: jnp.where(i < nbp, p_ref[...], s_ref[...])
    x = pick(xp_ref, xs_ref)
    xn = (x * lax.rsqrt(jnp.mean(x * x, axis=-1, keepdims=True) + EPS) * g_ref[...]).astype(BF16)
    bg = _sigmoid(_dot(xn, wbg_ref[...]))
    mixed = (bg[:, 0:1024] * _dot(pick(onsap_ref, onsas_ref), wn_ref[...])
             + bg[:, 1024:2048] * _dot(pick(oretp_ref, orets_ref), wr_ref[...])
             + bg[:, 2048:3072] * _dot(pick(omemp_ref, omems_ref), wm_ref[...]))
    hres = x + _dot(mixed.astype(BF16), wo_ref[...])
    h_ref[...] = hres
    hn = hres * lax.rsqrt(jnp.mean(hres * hres, axis=-1, keepdims=True) + EPS) * gffn_ref[...]
    hn_ref[...] = hn.astype(BF16)

    a, b, _ = _split3(hn)
    logits = _dot(a, wrt_ref[0]) + _dot(a, wrt_ref[1]) + _dot(b, wrt_ref[0]) + brt_ref[...]
    lane = lax.broadcasted_iota(jnp.int32, logits.shape, 1).astype(F32)
    vals, hots = [], []
    for _k in range(TOP_K):
        mx = jnp.max(logits, axis=-1, keepdims=True)
        idx = jnp.min(jnp.where(logits == mx, lane, float(LANES)), axis=-1, keepdims=True)
        hot = lane == idx
        vals.append(mx)
        hots.append(hot)
        logits = jnp.where(hot, -3.0e38, logits)
    es = [jnp.exp(v - vals[0]) for v in vals]
    den = es[0] + es[1] + es[2] + es[3]
    comb = jnp.zeros(logits.shape, F32)
    for e_k, hot in zip(es, hots):
        comb = comb + jnp.where(hot, e_k / den, 0.0)
    comb_ref[...] = comb


def _merge(xp, xs, onsa, onsa_s, oret, oret_s, omem, omem_s, g_attn, wbg, wn, wr, wm, wo, gffn, wrt, brt, tm):
    n_p, n_s = xp.shape[0], xs.shape[0]
    nbp, nbs = n_p // tm, n_s // tm
    n_all = n_p + n_s
    row = lambda w: pl.BlockSpec((tm, w), lambda i: (i, 0))
    rowp = lambda w: pl.BlockSpec((tm, w), lambda i: (jnp.minimum(i, nbp - 1), 0))
    rows = lambda w: pl.BlockSpec((tm, w), lambda i: (jnp.maximum(i - nbp, 0), 0))
    const = lambda a: pl.BlockSpec(a.shape, lambda i: (0,) * a.ndim)
    return pl.pallas_call(
        functools.partial(_merge_kernel, nbp),
        grid=(nbp + nbs,),
        in_specs=[rowp(1024), rows(1024), rowp(1024), rows(1024), rowp(512), rows(512), rowp(512), rows(512),
                  const(g_attn), const(wbg), const(wn), const(wr), const(wm),
                  const(wo), const(gffn), const(wrt), const(brt)],
        out_specs=[row(1024), row(1024), row(LANES)],
        out_shape=[jax.ShapeDtypeStruct((n_all, 1024), F32), jax.ShapeDtypeStruct((n_all, 1024), BF16),
                   jax.ShapeDtypeStruct((n_all, LANES), F32)],
        compiler_params=pltpu.CompilerParams(dimension_semantics=("arbitrary",), vmem_limit_bytes=VMEM_LIMIT),
        name="merge",
    )(xp, xs, onsa, onsa_s, oret, oret_s, omem, omem_s, g_attn, wbg, wn, wr, wm, wo, gffn, wrt, brt)


MOE_BLOCK_CAP = 1664
MOE_CHUNK = 256
ROUTE_TILE = 128


def _route_kernel(comb_ref, pos_ref, post_ref, cnt_ref):
    tb = comb_ref.shape[0]
    rt = ROUTE_TILE
    r = lax.broadcasted_iota(jnp.int32, (rt, rt), 0)
    c = lax.broadcasted_iota(jnp.int32, (rt, rt), 1)
    ltri = (c < r).astype(BF16)
    eye = (c == r).astype(BF16)
    carry = jnp.zeros((1, LANES), F32)
    for s in range(tb // rt):
        comb = comb_ref[pl.ds(rt * s, rt), :]
        sel = jnp.where(comb > 0.0, 1.0, 0.0)
        pos = jnp.where(sel > 0.0, _dot(ltri, sel.astype(BF16)) + carry, -1.0)
        carry = carry + jnp.sum(sel, axis=0, keepdims=True)
        pos_ref[pl.ds(rt * s, rt), :] = jnp.where(sel > 0.0, pos + 0.5 * comb, -1.0)
        hi = jnp.floor((pos + 1.0) * (1.0 / 256.0))
        lo = (pos + 1.0) - 256.0 * hi
        post_ref[:, rt * s:rt * (s + 1)] = 256.0 * _dot_nt(eye, hi.astype(BF16)) + _dot_nt(eye, lo.astype(BF16)) - 1.0
    cnt_ref[0] = jnp.broadcast_to(carry, (8, LANES))


def _route(comb, tb):
    n_all = comb.shape[0]
    nblk = n_all // tb
    return pl.pallas_call(
        _route_kernel,
        grid=(nblk,),
        in_specs=[pl.BlockSpec((tb, LANES), lambda i: (i, 0))],
        out_specs=[pl.BlockSpec((tb, LANES), lambda i: (i, 0)), pl.BlockSpec((LANES, tb), lambda i: (0, i)),
                   pl.BlockSpec((1, 8, LANES), lambda i: (i, 0, 0))],
        out_shape=[jax.ShapeDtypeStruct((n_all, LANES), F32), jax.ShapeDtypeStruct((LANES, n_all), F32),
                   jax.ShapeDtypeStruct((nblk, 8, LANES), F32)],
        compiler_params=pltpu.CompilerParams(dimension_semantics=("arbitrary",)),
        name="route",
    )(comb)


def _moe_kernel(cnt_ref, hn_ref, h_ref, pos_ref, post_ref, wgu_ref, wd_ref, bgu_ref, bd_ref, y_ref):
    i = pl.program_id(0)
    e = pl.program_id(1)
    tb = hn_ref.shape[0]
    ch = MOE_CHUNK

    @pl.when(e == 0)
    def _():
        y_ref[...] = h_ref[...]

    n_rows = cnt_ref[i * N_EXPERTS + e]
    lane = lax.broadcasted_iota(jnp.int32, (tb, LANES), 1)
    packed = jnp.sum(jnp.where(lane == e, pos_ref[...], 0.0), axis=-1, keepdims=True)
    pcol = jnp.floor(packed)
    ccol = 2.0 * (packed - pcol)
    prow = post_ref[pl.ds(e, 1), :]

    def chunk(c, carry):
        r0 = (c * ch).astype(F32)
        slot_g = r0 + lax.broadcasted_iota(jnp.int32, (ch, tb), 0).astype(F32)
        gather = jnp.where(prow == slot_g, 1.0, 0.0).astype(BF16)
        xg = _dot(gather, hn_ref[...]).astype(BF16)
        gu = _dot(xg, wgu_ref[0]) + bgu_ref[0]
        gate = jnp.minimum(gu[:, 0:D_FF], SWIGLU_LIMIT)
        up = jnp.clip(gu[:, D_FF:2 * D_FF], -SWIGLU_LIMIT, SWIGLU_LIMIT)
        act = (up + 1.0) * gate * _sigmoid(SWIGLU_ALPHA * gate)
        yc = _dot(act.astype(BF16), wd_ref[0]) + bd_ref[0]
        slot_s = r0 + lax.broadcasted_iota(jnp.int32, (tb, ch), 1).astype(F32)
        scatter = jnp.where(pcol == slot_s, ccol, 0.0).astype(BF16)
        y_ref[...] += _dot(scatter, yc.astype(BF16))
        return carry

    lax.fori_loop(0, (n_rows + ch - 1) // ch, chunk, 0)


def _moe(cnt, hn, h, pos, post, wgu, wd, b_gate_up, b_down, tb):
    n_all = hn.shape[0]
    row = lambda w: pl.BlockSpec((tb, w), lambda i, e, cnt: (i, 0))
    return pl.pallas_call(
        _moe_kernel,
        grid_spec=pltpu.PrefetchScalarGridSpec(
            num_scalar_prefetch=1, grid=(n_all // tb, N_EXPERTS),
            in_specs=[row(1024), row(1024), row(LANES),
                      pl.BlockSpec((LANES, tb), lambda i, e, cnt: (0, i)),
                      pl.BlockSpec((1, D_MODEL, 2 * D_FF), lambda i, e, cnt: (e, 0, 0)),
                      pl.BlockSpec((1, D_FF, D_MODEL), lambda i, e, cnt: (e, 0, 0)),
                      pl.BlockSpec((1, 1, 2 * D_FF), lambda i, e, cnt: (e, 0, 0)),
                      pl.BlockSpec((1, 1, D_MODEL), lambda i, e, cnt: (e, 0, 0))],
            out_specs=row(1024)),
        out_shape=jax.ShapeDtypeStruct((n_all, 1024), F32),
        compiler_params=pltpu.CompilerParams(dimension_semantics=("arbitrary",) * 2, vmem_limit_bytes=VMEM_LIMIT),
        name="moe",
    )(cnt, hn, h, pos, post, wgu, wd, b_gate_up, b_down)


def _pick_tile(n, pref):
    t = pref
    while n % t:
        t //= 2
    return t


def kernel(x_prompt, x_sample, cache_cmp, cache_slc, cache_win, state_ret, cache_mem, page_table, mem_prompt,
           g_attn, w_in, g_q_nsa, g_k_cmp, g_k_slc, g_k_win, pe_ck, w_ck1, w_ck2, pe_cv, w_cv1, w_cv2,
           g_ret_out, g_mem, w_mem_kv, g_q_mem, g_k_mem, w_br_nsa, w_br_ret, w_br_mem, w_out,
           g_ffn, w_router, b_router, w_gate_up, b_gate_up, w_down, b_down):
    nb, seq, _ = x_prompt.shape
    db, dseq, _ = x_sample.shape
    n_pages = page_table.shape[1]
    past = n_pages * PAGE_SIZE
    wb = cache_win.shape[1]
    mem_len = mem_prompt.shape[1]
    n_p, n_s = nb * seq, db * dseq
    n_all = n_p + n_s
    tm = _pick_tile(math.gcd(seq, n_s), 256)

    o = 0
    cols = {}
    for name, wdt in (("q", 512), ("kv", 768), ("ng", 24), ("rq", 256), ("rk", 256), ("rv", 512), ("rg", 512),
                      ("mq", 512), ("bg", 3072)):
        cols[name] = w_in[:, o:o + wdt]
        o += wdt
    wq = cols["q"].reshape(D_MODEL, NSA_HEADS, NSA_HD)
    zq = jnp.zeros_like(wq)
    wq_pad = jnp.concatenate(
        [jnp.concatenate([wq[:, hh], zq[:, hh]] if hh < NSA_GROUP else [zq[:, hh], wq[:, hh]], axis=-1)
         for hh in range(NSA_HEADS)], axis=-1)
    w_ng = jnp.pad(cols["ng"], ((0, 0), (0, C_NG - 24)))
    w1 = jnp.concatenate([wq_pad, cols["kv"], w_ng, cols["rq"], cols["rk"], cols["rv"], cols["rg"], cols["mq"]],
                         axis=-1).astype(BF16)
    w_bg = cols["bg"].astype(BF16)
    two = lambda g: jnp.concatenate([g, g]).reshape(1, LANES)
    r1 = lambda g: g.reshape(1, -1)

    half = RET_DK // 2
    inv = ROPE_BASE ** (-jnp.arange(half, dtype=F32) / half)
    pos = jnp.concatenate([jnp.arange(seq, dtype=jnp.int32),
                           jnp.tile(past + jnp.arange(dseq, dtype=jnp.int32), db)]).astype(F32)
    ang = pos[:, None] * inv[None, :]
    cos_t = jnp.tile(jnp.cos(ang), (1, 2 * RET_HEADS))
    sin_t = jnp.tile(jnp.concatenate([-jnp.sin(ang), jnp.sin(ang)], axis=-1), (1, RET_HEADS))

    xp = x_prompt.reshape(n_p, D_MODEL)
    xs = x_sample.reshape(n_s, D_MODEL)
    (q, kvc, kvs, kvw, ng, rq, rk, rv, rg, mq, kvct, kvst, kvwt) = _project(
        xp, xs, r1(g_attn), w1, cos_t, sin_t, two(g_q_nsa), two(g_k_slc), two(g_k_win), r1(g_q_mem), seq, tm)

    w1s = jnp.stack([w_ck1, w_ck1, w_cv1, w_cv1]).reshape(4, 2, CMP_STRIDE, NSA_HD, CMP_HIDDEN)
    wbig = jnp.einsum("shpdc,st->psdhtc", w1s, jnp.eye(4, dtype=F32)).reshape(CHUNK_W, 2 * 4 * CMP_HIDDEN).astype(BF16)
    pes = jnp.stack([pe_ck, pe_ck, pe_cv, pe_cv]).reshape(4, 2, CMP_STRIDE, NSA_HD)
    pe2 = jnp.pad(jnp.transpose(pes, (1, 2, 0, 3)).reshape(2, CHUNK_W), ((0, 6), (0, 0)))
    zc = jnp.zeros_like(w_ck2)
    bd2 = lambda w: jnp.concatenate([jnp.concatenate([w, zc], 1), jnp.concatenate([zc, w], 1)], 0).astype(BF16)
    gk2 = two(g_k_cmp)

    pages_p = seq // PAGE_SIZE
    pool_p = (kvc if n_all % PAGE_SIZE == 0 else kvc[:n_p]).reshape(-1, CHUNKS_PER_PAGE, CHUNK_W)
    pt_p = jnp.arange(nb * pages_p, dtype=jnp.int32).reshape(nb, pages_p)
    ck_p, cv_p = _compress(pt_p, pool_p, wbig, pe2, bd2(w_ck2), bd2(w_cv2), gk2)
    pages_t = lambda c: jnp.transpose(c, (0, 2, 3, 4, 1)).reshape(-1, 256, PAGE_SIZE)
    ck_s, cv_s = _compress_t(page_table, pages_t(cache_cmp), wbig, pe2, bd2(w_ck2), bd2(w_cv2), gk2)

    n_cmp_p = seq // CMP_STRIDE - CMP_BLOCK // CMP_STRIDE + 1
    onsa = _nsa_prompt(q, kvs, kvw, ck_p, cv_p, ng, nb, seq, n_cmp_p)

    ns_s = (past + dseq + SLC_BLOCK - 1) // SLC_BLOCK
    nsp = (ns_s + 7) // 8 * 8
    kpad = past + PAGE_SIZE
    e_s = (jnp.arange(kpad)[None, :] // SLC_BLOCK == jnp.arange(nsp)[:, None]).astype(BF16)
    s3 = lambda a: a[n_p:].reshape(db, dseq, a.shape[1])
    onsa_s, new_win_s = _nsa_sample(page_table, s3(q), s3(kvs), s3(kvw), ck_s, cv_s, s3(ng),
                                    cache_win.reshape(db, wb, 256), e_s,
                                    pages_t(cache_slc), past)
    onsa_s = onsa_s.reshape(n_s, 1024).astype(BF16)

    gro = r1(g_ret_out)
    oret, ret_state_p = _retention(rq, rk, rv, rg, jnp.zeros((nb, RET_HEADS, RET_DK, RET_DV), F32), gro,
                                   nb, seq // RET_CHUNK, RET_CHUNK)
    padc = lambda a: jnp.pad(s3(a), ((0, 0), (0, RET_CHUNK - dseq), (0, 0))).reshape(db * RET_CHUNK, a.shape[1])
    oret_s, ret_state_s = _retention(padc(rq), padc(rk), padc(rv), padc(rg), state_ret, gro, db, 1, dseq)
    oret = oret.reshape(n_p, 512)
    oret_s = oret_s[:, :dseq].reshape(n_s, 512)

    mem_kv_p = _mem_kv(mem_prompt.reshape(nb * mem_len, D_MODEL), r1(g_mem), w_mem_kv.astype(BF16), r1(g_k_mem),
                       _pick_tile(nb * mem_len, 256))
    tq = _pick_tile(seq, 512)
    omem = _mem_attend(mq, mem_kv_p, nb, seq // tq, tq, mem_len, n_p, BF16)
    omem_s = _mem_attend_cache(mq[n_p:].astype(F32), cache_mem.reshape(db, mem_len * 2 * MEM_HEADS, MEM_HD),
                               dseq).astype(BF16)

    wn = w_br_nsa.reshape(NSA_HEADS, NSA_HD, D_MODEL)
    zn = jnp.zeros_like(wn)
    wn_pad = jnp.concatenate(
        [jnp.concatenate([wn[hh], zn[hh]] if hh < NSA_GROUP else [zn[hh], wn[hh]], axis=0) for hh in range(NSA_HEADS)],
        axis=0).astype(BF16)
    wr_pad = jnp.pad(w_router, ((0, 0), (0, LANES - N_EXPERTS)))
    wr_hi = wr_pad.astype(BF16)
    wr_lo = (wr_pad - wr_hi.astype(F32)).astype(BF16)
    brt = jnp.concatenate([b_router, jnp.full((LANES - N_EXPERTS,), NEG, F32)]).reshape(1, LANES)
    h, hn, comb = _merge(xp, xs, onsa, onsa_s, oret, oret_s, omem, omem_s, r1(g_attn), w_bg, wn_pad, w_br_ret.astype(BF16),
                         w_br_mem.astype(BF16), w_out.astype(BF16), r1(g_ffn), jnp.stack([wr_hi, wr_lo]), brt, tm)

    tb = max(c for c in range(LANES, MOE_BLOCK_CAP + 1, LANES) if n_all % c == 0)
    pos, post, cnt = _route(comb, tb)
    cnt = cnt[:, 0, :N_EXPERTS].astype(jnp.int32).reshape(-1)
    y = _moe(cnt, hn, h, pos, post, w_gate_up.astype(BF16), w_down.astype(BF16),
             b_gate_up.reshape(N_EXPERTS, 1, 2 * D_FF), b_down.reshape(N_EXPERTS, 1, D_MODEL), tb)

    kv5 = lambda a, bsz, t: a.reshape(bsz, t, 2, NSA_KV_HEADS, NSA_HD)
    wp = min(WINDOW, seq)
    kv5t = lambda a: jnp.transpose(a.reshape(nb, 2, NSA_KV_HEADS, NSA_HD, a.shape[2]), (0, 4, 1, 2, 3))
    return (y[:n_p].reshape(nb, seq, D_MODEL), y[n_p:].reshape(db, dseq, D_MODEL),
            kv5t(kvct), kv5t(kvst), kv5t(kvwt[:, :, seq - wp:]),
            ret_state_p, mem_kv_p.reshape(nb, mem_len, 2, MEM_HEADS, MEM_HD),
            kv5(kvc[n_p:], db, dseq), kv5(kvs[n_p:], db, dseq), kv5(new_win_s, db, wb), ret_state_s)
```

```python
import functools
import math

import jax
import jax.numpy as jnp
from jax import lax
from jax.experimental import pallas as pl
from jax.experimental.pallas import tpu as pltpu

F32 = jnp.float32
BF16 = jnp.bfloat16

D_MODEL = 1024
NSA_HEADS = 8
NSA_KV_HEADS = 2
NSA_GROUP = 4
NSA_HD = 64
CMP_BLOCK = 32
CMP_STRIDE = 16
CMP_HIDDEN = 128
SLC_BLOCK = 64
N_SELECT = 16
N_LOCAL_FORCED = 2
FORCE_BONUS = 1.0e4
WINDOW = 512
Q_BLOCK = 128
PAGE_SIZE = 128
RET_HEADS = 4
RET_DK = 64
RET_DV = 128
RET_CHUNK = 128
ROPE_BASE = 10000.0
MEM_HEADS = 4
MEM_HD = 128
N_EXPERTS = 32
TOP_K = 4
D_FF = 1024
SWIGLU_ALPHA = 1.702
SWIGLU_LIMIT = 7.0
EPS = 1e-6
NEG = -1e30

LANES = 128
VMEM_LIMIT = 56 * 1024 * 1024


def _dot(a, b):
    return jnp.dot(a, b, preferred_element_type=F32)


def _dot_nt(a, b):
    return lax.dot_general(a, b, (((1,), (1,)), ((), ())), preferred_element_type=F32)


def _sigmoid(x):
    return 1.0 / (1.0 + jnp.exp(-x))


def _split3(x):
    a = x.astype(BF16)
    r = x - a.astype(F32)
    b = r.astype(BF16)
    c = (r - b.astype(F32)).astype(BF16)
    return a, b, c


def _msoftmax(s, m):
    s = jnp.where(m, s, NEG)
    mx = jnp.max(s, axis=-1, keepdims=True)
    p = jnp.where(m, jnp.exp(s - mx), 0.0)
    return p / jnp.maximum(jnp.sum(p, axis=-1, keepdims=True), 1e-30)


def _halfnorm(t, g2):
    lane = lax.broadcasted_iota(jnp.int32, t.shape, 1)
    lo = lane < NSA_HD
    t2 = t * t
    s0 = jnp.sum(jnp.where(lo, t2, 0.0), axis=-1, keepdims=True)
    s1 = jnp.sum(jnp.where(lo, 0.0, t2), axis=-1, keepdims=True)
    r = jnp.where(lo, lax.rsqrt(s0 * (1.0 / NSA_HD) + EPS), lax.rsqrt(s1 * (1.0 / NSA_HD) + EPS))
    return t * r * g2


def _tilenorm(t, g, width):
    return t * lax.rsqrt(jnp.sum(t * t, axis=-1, keepdims=True) * (1.0 / width) + EPS) * g


def _rot(x, cos, sin):
    lane = lax.broadcasted_iota(jnp.int32, x.shape, 1)
    first = (lane % RET_DK) < (RET_DK // 2)
    n = x.shape[1]
    sw = jnp.where(first, pltpu.roll(x, n - RET_DK // 2, 1), pltpu.roll(x, RET_DK // 2, 1))
    return x * cos + sw * sin


C_Q, C_KV, C_NG, C_RET, C_MQ = 1024, 768, 128, 1536, 512
O_KV = C_Q
O_NG = O_KV + C_KV
O_RET = O_NG + C_NG
O_MQ = O_RET + C_RET
W1_COLS = O_MQ + C_MQ


def _proj_kernel(nbp, xp_ref, xs_ref, g_ref, w_ref, cos_ref, sin_ref, gq_ref, gks_ref, gkw_ref, gqm_ref,
                 q_ref, kvc_ref, kvs_ref, kvw_ref, ng_ref, rq_ref, rk_ref, rv_ref, rg_ref, mq_ref,
                 kvct_ref, kvst_ref, kvwt_ref):
    i = pl.program_id(0)
    x = jnp.where(i < nbp, xp_ref[...], xs_ref[...])
    xn = (x * lax.rsqrt(jnp.mean(x * x, axis=-1, keepdims=True) + EPS) * g_ref[...]).astype(BF16)

    hq = _dot(xn, w_ref[:, 0:C_Q])
    for hh in range(NSA_HEADS):
        t = hq[:, LANES * hh:LANES * (hh + 1)]
        q_ref[:, LANES * hh:LANES * (hh + 1)] = (_tilenorm(t, gq_ref[...], NSA_HD) * NSA_HD ** -0.5).astype(BF16)

    hkv = _dot(xn, w_ref[:, O_KV:O_KV + C_KV])
    kvc = hkv[:, 0:256]
    kvs = jnp.concatenate([_halfnorm(hkv[:, 256:384], gks_ref[...]), hkv[:, 384:512]], axis=1)
    kvw = jnp.concatenate([_halfnorm(hkv[:, 512:640], gkw_ref[...]), hkv[:, 640:768]], axis=1)
    kvc_ref[...] = kvc
    kvs_ref[...] = kvs
    kvw_ref[...] = kvw

    @pl.when(i < nbp)
    def _():
        kvct_ref[0] = kvc.T
        kvst_ref[0] = kvs.T
        kvwt_ref[0] = kvw.T

    ng_ref[...] = _sigmoid(_dot(xn, w_ref[:, O_NG:O_NG + C_NG]))

    hr = _dot(xn, w_ref[:, O_RET:O_RET + C_RET])
    cos = cos_ref[...]
    sin = sin_ref[...]
    rq_ref[...] = _rot(hr[:, 0:256], cos, sin)
    rk_ref[...] = _rot(hr[:, 256:512], cos, sin) * RET_DK ** -0.5
    rv_ref[...] = hr[:, 512:1024]
    rg = hr[:, 1024:1536]
    rg_ref[...] = rg * _sigmoid(rg)

    hm = _dot(xn, w_ref[:, O_MQ:O_MQ + C_MQ])
    for hd in range(MEM_HEADS):
        t = hm[:, LANES * hd:LANES * (hd + 1)]
        mq_ref[:, LANES * hd:LANES * (hd + 1)] = (_tilenorm(t, gqm_ref[...], MEM_HD) * MEM_HD ** -0.5).astype(BF16)


def _project(xp, xs, g_attn, w1, cos_t, sin_t, gq2, gks2, gkw2, gqm, seq, tm):
    n_p, n_s = xp.shape[0], xs.shape[0]
    nbp, nbs = n_p // tm, n_s // tm
    n_all = n_p + n_s
    bps = seq // tm
    row = lambda w: pl.BlockSpec((tm, w), lambda i: (i, 0))
    const = lambda a: pl.BlockSpec(a.shape, lambda i: (0,) * a.ndim)
    rope = pl.BlockSpec((tm, 256), lambda i: (jnp.where(i < nbp, i % bps, bps + i - nbp), 0))
    widths = (1024, 256, 256, 256, 128, 256, 256, 512, 512, 512)
    dtypes = (BF16, F32, F32, F32, F32, F32, F32, F32, F32, BF16)
    kvt = pl.BlockSpec((1, 256, tm), lambda i: (jnp.minimum(i, nbp - 1) // bps, 0, jnp.minimum(i, nbp - 1) % bps))
    return pl.pallas_call(
        functools.partial(_proj_kernel, nbp),
        grid=(nbp + nbs,),
        in_specs=[pl.BlockSpec((tm, D_MODEL), lambda i: (jnp.minimum(i, nbp - 1), 0)),
                  pl.BlockSpec((tm, D_MODEL), lambda i: (jnp.maximum(i - nbp, 0), 0)),
                  const(g_attn), const(w1), rope, rope, const(gq2), const(gks2), const(gkw2), const(gqm)],
        out_specs=[row(w) for w in widths] + [kvt] * 3,
        out_shape=[jax.ShapeDtypeStruct((n_all, w), d) for w, d in zip(widths, dtypes)]
        + [jax.ShapeDtypeStruct((n_p // seq, 256, seq), F32)] * 3,
        compiler_params=pltpu.CompilerParams(dimension_semantics=("arbitrary",), vmem_limit_bytes=VMEM_LIMIT),
        name="proj",
    )(xp, xs, g_attn, w1, cos_t, sin_t, gq2, gks2, gkw2, gqm)


CHUNK_W = CMP_STRIDE * 256
CHUNKS_PER_PAGE = PAGE_SIZE // CMP_STRIDE


def _compress_kernel(n_pages, pt_ref, pool_ref, wbig_ref, pe_ref, w2k_ref, w2v_ref, gk_ref,
                     ck_ref, cv_ref, x_scr, r_scr, sem):
    b = pl.program_id(0)
    n = n_pages * CHUNKS_PER_PAGE

    def page_copy(j):
        return pltpu.make_async_copy(pool_ref.at[pt_ref[b, j]],
                                     x_scr.at[pl.ds(CHUNKS_PER_PAGE * j, CHUNKS_PER_PAGE)], sem)

    for j in range(n_pages):
        page_copy(j).start()
    x_scr[pl.ds(n, 8), :] = pe_ref[...]
    for j in range(n_pages):
        page_copy(j).wait()

    r_scr[...] = _dot(x_scr[...].astype(BF16), wbig_ref[...])
    cvec = r_scr[n:n + 1, 0:512] + r_scr[n + 1:n + 2, 512:1024]
    hid = r_scr[0:n, 0:512] + r_scr[pl.ds(1, n), 512:1024] + cvec
    hb = (hid * _sigmoid(hid)).astype(BF16)
    ck_ref[0] = _halfnorm(_dot(hb[:, 0:256], w2k_ref[...]), gk_ref[...])
    cv_ref[0] = _dot(hb[:, 256:512], w2v_ref[...])


def _compress(page_table, pool, wbig, pe2, w2k, w2v, gk2):
    nb, n_pages = page_table.shape
    n = n_pages * CHUNKS_PER_PAGE
    const = lambda a: pl.BlockSpec(a.shape, lambda b, pt: (0,) * a.ndim)
    out = pl.BlockSpec((1, n, LANES), lambda b, pt: (b, 0, 0))
    return pl.pallas_call(
        functools.partial(_compress_kernel, n_pages),
        grid_spec=pltpu.PrefetchScalarGridSpec(
            num_scalar_prefetch=1, grid=(nb,),
            in_specs=[pl.BlockSpec(memory_space=pl.ANY), const(wbig), const(pe2), const(w2k), const(w2v), const(gk2)],
            out_specs=[out, out],
            scratch_shapes=[pltpu.VMEM((n + 8, CHUNK_W), F32), pltpu.VMEM((n + 8, 1024), F32),
                            pltpu.SemaphoreType.DMA(())]),
        out_shape=[jax.ShapeDtypeStruct((nb, n, LANES), F32)] * 2,
        compiler_params=pltpu.CompilerParams(dimension_semantics=("arbitrary",), vmem_limit_bytes=VMEM_LIMIT),
        name="compress",
    )(page_table, pool, wbig, pe2, w2k, w2v, gk2)


def _compress_t_kernel(n_pages, pt_ref, pool_ref, wbig_ref, pe_ref, w2k_ref, w2v_ref, gk_ref,
                       ck_ref, cv_ref, pg_scr, t_scr, r_scr, cvec_scr, sem):
    b = pl.program_id(0)
    slot = b % 2
    n = n_pages * CHUNKS_PER_PAGE

    def page_copy(bb, sl, j):
        return pltpu.make_async_copy(pool_ref.at[pt_ref[bb, j]], pg_scr.at[sl, j], sem.at[sl])

    @pl.when(b == 0)
    def _():
        for j in range(n_pages):
            page_copy(0, 0, j).start()
        r = _dot(pe_ref[...].astype(BF16), wbig_ref[...])
        cvec_scr[...] = jnp.broadcast_to(r[0:1, 0:512] + r[1:2, 512:1024], cvec_scr.shape)
        r_scr[pl.ds(n, 8), :] = jnp.zeros((8, 1024), F32)

    @pl.when(b + 1 < pl.num_programs(0))
    def _():
        for j in range(n_pages):
            page_copy(b + 1, 1 - slot, j).start()

    for j in range(n_pages):
        page_copy(b, slot, j).wait()

    src = lax.broadcasted_iota(jnp.int32, (PAGE_SIZE, PAGE_SIZE), 0)
    dst = lax.broadcasted_iota(jnp.int32, (PAGE_SIZE, PAGE_SIZE), 1)
    perm = (dst == CHUNKS_PER_PAGE * (src % CMP_STRIDE) + src // CMP_STRIDE).astype(BF16)
    for j in range(n_pages):
        xp = _dot(pg_scr[slot, j].astype(BF16), perm)
        tk = xp[0:LANES, :].T
        tv = xp[LANES:2 * LANES, :].T
        for p in range(CMP_STRIDE):
            rows = pl.ds(CHUNKS_PER_PAGE * j, CHUNKS_PER_PAGE)
            t_scr[p, rows, 0:LANES] = tk[CHUNKS_PER_PAGE * p:CHUNKS_PER_PAGE * (p + 1)]
            t_scr[p, rows, LANES:2 * LANES] = tv[CHUNKS_PER_PAGE * p:CHUNKS_PER_PAGE * (p + 1)]

    acc = jnp.zeros((n, 1024), F32)
    for p in range(CMP_STRIDE):
        acc = acc + _dot(t_scr[p].astype(BF16), wbig_ref[256 * p:256 * (p + 1), :])
    r_scr[pl.ds(0, n), :] = acc
    hid = acc[:, 0:512] + r_scr[pl.ds(1, n), 512:1024] + cvec_scr[0:1, :]
    hb = (hid * _sigmoid(hid)).astype(BF16)
    ck_ref[0] = _halfnorm(_dot(hb[:, 0:256], w2k_ref[...]), gk_ref[...])
    cv_ref[0] = _dot(hb[:, 256:512], w2v_ref[...])


def _compress_t(page_table, pool_t, wbig, pe2, w2k, w2v, gk2):
    nb, n_pages = page_table.shape
    n = n_pages * CHUNKS_PER_PAGE
    const = lambda a: pl.BlockSpec(a.shape, lambda b, pt: (0,) * a.ndim)
    out = pl.BlockSpec((1, n, LANES), lambda b, pt: (b, 0, 0))
    return pl.pallas_call(
        functools.partial(_compress_t_kernel, n_pages),
        grid_spec=pltpu.PrefetchScalarGridSpec(
            num_scalar_prefetch=1, grid=(nb,),
            in_specs=[pl.BlockSpec(memory_space=pl.ANY), const(wbig), const(pe2), const(w2k), const(w2v), const(gk2)],
            out_specs=[out, out],
            scratch_shapes=[pltpu.VMEM((2, n_pages, 256, PAGE_SIZE), F32), pltpu.VMEM((CMP_STRIDE, n, 2 * LANES), F32),
                            pltpu.VMEM((n + 8, 1024), F32), pltpu.VMEM((8, 512), F32),
                            pltpu.SemaphoreType.DMA((2,))]),
        out_shape=[jax.ShapeDtypeStruct((nb, n, LANES), F32)] * 2,
        compiler_params=pltpu.CompilerParams(dimension_semantics=("arbitrary",), vmem_limit_bytes=VMEM_LIMIT),
        name="compress_t",
    )(page_table, pool_t, wbig, pe2, w2k, w2v, gk2)


def _overlap_t(ns_rows, n_cmp_cols, n_cmp):
    s = lax.broadcasted_iota(jnp.int32, (ns_rows, n_cmp_cols), 0)
    n = lax.broadcasted_iota(jnp.int32, (ns_rows, n_cmp_cols), 1)
    ov = (n * CMP_STRIDE < s * SLC_BLOCK + SLC_BLOCK) & (n * CMP_STRIDE + CMP_BLOCK > s * SLC_BLOCK) & (n < n_cmp)
    return ov.astype(BF16)


def _importance_t(ov_t, pcs):
    a, b, c = _split3(pcs)
    return _dot_nt(ov_t, a) + _dot_nt(ov_t, b) + _dot_nt(ov_t, c)


def _block_scores_t(imp_t, tl):
    blk = lax.broadcasted_iota(jnp.int32, imp_t.shape, 0)
    cur = tl // SLC_BLOCK
    valid = blk <= cur
    forced = (blk == 0) | (valid & (blk > cur - N_LOCAL_FORCED))
    return jnp.where(valid, imp_t + jnp.where(forced, FORCE_BONUS, 0.0), NEG), blk


SLC_CHUNK = 1024
RANK_ACCS = 4


def _nsa_prompt_kernel(seq, n_cmp, q_ref, kvs_ref, kvw_ref, ck_ref, cv_ref, ng_ref, et_ref, eg_ref, o_ref):
    i = pl.program_id(1)
    ns = seq // SLC_BLOCK
    nh = NSA_HEADS
    q8 = q_ref[...]
    qs = jnp.concatenate([q8[:, LANES * hh:LANES * (hh + 1)] for hh in range(nh)], axis=0)
    tq = i * Q_BLOCK + lax.broadcasted_iota(jnp.int32, (Q_BLOCK, 1), 0)
    rep = lambda a: jnp.concatenate([a] * nh, axis=0)
    t8 = rep(tq)

    ckb = ck_ref[0].astype(BF16)
    sc = _dot_nt(qs, ckb)
    jn = lax.broadcasted_iota(jnp.int32, sc.shape, 1)
    pc = _msoftmax(sc, (jn * CMP_STRIDE + CMP_BLOCK - 1) <= t8)
    o_cmp = _dot(pc.astype(BF16), cv_ref[0].astype(BF16))

    ov_t = _overlap_t(ns, pc.shape[1], n_cmp)
    tl = i * Q_BLOCK + lax.broadcasted_iota(jnp.int32, (ns, Q_BLOCK), 1)
    eye = (lax.broadcasted_iota(jnp.int32, (Q_BLOCK, Q_BLOCK), 0)
           == lax.broadcasted_iota(jnp.int32, (Q_BLOCK, Q_BLOCK), 1)).astype(BF16)
    lane_q = lax.broadcasted_iota(jnp.int32, (Q_BLOCK, LANES), 1)
    zpad = lambda n: [jnp.zeros((n, Q_BLOCK), BF16)] if n else []
    qxs = []
    for h in range(NSA_KV_HEADS):
        r0 = h * NSA_GROUP * Q_BLOCK
        pcs = pc[r0:r0 + 128] + pc[r0 + 128:r0 + 256] + pc[r0 + 256:r0 + 384] + pc[r0 + 384:r0 + 512]
        score, blk = _block_scores_t(_importance_t(ov_t, pcs), tl)
        ranks = [jnp.zeros(score.shape, F32) for _ in range(RANK_ACCS)]
        for k in range(ns):
            sk = score[k:k + 1, :]
            ranks[k % RANK_ACCS] += jnp.where((sk > score) | ((sk == score) & (blk > k)), 1.0, 0.0)
        rank = functools.reduce(lambda x, y: x + y, ranks)
        sel_t = jnp.where(rank < min(N_SELECT, ns), 1.0, 0.0).astype(BF16)
        sel_t = jnp.concatenate(zpad(NSA_HD * (1 - h)) + [sel_t] + zpad(LANES - ns - NSA_HD * (1 - h)), axis=0)
        in_blk = (lane_q >= NSA_HD) if h == 0 else (lane_q < NSA_HD)
        pen = jnp.where(in_blk, (_dot_nt(eye, sel_t) - 1.0) * -NEG, 0.0)
        qh = qs[r0:r0 + NSA_GROUP * Q_BLOCK]
        qxs.append(qh + jnp.concatenate([pen] * NSA_GROUP, axis=0).astype(BF16))

    kc = min(SLC_CHUNK, seq)
    lane_1 = lax.broadcasted_iota(jnp.int32, (1, LANES), 1)
    keep_lo = jnp.where(lane_1 < NSA_HD, 1.0, 0.0).astype(BF16)
    keep_hi = jnp.where(lane_1 < NSA_HD, 0.0, 1.0).astype(BF16)
    ones = jnp.ones((kc, LANES), BF16)

    def chunk(c, carry, bias):
        m, l, acc = carry
        k0 = pl.multiple_of(c * kc, kc)
        kk = kvs_ref[pl.ds(k0, kc), 0:128].astype(BF16)
        et = et_ref[pl.ds(k0, kc), :]
        vx = jnp.concatenate([kvs_ref[pl.ds(k0, kc), 128:256].astype(BF16), ones], axis=1)
        s = jnp.concatenate([_dot_nt(qxs[0], kk * keep_lo + et * keep_hi),
                             _dot_nt(qxs[1], kk * keep_hi + et * keep_lo)], axis=0)
        if bias is not None:
            s = s + bias
        m_new = jnp.maximum(m, jnp.max(s, axis=-1, keepdims=True))
        alpha = jnp.exp(m - m_new)
        pv = _dot(jnp.exp(s - m_new).astype(BF16), vx)
        return m_new, alpha * l + pv[:, LANES:LANES + 1], alpha * acc + pv[:, 0:LANES]

    rows = nh * Q_BLOCK
    init = (jnp.full((rows, 1), NEG, F32), jnp.zeros((rows, 1), F32), jnp.zeros((rows, LANES), F32))
    c_last = (i * Q_BLOCK) // kc
    carry = lax.fori_loop(0, c_last, lambda c, cr: chunk(c, cr, None), init)
    kpos = c_last * kc + lax.broadcasted_iota(jnp.int32, (Q_BLOCK, kc), 1)
    _, l, acc = chunk(c_last, carry, rep(jnp.where(kpos <= tq, 0.0, NEG)))
    o_slc = acc / l

    wk = WINDOW + Q_BLOCK
    start = pl.multiple_of(Q_BLOCK * jnp.maximum(i - WINDOW // Q_BLOCK, 0), Q_BLOCK)
    kk = kvw_ref[pl.ds(start, wk), 0:128].astype(BF16)
    vx = jnp.concatenate([kvw_ref[pl.ds(start, wk), 128:256].astype(BF16), jnp.ones((wk, LANES), BF16)], axis=1)
    d = tq - (start + lax.broadcasted_iota(jnp.int32, (Q_BLOCK, wk), 1))
    sw = _dot_nt(qs, kk) + rep(jnp.where((d >= 0) & (d < WINDOW), 0.0, NEG))
    aw = _dot(jnp.exp(sw - jnp.max(sw, axis=-1, keepdims=True)).astype(BF16), vx)
    o_win = aw[:, 0:LANES] / aw[:, LANES:LANES + 1]

    a, b, c = _split3(ng_ref[...])
    for h in range(NSA_KV_HEADS):
        eg = eg_ref[h]
        gexp = _dot(a, eg) + _dot(b, eg) + _dot(c, eg)
        for g in range(NSA_GROUP):
            hh = NSA_GROUP * h + g
            rs = slice(Q_BLOCK * hh, Q_BLOCK * (hh + 1))
            gate = lambda br: gexp[:, LANES * (NSA_GROUP * br + g):LANES * (NSA_GROUP * br + g + 1)]
            o = gate(0) * o_cmp[rs] + gate(1) * o_slc[rs] + gate(2) * o_win[rs]
            o_ref[:, LANES * hh:LANES * (hh + 1)] = o.astype(BF16)


def _nsa_prompt(q, kvs, kvw, ck, cv, ng, nb, seq, n_cmp):
    nq = seq // Q_BLOCK
    ncp = ck.shape[1]
    ns = seq // SLC_BLOCK
    assert ns <= NSA_HD and seq >= WINDOW + Q_BLOCK
    et = (jnp.arange(seq)[:, None] // SLC_BLOCK == jnp.arange(LANES)[None, :] % NSA_HD).astype(BF16)
    n_g = 3 * NSA_GROUP
    col = jnp.arange(n_g * LANES)[None, None, :] // LANES
    src = (col // NSA_GROUP) * NSA_HEADS + NSA_GROUP * jnp.arange(NSA_KV_HEADS)[:, None, None] + col % NSA_GROUP
    eg = (jnp.arange(LANES)[None, :, None] == src).astype(BF16)
    return pl.pallas_call(
        functools.partial(_nsa_prompt_kernel, seq, n_cmp),
        grid=(nb, nq),
        in_specs=[pl.BlockSpec((Q_BLOCK, 1024), lambda b, i: (b * nq + i, 0)),
                  pl.BlockSpec((seq, 256), lambda b, i: (b, 0)),
                  pl.BlockSpec((seq, 256), lambda b, i: (b, 0)),
                  pl.BlockSpec((1, ncp, LANES), lambda b, i: (b, 0, 0)),
                  pl.BlockSpec((1, ncp, LANES), lambda b, i: (b, 0, 0)),
                  pl.BlockSpec((Q_BLOCK, LANES), lambda b, i: (b * nq + i, 0)),
                  pl.BlockSpec((seq, LANES), lambda b, i: (0, 0)),
                  pl.BlockSpec((NSA_KV_HEADS, LANES, n_g * LANES), lambda b, i: (0, 0, 0))],
        out_specs=pl.BlockSpec((Q_BLOCK, 1024), lambda b, i: (b * nq + i, 0)),
        out_shape=jax.ShapeDtypeStruct((nb * seq, 1024), BF16),
        compiler_params=pltpu.CompilerParams(dimension_semantics=("arbitrary",) * 2, vmem_limit_bytes=VMEM_LIMIT),
        name="nsa_prompt",
    )(q, kvs, kvw, ck, cv, ng, et, eg)


def _nsa_sample_kernel(past, n_pages, dseq, wb, pt_ref, q_ref, kvs_ref, kvw_ref, ck_ref, cv_ref, ng_ref, cwin_ref,
                       e_ref, pool_ref, o_ref, nwin_ref, kvt_scr, kw_scr, sem):
    b = pl.program_id(0)
    slot = b % 2

    def page_copy(bb, sl, j):
        return pltpu.make_async_copy(pool_ref.at[pt_ref[bb, j]], kvt_scr.at[sl, :, pl.ds(PAGE_SIZE * j, PAGE_SIZE)],
                                     sem.at[sl])

    @pl.when(b == 0)
    def _():
        for j in range(n_pages):
            page_copy(0, 0, j).start()

    @pl.when(b + 1 < pl.num_programs(0))
    def _():
        for j in range(n_pages):
            page_copy(b + 1, 1 - slot, j).start()

    n_heads = NSA_KV_HEADS * NSA_GROUP
    rows = n_heads * dseq
    kpad = past + PAGE_SIZE
    ns = (past + dseq + SLC_BLOCK - 1) // SLC_BLOCK
    nsp = e_ref.shape[0]
    qf = q_ref[0].astype(F32)
    qs = jnp.concatenate([qf[:, LANES * hh:LANES * (hh + 1)] for hh in range(n_heads)], axis=0).astype(BF16)
    t1 = past + lax.broadcasted_iota(jnp.int32, (dseq, 1), 0)
    t = jnp.concatenate([t1] * n_heads, axis=0)

    sc = _dot_nt(qs, ck_ref[0].astype(BF16))
    jn = lax.broadcasted_iota(jnp.int32, sc.shape, 1)
    pc = _msoftmax(sc, (jn * CMP_STRIDE + CMP_BLOCK - 1) <= t)
    o_cmp = _dot(pc.astype(BF16), cv_ref[0].astype(BF16))

    per_kv = []
    for kvh in range(NSA_KV_HEADS):
        base = kvh * NSA_GROUP * dseq
        s = pc[base:base + dseq]
        for g in range(1, NSA_GROUP):
            s = s + pc[base + g * dseq:base + (g + 1) * dseq]
        per_kv += [s] * NSA_GROUP
    pcs = jnp.concatenate(per_kv, axis=0)
    n_cmp = (ns * SLC_BLOCK) // CMP_STRIDE - CMP_BLOCK // CMP_STRIDE + 1
    imp_t = _importance_t(_overlap_t(nsp, pcs.shape[1], n_cmp), pcs)
    tl = past + lax.broadcasted_iota(jnp.int32, (nsp, rows), 1) % dseq
    score, blk = _block_scores_t(imp_t, tl)
    ranks = [jnp.zeros(score.shape, F32) for _ in range(RANK_ACCS)]
    for k in range(ns):
        sk = score[k:k + 1, :]
        ranks[k % RANK_ACCS] += jnp.where((sk > score) | ((sk == score) & (blk > k)), 1.0, 0.0)
    rank = functools.reduce(lambda x, y: x + y, ranks)
    sel_t = jnp.where(rank < min(N_SELECT, ns), 1.0, 0.0).astype(BF16)
    eye = (lax.broadcasted_iota(jnp.int32, (rows, rows), 0)
           == lax.broadcasted_iota(jnp.int32, (rows, rows), 1)).astype(BF16)
    sel = _dot_nt(eye, sel_t)
    sele = _dot(sel.astype(BF16), e_ref[...])

    knew = jnp.concatenate([kvs_ref[0], jnp.zeros((PAGE_SIZE - dseq, 256), F32)], axis=0).astype(BF16)
    for j in range(n_pages):
        page_copy(b, slot, j).wait()
    s = jnp.concatenate([_dot(qs, kvt_scr[slot, 0:128, :].astype(BF16)), _dot_nt(qs, knew[:, 0:128])], axis=1)
    kpos = lax.broadcasted_iota(jnp.int32, (rows, kpad), 1)
    p = _msoftmax(s, (sele > 0.5) & (kpos <= t)).astype(BF16)
    o_slc = _dot_nt(p[:, 0:past], kvt_scr[slot, 128:256, :].astype(BF16)) + _dot(p[:, past:kpad], knew[:, 128:256])

    wpad = kw_scr.shape[0]
    kw_scr[pl.ds(0, wb), :] = cwin_ref[0]
    kw_scr[pl.ds(wb, dseq), :] = kvw_ref[0]
    kw_scr[pl.ds(wb + dseq, wpad - wb - dseq), :] = jnp.zeros((wpad - wb - dseq, 256), F32)
    sw = _dot_nt(qs, kw_scr[:, 0:128].astype(BF16))
    d = t - (past - wb + lax.broadcasted_iota(jnp.int32, (rows, wpad), 1))
    pw = _msoftmax(sw, (d >= 0) & (d < WINDOW))
    o_win = _dot(pw.astype(BF16), kw_scr[:, 128:256].astype(BF16))
    nwin_ref[0, pl.ds(0, wb - dseq), :] = cwin_ref[0, pl.ds(dseq, wb - dseq), :]
    nwin_ref[0, pl.ds(wb - dseq, dseq), :] = kvw_ref[0]

    ngv = ng_ref[0]
    for hh in range(n_heads):
        rs = slice(dseq * hh, dseq * (hh + 1))
        gate = lambda br: ngv[:, br * NSA_HEADS + hh:br * NSA_HEADS + hh + 1]
        o_ref[0, :, LANES * hh:LANES * (hh + 1)] = gate(0) * o_cmp[rs] + gate(1) * o_slc[rs] + gate(2) * o_win[rs]


def _nsa_sample(page_table, q, kvs, kvw, ck, cv, ng, cwin, e_s, pool, past):
    nb, n_pages = page_table.shape
    dseq = q.shape[1]
    wb = cwin.shape[1]
    kpad = past + PAGE_SIZE
    wpad = wb + PAGE_SIZE
    nsp = e_s.shape[0]
    rows = NSA_HEADS * dseq
    per_b = lambda a: pl.BlockSpec((1,) + a.shape[1:], lambda b, pt: (b,) + (0,) * (a.ndim - 1))
    return pl.pallas_call(
        functools.partial(_nsa_sample_kernel, past, n_pages, dseq, wb),
        grid_spec=pltpu.PrefetchScalarGridSpec(
            num_scalar_prefetch=1, grid=(nb,),
            in_specs=[per_b(q), per_b(kvs), per_b(kvw), per_b(ck), per_b(cv), per_b(ng), per_b(cwin),
                      pl.BlockSpec(e_s.shape, lambda b, pt: (0, 0)), pl.BlockSpec(memory_space=pl.ANY)],
            out_specs=[pl.BlockSpec((1, dseq, 1024), lambda b, pt: (b, 0, 0)),
                       pl.BlockSpec((1, wb, 256), lambda b, pt: (b, 0, 0))],
            scratch_shapes=[pltpu.VMEM((2, 256, past), F32), pltpu.VMEM((wpad, 256), F32),
                            pltpu.SemaphoreType.DMA((2,))]),
        out_shape=[jax.ShapeDtypeStruct((nb, dseq, 1024), F32), jax.ShapeDtypeStruct((nb, wb, 256), F32)],
        compiler_params=pltpu.CompilerParams(dimension_semantics=("arbitrary",), vmem_limit_bytes=VMEM_LIMIT),
        name="nsa_sample",
    )(page_table, q, kvs, kvw, ck, cv, ng, cwin, e_s, pool)


RET_GROUP = 4


def _ret_kernel(bg, c_true, *refs):
    rq_refs, rk_refs, rv_refs, rg_refs = (refs[k * bg:(k + 1) * bg] for k in range(4))
    s0_ref, gro_ref, o_ref, sout_ref, s_scr = refs[4 * bg:]
    c = pl.program_id(1)
    cs = RET_CHUNK

    @pl.when(c == 0)
    def _():
        for b in range(bg):
            for tile in range(RET_HEADS // 2):
                s_scr[b, tile] = jnp.concatenate([s0_ref[b, 2 * tile], s0_ref[b, 2 * tile + 1]], axis=0)

    diff = (lax.broadcasted_iota(jnp.int32, (cs, cs), 0) - lax.broadcasted_iota(jnp.int32, (cs, cs), 1)).astype(F32)
    ic = lax.broadcasted_iota(jnp.int32, (cs, 1), 0).astype(F32)
    lane = lax.broadcasted_iota(jnp.int32, (cs, LANES), 1)
    for hd in range(RET_HEADS):
        lg = math.log(1.0 - 2.0 ** (-5.0 - hd))
        tile, half = hd // 2, hd % 2
        in_half = (lane >= RET_DK * half) & (lane < RET_DK * (half + 1))
        dmat = jnp.where(diff >= 0, jnp.exp(jnp.maximum(diff, 0.0) * lg), 0.0)
        dec_q = jnp.exp((ic + 1.0) * lg)
        dec_k = jnp.exp((c_true - 1.0 - ic) * lg)
        for b in range(bg):
            qm = jnp.where(in_half, rq_refs[b][:, LANES * tile:LANES * (tile + 1)], 0.0).astype(BF16)
            k2 = rk_refs[b][:, LANES * tile:LANES * (tile + 1)]
            vb = rv_refs[b][:, LANES * hd:LANES * (hd + 1)].astype(BF16)
            o = _dot((_dot_nt(qm, k2.astype(BF16)) * dmat).astype(BF16), vb)
            st = s_scr[b, tile]
            o = o + _dot(qm, st.astype(BF16)) * dec_q
            kdec_t = (k2 * dec_k).T
            upd = _dot(kdec_t[RET_DK * half:RET_DK * (half + 1)].astype(BF16), vb)
            s_new = st[RET_DK * half:RET_DK * (half + 1)] * math.exp(c_true * lg) + upd
            s_scr[b, tile, pl.ds(RET_DK * half, RET_DK), :] = s_new
            on = _tilenorm(o, gro_ref[...], RET_DV)
            o_ref[b, :, LANES * hd:LANES * (hd + 1)] = (rg_refs[b][:, LANES * hd:LANES * (hd + 1)] * on).astype(BF16)

    @pl.when(c == pl.num_programs(1) - 1)
    def _():
        for b in range(bg):
            for hd in range(RET_HEADS):
                sout_ref[b, hd] = s_scr[b, hd // 2, pl.ds(RET_DK * (hd % 2), RET_DK), :]


def _retention(rq, rk, rv, rg, s0, gro, nb, n_chunks, c_true):
    cs = RET_CHUNK
    bg = math.gcd(nb, RET_GROUP)
    rows = lambda w: [pl.BlockSpec((cs, w), lambda g, c, b=b: ((g * bg + b) * n_chunks + c, 0)) for b in range(bg)]
    state = pl.BlockSpec((bg, RET_HEADS, RET_DK, RET_DV), lambda g, c: (g, 0, 0, 0))
    return pl.pallas_call(
        functools.partial(_ret_kernel, bg, float(c_true)),
        grid=(nb // bg, n_chunks),
        in_specs=rows(256) + rows(256) + rows(512) + rows(512) + [state, pl.BlockSpec(gro.shape, lambda g, c: (0, 0))],
        out_specs=[pl.BlockSpec((bg, cs, 512), lambda g, c: (g, c, 0)), state],
        out_shape=[jax.ShapeDtypeStruct((nb, n_chunks * cs, 512), BF16),
                   jax.ShapeDtypeStruct((nb, RET_HEADS, RET_DK, RET_DV), F32)],
        scratch_shapes=[pltpu.VMEM((bg, RET_HEADS // 2, LANES, RET_DV), F32)],
        compiler_params=pltpu.CompilerParams(dimension_semantics=("arbitrary", "arbitrary")),
        name="retention",
    )(*([rq] * bg + [rk] * bg + [rv] * bg + [rg] * bg), s0, gro)


def _memkv_kernel(m_ref, g_ref, w_ref, gk_ref, o_ref):
    x = m_ref[...]
    xn = (x * lax.rsqrt(jnp.mean(x * x, axis=-1, keepdims=True) + EPS) * g_ref[...]).astype(BF16)
    hk = _dot(xn, w_ref[...])
    half = MEM_HEADS * MEM_HD
    for hd in range(MEM_HEADS):
        o_ref[:, LANES * hd:LANES * (hd + 1)] = _tilenorm(hk[:, LANES * hd:LANES * (hd + 1)], gk_ref[...], MEM_HD)
    o_ref[:, half:2 * half] = hk[:, half:2 * half]


def _mem_kv(mem2d, g_mem, w_mem, gk_mem, tm):
    n = mem2d.shape[0]
    const = lambda a: pl.BlockSpec(a.shape, lambda i: (0,) * a.ndim)
    return pl.pallas_call(
        _memkv_kernel,
        grid=(n // tm,),
        in_specs=[pl.BlockSpec((tm, D_MODEL), lambda i: (i, 0)), const(g_mem), const(w_mem), const(gk_mem)],
        out_specs=pl.BlockSpec((tm, 1024), lambda i: (i, 0)),
        out_shape=jax.ShapeDtypeStruct((n, 1024), F32),
        compiler_params=pltpu.CompilerParams(dimension_semantics=("arbitrary",)),
        name="mem_kv",
    )(mem2d, g_mem, w_mem, gk_mem)


def _memattn_kernel(q_ref, kv_ref, o_ref):
    half = MEM_HEADS * MEM_HD
    for hd in range(MEM_HEADS):
        cols = slice(LANES * hd, LANES * (hd + 1))
        s = _dot_nt(q_ref[:, cols].astype(BF16), kv_ref[:, cols].astype(BF16))
        p = jnp.exp(s - jnp.max(s, axis=-1, keepdims=True))
        p = p / jnp.sum(p, axis=-1, keepdims=True)
        o = _dot(p.astype(BF16), kv_ref[:, half + LANES * hd:half + LANES * (hd + 1)].astype(BF16))
        o_ref[:, cols] = o.astype(o_ref.dtype)


def _memattn_cache_kernel(q_ref, kv_ref, o_ref):
    stride = 2 * MEM_HEADS
    mem_len = kv_ref.shape[1] // stride
    for hd in range(MEM_HEADS):
        cols = slice(LANES * hd, LANES * (hd + 1))
        k = kv_ref[0, pl.ds(hd, mem_len, stride=stride), :].astype(BF16)
        v = kv_ref[0, pl.ds(MEM_HEADS + hd, mem_len, stride=stride), :].astype(BF16)
        s = _dot_nt(q_ref[:, cols].astype(BF16), k)
        p = jnp.exp(s - jnp.max(s, axis=-1, keepdims=True))
        p = p / jnp.sum(p, axis=-1, keepdims=True)
        o_ref[:, cols] = _dot(p.astype(BF16), v).astype(o_ref.dtype)


def _mem_attend_cache(q2d, cache_rows, dseq):
    nb, n_rows, _ = cache_rows.shape
    return pl.pallas_call(
        _memattn_cache_kernel,
        grid=(nb,),
        in_specs=[pl.BlockSpec((dseq, 512), lambda b: (b, 0)), pl.BlockSpec((1, n_rows, LANES), lambda b: (b, 0, 0))],
        out_specs=pl.BlockSpec((dseq, 512), lambda b: (b, 0)),
        out_shape=jax.ShapeDtypeStruct((nb * dseq, 512), F32),
        compiler_params=pltpu.CompilerParams(dimension_semantics=("arbitrary",)),
        name="mem_attend_cache",
    )(q2d, cache_rows)


def _mem_attend(q2d, kv2d, nb, blocks_per_b, tq, mem_len, out_rows, out_dtype):
    return pl.pallas_call(
        _memattn_kernel,
        grid=(nb, blocks_per_b),
        in_specs=[pl.BlockSpec((tq, 512), lambda b, i: (b * blocks_per_b + i, 0)),
                  pl.BlockSpec((mem_len, 1024), lambda b, i: (b, 0))],
        out_specs=pl.BlockSpec((tq, 512), lambda b, i: (b * blocks_per_b + i, 0)),
        out_shape=jax.ShapeDtypeStruct((out_rows, 512), out_dtype),
        compiler_params=pltpu.CompilerParams(dimension_semantics=("arbitrary", "arbitrary")),
        name="mem_attend",
    )(q2d, kv2d)


def _merge_kernel(nbp, xp_ref, xs_ref, onsap_ref, onsas_ref, oretp_ref, orets_ref, omemp_ref, omems_ref, g_ref,
                  wbg_ref, wn_ref, wr_ref, wm_ref, wo_ref, gffn_ref, wrt_ref, brt_ref, h_ref, hn_ref, comb_ref):
    i = pl.program_id(0)
    pick = lambda p_ref, s_ref: jnp.where(i < nbp, p_ref[...], s_ref[...])
    x = pick(xp_ref, xs_ref)
    xn = (x * lax.rsqrt(jnp.mean(x * x, axis=-1, keepdims=True) + EPS) * g_ref[...]).astype(BF16)
    bg = _sigmoid(_dot(xn, wbg_ref[...]))
    mixed = (bg[:, 0:1024] * _dot(pick(onsap_ref, onsas_ref), wn_ref[...])
             + bg[:, 1024:2048] * _dot(pick(oretp_ref, orets_ref), wr_ref[...])
             + bg[:, 2048:3072] * _dot(pick(omemp_ref, omems_ref), wm_ref[...]))
    hres = x + _dot(mixed.astype(BF16), wo_ref[...])
    h_ref[...] = hres
    hn = hres * lax.rsqrt(jnp.mean(hres * hres, axis=-1, keepdims=True) + EPS) * gffn_ref[...]
    hn_ref[...] = hn.astype(BF16)

    a, b, _ = _split3(hn)
    logits = _dot(a, wrt_ref[0]) + _dot(a, wrt_ref[1]) + _dot(b, wrt_ref[0]) + brt_ref[...]
    lane = lax.broadcasted_iota(jnp.int32, logits.shape, 1).astype(F32)
    vals, hots = [], []
    for _k in range(TOP_K):
        mx = jnp.max(logits, axis=-1, keepdims=True)
        idx = jnp.min(jnp.where(logits == mx, lane, float(LANES)), axis=-1, keepdims=True)
        hot = lane == idx
        vals.append(mx)
        hots.append(hot)
        logits = jnp.where(hot, -3.0e38, logits)
    es = [jnp.exp(v - vals[0]) for v in vals]
    den = es[0] + es[1] + es[2] + es[3]
    comb = jnp.zeros(logits.shape, F32)
    for e_k, hot in zip(es, hots):
        comb = comb + jnp.where(hot, e_k / den, 0.0)
    comb_ref[...] = comb


def _merge(xp, xs, onsa, onsa_s, oret, oret_s, omem, omem_s, g_attn, wbg, wn, wr, wm, wo, gffn, wrt, brt, tm):
    n_p, n_s = xp.shape[0], xs.shape[0]
    nbp, nbs = n_p // tm, n_s // tm
    n_all = n_p + n_s
    row = lambda w: pl.BlockSpec((tm, w), lambda i: (i, 0))
    rowp = lambda w: pl.BlockSpec((tm, w), lambda i: (jnp.minimum(i, nbp - 1), 0))
    rows = lambda w: pl.BlockSpec((tm, w), lambda i: (jnp.maximum(i - nbp, 0), 0))
    const = lambda a: pl.BlockSpec(a.shape, lambda i: (0,) * a.ndim)
    return pl.pallas_call(
        functools.partial(_merge_kernel, nbp),
        grid=(nbp + nbs,),
        in_specs=[rowp(1024), rows(1024), rowp(1024), rows(1024), rowp(512), rows(512), rowp(512), rows(512),
                  const(g_attn), const(wbg), const(wn), const(wr), const(wm),
                  const(wo), const(gffn), const(wrt), const(brt)],
        out_specs=[row(1024), row(1024), row(LANES)],
        out_shape=[jax.ShapeDtypeStruct((n_all, 1024), F32), jax.ShapeDtypeStruct((n_all, 1024), BF16),
                   jax.ShapeDtypeStruct((n_all, LANES), F32)],
        compiler_params=pltpu.CompilerParams(dimension_semantics=("arbitrary",), vmem_limit_bytes=VMEM_LIMIT),
        name="merge",
    )(xp, xs, onsa, onsa_s, oret, oret_s, omem, omem_s, g_attn, wbg, wn, wr, wm, wo, gffn, wrt, brt)


MOE_BLOCK_CAP = 1664
MOE_CHUNK = 256
ROUTE_TILE = 128


def _route_kernel(comb_ref, pos_ref, post_ref, cnt_ref):
    tb = comb_ref.shape[0]
    rt = ROUTE_TILE
    r = lax.broadcasted_iota(jnp.int32, (rt, rt), 0)
    c = lax.broadcasted_iota(jnp.int32, (rt, rt), 1)
    ltri = (c < r).astype(BF16)
    eye = (c == r).astype(BF16)
    carry = jnp.zeros((1, LANES), F32)
    for s in range(tb // rt):
        comb = comb_ref[pl.ds(rt * s, rt), :]
        sel = jnp.where(comb > 0.0, 1.0, 0.0)
        pos = jnp.where(sel > 0.0, _dot(ltri, sel.astype(BF16)) + carry, -1.0)
        carry = carry + jnp.sum(sel, axis=0, keepdims=True)
        pos_ref[pl.ds(rt * s, rt), :] = jnp.where(sel > 0.0, pos + 0.5 * comb, -1.0)
        hi = jnp.floor((pos + 1.0) * (1.0 / 256.0))
        lo = (pos + 1.0) - 256.0 * hi
        post_ref[:, rt * s:rt * (s + 1)] = 256.0 * _dot_nt(eye, hi.astype(BF16)) + _dot_nt(eye, lo.astype(BF16)) - 1.0
    cnt_ref[0] = jnp.broadcast_to(carry, (8, LANES))


def _route(comb, tb):
    n_all = comb.shape[0]
    nblk = n_all // tb
    return pl.pallas_call(
        _route_kernel,
        grid=(nblk,),
        in_specs=[pl.BlockSpec((tb, LANES), lambda i: (i, 0))],
        out_specs=[pl.BlockSpec((tb, LANES), lambda i: (i, 0)), pl.BlockSpec((LANES, tb), lambda i: (0, i)),
                   pl.BlockSpec((1, 8, LANES), lambda i: (i, 0, 0))],
        out_shape=[jax.ShapeDtypeStruct((n_all, LANES), F32), jax.ShapeDtypeStruct((LANES, n_all), F32),
                   jax.ShapeDtypeStruct((nblk, 8, LANES), F32)],
        compiler_params=pltpu.CompilerParams(dimension_semantics=("arbitrary",)),
        name="route",
    )(comb)


def _moe_kernel(n_s, cnt_ref, hn_ref, h_ref, pos_ref, post_ref, wgu_ref, wd_ref, bgu_ref, bd_ref, y_ref, ys_ref):
    i = pl.program_id(0)
    e = pl.program_id(1)
    tb = hn_ref.shape[0]
    ch = MOE_CHUNK

    @pl.when(e == 0)
    def _():
        y_ref[...] = h_ref[...]

    n_rows = cnt_ref[i * N_EXPERTS + e]
    lane = lax.broadcasted_iota(jnp.int32, (tb, LANES), 1)
    packed = jnp.sum(jnp.where(lane == e, pos_ref[...], 0.0), axis=-1, keepdims=True)
    pcol = jnp.floor(packed)
    ccol = 2.0 * (packed - pcol)
    prow = post_ref[pl.ds(e, 1), :]

    def chunk(c, carry):
        r0 = (c * ch).astype(F32)
        slot_g = r0 + lax.broadcasted_iota(jnp.int32, (ch, tb), 0).astype(F32)
        gather = jnp.where(prow == slot_g, 1.0, 0.0).astype(BF16)
        xg = _dot(gather, hn_ref[...]).astype(BF16)
        gu = _dot(xg, wgu_ref[0]) + bgu_ref[0]
        gate = jnp.minimum(gu[:, 0:D_FF], SWIGLU_LIMIT)
        up = jnp.clip(gu[:, D_FF:2 * D_FF], -SWIGLU_LIMIT, SWIGLU_LIMIT)
        act = (up + 1.0) * gate * _sigmoid(SWIGLU_ALPHA * gate)
        yc = _dot(act.astype(BF16), wd_ref[0]) + bd_ref[0]
        slot_s = r0 + lax.broadcasted_iota(jnp.int32, (tb, ch), 1).astype(F32)
        scatter = jnp.where(pcol == slot_s, ccol, 0.0).astype(BF16)
        y_ref[...] += _dot(scatter, yc.astype(BF16))
        return carry

    lax.fori_loop(0, (n_rows + ch - 1) // ch, chunk, 0)

    @pl.when((i == pl.num_programs(0) - 1) & (e == N_EXPERTS - 1))
    def _():
        ys_ref[...] = y_ref[pl.ds(tb - n_s, n_s), :]


def _moe(cnt, hn, h, pos, post, wgu, wd, b_gate_up, b_down, tb, n_s):
    n_all = hn.shape[0]
    n_p = n_all - n_s
    assert n_s <= tb and n_s % 8 == 0 and (n_all // tb - 1) * tb < n_p
    row = lambda w: pl.BlockSpec((tb, w), lambda i, e, cnt: (i, 0))
    return pl.pallas_call(
        functools.partial(_moe_kernel, n_s),
        grid_spec=pltpu.PrefetchScalarGridSpec(
            num_scalar_prefetch=1, grid=(n_all // tb, N_EXPERTS),
            in_specs=[row(1024), row(1024), row(LANES),
                      pl.BlockSpec((LANES, tb), lambda i, e, cnt: (0, i)),
                      pl.BlockSpec((1, D_MODEL, 2 * D_FF), lambda i, e, cnt: (e, 0, 0)),
                      pl.BlockSpec((1, D_FF, D_MODEL), lambda i, e, cnt: (e, 0, 0)),
                      pl.BlockSpec((1, 1, 2 * D_FF), lambda i, e, cnt: (e, 0, 0)),
                      pl.BlockSpec((1, 1, D_MODEL), lambda i, e, cnt: (e, 0, 0))],
            out_specs=[row(1024), pl.BlockSpec((n_s, 1024), lambda i, e, cnt: (0, 0))]),
        out_shape=[jax.ShapeDtypeStruct((n_p, 1024), F32), jax.ShapeDtypeStruct((n_s, 1024), F32)],
        compiler_params=pltpu.CompilerParams(dimension_semantics=("arbitrary",) * 2, vmem_limit_bytes=VMEM_LIMIT),
        name="moe",
    )(cnt, hn, h, pos, post, wgu, wd, b_gate_up, b_down)


def _pick_tile(n, pref):
    t = pref
    while n % t:
        t //= 2
    return t


def kernel(x_prompt, x_sample, cache_cmp, cache_slc, cache_win, state_ret, cache_mem, page_table, mem_prompt,
           g_attn, w_in, g_q_nsa, g_k_cmp, g_k_slc, g_k_win, pe_ck, w_ck1, w_ck2, pe_cv, w_cv1, w_cv2,
           g_ret_out, g_mem, w_mem_kv, g_q_mem, g_k_mem, w_br_nsa, w_br_ret, w_br_mem, w_out,
           g_ffn, w_router, b_router, w_gate_up, b_gate_up, w_down, b_down):
    nb, seq, _ = x_prompt.shape
    db, dseq, _ = x_sample.shape
    n_pages = page_table.shape[1]
    past = n_pages * PAGE_SIZE
    wb = cache_win.shape[1]
    mem_len = mem_prompt.shape[1]
    n_p, n_s = nb * seq, db * dseq
    n_all = n_p + n_s
    tm = _pick_tile(math.gcd(seq, n_s), 256)

    o = 0
    cols = {}
    for name, wdt in (("q", 512), ("kv", 768), ("ng", 24), ("rq", 256), ("rk", 256), ("rv", 512), ("rg", 512),
                      ("mq", 512), ("bg", 3072)):
        cols[name] = w_in[:, o:o + wdt]
        o += wdt
    wq = cols["q"].reshape(D_MODEL, NSA_HEADS, NSA_HD)
    zq = jnp.zeros_like(wq)
    wq_pad = jnp.concatenate(
        [jnp.concatenate([wq[:, hh], zq[:, hh]] if hh < NSA_GROUP else [zq[:, hh], wq[:, hh]], axis=-1)
         for hh in range(NSA_HEADS)], axis=-1)
    w_ng = jnp.pad(cols["ng"], ((0, 0), (0, C_NG - 24)))
    w1 = jnp.concatenate([wq_pad, cols["kv"], w_ng, cols["rq"], cols["rk"], cols["rv"], cols["rg"], cols["mq"]],
                         axis=-1).astype(BF16)
    w_bg = cols["bg"].astype(BF16)
    two = lambda g: jnp.concatenate([g, g]).reshape(1, LANES)
    r1 = lambda g: g.reshape(1, -1)

    half = RET_DK // 2
    inv = ROPE_BASE ** (-jnp.arange(half, dtype=F32) / half)
    pos = jnp.concatenate([jnp.arange(seq, dtype=jnp.int32),
                           jnp.tile(past + jnp.arange(dseq, dtype=jnp.int32), db)]).astype(F32)
    ang = pos[:, None] * inv[None, :]
    cos_t = jnp.tile(jnp.cos(ang), (1, 2 * RET_HEADS))
    sin_t = jnp.tile(jnp.concatenate([-jnp.sin(ang), jnp.sin(ang)], axis=-1), (1, RET_HEADS))

    xp = x_prompt.reshape(n_p, D_MODEL)
    xs = x_sample.reshape(n_s, D_MODEL)
    (q, kvc, kvs, kvw, ng, rq, rk, rv, rg, mq, kvct, kvst, kvwt) = _project(
        xp, xs, r1(g_attn), w1, cos_t, sin_t, two(g_q_nsa), two(g_k_slc), two(g_k_win), r1(g_q_mem), seq, tm)

    w1s = jnp.stack([w_ck1, w_ck1, w_cv1, w_cv1]).reshape(4, 2, CMP_STRIDE, NSA_HD, CMP_HIDDEN)
    wbig = jnp.einsum("shpdc,st->psdhtc", w1s, jnp.eye(4, dtype=F32)).reshape(CHUNK_W, 2 * 4 * CMP_HIDDEN).astype(BF16)
    pes = jnp.stack([pe_ck, pe_ck, pe_cv, pe_cv]).reshape(4, 2, CMP_STRIDE, NSA_HD)
    pe2 = jnp.pad(jnp.transpose(pes, (1, 2, 0, 3)).reshape(2, CHUNK_W), ((0, 6), (0, 0)))
    zc = jnp.zeros_like(w_ck2)
    bd2 = lambda w: jnp.concatenate([jnp.concatenate([w, zc], 1), jnp.concatenate([zc, w], 1)], 0).astype(BF16)
    gk2 = two(g_k_cmp)

    pages_p = seq // PAGE_SIZE
    pool_p = (kvc if n_all % PAGE_SIZE == 0 else kvc[:n_p]).reshape(-1, CHUNKS_PER_PAGE, CHUNK_W)
    pt_p = jnp.arange(nb * pages_p, dtype=jnp.int32).reshape(nb, pages_p)
    ck_p, cv_p = _compress(pt_p, pool_p, wbig, pe2, bd2(w_ck2), bd2(w_cv2), gk2)
    pages_t = lambda c: jnp.transpose(c, (0, 2, 3, 4, 1)).reshape(-1, 256, PAGE_SIZE)
    ck_s, cv_s = _compress_t(page_table, pages_t(cache_cmp), wbig, pe2, bd2(w_ck2), bd2(w_cv2), gk2)

    n_cmp_p = seq // CMP_STRIDE - CMP_BLOCK // CMP_STRIDE + 1
    onsa = _nsa_prompt(q, kvs, kvw, ck_p, cv_p, ng, nb, seq, n_cmp_p)

    ns_s = (past + dseq + SLC_BLOCK - 1) // SLC_BLOCK
    nsp = (ns_s + 7) // 8 * 8
    kpad = past + PAGE_SIZE
    e_s = (jnp.arange(kpad)[None, :] // SLC_BLOCK == jnp.arange(nsp)[:, None]).astype(BF16)
    s3 = lambda a: a[n_p:].reshape(db, dseq, a.shape[1])
    onsa_s, new_win_s = _nsa_sample(page_table, s3(q), s3(kvs), s3(kvw), ck_s, cv_s, s3(ng),
                                    cache_win.reshape(db, wb, 256), e_s,
                                    pages_t(cache_slc), past)
    onsa_s = onsa_s.reshape(n_s, 1024).astype(BF16)

    gro = r1(g_ret_out)
    oret, ret_state_p = _retention(rq, rk, rv, rg, jnp.zeros((nb, RET_HEADS, RET_DK, RET_DV), F32), gro,
                                   nb, seq // RET_CHUNK, RET_CHUNK)
    padc = lambda a: jnp.pad(s3(a), ((0, 0), (0, RET_CHUNK - dseq), (0, 0))).reshape(db * RET_CHUNK, a.shape[1])
    oret_s, ret_state_s = _retention(padc(rq), padc(rk), padc(rv), padc(rg), state_ret, gro, db, 1, dseq)
    oret = oret.reshape(n_p, 512)
    oret_s = oret_s[:, :dseq].reshape(n_s, 512)

    mem_kv_p = _mem_kv(mem_prompt.reshape(nb * mem_len, D_MODEL), r1(g_mem), w_mem_kv.astype(BF16), r1(g_k_mem),
                       _pick_tile(nb * mem_len, 256))
    tq = _pick_tile(seq, 512)
    omem = _mem_attend(mq, mem_kv_p, nb, seq // tq, tq, mem_len, n_p, BF16)
    omem_s = _mem_attend_cache(mq[n_p:].astype(F32), cache_mem.reshape(db, mem_len * 2 * MEM_HEADS, MEM_HD),
                               dseq).astype(BF16)

    wn = w_br_nsa.reshape(NSA_HEADS, NSA_HD, D_MODEL)
    zn = jnp.zeros_like(wn)
    wn_pad = jnp.concatenate(
        [jnp.concatenate([wn[hh], zn[hh]] if hh < NSA_GROUP else [zn[hh], wn[hh]], axis=0) for hh in range(NSA_HEADS)],
        axis=0).astype(BF16)
    wr_pad = jnp.pad(w_router, ((0, 0), (0, LANES - N_EXPERTS)))
    wr_hi = wr_pad.astype(BF16)
    wr_lo = (wr_pad - wr_hi.astype(F32)).astype(BF16)
    brt = jnp.concatenate([b_router, jnp.full((LANES - N_EXPERTS,), NEG, F32)]).reshape(1, LANES)
    h, hn, comb = _merge(xp, xs, onsa, onsa_s, oret, oret_s, omem, omem_s, r1(g_attn), w_bg, wn_pad, w_br_ret.astype(BF16),
                         w_br_mem.astype(BF16), w_out.astype(BF16), r1(g_ffn), jnp.stack([wr_hi, wr_lo]), brt, tm)

    tb = max(c for c in range(LANES, MOE_BLOCK_CAP + 1, LANES) if n_all % c == 0)
    pos, post, cnt = _route(comb, tb)
    cnt = cnt[:, 0, :N_EXPERTS].astype(jnp.int32).reshape(-1)
    y_p, y_s = _moe(cnt, hn, h, pos, post, w_gate_up.astype(BF16), w_down.astype(BF16),
                    b_gate_up.reshape(N_EXPERTS, 1, 2 * D_FF), b_down.reshape(N_EXPERTS, 1, D_MODEL), tb, n_s)

    kv5 = lambda a, bsz, t: a.reshape(bsz, t, 2, NSA_KV_HEADS, NSA_HD)
    wp = min(WINDOW, seq)
    kv5t = lambda a: jnp.transpose(a.reshape(nb, 2, NSA_KV_HEADS, NSA_HD, a.shape[2]), (0, 4, 1, 2, 3))
    return (y_p.reshape(nb, seq, D_MODEL), y_s.reshape(db, dseq, D_MODEL),
            kv5t(kvct), kv5t(kvst), kv5t(kvwt[:, :, seq - wp:]),
            ret_state_p, mem_kv_p.reshape(nb, mem_len, 2, MEM_HEADS, MEM_HD),
            kv5(kvc[n_p:], db, dseq), kv5(kvs[n_p:], db, dseq), kv5(new_win_s, db, wb), ret_state_s)
```

```python
import functools
import math

import jax
import jax.numpy as jnp
from jax import lax
from jax.experimental import pallas as pl
from jax.experimental.pallas import tpu as pltpu

F32 = jnp.float32
BF16 = jnp.bfloat16

D_MODEL = 1024
NSA_HEADS = 8
NSA_KV_HEADS = 2
NSA_GROUP = 4
NSA_HD = 64
CMP_BLOCK = 32
CMP_STRIDE = 16
CMP_HIDDEN = 128
SLC_BLOCK = 64
N_SELECT = 16
N_LOCAL_FORCED = 2
FORCE_BONUS = 1.0e4
WINDOW = 512
Q_BLOCK = 128
PAGE_SIZE = 128
RET_HEADS = 4
RET_DK = 64
RET_DV = 128
RET_CHUNK = 128
ROPE_BASE = 10000.0
MEM_HEADS = 4
MEM_HD = 128
N_EXPERTS = 32
TOP_K = 4
D_FF = 1024
SWIGLU_ALPHA = 1.702
SWIGLU_LIMIT = 7.0
EPS = 1e-6
NEG = -1e30

LANES = 128
ROW_TILE = 512
VMEM_LIMIT = 56 * 1024 * 1024


def _dot(a, b):
    return jnp.dot(a, b, preferred_element_type=F32)


def _dot_nt(a, b):
    return lax.dot_general(a, b, (((1,), (1,)), ((), ())), preferred_element_type=F32)


def _sigmoid(x):
    return 1.0 / (1.0 + jnp.exp(-x))


def _split3(x):
    a = x.astype(BF16)
    r = x - a.astype(F32)
    b = r.astype(BF16)
    c = (r - b.astype(F32)).astype(BF16)
    return a, b, c


def _msoftmax(s, m):
    s = jnp.where(m, s, NEG)
    mx = jnp.max(s, axis=-1, keepdims=True)
    p = jnp.where(m, jnp.exp(s - mx), 0.0)
    return p / jnp.maximum(jnp.sum(p, axis=-1, keepdims=True), 1e-30)


def _halfnorm(t, g2):
    lane = lax.broadcasted_iota(jnp.int32, t.shape, 1)
    lo = lane < NSA_HD
    t2 = t * t
    s0 = jnp.sum(jnp.where(lo, t2, 0.0), axis=-1, keepdims=True)
    s1 = jnp.sum(jnp.where(lo, 0.0, t2), axis=-1, keepdims=True)
    r = jnp.where(lo, lax.rsqrt(s0 * (1.0 / NSA_HD) + EPS), lax.rsqrt(s1 * (1.0 / NSA_HD) + EPS))
    return t * r * g2


def _tilenorm(t, g, width):
    return t * lax.rsqrt(jnp.sum(t * t, axis=-1, keepdims=True) * (1.0 / width) + EPS) * g


def _rot(x, cos, sin):
    lane = lax.broadcasted_iota(jnp.int32, x.shape, 1)
    first = (lane % RET_DK) < (RET_DK // 2)
    n = x.shape[1]
    sw = jnp.where(first, pltpu.roll(x, n - RET_DK // 2, 1), pltpu.roll(x, RET_DK // 2, 1))
    return x * cos + sw * sin


C_Q, C_KV, C_NG, C_RET, C_MQ = 1024, 768, 128, 1536, 512
O_KV = C_Q
O_NG = O_KV + C_KV
O_RET = O_NG + C_NG
O_MQ = O_RET + C_RET
W1_COLS = O_MQ + C_MQ


def _proj_kernel(nbp, xp_ref, xs_ref, g_ref, w_ref, cos_ref, sin_ref, gq_ref, gks_ref, gkw_ref, gqm_ref,
                 q_ref, kvc_ref, kvs_ref, kvw_ref, ng_ref, rq_ref, rk_ref, rv_ref, rg_ref, mq_ref,
                 kvct_ref, kvst_ref, kvwt_ref):
    i = pl.program_id(0)
    x = jnp.where(i < nbp, xp_ref[...], xs_ref[...])
    xn = (x * lax.rsqrt(jnp.mean(x * x, axis=-1, keepdims=True) + EPS) * g_ref[...]).astype(BF16)

    hq = _dot(xn, w_ref[:, 0:C_Q])
    for hh in range(NSA_HEADS):
        t = hq[:, LANES * hh:LANES * (hh + 1)]
        q_ref[:, LANES * hh:LANES * (hh + 1)] = (_tilenorm(t, gq_ref[...], NSA_HD) * NSA_HD ** -0.5).astype(BF16)

    hkv = _dot(xn, w_ref[:, O_KV:O_KV + C_KV])
    kvc = hkv[:, 0:256]
    kvs = jnp.concatenate([_halfnorm(hkv[:, 256:384], gks_ref[...]), hkv[:, 384:512]], axis=1)
    kvw = jnp.concatenate([_halfnorm(hkv[:, 512:640], gkw_ref[...]), hkv[:, 640:768]], axis=1)
    kvc_ref[...] = kvc
    kvs_ref[...] = kvs
    kvw_ref[...] = kvw

    @pl.when(i < nbp)
    def _():
        kvct_ref[0] = kvc.T
        kvst_ref[0] = kvs.T
        kvwt_ref[0] = kvw.T

    ng_ref[...] = _sigmoid(_dot(xn, w_ref[:, O_NG:O_NG + C_NG]))

    hr = _dot(xn, w_ref[:, O_RET:O_RET + C_RET])
    cos = cos_ref[...]
    sin = sin_ref[...]
    rq_ref[...] = _rot(hr[:, 0:256], cos, sin)
    rk_ref[...] = _rot(hr[:, 256:512], cos, sin) * RET_DK ** -0.5
    rv_ref[...] = hr[:, 512:1024]
    rg = hr[:, 1024:1536]
    rg_ref[...] = rg * _sigmoid(rg)

    hm = _dot(xn, w_ref[:, O_MQ:O_MQ + C_MQ])
    for hd in range(MEM_HEADS):
        t = hm[:, LANES * hd:LANES * (hd + 1)]
        mq_ref[:, LANES * hd:LANES * (hd + 1)] = (_tilenorm(t, gqm_ref[...], MEM_HD) * MEM_HD ** -0.5).astype(BF16)


def _project(xp, xs, g_attn, w1, cos_t, sin_t, gq2, gks2, gkw2, gqm, seq, tm):
    n_p, n_s = xp.shape[0], xs.shape[0]
    nbp, nbs = n_p // tm, -(-n_s // tm)
    n_all = n_p + n_s
    bps = seq // tm
    row = lambda w: pl.BlockSpec((tm, w), lambda i: (i, 0))
    const = lambda a: pl.BlockSpec(a.shape, lambda i: (0,) * a.ndim)
    rope = pl.BlockSpec((tm, 256), lambda i: (jnp.where(i < nbp, i % bps, bps + i - nbp), 0))
    widths = (1024, 256, 256, 256, 128, 256, 256, 512, 512, 512)
    dtypes = (BF16, F32, F32, F32, F32, F32, F32, F32, F32, BF16)
    kvt = pl.BlockSpec((1, 256, tm), lambda i: (jnp.minimum(i, nbp - 1) // bps, 0, jnp.minimum(i, nbp - 1) % bps))
    return pl.pallas_call(
        functools.partial(_proj_kernel, nbp),
        grid=(nbp + nbs,),
        in_specs=[pl.BlockSpec((tm, D_MODEL), lambda i: (jnp.minimum(i, nbp - 1), 0)),
                  pl.BlockSpec((tm, D_MODEL), lambda i: (jnp.maximum(i - nbp, 0), 0)),
                  const(g_attn), const(w1), rope, rope, const(gq2), const(gks2), const(gkw2), const(gqm)],
        out_specs=[row(w) for w in widths] + [kvt] * 3,
        out_shape=[jax.ShapeDtypeStruct((n_all, w), d) for w, d in zip(widths, dtypes)]
        + [jax.ShapeDtypeStruct((n_p // seq, 256, seq), F32)] * 3,
        compiler_params=pltpu.CompilerParams(dimension_semantics=("arbitrary",), vmem_limit_bytes=VMEM_LIMIT),
        name="proj",
    )(xp, xs, g_attn, w1, cos_t, sin_t, gq2, gks2, gkw2, gqm)


CHUNK_W = CMP_STRIDE * 256
CHUNKS_PER_PAGE = PAGE_SIZE // CMP_STRIDE


def _compress_kernel(n_pages, pt_ref, pool_ref, wbig_ref, pe_ref, w2k_ref, w2v_ref, gk_ref,
                     ck_ref, cv_ref, x_scr, r_scr, sem):
    b = pl.program_id(0)
    n = n_pages * CHUNKS_PER_PAGE

    def page_copy(j):
        return pltpu.make_async_copy(pool_ref.at[pt_ref[b, j]],
                                     x_scr.at[pl.ds(CHUNKS_PER_PAGE * j, CHUNKS_PER_PAGE)], sem)

    for j in range(n_pages):
        page_copy(j).start()
    x_scr[pl.ds(n, 8), :] = pe_ref[...]
    for j in range(n_pages):
        page_copy(j).wait()

    r_scr[...] = _dot(x_scr[...].astype(BF16), wbig_ref[...])
    cvec = r_scr[n:n + 1, 0:512] + r_scr[n + 1:n + 2, 512:1024]
    hid = r_scr[0:n, 0:512] + r_scr[pl.ds(1, n), 512:1024] + cvec
    hb = (hid * _sigmoid(hid)).astype(BF16)
    ck_ref[0] = _halfnorm(_dot(hb[:, 0:256], w2k_ref[...]), gk_ref[...])
    cv_ref[0] = _dot(hb[:, 256:512], w2v_ref[...])


def _compress(page_table, pool, wbig, pe2, w2k, w2v, gk2):
    nb, n_pages = page_table.shape
    n = n_pages * CHUNKS_PER_PAGE
    const = lambda a: pl.BlockSpec(a.shape, lambda b, pt: (0,) * a.ndim)
    out = pl.BlockSpec((1, n, LANES), lambda b, pt: (b, 0, 0))
    return pl.pallas_call(
        functools.partial(_compress_kernel, n_pages),
        grid_spec=pltpu.PrefetchScalarGridSpec(
            num_scalar_prefetch=1, grid=(nb,),
            in_specs=[pl.BlockSpec(memory_space=pl.ANY), const(wbig), const(pe2), const(w2k), const(w2v), const(gk2)],
            out_specs=[out, out],
            scratch_shapes=[pltpu.VMEM((n + 8, CHUNK_W), F32), pltpu.VMEM((n + 8, 1024), F32),
                            pltpu.SemaphoreType.DMA(())]),
        out_shape=[jax.ShapeDtypeStruct((nb, n, LANES), F32)] * 2,
        compiler_params=pltpu.CompilerParams(dimension_semantics=("arbitrary",), vmem_limit_bytes=VMEM_LIMIT),
        name="compress",
    )(page_table, pool, wbig, pe2, w2k, w2v, gk2)


def _compress_t_kernel(n_pages, pt_ref, pool_ref, wbig_ref, pe_ref, w2k_ref, w2v_ref, gk_ref,
                       ck_ref, cv_ref, pg_scr, t_scr, r_scr, cvec_scr, sem):
    b = pl.program_id(0)
    slot = b % 2
    n = n_pages * CHUNKS_PER_PAGE

    def page_copy(bb, sl, j):
        return pltpu.make_async_copy(pool_ref.at[pt_ref[bb, j]], pg_scr.at[sl, j], sem.at[sl])

    @pl.when(b == 0)
    def _():
        for j in range(n_pages):
            page_copy(0, 0, j).start()
        r = _dot(pe_ref[...].astype(BF16), wbig_ref[...])
        cvec_scr[...] = jnp.broadcast_to(r[0:1, 0:512] + r[1:2, 512:1024], cvec_scr.shape)
        r_scr[pl.ds(n, 8), :] = jnp.zeros((8, 1024), F32)

    @pl.when(b + 1 < pl.num_programs(0))
    def _():
        for j in range(n_pages):
            page_copy(b + 1, 1 - slot, j).start()

    for j in range(n_pages):
        page_copy(b, slot, j).wait()

    src = lax.broadcasted_iota(jnp.int32, (PAGE_SIZE, PAGE_SIZE), 0)
    dst = lax.broadcasted_iota(jnp.int32, (PAGE_SIZE, PAGE_SIZE), 1)
    perm = (dst == CHUNKS_PER_PAGE * (src % CMP_STRIDE) + src // CMP_STRIDE).astype(BF16)
    for j in range(n_pages):
        xp = _dot(pg_scr[slot, j].astype(BF16), perm)
        tk = xp[0:LANES, :].T
        tv = xp[LANES:2 * LANES, :].T
        for p in range(CMP_STRIDE):
            rows = pl.ds(CHUNKS_PER_PAGE * j, CHUNKS_PER_PAGE)
            t_scr[p, rows, 0:LANES] = tk[CHUNKS_PER_PAGE * p:CHUNKS_PER_PAGE * (p + 1)]
            t_scr[p, rows, LANES:2 * LANES] = tv[CHUNKS_PER_PAGE * p:CHUNKS_PER_PAGE * (p + 1)]

    acc = jnp.zeros((n, 1024), F32)
    for p in range(CMP_STRIDE):
        acc = acc + _dot(t_scr[p].astype(BF16), wbig_ref[256 * p:256 * (p + 1), :])
    r_scr[pl.ds(0, n), :] = acc
    hid = acc[:, 0:512] + r_scr[pl.ds(1, n), 512:1024] + cvec_scr[0:1, :]
    hb = (hid * _sigmoid(hid)).astype(BF16)
    ck_ref[0] = _halfnorm(_dot(hb[:, 0:256], w2k_ref[...]), gk_ref[...])
    cv_ref[0] = _dot(hb[:, 256:512], w2v_ref[...])


def _compress_t(page_table, pool_t, wbig, pe2, w2k, w2v, gk2):
    nb, n_pages = page_table.shape
    n = n_pages * CHUNKS_PER_PAGE
    const = lambda a: pl.BlockSpec(a.shape, lambda b, pt: (0,) * a.ndim)
    out = pl.BlockSpec((1, n, LANES), lambda b, pt: (b, 0, 0))
    return pl.pallas_call(
        functools.partial(_compress_t_kernel, n_pages),
        grid_spec=pltpu.PrefetchScalarGridSpec(
            num_scalar_prefetch=1, grid=(nb,),
            in_specs=[pl.BlockSpec(memory_space=pl.ANY), const(wbig), const(pe2), const(w2k), const(w2v), const(gk2)],
            out_specs=[out, out],
            scratch_shapes=[pltpu.VMEM((2, n_pages, 256, PAGE_SIZE), F32), pltpu.VMEM((CMP_STRIDE, n, 2 * LANES), F32),
                            pltpu.VMEM((n + 8, 1024), F32), pltpu.VMEM((8, 512), F32),
                            pltpu.SemaphoreType.DMA((2,))]),
        out_shape=[jax.ShapeDtypeStruct((nb, n, LANES), F32)] * 2,
        compiler_params=pltpu.CompilerParams(dimension_semantics=("arbitrary",), vmem_limit_bytes=VMEM_LIMIT),
        name="compress_t",
    )(page_table, pool_t, wbig, pe2, w2k, w2v, gk2)


def _overlap_t(ns_rows, n_cmp_cols, n_cmp):
    s = lax.broadcasted_iota(jnp.int32, (ns_rows, n_cmp_cols), 0)
    n = lax.broadcasted_iota(jnp.int32, (ns_rows, n_cmp_cols), 1)
    ov = (n * CMP_STRIDE < s * SLC_BLOCK + SLC_BLOCK) & (n * CMP_STRIDE + CMP_BLOCK > s * SLC_BLOCK) & (n < n_cmp)
    return ov.astype(BF16)


def _importance_t(ov_t, pcs):
    a, b, c = _split3(pcs)
    return _dot_nt(ov_t, a) + _dot_nt(ov_t, b) + _dot_nt(ov_t, c)


def _block_scores_t(imp_t, tl):
    blk = lax.broadcasted_iota(jnp.int32, imp_t.shape, 0)
    cur = tl // SLC_BLOCK
    valid = blk <= cur
    forced = (blk == 0) | (valid & (blk > cur - N_LOCAL_FORCED))
    return jnp.where(valid, imp_t + jnp.where(forced, FORCE_BONUS, 0.0), NEG), blk


SLC_CHUNK = 1024
RANK_ACCS = 4


def _nsa_prompt_kernel(seq, n_cmp, q_ref, kvs_ref, kvw_ref, ck_ref, cv_ref, ng_ref, et_ref, eg_ref, o_ref):
    i = pl.program_id(1)
    ns = seq // SLC_BLOCK
    nh = NSA_HEADS
    q8 = q_ref[...]
    qs = jnp.concatenate([q8[:, LANES * hh:LANES * (hh + 1)] for hh in range(nh)], axis=0)
    tq = i * Q_BLOCK + lax.broadcasted_iota(jnp.int32, (Q_BLOCK, 1), 0)
    rep = lambda a: jnp.concatenate([a] * nh, axis=0)
    t8 = rep(tq)

    ckb = ck_ref[0].astype(BF16)
    sc = _dot_nt(qs, ckb)
    jn = lax.broadcasted_iota(jnp.int32, sc.shape, 1)
    pc = _msoftmax(sc, (jn * CMP_STRIDE + CMP_BLOCK - 1) <= t8)
    o_cmp = _dot(pc.astype(BF16), cv_ref[0].astype(BF16))

    ov_t = _overlap_t(ns, pc.shape[1], n_cmp)
    tl = i * Q_BLOCK + lax.broadcasted_iota(jnp.int32, (ns, Q_BLOCK), 1)
    eye = (lax.broadcasted_iota(jnp.int32, (Q_BLOCK, Q_BLOCK), 0)
           == lax.broadcasted_iota(jnp.int32, (Q_BLOCK, Q_BLOCK), 1)).astype(BF16)
    lane_q = lax.broadcasted_iota(jnp.int32, (Q_BLOCK, LANES), 1)
    zpad = lambda n: [jnp.zeros((n, Q_BLOCK), BF16)] if n else []
    qxs = []
    for h in range(NSA_KV_HEADS):
        r0 = h * NSA_GROUP * Q_BLOCK
        pcs = pc[r0:r0 + 128] + pc[r0 + 128:r0 + 256] + pc[r0 + 256:r0 + 384] + pc[r0 + 384:r0 + 512]
        score, blk = _block_scores_t(_importance_t(ov_t, pcs), tl)
        ranks = [jnp.zeros(score.shape, F32) for _ in range(RANK_ACCS)]
        for k in range(ns):
            sk = score[k:k + 1, :]
            ranks[k % RANK_ACCS] += jnp.where((sk > score) | ((sk == score) & (blk > k)), 1.0, 0.0)
        rank = functools.reduce(lambda x, y: x + y, ranks)
        sel_t = jnp.where(rank < min(N_SELECT, ns), 1.0, 0.0).astype(BF16)
        sel_t = jnp.concatenate(zpad(NSA_HD * (1 - h)) + [sel_t] + zpad(LANES - ns - NSA_HD * (1 - h)), axis=0)
        in_blk = (lane_q >= NSA_HD) if h == 0 else (lane_q < NSA_HD)
        pen = jnp.where(in_blk, (_dot_nt(eye, sel_t) - 1.0) * -NEG, 0.0)
        qh = qs[r0:r0 + NSA_GROUP * Q_BLOCK]
        qxs.append(qh + jnp.concatenate([pen] * NSA_GROUP, axis=0).astype(BF16))

    kc = min(SLC_CHUNK, seq)
    lane_1 = lax.broadcasted_iota(jnp.int32, (1, LANES), 1)
    keep_lo = jnp.where(lane_1 < NSA_HD, 1.0, 0.0).astype(BF16)
    keep_hi = jnp.where(lane_1 < NSA_HD, 0.0, 1.0).astype(BF16)
    ones = jnp.ones((kc, LANES), BF16)

    def chunk(c, carry, bias):
        m, l, acc = carry
        k0 = pl.multiple_of(c * kc, kc)
        kk = kvs_ref[pl.ds(k0, kc), 0:128].astype(BF16)
        et = et_ref[pl.ds(k0, kc), :]
        vx = jnp.concatenate([kvs_ref[pl.ds(k0, kc), 128:256].astype(BF16), ones], axis=1)
        s = jnp.concatenate([_dot_nt(qxs[0], kk * keep_lo + et * keep_hi),
                             _dot_nt(qxs[1], kk * keep_hi + et * keep_lo)], axis=0)
        if bias is not None:
            s = s + bias
        m_new = jnp.maximum(m, jnp.max(s, axis=-1, keepdims=True))
        alpha = jnp.exp(m - m_new)
        pv = _dot(jnp.exp(s - m_new).astype(BF16), vx)
        return m_new, alpha * l + pv[:, LANES:LANES + 1], alpha * acc + pv[:, 0:LANES]

    rows = nh * Q_BLOCK
    init = (jnp.full((rows, 1), NEG, F32), jnp.zeros((rows, 1), F32), jnp.zeros((rows, LANES), F32))
    c_last = (i * Q_BLOCK) // kc
    carry = lax.fori_loop(0, c_last, lambda c, cr: chunk(c, cr, None), init)
    kpos = c_last * kc + lax.broadcasted_iota(jnp.int32, (Q_BLOCK, kc), 1)
    _, l, acc = chunk(c_last, carry, rep(jnp.where(kpos <= tq, 0.0, NEG)))
    o_slc = acc / l

    wk = WINDOW + Q_BLOCK
    start = pl.multiple_of(Q_BLOCK * jnp.maximum(i - WINDOW // Q_BLOCK, 0), Q_BLOCK)
    kk = kvw_ref[pl.ds(start, wk), 0:128].astype(BF16)
    vx = jnp.concatenate([kvw_ref[pl.ds(start, wk), 128:256].astype(BF16), jnp.ones((wk, LANES), BF16)], axis=1)
    d = tq - (start + lax.broadcasted_iota(jnp.int32, (Q_BLOCK, wk), 1))
    sw = _dot_nt(qs, kk) + rep(jnp.where((d >= 0) & (d < WINDOW), 0.0, NEG))
    aw = _dot(jnp.exp(sw - jnp.max(sw, axis=-1, keepdims=True)).astype(BF16), vx)
    o_win = aw[:, 0:LANES] / aw[:, LANES:LANES + 1]

    a, b, c = _split3(ng_ref[...])
    for h in range(NSA_KV_HEADS):
        eg = eg_ref[h]
        gexp = _dot(a, eg) + _dot(b, eg) + _dot(c, eg)
        for g in range(NSA_GROUP):
            hh = NSA_GROUP * h + g
            rs = slice(Q_BLOCK * hh, Q_BLOCK * (hh + 1))
            gate = lambda br: gexp[:, LANES * (NSA_GROUP * br + g):LANES * (NSA_GROUP * br + g + 1)]
            o = gate(0) * o_cmp[rs] + gate(1) * o_slc[rs] + gate(2) * o_win[rs]
            o_ref[:, LANES * hh:LANES * (hh + 1)] = o.astype(BF16)


def _nsa_prompt(q, kvs, kvw, ck, cv, ng, nb, seq, n_cmp):
    nq = seq // Q_BLOCK
    ncp = ck.shape[1]
    ns = seq // SLC_BLOCK
    assert ns <= NSA_HD and seq >= WINDOW + Q_BLOCK
    et = (jnp.arange(seq)[:, None] // SLC_BLOCK == jnp.arange(LANES)[None, :] % NSA_HD).astype(BF16)
    n_g = 3 * NSA_GROUP
    col = jnp.arange(n_g * LANES)[None, None, :] // LANES
    src = (col // NSA_GROUP) * NSA_HEADS + NSA_GROUP * jnp.arange(NSA_KV_HEADS)[:, None, None] + col % NSA_GROUP
    eg = (jnp.arange(LANES)[None, :, None] == src).astype(BF16)
    return pl.pallas_call(
        functools.partial(_nsa_prompt_kernel, seq, n_cmp),
        grid=(nb, nq),
        in_specs=[pl.BlockSpec((Q_BLOCK, 1024), lambda b, i: (b * nq + i, 0)),
                  pl.BlockSpec((seq, 256), lambda b, i: (b, 0)),
                  pl.BlockSpec((seq, 256), lambda b, i: (b, 0)),
                  pl.BlockSpec((1, ncp, LANES), lambda b, i: (b, 0, 0)),
                  pl.BlockSpec((1, ncp, LANES), lambda b, i: (b, 0, 0)),
                  pl.BlockSpec((Q_BLOCK, LANES), lambda b, i: (b * nq + i, 0)),
                  pl.BlockSpec((seq, LANES), lambda b, i: (0, 0)),
                  pl.BlockSpec((NSA_KV_HEADS, LANES, n_g * LANES), lambda b, i: (0, 0, 0))],
        out_specs=pl.BlockSpec((Q_BLOCK, 1024), lambda b, i: (b * nq + i, 0)),
        out_shape=jax.ShapeDtypeStruct((nb * seq, 1024), BF16),
        compiler_params=pltpu.CompilerParams(dimension_semantics=("arbitrary",) * 2, vmem_limit_bytes=VMEM_LIMIT),
        name="nsa_prompt",
    )(q, kvs, kvw, ck, cv, ng, et, eg)


def _nsa_sample_kernel(past, n_pages, dseq, wb, pt_ref, q_ref, kvs_ref, kvw_ref, ck_ref, cv_ref, ng_ref, cwin_ref,
                       e_ref, pool_ref, o_ref, nwin_ref, kvt_scr, kw_scr, sem):
    b = pl.program_id(0)
    slot = b % 2

    def page_copy(bb, sl, j):
        return pltpu.make_async_copy(pool_ref.at[pt_ref[bb, j]], kvt_scr.at[sl, :, pl.ds(PAGE_SIZE * j, PAGE_SIZE)],
                                     sem.at[sl])

    @pl.when(b == 0)
    def _():
        for j in range(n_pages):
            page_copy(0, 0, j).start()

    @pl.when(b + 1 < pl.num_programs(0))
    def _():
        for j in range(n_pages):
            page_copy(b + 1, 1 - slot, j).start()

    n_heads = NSA_KV_HEADS * NSA_GROUP
    rows = n_heads * dseq
    kpad = past + PAGE_SIZE
    ns = (past + dseq + SLC_BLOCK - 1) // SLC_BLOCK
    nsp = e_ref.shape[0]
    qf = q_ref[0].astype(F32)
    qs = jnp.concatenate([qf[:, LANES * hh:LANES * (hh + 1)] for hh in range(n_heads)], axis=0).astype(BF16)
    t1 = past + lax.broadcasted_iota(jnp.int32, (dseq, 1), 0)
    t = jnp.concatenate([t1] * n_heads, axis=0)

    sc = _dot_nt(qs, ck_ref[0].astype(BF16))
    jn = lax.broadcasted_iota(jnp.int32, sc.shape, 1)
    pc = _msoftmax(sc, (jn * CMP_STRIDE + CMP_BLOCK - 1) <= t)
    o_cmp = _dot(pc.astype(BF16), cv_ref[0].astype(BF16))

    per_kv = []
    for kvh in range(NSA_KV_HEADS):
        base = kvh * NSA_GROUP * dseq
        s = pc[base:base + dseq]
        for g in range(1, NSA_GROUP):
            s = s + pc[base + g * dseq:base + (g + 1) * dseq]
        per_kv += [s] * NSA_GROUP
    pcs = jnp.concatenate(per_kv, axis=0)
    n_cmp = (ns * SLC_BLOCK) // CMP_STRIDE - CMP_BLOCK // CMP_STRIDE + 1
    imp_t = _importance_t(_overlap_t(nsp, pcs.shape[1], n_cmp), pcs)
    tl = past + lax.broadcasted_iota(jnp.int32, (nsp, rows), 1) % dseq
    score, blk = _block_scores_t(imp_t, tl)
    ranks = [jnp.zeros(score.shape, F32) for _ in range(RANK_ACCS)]
    for k in range(ns):
        sk = score[k:k + 1, :]
        ranks[k % RANK_ACCS] += jnp.where((sk > score) | ((sk == score) & (blk > k)), 1.0, 0.0)
    rank = functools.reduce(lambda x, y: x + y, ranks)
    sel_t = jnp.where(rank < min(N_SELECT, ns), 1.0, 0.0).astype(BF16)
    eye = (lax.broadcasted_iota(jnp.int32, (rows, rows), 0)
           == lax.broadcasted_iota(jnp.int32, (rows, rows), 1)).astype(BF16)
    sel = _dot_nt(eye, sel_t)
    sele = _dot(sel.astype(BF16), e_ref[...])

    knew = jnp.concatenate([kvs_ref[0], jnp.zeros((PAGE_SIZE - dseq, 256), F32)], axis=0).astype(BF16)
    for j in range(n_pages):
        page_copy(b, slot, j).wait()
    s = jnp.concatenate([_dot(qs, kvt_scr[slot, 0:128, :].astype(BF16)), _dot_nt(qs, knew[:, 0:128])], axis=1)
    kpos = lax.broadcasted_iota(jnp.int32, (rows, kpad), 1)
    p = _msoftmax(s, (sele > 0.5) & (kpos <= t)).astype(BF16)
    o_slc = _dot_nt(p[:, 0:past], kvt_scr[slot, 128:256, :].astype(BF16)) + _dot(p[:, past:kpad], knew[:, 128:256])

    wpad = kw_scr.shape[0]
    kw_scr[pl.ds(0, wb), :] = cwin_ref[0]
    kw_scr[pl.ds(wb, dseq), :] = kvw_ref[0]
    kw_scr[pl.ds(wb + dseq, wpad - wb - dseq), :] = jnp.zeros((wpad - wb - dseq, 256), F32)
    sw = _dot_nt(qs, kw_scr[:, 0:128].astype(BF16))
    d = t - (past - wb + lax.broadcasted_iota(jnp.int32, (rows, wpad), 1))
    pw = _msoftmax(sw, (d >= 0) & (d < WINDOW))
    o_win = _dot(pw.astype(BF16), kw_scr[:, 128:256].astype(BF16))
    nwin_ref[0, pl.ds(0, wb - dseq), :] = cwin_ref[0, pl.ds(dseq, wb - dseq), :]
    nwin_ref[0, pl.ds(wb - dseq, dseq), :] = kvw_ref[0]

    ngv = ng_ref[0]
    for hh in range(n_heads):
        rs = slice(dseq * hh, dseq * (hh + 1))
        gate = lambda br: ngv[:, br * NSA_HEADS + hh:br * NSA_HEADS + hh + 1]
        o_ref[0, :, LANES * hh:LANES * (hh + 1)] = gate(0) * o_cmp[rs] + gate(1) * o_slc[rs] + gate(2) * o_win[rs]


def _nsa_sample(page_table, q, kvs, kvw, ck, cv, ng, cwin, e_s, pool, past):
    nb, n_pages = page_table.shape
    dseq = q.shape[1]
    wb = cwin.shape[1]
    kpad = past + PAGE_SIZE
    wpad = wb + PAGE_SIZE
    nsp = e_s.shape[0]
    rows = NSA_HEADS * dseq
    per_b = lambda a: pl.BlockSpec((1,) + a.shape[1:], lambda b, pt: (b,) + (0,) * (a.ndim - 1))
    return pl.pallas_call(
        functools.partial(_nsa_sample_kernel, past, n_pages, dseq, wb),
        grid_spec=pltpu.PrefetchScalarGridSpec(
            num_scalar_prefetch=1, grid=(nb,),
            in_specs=[per_b(q), per_b(kvs), per_b(kvw), per_b(ck), per_b(cv), per_b(ng), per_b(cwin),
                      pl.BlockSpec(e_s.shape, lambda b, pt: (0, 0)), pl.BlockSpec(memory_space=pl.ANY)],
            out_specs=[pl.BlockSpec((1, dseq, 1024), lambda b, pt: (b, 0, 0)),
                       pl.BlockSpec((1, wb, 256), lambda b, pt: (b, 0, 0))],
            scratch_shapes=[pltpu.VMEM((2, 256, past), F32), pltpu.VMEM((wpad, 256), F32),
                            pltpu.SemaphoreType.DMA((2,))]),
        out_shape=[jax.ShapeDtypeStruct((nb, dseq, 1024), F32), jax.ShapeDtypeStruct((nb, wb, 256), F32)],
        compiler_params=pltpu.CompilerParams(dimension_semantics=("arbitrary",), vmem_limit_bytes=VMEM_LIMIT),
        name="nsa_sample",
    )(page_table, q, kvs, kvw, ck, cv, ng, cwin, e_s, pool)


RET_GROUP = 4


def _ret_kernel(bg, c_true, *refs):
    rq_refs, rk_refs, rv_refs, rg_refs = (refs[k * bg:(k + 1) * bg] for k in range(4))
    s0_ref, gro_ref, o_ref, sout_ref, s_scr = refs[4 * bg:]
    c = pl.program_id(1)
    cs = RET_CHUNK

    @pl.when(c == 0)
    def _():
        for b in range(bg):
            for tile in range(RET_HEADS // 2):
                s_scr[b, tile] = jnp.concatenate([s0_ref[b, 2 * tile], s0_ref[b, 2 * tile + 1]], axis=0)

    diff = (lax.broadcasted_iota(jnp.int32, (cs, cs), 0) - lax.broadcasted_iota(jnp.int32, (cs, cs), 1)).astype(F32)
    ic = lax.broadcasted_iota(jnp.int32, (cs, 1), 0).astype(F32)
    lane = lax.broadcasted_iota(jnp.int32, (cs, LANES), 1)
    for hd in range(RET_HEADS):
        lg = math.log(1.0 - 2.0 ** (-5.0 - hd))
        tile, half = hd // 2, hd % 2
        in_half = (lane >= RET_DK * half) & (lane < RET_DK * (half + 1))
        dmat = jnp.where(diff >= 0, jnp.exp(jnp.maximum(diff, 0.0) * lg), 0.0)
        dec_q = jnp.exp((ic + 1.0) * lg)
        dec_k = jnp.exp((c_true - 1.0 - ic) * lg)
        for b in range(bg):
            qm = jnp.where(in_half, rq_refs[b][:, LANES * tile:LANES * (tile + 1)], 0.0).astype(BF16)
            k2 = rk_refs[b][:, LANES * tile:LANES * (tile + 1)]
            vb = rv_refs[b][:, LANES * hd:LANES * (hd + 1)].astype(BF16)
            o = _dot((_dot_nt(qm, k2.astype(BF16)) * dmat).astype(BF16), vb)
            st = s_scr[b, tile]
            o = o + _dot(qm, st.astype(BF16)) * dec_q
            kdec_t = (k2 * dec_k).T
            upd = _dot(kdec_t[RET_DK * half:RET_DK * (half + 1)].astype(BF16), vb)
            s_new = st[RET_DK * half:RET_DK * (half + 1)] * math.exp(c_true * lg) + upd
            s_scr[b, tile, pl.ds(RET_DK * half, RET_DK), :] = s_new
            on = _tilenorm(o, gro_ref[...], RET_DV)
            o_ref[b, :, LANES * hd:LANES * (hd + 1)] = (rg_refs[b][:, LANES * hd:LANES * (hd + 1)] * on).astype(BF16)

    @pl.when(c == pl.num_programs(1) - 1)
    def _():
        for b in range(bg):
            for hd in range(RET_HEADS):
                sout_ref[b, hd] = s_scr[b, hd // 2, pl.ds(RET_DK * (hd % 2), RET_DK), :]


def _retention(rq, rk, rv, rg, s0, gro, nb, n_chunks, c_true):
    cs = RET_CHUNK
    bg = math.gcd(nb, RET_GROUP)
    rows = lambda w: [pl.BlockSpec((cs, w), lambda g, c, b=b: ((g * bg + b) * n_chunks + c, 0)) for b in range(bg)]
    state = pl.BlockSpec((bg, RET_HEADS, RET_DK, RET_DV), lambda g, c: (g, 0, 0, 0))
    return pl.pallas_call(
        functools.partial(_ret_kernel, bg, float(c_true)),
        grid=(nb // bg, n_chunks),
        in_specs=rows(256) + rows(256) + rows(512) + rows(512) + [state, pl.BlockSpec(gro.shape, lambda g, c: (0, 0))],
        out_specs=[pl.BlockSpec((bg, cs, 512), lambda g, c: (g, c, 0)), state],
        out_shape=[jax.ShapeDtypeStruct((nb, n_chunks * cs, 512), BF16),
                   jax.ShapeDtypeStruct((nb, RET_HEADS, RET_DK, RET_DV), F32)],
        scratch_shapes=[pltpu.VMEM((bg, RET_HEADS // 2, LANES, RET_DV), F32)],
        compiler_params=pltpu.CompilerParams(dimension_semantics=("arbitrary", "arbitrary")),
        name="retention",
    )(*([rq] * bg + [rk] * bg + [rv] * bg + [rg] * bg), s0, gro)


def _memkv_kernel(m_ref, g_ref, w_ref, gk_ref, o_ref):
    x = m_ref[...]
    xn = (x * lax.rsqrt(jnp.mean(x * x, axis=-1, keepdims=True) + EPS) * g_ref[...]).astype(BF16)
    hk = _dot(xn, w_ref[...])
    half = MEM_HEADS * MEM_HD
    for hd in range(MEM_HEADS):
        o_ref[:, LANES * hd:LANES * (hd + 1)] = _tilenorm(hk[:, LANES * hd:LANES * (hd + 1)], gk_ref[...], MEM_HD)
    o_ref[:, half:2 * half] = hk[:, half:2 * half]


def _mem_kv(mem2d, g_mem, w_mem, gk_mem, tm):
    n = mem2d.shape[0]
    const = lambda a: pl.BlockSpec(a.shape, lambda i: (0,) * a.ndim)
    return pl.pallas_call(
        _memkv_kernel,
        grid=(n // tm,),
        in_specs=[pl.BlockSpec((tm, D_MODEL), lambda i: (i, 0)), const(g_mem), const(w_mem), const(gk_mem)],
        out_specs=pl.BlockSpec((tm, 1024), lambda i: (i, 0)),
        out_shape=jax.ShapeDtypeStruct((n, 1024), F32),
        compiler_params=pltpu.CompilerParams(dimension_semantics=("arbitrary",)),
        name="mem_kv",
    )(mem2d, g_mem, w_mem, gk_mem)


def _memattn_kernel(q_ref, kv_ref, o_ref):
    half = MEM_HEADS * MEM_HD
    for hd in range(MEM_HEADS):
        cols = slice(LANES * hd, LANES * (hd + 1))
        s = _dot_nt(q_ref[:, cols].astype(BF16), kv_ref[:, cols].astype(BF16))
        p = jnp.exp(s - jnp.max(s, axis=-1, keepdims=True))
        p = p / jnp.sum(p, axis=-1, keepdims=True)
        o = _dot(p.astype(BF16), kv_ref[:, half + LANES * hd:half + LANES * (hd + 1)].astype(BF16))
        o_ref[:, cols] = o.astype(o_ref.dtype)


def _memattn_cache_kernel(q_ref, kv_ref, o_ref):
    stride = 2 * MEM_HEADS
    mem_len = kv_ref.shape[1] // stride
    for hd in range(MEM_HEADS):
        cols = slice(LANES * hd, LANES * (hd + 1))
        k = kv_ref[0, pl.ds(hd, mem_len, stride=stride), :].astype(BF16)
        v = kv_ref[0, pl.ds(MEM_HEADS + hd, mem_len, stride=stride), :].astype(BF16)
        s = _dot_nt(q_ref[:, cols].astype(BF16), k)
        p = jnp.exp(s - jnp.max(s, axis=-1, keepdims=True))
        p = p / jnp.sum(p, axis=-1, keepdims=True)
        o_ref[:, cols] = _dot(p.astype(BF16), v).astype(o_ref.dtype)


def _mem_attend_cache(q2d, cache_rows, dseq):
    nb, n_rows, _ = cache_rows.shape
    return pl.pallas_call(
        _memattn_cache_kernel,
        grid=(nb,),
        in_specs=[pl.BlockSpec((dseq, 512), lambda b: (b, 0)), pl.BlockSpec((1, n_rows, LANES), lambda b: (b, 0, 0))],
        out_specs=pl.BlockSpec((dseq, 512), lambda b: (b, 0)),
        out_shape=jax.ShapeDtypeStruct((nb * dseq, 512), F32),
        compiler_params=pltpu.CompilerParams(dimension_semantics=("arbitrary",)),
        name="mem_attend_cache",
    )(q2d, cache_rows)


def _mem_attend(q2d, kv2d, nb, blocks_per_b, tq, mem_len, out_rows, out_dtype):
    return pl.pallas_call(
        _memattn_kernel,
        grid=(nb, blocks_per_b),
        in_specs=[pl.BlockSpec((tq, 512), lambda b, i: (b * blocks_per_b + i, 0)),
                  pl.BlockSpec((mem_len, 1024), lambda b, i: (b, 0))],
        out_specs=pl.BlockSpec((tq, 512), lambda b, i: (b * blocks_per_b + i, 0)),
        out_shape=jax.ShapeDtypeStruct((out_rows, 512), out_dtype),
        compiler_params=pltpu.CompilerParams(dimension_semantics=("arbitrary", "arbitrary")),
        name="mem_attend",
    )(q2d, kv2d)


def _merge_kernel(nbp, xp_ref, xs_ref, onsap_ref, onsas_ref, oretp_ref, orets_ref, omemp_ref, omems_ref, g_ref,
                  wbg_ref, wn_ref, wr_ref, wm_ref, wo_ref, gffn_ref, wrt_ref, brt_ref, wgu_ref, wd_ref,
                  h_ref, hn_ref, comb_ref, wgu_bf_ref, wd_bf_ref):
    i = pl.program_id(0)
    wgu_bf_ref[...] = wgu_ref[...].astype(BF16)
    wd_bf_ref[...] = wd_ref[...].astype(BF16)
    pick = lambda p_ref, s_ref: jnp.where(i < nbp, p_ref[...], s_ref[...])
    x = pick(xp_ref, xs_ref)
    xn = (x * lax.rsqrt(jnp.mean(x * x, axis=-1, keepdims=True) + EPS) * g_ref[...]).astype(BF16)
    bg = _sigmoid(_dot(xn, wbg_ref[...]))
    mixed = (bg[:, 0:1024] * _dot(pick(onsap_ref, onsas_ref), wn_ref[...])
             + bg[:, 1024:2048] * _dot(pick(oretp_ref, orets_ref), wr_ref[...])
             + bg[:, 2048:3072] * _dot(pick(omemp_ref, omems_ref), wm_ref[...]))
    hres = x + _dot(mixed.astype(BF16), wo_ref[...])
    h_ref[...] = hres
    hn = hres * lax.rsqrt(jnp.mean(hres * hres, axis=-1, keepdims=True) + EPS) * gffn_ref[...]
    hn_ref[...] = hn.astype(BF16)

    a, b, _ = _split3(hn)
    logits = _dot(a, wrt_ref[0]) + _dot(a, wrt_ref[1]) + _dot(b, wrt_ref[0]) + brt_ref[...]
    lane = lax.broadcasted_iota(jnp.int32, logits.shape, 1).astype(F32)
    vals, hots = [], []
    for _k in range(TOP_K):
        mx = jnp.max(logits, axis=-1, keepdims=True)
        idx = jnp.min(jnp.where(logits == mx, lane, float(LANES)), axis=-1, keepdims=True)
        hot = lane == idx
        vals.append(mx)
        hots.append(hot)
        logits = jnp.where(hot, -3.0e38, logits)
    es = [jnp.exp(v - vals[0]) for v in vals]
    den = es[0] + es[1] + es[2] + es[3]
    comb = jnp.zeros(logits.shape, F32)
    for e_k, hot in zip(es, hots):
        comb = comb + jnp.where(hot, e_k / den, 0.0)
    comb_ref[...] = comb


def _merge(xp, xs, onsa, onsa_s, oret, oret_s, omem, omem_s, g_attn, wbg, wn, wr, wm, wo, gffn, wrt, brt,
           w_gate_up, w_down, tm):
    n_p, n_s = xp.shape[0], xs.shape[0]
    nbp, nbs = n_p // tm, n_s // tm
    n_all = n_p + n_s
    n_halves = 2 * N_EXPERTS
    assert nbp + nbs >= n_halves
    row = lambda w: pl.BlockSpec((tm, w), lambda i: (i, 0))
    rowp = lambda w: pl.BlockSpec((tm, w), lambda i: (jnp.minimum(i, nbp - 1), 0))
    rows = lambda w: pl.BlockSpec((tm, w), lambda i: (jnp.maximum(i - nbp, 0), 0))
    const = lambda a: pl.BlockSpec(a.shape, lambda i: (0,) * a.ndim, pipeline_mode=pl.Buffered(1))
    half = lambda i: jnp.minimum(i, n_halves - 1)
    wgu_spec = pl.BlockSpec((1, D_MODEL, D_FF), lambda i: (half(i) // 2, 0, half(i) % 2))
    wd_spec = pl.BlockSpec((1, D_FF // 2, D_MODEL), lambda i: (half(i) // 2, half(i) % 2, 0))
    return pl.pallas_call(
        functools.partial(_merge_kernel, nbp),
        grid=(nbp + nbs,),
        in_specs=[rowp(1024), rows(1024), rowp(1024), rows(1024), rowp(512), rows(512), rowp(512), rows(512),
                  const(g_attn), const(wbg), const(wn), const(wr), const(wm),
                  const(wo), const(gffn), const(wrt), const(brt), wgu_spec, wd_spec],
        out_specs=[row(1024), row(1024), row(LANES), wgu_spec, wd_spec],
        out_shape=[jax.ShapeDtypeStruct((n_all, 1024), F32), jax.ShapeDtypeStruct((n_all, 1024), BF16),
                   jax.ShapeDtypeStruct((n_all, LANES), F32),
                   jax.ShapeDtypeStruct(w_gate_up.shape, BF16), jax.ShapeDtypeStruct(w_down.shape, BF16)],
        compiler_params=pltpu.CompilerParams(dimension_semantics=("arbitrary",), vmem_limit_bytes=VMEM_LIMIT),
        name="merge",
    )(xp, xs, onsa, onsa_s, oret, oret_s, omem, omem_s, g_attn, wbg, wn, wr, wm, wo, gffn, wrt, brt,
      w_gate_up, w_down)


MOE_BLOCK_CAP = 1664
MOE_CHUNK = 256
ROUTE_TILE = 128


def _route_kernel(comb_ref, pos_ref, post_ref, cnt_ref):
    tb = comb_ref.shape[0]
    rt = ROUTE_TILE
    r = lax.broadcasted_iota(jnp.int32, (rt, rt), 0)
    c = lax.broadcasted_iota(jnp.int32, (rt, rt), 1)
    ltri = (c < r).astype(BF16)
    eye = (c == r).astype(BF16)
    carry = jnp.zeros((1, LANES), F32)
    for s in range(tb // rt):
        comb = comb_ref[pl.ds(rt * s, rt), :]
        sel = jnp.where(comb > 0.0, 1.0, 0.0)
        pos = jnp.where(sel > 0.0, _dot(ltri, sel.astype(BF16)) + carry, -1.0)
        carry = carry + jnp.sum(sel, axis=0, keepdims=True)
        pos_ref[pl.ds(rt * s, rt), :] = jnp.where(sel > 0.0, pos + 0.5 * comb, -1.0)
        hi = jnp.floor((pos + 1.0) * (1.0 / 256.0))
        lo = (pos + 1.0) - 256.0 * hi
        post_ref[:, rt * s:rt * (s + 1)] = 256.0 * _dot_nt(eye, hi.astype(BF16)) + _dot_nt(eye, lo.astype(BF16)) - 1.0
    cnt_ref[0] = jnp.broadcast_to(carry, (8, LANES))


def _route(comb, tb):
    n_all = comb.shape[0]
    nblk = n_all // tb
    return pl.pallas_call(
        _route_kernel,
        grid=(nblk,),
        in_specs=[pl.BlockSpec((tb, LANES), lambda i: (i, 0))],
        out_specs=[pl.BlockSpec((tb, LANES), lambda i: (i, 0)), pl.BlockSpec((LANES, tb), lambda i: (0, i)),
                   pl.BlockSpec((1, 8, LANES), lambda i: (i, 0, 0))],
        out_shape=[jax.ShapeDtypeStruct((n_all, LANES), F32), jax.ShapeDtypeStruct((LANES, n_all), F32),
                   jax.ShapeDtypeStruct((nblk, 8, LANES), F32)],
        compiler_params=pltpu.CompilerParams(dimension_semantics=("arbitrary",)),
        name="route",
    )(comb)


def _moe_kernel(n_s, cnt_ref, hn_ref, h_ref, pos_ref, post_ref, wgu_ref, wd_ref, bgu_ref, bd_ref, y_ref, ys_ref):
    i = pl.program_id(0)
    e = pl.program_id(1)
    tb = hn_ref.shape[0]
    ch = MOE_CHUNK

    @pl.when(e == 0)
    def _():
        y_ref[...] = h_ref[...]

    n_rows = cnt_ref[i * N_EXPERTS + e]
    lane = lax.broadcasted_iota(jnp.int32, (tb, LANES), 1)
    packed = jnp.sum(jnp.where(lane == e, pos_ref[...], 0.0), axis=-1, keepdims=True)
    pcol = jnp.floor(packed)
    ccol = 2.0 * (packed - pcol)
    prow = post_ref[pl.ds(e, 1), :]

    def chunk(c, carry):
        r0 = (c * ch).astype(F32)
        slot_g = r0 + lax.broadcasted_iota(jnp.int32, (ch, tb), 0).astype(F32)
        gather = jnp.where(prow == slot_g, 1.0, 0.0).astype(BF16)
        xg = _dot(gather, hn_ref[...]).astype(BF16)
        gu = _dot(xg, wgu_ref[0]) + bgu_ref[0]
        gate = jnp.minimum(gu[:, 0:D_FF], SWIGLU_LIMIT)
        up = jnp.clip(gu[:, D_FF:2 * D_FF], -SWIGLU_LIMIT, SWIGLU_LIMIT)
        act = (up + 1.0) * gate * _sigmoid(SWIGLU_ALPHA * gate)
        yc = _dot(act.astype(BF16), wd_ref[0]) + bd_ref[0]
        slot_s = r0 + lax.broadcasted_iota(jnp.int32, (tb, ch), 1).astype(F32)
        scatter = jnp.where(pcol == slot_s, ccol, 0.0).astype(BF16)
        y_ref[...] += _dot(scatter, yc.astype(BF16))
        return carry

    lax.fori_loop(0, (n_rows + ch - 1) // ch, chunk, 0)

    @pl.when((i == pl.num_programs(0) - 1) & (e == N_EXPERTS - 1))
    def _():
        ys_ref[...] = y_ref[pl.ds(tb - n_s, n_s), :]


def _moe(cnt, hn, h, pos, post, wgu, wd, b_gate_up, b_down, tb, n_s):
    n_all = hn.shape[0]
    n_p = n_all - n_s
    assert n_s <= tb and n_s % 8 == 0 and (n_all // tb - 1) * tb < n_p
    row = lambda w: pl.BlockSpec((tb, w), lambda i, e, cnt: (i, 0))
    return pl.pallas_call(
        functools.partial(_moe_kernel, n_s),
        grid_spec=pltpu.PrefetchScalarGridSpec(
            num_scalar_prefetch=1, grid=(n_all // tb, N_EXPERTS),
            in_specs=[row(1024), row(1024), row(LANES),
                      pl.BlockSpec((LANES, tb), lambda i, e, cnt: (0, i)),
                      pl.BlockSpec((1, D_MODEL, 2 * D_FF), lambda i, e, cnt: (e, 0, 0)),
                      pl.BlockSpec((1, D_FF, D_MODEL), lambda i, e, cnt: (e, 0, 0)),
                      pl.BlockSpec((1, 1, 2 * D_FF), lambda i, e, cnt: (e, 0, 0)),
                      pl.BlockSpec((1, 1, D_MODEL), lambda i, e, cnt: (e, 0, 0))],
            out_specs=[row(1024), pl.BlockSpec((n_s, 1024), lambda i, e, cnt: (0, 0))]),
        out_shape=[jax.ShapeDtypeStruct((n_p, 1024), F32), jax.ShapeDtypeStruct((n_s, 1024), F32)],
        compiler_params=pltpu.CompilerParams(dimension_semantics=("arbitrary",) * 2, vmem_limit_bytes=VMEM_LIMIT),
        name="moe",
    )(cnt, hn, h, pos, post, wgu, wd, b_gate_up, b_down)


def _pick_tile(n, pref):
    t = pref
    while n % t:
        t //= 2
    return t


def kernel(x_prompt, x_sample, cache_cmp, cache_slc, cache_win, state_ret, cache_mem, page_table, mem_prompt,
           g_attn, w_in, g_q_nsa, g_k_cmp, g_k_slc, g_k_win, pe_ck, w_ck1, w_ck2, pe_cv, w_cv1, w_cv2,
           g_ret_out, g_mem, w_mem_kv, g_q_mem, g_k_mem, w_br_nsa, w_br_ret, w_br_mem, w_out,
           g_ffn, w_router, b_router, w_gate_up, b_gate_up, w_down, b_down):
    nb, seq, _ = x_prompt.shape
    db, dseq, _ = x_sample.shape
    n_pages = page_table.shape[1]
    past = n_pages * PAGE_SIZE
    wb = cache_win.shape[1]
    mem_len = mem_prompt.shape[1]
    n_p, n_s = nb * seq, db * dseq
    n_all = n_p + n_s
    tm = _pick_tile(seq, ROW_TILE)

    o = 0
    cols = {}
    for name, wdt in (("q", 512), ("kv", 768), ("ng", 24), ("rq", 256), ("rk", 256), ("rv", 512), ("rg", 512),
                      ("mq", 512), ("bg", 3072)):
        cols[name] = w_in[:, o:o + wdt]
        o += wdt
    wq = cols["q"].reshape(D_MODEL, NSA_HEADS, NSA_HD)
    zq = jnp.zeros_like(wq)
    wq_pad = jnp.concatenate(
        [jnp.concatenate([wq[:, hh], zq[:, hh]] if hh < NSA_GROUP else [zq[:, hh], wq[:, hh]], axis=-1)
         for hh in range(NSA_HEADS)], axis=-1)
    w_ng = jnp.pad(cols["ng"], ((0, 0), (0, C_NG - 24)))
    w1 = jnp.concatenate([wq_pad, cols["kv"], w_ng, cols["rq"], cols["rk"], cols["rv"], cols["rg"], cols["mq"]],
                         axis=-1).astype(BF16)
    w_bg = cols["bg"].astype(BF16)
    two = lambda g: jnp.concatenate([g, g]).reshape(1, LANES)
    r1 = lambda g: g.reshape(1, -1)

    half = RET_DK // 2
    inv = ROPE_BASE ** (-jnp.arange(half, dtype=F32) / half)
    pos = jnp.concatenate([jnp.arange(seq, dtype=jnp.int32),
                           jnp.tile(past + jnp.arange(dseq, dtype=jnp.int32), db)]).astype(F32)
    ang = pos[:, None] * inv[None, :]
    cos_t = jnp.tile(jnp.cos(ang), (1, 2 * RET_HEADS))
    sin_t = jnp.tile(jnp.concatenate([-jnp.sin(ang), jnp.sin(ang)], axis=-1), (1, RET_HEADS))

    xp = x_prompt.reshape(n_p, D_MODEL)
    xs = x_sample.reshape(n_s, D_MODEL)
    (q, kvc, kvs, kvw, ng, rq, rk, rv, rg, mq, kvct, kvst, kvwt) = _project(
        xp, xs, r1(g_attn), w1, cos_t, sin_t, two(g_q_nsa), two(g_k_slc), two(g_k_win), r1(g_q_mem), seq, tm)

    w1s = jnp.stack([w_ck1, w_ck1, w_cv1, w_cv1]).reshape(4, 2, CMP_STRIDE, NSA_HD, CMP_HIDDEN)
    wbig = jnp.einsum("shpdc,st->psdhtc", w1s, jnp.eye(4, dtype=F32)).reshape(CHUNK_W, 2 * 4 * CMP_HIDDEN).astype(BF16)
    pes = jnp.stack([pe_ck, pe_ck, pe_cv, pe_cv]).reshape(4, 2, CMP_STRIDE, NSA_HD)
    pe2 = jnp.pad(jnp.transpose(pes, (1, 2, 0, 3)).reshape(2, CHUNK_W), ((0, 6), (0, 0)))
    zc = jnp.zeros_like(w_ck2)
    bd2 = lambda w: jnp.concatenate([jnp.concatenate([w, zc], 1), jnp.concatenate([zc, w], 1)], 0).astype(BF16)
    gk2 = two(g_k_cmp)

    pages_p = seq // PAGE_SIZE
    pool_p = (kvc if n_all % PAGE_SIZE == 0 else kvc[:n_p]).reshape(-1, CHUNKS_PER_PAGE, CHUNK_W)
    pt_p = jnp.arange(nb * pages_p, dtype=jnp.int32).reshape(nb, pages_p)
    ck_p, cv_p = _compress(pt_p, pool_p, wbig, pe2, bd2(w_ck2), bd2(w_cv2), gk2)
    pages_t = lambda c: jnp.transpose(c, (0, 2, 3, 4, 1)).reshape(-1, 256, PAGE_SIZE)
    ck_s, cv_s = _compress_t(page_table, pages_t(cache_cmp), wbig, pe2, bd2(w_ck2), bd2(w_cv2), gk2)

    n_cmp_p = seq // CMP_STRIDE - CMP_BLOCK // CMP_STRIDE + 1
    onsa = _nsa_prompt(q, kvs, kvw, ck_p, cv_p, ng, nb, seq, n_cmp_p)

    ns_s = (past + dseq + SLC_BLOCK - 1) // SLC_BLOCK
    nsp = (ns_s + 7) // 8 * 8
    kpad = past + PAGE_SIZE
    e_s = (jnp.arange(kpad)[None, :] // SLC_BLOCK == jnp.arange(nsp)[:, None]).astype(BF16)
    s3 = lambda a: a[n_p:].reshape(db, dseq, a.shape[1])
    onsa_s, new_win_s = _nsa_sample(page_table, s3(q), s3(kvs), s3(kvw), ck_s, cv_s, s3(ng),
                                    cache_win.reshape(db, wb, 256), e_s,
                                    pages_t(cache_slc), past)
    onsa_s = onsa_s.reshape(n_s, 1024).astype(BF16)

    gro = r1(g_ret_out)
    oret, ret_state_p = _retention(rq, rk, rv, rg, jnp.zeros((nb, RET_HEADS, RET_DK, RET_DV), F32), gro,
                                   nb, seq // RET_CHUNK, RET_CHUNK)
    padc = lambda a: jnp.pad(s3(a), ((0, 0), (0, RET_CHUNK - dseq), (0, 0))).reshape(db * RET_CHUNK, a.shape[1])
    oret_s, ret_state_s = _retention(padc(rq), padc(rk), padc(rv), padc(rg), state_ret, gro, db, 1, dseq)
    oret = oret.reshape(n_p, 512)
    oret_s = oret_s[:, :dseq].reshape(n_s, 512)

    mem_kv_p = _mem_kv(mem_prompt.reshape(nb * mem_len, D_MODEL), r1(g_mem), w_mem_kv.astype(BF16), r1(g_k_mem),
                       _pick_tile(nb * mem_len, 256))
    tq = _pick_tile(seq, 512)
    omem = _mem_attend(mq, mem_kv_p, nb, seq // tq, tq, mem_len, n_p, BF16)
    omem_s = _mem_attend_cache(mq[n_p:].astype(F32), cache_mem.reshape(db, mem_len * 2 * MEM_HEADS, MEM_HD),
                               dseq).astype(BF16)

    wn = w_br_nsa.reshape(NSA_HEADS, NSA_HD, D_MODEL)
    zn = jnp.zeros_like(wn)
    wn_pad = jnp.concatenate(
        [jnp.concatenate([wn[hh], zn[hh]] if hh < NSA_GROUP else [zn[hh], wn[hh]], axis=0) for hh in range(NSA_HEADS)],
        axis=0).astype(BF16)
    wr_pad = jnp.pad(w_router, ((0, 0), (0, LANES - N_EXPERTS)))
    wr_hi = wr_pad.astype(BF16)
    wr_lo = (wr_pad - wr_hi.astype(F32)).astype(BF16)
    brt = jnp.concatenate([b_router, jnp.full((LANES - N_EXPERTS,), NEG, F32)]).reshape(1, LANES)
    tmm = _pick_tile(math.gcd(seq, n_s), 256)
    h, hn, comb, wgu_bf, wd_bf = _merge(
        xp, xs, onsa, onsa_s, oret, oret_s, omem, omem_s, r1(g_attn), w_bg, wn_pad, w_br_ret.astype(BF16),
        w_br_mem.astype(BF16), w_out.astype(BF16), r1(g_ffn), jnp.stack([wr_hi, wr_lo]), brt, w_gate_up, w_down, tmm)

    tb = max(c for c in range(LANES, MOE_BLOCK_CAP + 1, LANES) if n_all % c == 0)
    pos, post, cnt = _route(comb, tb)
    cnt = cnt[:, 0, :N_EXPERTS].astype(jnp.int32).reshape(-1)
    y_p, y_s = _moe(cnt, hn, h, pos, post, wgu_bf, wd_bf,
                    b_gate_up.reshape(N_EXPERTS, 1, 2 * D_FF), b_down.reshape(N_EXPERTS, 1, D_MODEL), tb, n_s)

    kv5 = lambda a, bsz, t: a.reshape(bsz, t, 2, NSA_KV_HEADS, NSA_HD)
    wp = min(WINDOW, seq)
    kv5t = lambda a: jnp.transpose(a.reshape(nb, 2, NSA_KV_HEADS, NSA_HD, a.shape[2]), (0, 4, 1, 2, 3))
    return (y_p.reshape(nb, seq, D_MODEL), y_s.reshape(db, dseq, D_MODEL),
            kv5t(kvct), kv5t(kvst), kv5t(kvwt[:, :, seq - wp:]),
            ret_state_p, mem_kv_p.reshape(nb, mem_len, 2, MEM_HEADS, MEM_HD),
            kv5(kvc[n_p:], db, dseq), kv5(kvs[n_p:], db, dseq), kv5(new_win_s, db, wb), ret_state_s)
```

```python
import functools
import math

import jax
import jax.numpy as jnp
from jax import lax
from jax.experimental import pallas as pl
from jax.experimental.pallas import tpu as pltpu

F32 = jnp.float32
BF16 = jnp.bfloat16

D_MODEL = 1024
NSA_HEADS = 8
NSA_KV_HEADS = 2
NSA_GROUP = 4
NSA_HD = 64
CMP_BLOCK = 32
CMP_STRIDE = 16
CMP_HIDDEN = 128
SLC_BLOCK = 64
N_SELECT = 16
N_LOCAL_FORCED = 2
FORCE_BONUS = 1.0e4
WINDOW = 512
Q_BLOCK = 128
PAGE_SIZE = 128
RET_HEADS = 4
RET_DK = 64
RET_DV = 128
RET_CHUNK = 128
ROPE_BASE = 10000.0
MEM_HEADS = 4
MEM_HD = 128
N_EXPERTS = 32
TOP_K = 4
D_FF = 1024
SWIGLU_ALPHA = 1.702
SWIGLU_LIMIT = 7.0
EPS = 1e-6
NEG = -1e30

LANES = 128
ROW_TILE = 512
VMEM_LIMIT = 56 * 1024 * 1024


def _dot(a, b):
    return jnp.dot(a, b, preferred_element_type=F32)


def _dot_nt(a, b):
    return lax.dot_general(a, b, (((1,), (1,)), ((), ())), preferred_element_type=F32)


def _sigmoid(x):
    return 1.0 / (1.0 + jnp.exp(-x))


def _split3(x):
    a = x.astype(BF16)
    r = x - a.astype(F32)
    b = r.astype(BF16)
    c = (r - b.astype(F32)).astype(BF16)
    return a, b, c


def _msoftmax(s, m):
    s = jnp.where(m, s, NEG)
    mx = jnp.max(s, axis=-1, keepdims=True)
    p = jnp.where(m, jnp.exp(s - mx), 0.0)
    return p / jnp.maximum(jnp.sum(p, axis=-1, keepdims=True), 1e-30)


def _halfnorm(t, g2):
    lane = lax.broadcasted_iota(jnp.int32, t.shape, 1)
    lo = lane < NSA_HD
    t2 = t * t
    s0 = jnp.sum(jnp.where(lo, t2, 0.0), axis=-1, keepdims=True)
    s1 = jnp.sum(jnp.where(lo, 0.0, t2), axis=-1, keepdims=True)
    r = jnp.where(lo, lax.rsqrt(s0 * (1.0 / NSA_HD) + EPS), lax.rsqrt(s1 * (1.0 / NSA_HD) + EPS))
    return t * r * g2


def _tilenorm(t, g, width):
    return t * lax.rsqrt(jnp.sum(t * t, axis=-1, keepdims=True) * (1.0 / width) + EPS) * g


def _rot(x, cos, sin):
    lane = lax.broadcasted_iota(jnp.int32, x.shape, 1)
    first = (lane % RET_DK) < (RET_DK // 2)
    n = x.shape[1]
    sw = jnp.where(first, pltpu.roll(x, n - RET_DK // 2, 1), pltpu.roll(x, RET_DK // 2, 1))
    return x * cos + sw * sin


C_Q, C_KV, C_NG, C_RET, C_MQ = 1024, 768, 128, 1536, 512
O_KV = C_Q
O_NG = O_KV + C_KV
O_RET = O_NG + C_NG
O_MQ = O_RET + C_RET
W1_COLS = O_MQ + C_MQ


def _proj_kernel(nbp, xp_ref, xs_ref, g_ref, w_ref, cos_ref, sin_ref, gq_ref, gks_ref, gkw_ref, gqm_ref,
                 q_ref, kvc_ref, kvs_ref, kvw_ref, ng_ref, rq_ref, rk_ref, rv_ref, rg_ref, mq_ref,
                 kvct_ref, kvst_ref, kvwt_ref):
    i = pl.program_id(0)
    x = jnp.where(i < nbp, xp_ref[...], xs_ref[...])
    xn = (x * lax.rsqrt(jnp.mean(x * x, axis=-1, keepdims=True) + EPS) * g_ref[...]).astype(BF16)

    hq = _dot(xn, w_ref[:, 0:C_Q])
    for hh in range(NSA_HEADS):
        t = hq[:, LANES * hh:LANES * (hh + 1)]
        q_ref[:, LANES * hh:LANES * (hh + 1)] = (_tilenorm(t, gq_ref[...], NSA_HD) * NSA_HD ** -0.5).astype(BF16)

    hkv = _dot(xn, w_ref[:, O_KV:O_KV + C_KV])
    kvc = hkv[:, 0:256]
    kvs = jnp.concatenate([_halfnorm(hkv[:, 256:384], gks_ref[...]), hkv[:, 384:512]], axis=1)
    kvw = jnp.concatenate([_halfnorm(hkv[:, 512:640], gkw_ref[...]), hkv[:, 640:768]], axis=1)
    kvc_ref[...] = kvc
    kvs_ref[...] = kvs
    kvw_ref[...] = kvw

    @pl.when(i < nbp)
    def _():
        kvct_ref[0] = kvc.T
        kvst_ref[0] = kvs.T
        kvwt_ref[0] = kvw.T

    ng_ref[...] = _sigmoid(_dot(xn, w_ref[:, O_NG:O_NG + C_NG]))

    hr = _dot(xn, w_ref[:, O_RET:O_RET + C_RET])
    cos = cos_ref[...]
    sin = sin_ref[...]
    rq_ref[...] = _rot(hr[:, 0:256], cos, sin)
    rk_ref[...] = _rot(hr[:, 256:512], cos, sin) * RET_DK ** -0.5
    rv_ref[...] = hr[:, 512:1024]
    rg = hr[:, 1024:1536]
    rg_ref[...] = rg * _sigmoid(rg)

    hm = _dot(xn, w_ref[:, O_MQ:O_MQ + C_MQ])
    for hd in range(MEM_HEADS):
        t = hm[:, LANES * hd:LANES * (hd + 1)]
        mq_ref[:, LANES * hd:LANES * (hd + 1)] = (_tilenorm(t, gqm_ref[...], MEM_HD) * MEM_HD ** -0.5).astype(BF16)


def _project(xp, xs, g_attn, w1, cos_t, sin_t, gq2, gks2, gkw2, gqm, seq, tm):
    n_p, n_s = xp.shape[0], xs.shape[0]
    nbp, nbs = n_p // tm, -(-n_s // tm)
    n_all = n_p + n_s
    bps = seq // tm
    row = lambda w: pl.BlockSpec((tm, w), lambda i: (i, 0))
    const = lambda a: pl.BlockSpec(a.shape, lambda i: (0,) * a.ndim)
    rope = pl.BlockSpec((tm, 256), lambda i: (jnp.where(i < nbp, i % bps, bps + i - nbp), 0))
    widths = (1024, 256, 256, 256, 128, 256, 256, 512, 512, 512)
    dtypes = (BF16, F32, F32, F32, F32, F32, F32, F32, F32, BF16)
    kvt = pl.BlockSpec((1, 256, tm), lambda i: (jnp.minimum(i, nbp - 1) // bps, 0, jnp.minimum(i, nbp - 1) % bps))
    return pl.pallas_call(
        functools.partial(_proj_kernel, nbp),
        grid=(nbp + nbs,),
        in_specs=[pl.BlockSpec((tm, D_MODEL), lambda i: (jnp.minimum(i, nbp - 1), 0)),
                  pl.BlockSpec((tm, D_MODEL), lambda i: (jnp.maximum(i - nbp, 0), 0)),
                  const(g_attn), const(w1), rope, rope, const(gq2), const(gks2), const(gkw2), const(gqm)],
        out_specs=[row(w) for w in widths] + [kvt] * 3,
        out_shape=[jax.ShapeDtypeStruct((n_all, w), d) for w, d in zip(widths, dtypes)]
        + [jax.ShapeDtypeStruct((n_p // seq, 256, seq), F32)] * 3,
        compiler_params=pltpu.CompilerParams(dimension_semantics=("arbitrary",), vmem_limit_bytes=VMEM_LIMIT),
        name="proj",
    )(xp, xs, g_attn, w1, cos_t, sin_t, gq2, gks2, gkw2, gqm)


CHUNK_W = CMP_STRIDE * 256
CHUNKS_PER_PAGE = PAGE_SIZE // CMP_STRIDE


def _compress_kernel(n_pages, pt_ref, pool_ref, wbig_ref, pe_ref, w2k_ref, w2v_ref, gk_ref,
                     ck_ref, cv_ref, x_scr, r_scr, sem):
    b = pl.program_id(0)
    n = n_pages * CHUNKS_PER_PAGE

    def page_copy(j):
        return pltpu.make_async_copy(pool_ref.at[pt_ref[b, j]],
                                     x_scr.at[pl.ds(CHUNKS_PER_PAGE * j, CHUNKS_PER_PAGE)], sem)

    for j in range(n_pages):
        page_copy(j).start()
    x_scr[pl.ds(n, 8), :] = pe_ref[...]
    for j in range(n_pages):
        page_copy(j).wait()

    r_scr[...] = _dot(x_scr[...].astype(BF16), wbig_ref[...])
    cvec = r_scr[n:n + 1, 0:512] + r_scr[n + 1:n + 2, 512:1024]
    hid = r_scr[0:n, 0:512] + r_scr[pl.ds(1, n), 512:1024] + cvec
    hb = (hid * _sigmoid(hid)).astype(BF16)
    ck_ref[0] = _halfnorm(_dot(hb[:, 0:256], w2k_ref[...]), gk_ref[...])
    cv_ref[0] = _dot(hb[:, 256:512], w2v_ref[...])


def _compress(page_table, pool, wbig, pe2, w2k, w2v, gk2):
    nb, n_pages = page_table.shape
    n = n_pages * CHUNKS_PER_PAGE
    const = lambda a: pl.BlockSpec(a.shape, lambda b, pt: (0,) * a.ndim)
    out = pl.BlockSpec((1, n, LANES), lambda b, pt: (b, 0, 0))
    return pl.pallas_call(
        functools.partial(_compress_kernel, n_pages),
        grid_spec=pltpu.PrefetchScalarGridSpec(
            num_scalar_prefetch=1, grid=(nb,),
            in_specs=[pl.BlockSpec(memory_space=pl.ANY), const(wbig), const(pe2), const(w2k), const(w2v), const(gk2)],
            out_specs=[out, out],
            scratch_shapes=[pltpu.VMEM((n + 8, CHUNK_W), F32), pltpu.VMEM((n + 8, 1024), F32),
                            pltpu.SemaphoreType.DMA(())]),
        out_shape=[jax.ShapeDtypeStruct((nb, n, LANES), F32)] * 2,
        compiler_params=pltpu.CompilerParams(dimension_semantics=("arbitrary",), vmem_limit_bytes=VMEM_LIMIT),
        name="compress",
    )(page_table, pool, wbig, pe2, w2k, w2v, gk2)


def _compress_t_kernel(n_pages, pt_ref, pool_ref, wbig_ref, pe_ref, w2k_ref, w2v_ref, gk_ref,
                       ck_ref, cv_ref, pg_scr, t_scr, r_scr, cvec_scr, sem):
    b = pl.program_id(0)
    slot = b % 2
    n = n_pages * CHUNKS_PER_PAGE

    def page_copy(bb, sl, j):
        return pltpu.make_async_copy(pool_ref.at[pt_ref[bb, j]], pg_scr.at[sl, j], sem.at[sl])

    @pl.when(b == 0)
    def _():
        for j in range(n_pages):
            page_copy(0, 0, j).start()
        r = _dot(pe_ref[...].astype(BF16), wbig_ref[...])
        cvec_scr[...] = jnp.broadcast_to(r[0:1, 0:512] + r[1:2, 512:1024], cvec_scr.shape)
        r_scr[pl.ds(n, 8), :] = jnp.zeros((8, 1024), F32)

    @pl.when(b + 1 < pl.num_programs(0))
    def _():
        for j in range(n_pages):
            page_copy(b + 1, 1 - slot, j).start()

    for j in range(n_pages):
        page_copy(b, slot, j).wait()

    src = lax.broadcasted_iota(jnp.int32, (PAGE_SIZE, PAGE_SIZE), 0)
    dst = lax.broadcasted_iota(jnp.int32, (PAGE_SIZE, PAGE_SIZE), 1)
    perm = (dst == CHUNKS_PER_PAGE * (src % CMP_STRIDE) + src // CMP_STRIDE).astype(BF16)
    for j in range(n_pages):
        xp = _dot(pg_scr[slot, j].astype(BF16), perm)
        tk = xp[0:LANES, :].T
        tv = xp[LANES:2 * LANES, :].T
        for p in range(CMP_STRIDE):
            rows = pl.ds(CHUNKS_PER_PAGE * j, CHUNKS_PER_PAGE)
            t_scr[p, rows, 0:LANES] = tk[CHUNKS_PER_PAGE * p:CHUNKS_PER_PAGE * (p + 1)]
            t_scr[p, rows, LANES:2 * LANES] = tv[CHUNKS_PER_PAGE * p:CHUNKS_PER_PAGE * (p + 1)]

    acc = jnp.zeros((n, 1024), F32)
    for p in range(CMP_STRIDE):
        acc = acc + _dot(t_scr[p].astype(BF16), wbig_ref[256 * p:256 * (p + 1), :])
    r_scr[pl.ds(0, n), :] = acc
    hid = acc[:, 0:512] + r_scr[pl.ds(1, n), 512:1024] + cvec_scr[0:1, :]
    hb = (hid * _sigmoid(hid)).astype(BF16)
    ck_ref[0] = _halfnorm(_dot(hb[:, 0:256], w2k_ref[...]), gk_ref[...])
    cv_ref[0] = _dot(hb[:, 256:512], w2v_ref[...])


def _compress_t(page_table, pool_t, wbig, pe2, w2k, w2v, gk2):
    nb, n_pages = page_table.shape
    n = n_pages * CHUNKS_PER_PAGE
    const = lambda a: pl.BlockSpec(a.shape, lambda b, pt: (0,) * a.ndim)
    out = pl.BlockSpec((1, n, LANES), lambda b, pt: (b, 0, 0))
    return pl.pallas_call(
        functools.partial(_compress_t_kernel, n_pages),
        grid_spec=pltpu.PrefetchScalarGridSpec(
            num_scalar_prefetch=1, grid=(nb,),
            in_specs=[pl.BlockSpec(memory_space=pl.ANY), const(wbig), const(pe2), const(w2k), const(w2v), const(gk2)],
            out_specs=[out, out],
            scratch_shapes=[pltpu.VMEM((2, n_pages, 256, PAGE_SIZE), F32), pltpu.VMEM((CMP_STRIDE, n, 2 * LANES), F32),
                            pltpu.VMEM((n + 8, 1024), F32), pltpu.VMEM((8, 512), F32),
                            pltpu.SemaphoreType.DMA((2,))]),
        out_shape=[jax.ShapeDtypeStruct((nb, n, LANES), F32)] * 2,
        compiler_params=pltpu.CompilerParams(dimension_semantics=("arbitrary",), vmem_limit_bytes=VMEM_LIMIT),
        name="compress_t",
    )(page_table, pool_t, wbig, pe2, w2k, w2v, gk2)


def _overlap_t(ns_rows, n_cmp_cols, n_cmp):
    s = lax.broadcasted_iota(jnp.int32, (ns_rows, n_cmp_cols), 0)
    n = lax.broadcasted_iota(jnp.int32, (ns_rows, n_cmp_cols), 1)
    ov = (n * CMP_STRIDE < s * SLC_BLOCK + SLC_BLOCK) & (n * CMP_STRIDE + CMP_BLOCK > s * SLC_BLOCK) & (n < n_cmp)
    return ov.astype(BF16)


def _importance_t(ov_t, pcs):
    a, b, c = _split3(pcs)
    return _dot_nt(ov_t, a) + _dot_nt(ov_t, b) + _dot_nt(ov_t, c)


def _block_scores_t(imp_t, tl):
    blk = lax.broadcasted_iota(jnp.int32, imp_t.shape, 0)
    cur = tl // SLC_BLOCK
    valid = blk <= cur
    forced = (blk == 0) | (valid & (blk > cur - N_LOCAL_FORCED))
    return jnp.where(valid, imp_t + jnp.where(forced, FORCE_BONUS, 0.0), NEG), blk


SLC_CHUNK = 1024
RANK_ACCS = 4


def _nsa_prompt_kernel(seq, n_cmp, q_ref, kvs_ref, kvw_ref, ck_ref, cv_ref, ng_ref, et_ref, eg_ref, o_ref):
    i = pl.program_id(1)
    ns = seq // SLC_BLOCK
    nh = NSA_HEADS
    q8 = q_ref[...]
    qs = jnp.concatenate([q8[:, LANES * hh:LANES * (hh + 1)] for hh in range(nh)], axis=0)
    tq = i * Q_BLOCK + lax.broadcasted_iota(jnp.int32, (Q_BLOCK, 1), 0)
    rep = lambda a: jnp.concatenate([a] * nh, axis=0)
    t8 = rep(tq)

    ckb = ck_ref[0].astype(BF16)
    sc = _dot_nt(qs, ckb)
    jn = lax.broadcasted_iota(jnp.int32, sc.shape, 1)
    pc = _msoftmax(sc, (jn * CMP_STRIDE + CMP_BLOCK - 1) <= t8)
    o_cmp = _dot(pc.astype(BF16), cv_ref[0].astype(BF16))

    ov_t = _overlap_t(ns, pc.shape[1], n_cmp)
    tl = i * Q_BLOCK + lax.broadcasted_iota(jnp.int32, (ns, Q_BLOCK), 1)
    eye = (lax.broadcasted_iota(jnp.int32, (Q_BLOCK, Q_BLOCK), 0)
           == lax.broadcasted_iota(jnp.int32, (Q_BLOCK, Q_BLOCK), 1)).astype(BF16)
    lane_q = lax.broadcasted_iota(jnp.int32, (Q_BLOCK, LANES), 1)
    zpad = lambda n: [jnp.zeros((n, Q_BLOCK), BF16)] if n else []
    qxs = []
    for h in range(NSA_KV_HEADS):
        r0 = h * NSA_GROUP * Q_BLOCK
        pcs = pc[r0:r0 + 128] + pc[r0 + 128:r0 + 256] + pc[r0 + 256:r0 + 384] + pc[r0 + 384:r0 + 512]
        score, blk = _block_scores_t(_importance_t(ov_t, pcs), tl)
        ranks = [jnp.zeros(score.shape, F32) for _ in range(RANK_ACCS)]
        for k in range(ns):
            sk = score[k:k + 1, :]
            ranks[k % RANK_ACCS] += jnp.where((sk > score) | ((sk == score) & (blk > k)), 1.0, 0.0)
        rank = functools.reduce(lambda x, y: x + y, ranks)
        sel_t = jnp.where(rank < min(N_SELECT, ns), 1.0, 0.0).astype(BF16)
        sel_t = jnp.concatenate(zpad(NSA_HD * (1 - h)) + [sel_t] + zpad(LANES - ns - NSA_HD * (1 - h)), axis=0)
        in_blk = (lane_q >= NSA_HD) if h == 0 else (lane_q < NSA_HD)
        pen = jnp.where(in_blk, (_dot_nt(eye, sel_t) - 1.0) * -NEG, 0.0)
        qh = qs[r0:r0 + NSA_GROUP * Q_BLOCK]
        qxs.append(qh + jnp.concatenate([pen] * NSA_GROUP, axis=0).astype(BF16))

    kc = min(SLC_CHUNK, seq)
    lane_1 = lax.broadcasted_iota(jnp.int32, (1, LANES), 1)
    keep_lo = jnp.where(lane_1 < NSA_HD, 1.0, 0.0).astype(BF16)
    keep_hi = jnp.where(lane_1 < NSA_HD, 0.0, 1.0).astype(BF16)
    ones = jnp.ones((kc, LANES), BF16)

    def chunk(c, carry, bias):
        m, l, acc = carry
        k0 = pl.multiple_of(c * kc, kc)
        kk = kvs_ref[pl.ds(k0, kc), 0:128].astype(BF16)
        et = et_ref[pl.ds(k0, kc), :]
        vx = jnp.concatenate([kvs_ref[pl.ds(k0, kc), 128:256].astype(BF16), ones], axis=1)
        s = jnp.concatenate([_dot_nt(qxs[0], kk * keep_lo + et * keep_hi),
                             _dot_nt(qxs[1], kk * keep_hi + et * keep_lo)], axis=0)
        if bias is not None:
            s = s + bias
        m_new = jnp.maximum(m, jnp.max(s, axis=-1, keepdims=True))
        alpha = jnp.exp(m - m_new)
        pv = _dot(jnp.exp(s - m_new).astype(BF16), vx)
        return m_new, alpha * l + pv[:, LANES:LANES + 1], alpha * acc + pv[:, 0:LANES]

    rows = nh * Q_BLOCK
    init = (jnp.full((rows, 1), NEG, F32), jnp.zeros((rows, 1), F32), jnp.zeros((rows, LANES), F32))
    c_last = (i * Q_BLOCK) // kc
    carry = lax.fori_loop(0, c_last, lambda c, cr: chunk(c, cr, None), init)
    kpos = c_last * kc + lax.broadcasted_iota(jnp.int32, (Q_BLOCK, kc), 1)
    _, l, acc = chunk(c_last, carry, rep(jnp.where(kpos <= tq, 0.0, NEG)))
    o_slc = acc / l

    wk = WINDOW + Q_BLOCK
    start = pl.multiple_of(Q_BLOCK * jnp.maximum(i - WINDOW // Q_BLOCK, 0), Q_BLOCK)
    kk = kvw_ref[pl.ds(start, wk), 0:128].astype(BF16)
    vx = jnp.concatenate([kvw_ref[pl.ds(start, wk), 128:256].astype(BF16), jnp.ones((wk, LANES), BF16)], axis=1)
    d = tq - (start + lax.broadcasted_iota(jnp.int32, (Q_BLOCK, wk), 1))
    sw = _dot_nt(qs, kk) + rep(jnp.where((d >= 0) & (d < WINDOW), 0.0, NEG))
    aw = _dot(jnp.exp(sw - jnp.max(sw, axis=-1, keepdims=True)).astype(BF16), vx)
    o_win = aw[:, 0:LANES] / aw[:, LANES:LANES + 1]

    a, b, c = _split3(ng_ref[...])
    for h in range(NSA_KV_HEADS):
        eg = eg_ref[h]
        gexp = _dot(a, eg) + _dot(b, eg) + _dot(c, eg)
        for g in range(NSA_GROUP):
            hh = NSA_GROUP * h + g
            rs = slice(Q_BLOCK * hh, Q_BLOCK * (hh + 1))
            gate = lambda br: gexp[:, LANES * (NSA_GROUP * br + g):LANES * (NSA_GROUP * br + g + 1)]
            o = gate(0) * o_cmp[rs] + gate(1) * o_slc[rs] + gate(2) * o_win[rs]
            o_ref[:, LANES * hh:LANES * (hh + 1)] = o.astype(BF16)


def _nsa_prompt(q, kvs, kvw, ck, cv, ng, nb, seq, n_cmp):
    nq = seq // Q_BLOCK
    ncp = ck.shape[1]
    ns = seq // SLC_BLOCK
    assert ns <= NSA_HD and seq >= WINDOW + Q_BLOCK
    et = (jnp.arange(seq)[:, None] // SLC_BLOCK == jnp.arange(LANES)[None, :] % NSA_HD).astype(BF16)
    n_g = 3 * NSA_GROUP
    col = jnp.arange(n_g * LANES)[None, None, :] // LANES
    src = (col // NSA_GROUP) * NSA_HEADS + NSA_GROUP * jnp.arange(NSA_KV_HEADS)[:, None, None] + col % NSA_GROUP
    eg = (jnp.arange(LANES)[None, :, None] == src).astype(BF16)
    return pl.pallas_call(
        functools.partial(_nsa_prompt_kernel, seq, n_cmp),
        grid=(nb, nq),
        in_specs=[pl.BlockSpec((Q_BLOCK, 1024), lambda b, i: (b * nq + i, 0)),
                  pl.BlockSpec((seq, 256), lambda b, i: (b, 0)),
                  pl.BlockSpec((seq, 256), lambda b, i: (b, 0)),
                  pl.BlockSpec((1, ncp, LANES), lambda b, i: (b, 0, 0)),
                  pl.BlockSpec((1, ncp, LANES), lambda b, i: (b, 0, 0)),
                  pl.BlockSpec((Q_BLOCK, LANES), lambda b, i: (b * nq + i, 0)),
                  pl.BlockSpec((seq, LANES), lambda b, i: (0, 0)),
                  pl.BlockSpec((NSA_KV_HEADS, LANES, n_g * LANES), lambda b, i: (0, 0, 0))],
        out_specs=pl.BlockSpec((Q_BLOCK, 1024), lambda b, i: (b * nq + i, 0)),
        out_shape=jax.ShapeDtypeStruct((nb * seq, 1024), BF16),
        compiler_params=pltpu.CompilerParams(dimension_semantics=("arbitrary",) * 2, vmem_limit_bytes=VMEM_LIMIT),
        name="nsa_prompt",
    )(q, kvs, kvw, ck, cv, ng, et, eg)


def _nsa_sample_kernel(past, n_pages, dseq, wb, pt_ref, q_ref, kvs_ref, kvw_ref, ck_ref, cv_ref, ng_ref, cwin_ref,
                       e_ref, pool_ref, o_ref, nwin_ref, kvt_scr, kw_scr, sem):
    b = pl.program_id(0)
    slot = b % 2

    def page_copy(bb, sl, j):
        return pltpu.make_async_copy(pool_ref.at[pt_ref[bb, j]], kvt_scr.at[sl, :, pl.ds(PAGE_SIZE * j, PAGE_SIZE)],
                                     sem.at[sl])

    @pl.when(b == 0)
    def _():
        for j in range(n_pages):
            page_copy(0, 0, j).start()

    @pl.when(b + 1 < pl.num_programs(0))
    def _():
        for j in range(n_pages):
            page_copy(b + 1, 1 - slot, j).start()

    n_heads = NSA_KV_HEADS * NSA_GROUP
    rows = n_heads * dseq
    kpad = past + PAGE_SIZE
    ns = (past + dseq + SLC_BLOCK - 1) // SLC_BLOCK
    nsp = e_ref.shape[0]
    qf = q_ref[0].astype(F32)
    qs = jnp.concatenate([qf[:, LANES * hh:LANES * (hh + 1)] for hh in range(n_heads)], axis=0).astype(BF16)
    t1 = past + lax.broadcasted_iota(jnp.int32, (dseq, 1), 0)
    t = jnp.concatenate([t1] * n_heads, axis=0)

    sc = _dot_nt(qs, ck_ref[0].astype(BF16))
    jn = lax.broadcasted_iota(jnp.int32, sc.shape, 1)
    pc = _msoftmax(sc, (jn * CMP_STRIDE + CMP_BLOCK - 1) <= t)
    o_cmp = _dot(pc.astype(BF16), cv_ref[0].astype(BF16))

    per_kv = []
    for kvh in range(NSA_KV_HEADS):
        base = kvh * NSA_GROUP * dseq
        s = pc[base:base + dseq]
        for g in range(1, NSA_GROUP):
            s = s + pc[base + g * dseq:base + (g + 1) * dseq]
        per_kv += [s] * NSA_GROUP
    pcs = jnp.concatenate(per_kv, axis=0)
    n_cmp = (ns * SLC_BLOCK) // CMP_STRIDE - CMP_BLOCK // CMP_STRIDE + 1
    imp_t = _importance_t(_overlap_t(nsp, pcs.shape[1], n_cmp), pcs)
    tl = past + lax.broadcasted_iota(jnp.int32, (nsp, rows), 1) % dseq
    score, blk = _block_scores_t(imp_t, tl)
    ranks = [jnp.zeros(score.shape, F32) for _ in range(RANK_ACCS)]
    for k in range(ns):
        sk = score[k:k + 1, :]
        ranks[k % RANK_ACCS] += jnp.where((sk > score) | ((sk == score) & (blk > k)), 1.0, 0.0)
    rank = functools.reduce(lambda x, y: x + y, ranks)
    sel_t = jnp.where(rank < min(N_SELECT, ns), 1.0, 0.0).astype(BF16)
    eye = (lax.broadcasted_iota(jnp.int32, (rows, rows), 0)
           == lax.broadcasted_iota(jnp.int32, (rows, rows), 1)).astype(BF16)
    sel = _dot_nt(eye, sel_t)
    sele = _dot(sel.astype(BF16), e_ref[...])

    knew = jnp.concatenate([kvs_ref[0], jnp.zeros((PAGE_SIZE - dseq, 256), F32)], axis=0).astype(BF16)
    for j in range(n_pages):
        page_copy(b, slot, j).wait()
    s = jnp.concatenate([_dot(qs, kvt_scr[slot, 0:128, :].astype(BF16)), _dot_nt(qs, knew[:, 0:128])], axis=1)
    kpos = lax.broadcasted_iota(jnp.int32, (rows, kpad), 1)
    p = _msoftmax(s, (sele > 0.5) & (kpos <= t)).astype(BF16)
    o_slc = _dot_nt(p[:, 0:past], kvt_scr[slot, 128:256, :].astype(BF16)) + _dot(p[:, past:kpad], knew[:, 128:256])

    wpad = kw_scr.shape[0]
    kw_scr[pl.ds(0, wb), :] = cwin_ref[0]
    kw_scr[pl.ds(wb, dseq), :] = kvw_ref[0]
    kw_scr[pl.ds(wb + dseq, wpad - wb - dseq), :] = jnp.zeros((wpad - wb - dseq, 256), F32)
    sw = _dot_nt(qs, kw_scr[:, 0:128].astype(BF16))
    d = t - (past - wb + lax.broadcasted_iota(jnp.int32, (rows, wpad), 1))
    pw = _msoftmax(sw, (d >= 0) & (d < WINDOW))
    o_win = _dot(pw.astype(BF16), kw_scr[:, 128:256].astype(BF16))
    nwin_ref[0, pl.ds(0, wb - dseq), :] = cwin_ref[0, pl.ds(dseq, wb - dseq), :]
    nwin_ref[0, pl.ds(wb - dseq, dseq), :] = kvw_ref[0]

    ngv = ng_ref[0]
    for hh in range(n_heads):
        rs = slice(dseq * hh, dseq * (hh + 1))
        gate = lambda br: ngv[:, br * NSA_HEADS + hh:br * NSA_HEADS + hh + 1]
        o_ref[0, :, LANES * hh:LANES * (hh + 1)] = gate(0) * o_cmp[rs] + gate(1) * o_slc[rs] + gate(2) * o_win[rs]


def _nsa_sample(page_table, q, kvs, kvw, ck, cv, ng, cwin, e_s, pool, past):
    nb, n_pages = page_table.shape
    dseq = q.shape[1]
    wb = cwin.shape[1]
    kpad = past + PAGE_SIZE
    wpad = wb + PAGE_SIZE
    nsp = e_s.shape[0]
    rows = NSA_HEADS * dseq
    per_b = lambda a: pl.BlockSpec((1,) + a.shape[1:], lambda b, pt: (b,) + (0,) * (a.ndim - 1))
    return pl.pallas_call(
        functools.partial(_nsa_sample_kernel, past, n_pages, dseq, wb),
        grid_spec=pltpu.PrefetchScalarGridSpec(
            num_scalar_prefetch=1, grid=(nb,),
            in_specs=[per_b(q), per_b(kvs), per_b(kvw), per_b(ck), per_b(cv), per_b(ng), per_b(cwin),
                      pl.BlockSpec(e_s.shape, lambda b, pt: (0, 0)), pl.BlockSpec(memory_space=pl.ANY)],
            out_specs=[pl.BlockSpec((1, dseq, 1024), lambda b, pt: (b, 0, 0)),
                       pl.BlockSpec((1, wb, 256), lambda b, pt: (b, 0, 0))],
            scratch_shapes=[pltpu.VMEM((2, 256, past), F32), pltpu.VMEM((wpad, 256), F32),
                            pltpu.SemaphoreType.DMA((2,))]),
        out_shape=[jax.ShapeDtypeStruct((nb, dseq, 1024), F32), jax.ShapeDtypeStruct((nb, wb, 256), F32)],
        compiler_params=pltpu.CompilerParams(dimension_semantics=("arbitrary",), vmem_limit_bytes=VMEM_LIMIT),
        name="nsa_sample",
    )(page_table, q, kvs, kvw, ck, cv, ng, cwin, e_s, pool)


RET_GROUP = 4


def _ret_kernel(bg, c_true, *refs):
    rq_refs, rk_refs, rv_refs, rg_refs = (refs[k * bg:(k + 1) * bg] for k in range(4))
    s0_ref, gro_ref, o_ref, sout_ref, s_scr = refs[4 * bg:]
    c = pl.program_id(1)
    cs = RET_CHUNK

    @pl.when(c == 0)
    def _():
        for b in range(bg):
            for tile in range(RET_HEADS // 2):
                s_scr[b, tile] = jnp.concatenate([s0_ref[b, 2 * tile], s0_ref[b, 2 * tile + 1]], axis=0)

    diff = (lax.broadcasted_iota(jnp.int32, (cs, cs), 0) - lax.broadcasted_iota(jnp.int32, (cs, cs), 1)).astype(F32)
    ic = lax.broadcasted_iota(jnp.int32, (cs, 1), 0).astype(F32)
    lane = lax.broadcasted_iota(jnp.int32, (cs, LANES), 1)
    for hd in range(RET_HEADS):
        lg = math.log(1.0 - 2.0 ** (-5.0 - hd))
        tile, half = hd // 2, hd % 2
        in_half = (lane >= RET_DK * half) & (lane < RET_DK * (half + 1))
        dmat = jnp.where(diff >= 0, jnp.exp(jnp.maximum(diff, 0.0) * lg), 0.0)
        dec_q = jnp.exp((ic + 1.0) * lg)
        dec_k = jnp.exp((c_true - 1.0 - ic) * lg)
        for b in range(bg):
            qm = jnp.where(in_half, rq_refs[b][:, LANES * tile:LANES * (tile + 1)], 0.0).astype(BF16)
            k2 = rk_refs[b][:, LANES * tile:LANES * (tile + 1)]
            vb = rv_refs[b][:, LANES * hd:LANES * (hd + 1)].astype(BF16)
            o = _dot((_dot_nt(qm, k2.astype(BF16)) * dmat).astype(BF16), vb)
            st = s_scr[b, tile]
            o = o + _dot(qm, st.astype(BF16)) * dec_q
            kdec_t = (k2 * dec_k).T
            upd = _dot(kdec_t[RET_DK * half:RET_DK * (half + 1)].astype(BF16), vb)
            s_new = st[RET_DK * half:RET_DK * (half + 1)] * math.exp(c_true * lg) + upd
            s_scr[b, tile, pl.ds(RET_DK * half, RET_DK), :] = s_new
            on = _tilenorm(o, gro_ref[...], RET_DV)
            o_ref[b, :, LANES * hd:LANES * (hd + 1)] = (rg_refs[b][:, LANES * hd:LANES * (hd + 1)] * on).astype(BF16)

    @pl.when(c == pl.num_programs(1) - 1)
    def _():
        for b in range(bg):
            for hd in range(RET_HEADS):
                sout_ref[b, hd] = s_scr[b, hd // 2, pl.ds(RET_DK * (hd % 2), RET_DK), :]


def _retention(rq, rk, rv, rg, s0, gro, nb, n_chunks, c_true):
    cs = RET_CHUNK
    bg = math.gcd(nb, RET_GROUP)
    rows = lambda w: [pl.BlockSpec((cs, w), lambda g, c, b=b: ((g * bg + b) * n_chunks + c, 0)) for b in range(bg)]
    state = pl.BlockSpec((bg, RET_HEADS, RET_DK, RET_DV), lambda g, c: (g, 0, 0, 0))
    return pl.pallas_call(
        functools.partial(_ret_kernel, bg, float(c_true)),
        grid=(nb // bg, n_chunks),
        in_specs=rows(256) + rows(256) + rows(512) + rows(512) + [state, pl.BlockSpec(gro.shape, lambda g, c: (0, 0))],
        out_specs=[pl.BlockSpec((bg, cs, 512), lambda g, c: (g, c, 0)), state],
        out_shape=[jax.ShapeDtypeStruct((nb, n_chunks * cs, 512), BF16),
                   jax.ShapeDtypeStruct((nb, RET_HEADS, RET_DK, RET_DV), F32)],
        scratch_shapes=[pltpu.VMEM((bg, RET_HEADS // 2, LANES, RET_DV), F32)],
        compiler_params=pltpu.CompilerParams(dimension_semantics=("arbitrary", "arbitrary")),
        name="retention",
    )(*([rq] * bg + [rk] * bg + [rv] * bg + [rg] * bg), s0, gro)


def _memkv_kernel(m_ref, g_ref, w_ref, gk_ref, o_ref):
    x = m_ref[...]
    xn = (x * lax.rsqrt(jnp.mean(x * x, axis=-1, keepdims=True) + EPS) * g_ref[...]).astype(BF16)
    hk = _dot(xn, w_ref[...])
    half = MEM_HEADS * MEM_HD
    for hd in range(MEM_HEADS):
        o_ref[:, LANES * hd:LANES * (hd + 1)] = _tilenorm(hk[:, LANES * hd:LANES * (hd + 1)], gk_ref[...], MEM_HD)
    o_ref[:, half:2 * half] = hk[:, half:2 * half]


def _mem_kv(mem2d, g_mem, w_mem, gk_mem, tm):
    n = mem2d.shape[0]
    const = lambda a: pl.BlockSpec(a.shape, lambda i: (0,) * a.ndim)
    return pl.pallas_call(
        _memkv_kernel,
        grid=(n // tm,),
        in_specs=[pl.BlockSpec((tm, D_MODEL), lambda i: (i, 0)), const(g_mem), const(w_mem), const(gk_mem)],
        out_specs=pl.BlockSpec((tm, 1024), lambda i: (i, 0)),
        out_shape=jax.ShapeDtypeStruct((n, 1024), F32),
        compiler_params=pltpu.CompilerParams(dimension_semantics=("arbitrary",)),
        name="mem_kv",
    )(mem2d, g_mem, w_mem, gk_mem)


def _memattn_kernel(q_ref, kv_ref, o_ref):
    half = MEM_HEADS * MEM_HD
    for hd in range(MEM_HEADS):
        cols = slice(LANES * hd, LANES * (hd + 1))
        s = _dot_nt(q_ref[:, cols].astype(BF16), kv_ref[:, cols].astype(BF16))
        p = jnp.exp(s - jnp.max(s, axis=-1, keepdims=True))
        p = p / jnp.sum(p, axis=-1, keepdims=True)
        o = _dot(p.astype(BF16), kv_ref[:, half + LANES * hd:half + LANES * (hd + 1)].astype(BF16))
        o_ref[:, cols] = o.astype(o_ref.dtype)


def _memattn_cache_kernel(q_ref, kv_ref, o_ref):
    stride = 2 * MEM_HEADS
    mem_len = kv_ref.shape[1] // stride
    for hd in range(MEM_HEADS):
        cols = slice(LANES * hd, LANES * (hd + 1))
        k = kv_ref[0, pl.ds(hd, mem_len, stride=stride), :].astype(BF16)
        v = kv_ref[0, pl.ds(MEM_HEADS + hd, mem_len, stride=stride), :].astype(BF16)
        s = _dot_nt(q_ref[:, cols].astype(BF16), k)
        p = jnp.exp(s - jnp.max(s, axis=-1, keepdims=True))
        p = p / jnp.sum(p, axis=-1, keepdims=True)
        o_ref[:, cols] = _dot(p.astype(BF16), v).astype(o_ref.dtype)


def _mem_attend_cache(q2d, cache_rows, dseq):
    nb, n_rows, _ = cache_rows.shape
    return pl.pallas_call(
        _memattn_cache_kernel,
        grid=(nb,),
        in_specs=[pl.BlockSpec((dseq, 512), lambda b: (b, 0)), pl.BlockSpec((1, n_rows, LANES), lambda b: (b, 0, 0))],
        out_specs=pl.BlockSpec((dseq, 512), lambda b: (b, 0)),
        out_shape=jax.ShapeDtypeStruct((nb * dseq, 512), F32),
        compiler_params=pltpu.CompilerParams(dimension_semantics=("arbitrary",)),
        name="mem_attend_cache",
    )(q2d, cache_rows)


def _mem_attend(q2d, kv2d, nb, blocks_per_b, tq, mem_len, out_rows, out_dtype):
    return pl.pallas_call(
        _memattn_kernel,
        grid=(nb, blocks_per_b),
        in_specs=[pl.BlockSpec((tq, 512), lambda b, i: (b * blocks_per_b + i, 0)),
                  pl.BlockSpec((mem_len, 1024), lambda b, i: (b, 0))],
        out_specs=pl.BlockSpec((tq, 512), lambda b, i: (b * blocks_per_b + i, 0)),
        out_shape=jax.ShapeDtypeStruct((out_rows, 512), out_dtype),
        compiler_params=pltpu.CompilerParams(dimension_semantics=("arbitrary", "arbitrary")),
        name="mem_attend",
    )(q2d, kv2d)


def _merge_kernel(nbp, xp_ref, xs_ref, onsap_ref, onsas_ref, oretp_ref, orets_ref, omemp_ref, omems_ref, g_ref,
                  wbg_ref, wn_ref, wr_ref, wm_ref, wo_ref, gffn_ref, wrt_ref, brt_ref, wgu_ref, wd_ref,
                  h_ref, hn_ref, comb_ref, wgu_bf_ref, wd_bf_ref):
    i = pl.program_id(0)
    wgu_bf_ref[...] = wgu_ref[...].astype(BF16)
    wd_bf_ref[...] = wd_ref[...].astype(BF16)
    pick = lambda p_ref, s_ref: jnp.where(i < nbp, p_ref[...], s_ref[...])
    x = pick(xp_ref, xs_ref)
    xn = (x * lax.rsqrt(jnp.mean(x * x, axis=-1, keepdims=True) + EPS) * g_ref[...]).astype(BF16)
    bg = _sigmoid(_dot(xn, wbg_ref[...]))
    mixed = (bg[:, 0:1024] * _dot(pick(onsap_ref, onsas_ref), wn_ref[...])
             + bg[:, 1024:2048] * _dot(pick(oretp_ref, orets_ref), wr_ref[...])
             + bg[:, 2048:3072] * _dot(pick(omemp_ref, omems_ref), wm_ref[...]))
    hres = x + _dot(mixed.astype(BF16), wo_ref[...])
    h_ref[...] = hres
    hn = hres * lax.rsqrt(jnp.mean(hres * hres, axis=-1, keepdims=True) + EPS) * gffn_ref[...]
    hn_ref[...] = hn.astype(BF16)

    a, b, _ = _split3(hn)
    logits = _dot(a, wrt_ref[0]) + _dot(a, wrt_ref[1]) + _dot(b, wrt_ref[0]) + brt_ref[...]
    lane = lax.broadcasted_iota(jnp.int32, logits.shape, 1).astype(F32)
    vals, hots = [], []
    for _k in range(TOP_K):
        mx = jnp.max(logits, axis=-1, keepdims=True)
        idx = jnp.min(jnp.where(logits == mx, lane, float(LANES)), axis=-1, keepdims=True)
        hot = lane == idx
        vals.append(mx)
        hots.append(hot)
        logits = jnp.where(hot, -3.0e38, logits)
    es = [jnp.exp(v - vals[0]) for v in vals]
    den = es[0] + es[1] + es[2] + es[3]
    comb = jnp.zeros(logits.shape, F32)
    for e_k, hot in zip(es, hots):
        comb = comb + jnp.where(hot, e_k / den, 0.0)
    comb_ref[...] = comb


def _merge(xp, xs, onsa, onsa_s, oret, oret_s, omem, omem_s, g_attn, wbg, wn, wr, wm, wo, gffn, wrt, brt,
           w_gate_up, w_down, tm):
    n_p, n_s = xp.shape[0], xs.shape[0]
    nbp, nbs = n_p // tm, n_s // tm
    n_all = n_p + n_s
    n_halves = 2 * N_EXPERTS
    assert nbp + nbs >= n_halves
    row = lambda w: pl.BlockSpec((tm, w), lambda i: (i, 0))
    rowp = lambda w: pl.BlockSpec((tm, w), lambda i: (jnp.minimum(i, nbp - 1), 0))
    rows = lambda w: pl.BlockSpec((tm, w), lambda i: (jnp.maximum(i - nbp, 0), 0))
    const = lambda a: pl.BlockSpec(a.shape, lambda i: (0,) * a.ndim, pipeline_mode=pl.Buffered(1))
    half = lambda i: jnp.minimum(i, n_halves - 1)
    wgu_spec = pl.BlockSpec((1, D_MODEL, D_FF), lambda i: (half(i) // 2, 0, half(i) % 2))
    wd_spec = pl.BlockSpec((1, D_FF // 2, D_MODEL), lambda i: (half(i) // 2, half(i) % 2, 0))
    return pl.pallas_call(
        functools.partial(_merge_kernel, nbp),
        grid=(nbp + nbs,),
        in_specs=[rowp(1024), rows(1024), rowp(1024), rows(1024), rowp(512), rows(512), rowp(512), rows(512),
                  const(g_attn), const(wbg), const(wn), const(wr), const(wm),
                  const(wo), const(gffn), const(wrt), const(brt), wgu_spec, wd_spec],
        out_specs=[row(1024), row(1024), row(LANES), wgu_spec, wd_spec],
        out_shape=[jax.ShapeDtypeStruct((n_all, 1024), F32), jax.ShapeDtypeStruct((n_all, 1024), BF16),
                   jax.ShapeDtypeStruct((n_all, LANES), F32),
                   jax.ShapeDtypeStruct(w_gate_up.shape, BF16), jax.ShapeDtypeStruct(w_down.shape, BF16)],
        compiler_params=pltpu.CompilerParams(dimension_semantics=("arbitrary",), vmem_limit_bytes=VMEM_LIMIT),
        name="merge",
    )(xp, xs, onsa, onsa_s, oret, oret_s, omem, omem_s, g_attn, wbg, wn, wr, wm, wo, gffn, wrt, brt,
      w_gate_up, w_down)


MOE_BLOCK_CAP = 1664
ROUTE_TILE = 128
SLAB_MARGIN = 1.3


def _slabs(tb):
    tok_a = ROUTE_TILE * (-(-(tb // ROUTE_TILE) // 2))
    tok_b = tb - tok_a
    cap = lambda tok: 16 * max(1, -(-int(tok * TOP_K / N_EXPERTS * SLAB_MARGIN) // 16))
    return tok_a, tok_b, cap(tok_a), (cap(tok_b) if tok_b else 0)


def _route_kernel(comb_ref, pos_ref, post_ref, cnt_ref):
    tb = comb_ref.shape[0]
    rt = ROUTE_TILE
    tiles_a = _slabs(tb)[0] // rt
    r = lax.broadcasted_iota(jnp.int32, (rt, rt), 0)
    c = lax.broadcasted_iota(jnp.int32, (rt, rt), 1)
    ltri = (c < r).astype(BF16)
    eye = (c == r).astype(BF16)
    carry = jnp.zeros((1, LANES), F32)
    counts = []
    for s in range(tb // rt):
        if s == tiles_a:
            counts.append(carry)
            carry = jnp.zeros((1, LANES), F32)
        comb = comb_ref[pl.ds(rt * s, rt), :]
        sel = jnp.where(comb > 0.0, 1.0, 0.0)
        pos = jnp.where(sel > 0.0, _dot(ltri, sel.astype(BF16)) + carry, -1.0)
        carry = carry + jnp.sum(sel, axis=0, keepdims=True)
        pos_ref[pl.ds(rt * s, rt), :] = jnp.where(sel > 0.0, pos + 0.5 * comb, -1.0)
        hi = jnp.floor((pos + 1.0) * (1.0 / 256.0))
        lo = (pos + 1.0) - 256.0 * hi
        post_ref[:, rt * s:rt * (s + 1)] = 256.0 * _dot_nt(eye, hi.astype(BF16)) + _dot_nt(eye, lo.astype(BF16)) - 1.0
    counts.append(carry)
    counts += [jnp.zeros((1, LANES), F32)] * (8 - len(counts))
    cnt_ref[0] = jnp.concatenate(counts, axis=0)


def _route(comb, tb):
    n_all = comb.shape[0]
    nblk = n_all // tb
    return pl.pallas_call(
        _route_kernel,
        grid=(nblk,),
        in_specs=[pl.BlockSpec((tb, LANES), lambda i: (i, 0))],
        out_specs=[pl.BlockSpec((tb, LANES), lambda i: (i, 0)), pl.BlockSpec((LANES, tb), lambda i: (0, i)),
                   pl.BlockSpec((1, 8, LANES), lambda i: (i, 0, 0))],
        out_shape=[jax.ShapeDtypeStruct((n_all, LANES), F32), jax.ShapeDtypeStruct((LANES, n_all), F32),
                   jax.ShapeDtypeStruct((nblk, 8, LANES), F32)],
        compiler_params=pltpu.CompilerParams(dimension_semantics=("arbitrary",)),
        name="route",
    )(comb)


def _moe_kernel(n_s, cnt_ref, hn_ref, h_ref, pos_ref, post_ref, wgu_ref, wd_ref, bgu_ref, bd_ref, y_ref, ys_ref):
    i = pl.program_id(0)
    e = pl.program_id(1)
    tb = hn_ref.shape[0]
    tok_a, tok_b, cap_a, cap_b = _slabs(tb)
    halves = [(0, tok_a, 0, cap_a)] + ([(tok_a, tok_b, cap_a, cap_b)] if tok_b else [])

    @pl.when(e == 0)
    def _():
        y_ref[...] = h_ref[...]

    lane = lax.broadcasted_iota(jnp.int32, (tb, LANES), 1)
    packed = jnp.sum(jnp.where(lane == e, pos_ref[...], 0.0), axis=-1, keepdims=True)
    pcol = jnp.floor(packed)
    ccol = 2.0 * (packed - pcol)
    prow = post_ref[pl.ds(e, 1), :]

    def chunk(c, carry):
        cf = c.astype(F32)
        parts = []
        for t0, tok, _, cap in halves:
            slot_g = cf * cap + lax.broadcasted_iota(jnp.int32, (cap, tok), 0).astype(F32)
            gather = jnp.where(prow[:, t0:t0 + tok] == slot_g, 1.0, 0.0).astype(BF16)
            parts.append(_dot(gather, hn_ref[pl.ds(t0, tok), :]).astype(BF16))
        xg = jnp.concatenate(parts, axis=0)
        gu = _dot(xg, wgu_ref[0]) + bgu_ref[0]
        gate = jnp.minimum(gu[:, 0:D_FF], SWIGLU_LIMIT)
        up = jnp.clip(gu[:, D_FF:2 * D_FF], -SWIGLU_LIMIT, SWIGLU_LIMIT)
        act = (up + 1.0) * gate * _sigmoid(SWIGLU_ALPHA * gate)
        yc = (_dot(act.astype(BF16), wd_ref[0]) + bd_ref[0]).astype(BF16)
        for t0, tok, r0, cap in halves:
            slot_s = cf * cap + lax.broadcasted_iota(jnp.int32, (tok, cap), 1).astype(F32)
            scatter = jnp.where(pcol[t0:t0 + tok] == slot_s, ccol[t0:t0 + tok], 0.0).astype(BF16)
            y_ref[pl.ds(t0, tok), :] += _dot(scatter, yc[r0:r0 + cap])
        return carry

    lax.fori_loop(0, cnt_ref[i * N_EXPERTS + e], chunk, 0)

    @pl.when((i == pl.num_programs(0) - 1) & (e == N_EXPERTS - 1))
    def _():
        ys_ref[...] = y_ref[pl.ds(tb - n_s, n_s), :]


def _moe(cnt, hn, h, pos, post, wgu, wd, b_gate_up, b_down, tb, n_s):
    n_all = hn.shape[0]
    n_p = n_all - n_s
    assert n_s <= tb and n_s % 8 == 0 and (n_all // tb - 1) * tb < n_p
    row = lambda w: pl.BlockSpec((tb, w), lambda i, e, cnt: (i, 0))
    return pl.pallas_call(
        functools.partial(_moe_kernel, n_s),
        grid_spec=pltpu.PrefetchScalarGridSpec(
            num_scalar_prefetch=1, grid=(n_all // tb, N_EXPERTS),
            in_specs=[row(1024), row(1024), row(LANES),
                      pl.BlockSpec((LANES, tb), lambda i, e, cnt: (0, i)),
                      pl.BlockSpec((1, D_MODEL, 2 * D_FF), lambda i, e, cnt: (e, 0, 0)),
                      pl.BlockSpec((1, D_FF, D_MODEL), lambda i, e, cnt: (e, 0, 0)),
                      pl.BlockSpec((1, 1, 2 * D_FF), lambda i, e, cnt: (e, 0, 0)),
                      pl.BlockSpec((1, 1, D_MODEL), lambda i, e, cnt: (e, 0, 0))],
            out_specs=[row(1024), pl.BlockSpec((n_s, 1024), lambda i, e, cnt: (0, 0))]),
        out_shape=[jax.ShapeDtypeStruct((n_p, 1024), F32), jax.ShapeDtypeStruct((n_s, 1024), F32)],
        compiler_params=pltpu.CompilerParams(dimension_semantics=("arbitrary",) * 2, vmem_limit_bytes=VMEM_LIMIT),
        name="moe",
    )(cnt, hn, h, pos, post, wgu, wd, b_gate_up, b_down)


def _pick_tile(n, pref):
    t = pref
    while n % t:
        t //= 2
    return t


def kernel(x_prompt, x_sample, cache_cmp, cache_slc, cache_win, state_ret, cache_mem, page_table, mem_prompt,
           g_attn, w_in, g_q_nsa, g_k_cmp, g_k_slc, g_k_win, pe_ck, w_ck1, w_ck2, pe_cv, w_cv1, w_cv2,
           g_ret_out, g_mem, w_mem_kv, g_q_mem, g_k_mem, w_br_nsa, w_br_ret, w_br_mem, w_out,
           g_ffn, w_router, b_router, w_gate_up, b_gate_up, w_down, b_down):
    nb, seq, _ = x_prompt.shape
    db, dseq, _ = x_sample.shape
    n_pages = page_table.shape[1]
    past = n_pages * PAGE_SIZE
    wb = cache_win.shape[1]
    mem_len = mem_prompt.shape[1]
    n_p, n_s = nb * seq, db * dseq
    n_all = n_p + n_s
    tm = _pick_tile(seq, ROW_TILE)

    o = 0
    cols = {}
    for name, wdt in (("q", 512), ("kv", 768), ("ng", 24), ("rq", 256), ("rk", 256), ("rv", 512), ("rg", 512),
                      ("mq", 512), ("bg", 3072)):
        cols[name] = w_in[:, o:o + wdt]
        o += wdt
    wq = cols["q"].reshape(D_MODEL, NSA_HEADS, NSA_HD)
    zq = jnp.zeros_like(wq)
    wq_pad = jnp.concatenate(
        [jnp.concatenate([wq[:, hh], zq[:, hh]] if hh < NSA_GROUP else [zq[:, hh], wq[:, hh]], axis=-1)
         for hh in range(NSA_HEADS)], axis=-1)
    w_ng = jnp.pad(cols["ng"], ((0, 0), (0, C_NG - 24)))
    w1 = jnp.concatenate([wq_pad, cols["kv"], w_ng, cols["rq"], cols["rk"], cols["rv"], cols["rg"], cols["mq"]],
                         axis=-1).astype(BF16)
    w_bg = cols["bg"].astype(BF16)
    two = lambda g: jnp.concatenate([g, g]).reshape(1, LANES)
    r1 = lambda g: g.reshape(1, -1)

    half = RET_DK // 2
    inv = ROPE_BASE ** (-jnp.arange(half, dtype=F32) / half)
    pos = jnp.concatenate([jnp.arange(seq, dtype=jnp.int32),
                           jnp.tile(past + jnp.arange(dseq, dtype=jnp.int32), db)]).astype(F32)
    ang = pos[:, None] * inv[None, :]
    cos_t = jnp.tile(jnp.cos(ang), (1, 2 * RET_HEADS))
    sin_t = jnp.tile(jnp.concatenate([-jnp.sin(ang), jnp.sin(ang)], axis=-1), (1, RET_HEADS))

    xp = x_prompt.reshape(n_p, D_MODEL)
    xs = x_sample.reshape(n_s, D_MODEL)
    (q, kvc, kvs, kvw, ng, rq, rk, rv, rg, mq, kvct, kvst, kvwt) = _project(
        xp, xs, r1(g_attn), w1, cos_t, sin_t, two(g_q_nsa), two(g_k_slc), two(g_k_win), r1(g_q_mem), seq, tm)

    w1s = jnp.stack([w_ck1, w_ck1, w_cv1, w_cv1]).reshape(4, 2, CMP_STRIDE, NSA_HD, CMP_HIDDEN)
    wbig = jnp.einsum("shpdc,st->psdhtc", w1s, jnp.eye(4, dtype=F32)).reshape(CHUNK_W, 2 * 4 * CMP_HIDDEN).astype(BF16)
    pes = jnp.stack([pe_ck, pe_ck, pe_cv, pe_cv]).reshape(4, 2, CMP_STRIDE, NSA_HD)
    pe2 = jnp.pad(jnp.transpose(pes, (1, 2, 0, 3)).reshape(2, CHUNK_W), ((0, 6), (0, 0)))
    zc = jnp.zeros_like(w_ck2)
    bd2 = lambda w: jnp.concatenate([jnp.concatenate([w, zc], 1), jnp.concatenate([zc, w], 1)], 0).astype(BF16)
    gk2 = two(g_k_cmp)

    pages_p = seq // PAGE_SIZE
    pool_p = (kvc if n_all % PAGE_SIZE == 0 else kvc[:n_p]).reshape(-1, CHUNKS_PER_PAGE, CHUNK_W)
    pt_p = jnp.arange(nb * pages_p, dtype=jnp.int32).reshape(nb, pages_p)
    ck_p, cv_p = _compress(pt_p, pool_p, wbig, pe2, bd2(w_ck2), bd2(w_cv2), gk2)
    pages_t = lambda c: jnp.transpose(c, (0, 2, 3, 4, 1)).reshape(-1, 256, PAGE_SIZE)
    ck_s, cv_s = _compress_t(page_table, pages_t(cache_cmp), wbig, pe2, bd2(w_ck2), bd2(w_cv2), gk2)

    n_cmp_p = seq // CMP_STRIDE - CMP_BLOCK // CMP_STRIDE + 1
    onsa = _nsa_prompt(q, kvs, kvw, ck_p, cv_p, ng, nb, seq, n_cmp_p)

    ns_s = (past + dseq + SLC_BLOCK - 1) // SLC_BLOCK
    nsp = (ns_s + 7) // 8 * 8
    kpad = past + PAGE_SIZE
    e_s = (jnp.arange(kpad)[None, :] // SLC_BLOCK == jnp.arange(nsp)[:, None]).astype(BF16)
    s3 = lambda a: a[n_p:].reshape(db, dseq, a.shape[1])
    onsa_s, new_win_s = _nsa_sample(page_table, s3(q), s3(kvs), s3(kvw), ck_s, cv_s, s3(ng),
                                    cache_win.reshape(db, wb, 256), e_s,
                                    pages_t(cache_slc), past)
    onsa_s = onsa_s.reshape(n_s, 1024).astype(BF16)

    gro = r1(g_ret_out)
    oret, ret_state_p = _retention(rq, rk, rv, rg, jnp.zeros((nb, RET_HEADS, RET_DK, RET_DV), F32), gro,
                                   nb, seq // RET_CHUNK, RET_CHUNK)
    padc = lambda a: jnp.pad(s3(a), ((0, 0), (0, RET_CHUNK - dseq), (0, 0))).reshape(db * RET_CHUNK, a.shape[1])
    oret_s, ret_state_s = _retention(padc(rq), padc(rk), padc(rv), padc(rg), state_ret, gro, db, 1, dseq)
    oret = oret.reshape(n_p, 512)
    oret_s = oret_s[:, :dseq].reshape(n_s, 512)

    mem_kv_p = _mem_kv(mem_prompt.reshape(nb * mem_len, D_MODEL), r1(g_mem), w_mem_kv.astype(BF16), r1(g_k_mem),
                       _pick_tile(nb * mem_len, 256))
    tq = _pick_tile(seq, 512)
    omem = _mem_attend(mq, mem_kv_p, nb, seq // tq, tq, mem_len, n_p, BF16)
    omem_s = _mem_attend_cache(mq[n_p:].astype(F32), cache_mem.reshape(db, mem_len * 2 * MEM_HEADS, MEM_HD),
                               dseq).astype(BF16)

    wn = w_br_nsa.reshape(NSA_HEADS, NSA_HD, D_MODEL)
    zn = jnp.zeros_like(wn)
    wn_pad = jnp.concatenate(
        [jnp.concatenate([wn[hh], zn[hh]] if hh < NSA_GROUP else [zn[hh], wn[hh]], axis=0) for hh in range(NSA_HEADS)],
        axis=0).astype(BF16)
    wr_pad = jnp.pad(w_router, ((0, 0), (0, LANES - N_EXPERTS)))
    wr_hi = wr_pad.astype(BF16)
    wr_lo = (wr_pad - wr_hi.astype(F32)).astype(BF16)
    brt = jnp.concatenate([b_router, jnp.full((LANES - N_EXPERTS,), NEG, F32)]).reshape(1, LANES)
    tmm = _pick_tile(math.gcd(seq, n_s), 256)
    h, hn, comb, wgu_bf, wd_bf = _merge(
        xp, xs, onsa, onsa_s, oret, oret_s, omem, omem_s, r1(g_attn), w_bg, wn_pad, w_br_ret.astype(BF16),
        w_br_mem.astype(BF16), w_out.astype(BF16), r1(g_ffn), jnp.stack([wr_hi, wr_lo]), brt, w_gate_up, w_down, tmm)

    tb = max(c for c in range(LANES, MOE_BLOCK_CAP + 1, LANES) if n_all % c == 0)
    pos, post, cnt = _route(comb, tb)
    _, _, cap_a, cap_b = _slabs(tb)
    passes = jnp.ceil(cnt[:, 0, :N_EXPERTS] / cap_a)
    if cap_b:
        passes = jnp.maximum(passes, jnp.ceil(cnt[:, 1, :N_EXPERTS] / cap_b))
    cnt = passes.astype(jnp.int32).reshape(-1)
    y_p, y_s = _moe(cnt, hn, h, pos, post, wgu_bf, wd_bf,
                    b_gate_up.reshape(N_EXPERTS, 1, 2 * D_FF), b_down.reshape(N_EXPERTS, 1, D_MODEL), tb, n_s)

    kv5 = lambda a, bsz, t: a.reshape(bsz, t, 2, NSA_KV_HEADS, NSA_HD)
    wp = min(WINDOW, seq)
    kv5t = lambda a: jnp.transpose(a.reshape(nb, 2, NSA_KV_HEADS, NSA_HD, a.shape[2]), (0, 4, 1, 2, 3))
    return (y_p.reshape(nb, seq, D_MODEL), y_s.reshape(db, dseq, D_MODEL),
            kv5t(kvct), kv5t(kvst), kv5t(kvwt[:, :, seq - wp:]),
            ret_state_p, mem_kv_p.reshape(nb, mem_len, 2, MEM_HEADS, MEM_HD),
            kv5(kvc[n_p:], db, dseq), kv5(kvs[n_p:], db, dseq), kv5(new_win_s, db, wb), ret_state_s)
```

```python
import functools
import math

import jax
import jax.numpy as jnp
from jax import lax
from jax.experimental import pallas as pl
from jax.experimental.pallas import tpu as pltpu

F32 = jnp.float32
BF16 = jnp.bfloat16

D_MODEL = 1024
NSA_HEADS = 8
NSA_KV_HEADS = 2
NSA_GROUP = 4
NSA_HD = 64
CMP_BLOCK = 32
CMP_STRIDE = 16
CMP_HIDDEN = 128
SLC_BLOCK = 64
N_SELECT = 16
N_LOCAL_FORCED = 2
FORCE_BONUS = 1.0e4
WINDOW = 512
Q_BLOCK = 128
PAGE_SIZE = 128
RET_HEADS = 4
RET_DK = 64
RET_DV = 128
RET_CHUNK = 128
ROPE_BASE = 10000.0
MEM_HEADS = 4
MEM_HD = 128
N_EXPERTS = 32
TOP_K = 4
D_FF = 1024
SWIGLU_ALPHA = 1.702
SWIGLU_LIMIT = 7.0
EPS = 1e-6
NEG = -1e30

LANES = 128
ROW_TILE = 512
VMEM_LIMIT = 56 * 1024 * 1024


def _dot(a, b):
    return jnp.dot(a, b, preferred_element_type=F32)


def _dot_nt(a, b):
    return lax.dot_general(a, b, (((1,), (1,)), ((), ())), preferred_element_type=F32)


def _sigmoid(x):
    return 1.0 / (1.0 + jnp.exp(-x))


def _split3(x):
    a = x.astype(BF16)
    r = x - a.astype(F32)
    b = r.astype(BF16)
    c = (r - b.astype(F32)).astype(BF16)
    return a, b, c


def _msoftmax(s, m):
    s = jnp.where(m, s, NEG)
    mx = jnp.max(s, axis=-1, keepdims=True)
    p = jnp.where(m, jnp.exp(s - mx), 0.0)
    return p / jnp.maximum(jnp.sum(p, axis=-1, keepdims=True), 1e-30)


def _halfnorm(t, g2):
    lane = lax.broadcasted_iota(jnp.int32, t.shape, 1)
    lo = lane < NSA_HD
    t2 = t * t
    s0 = jnp.sum(jnp.where(lo, t2, 0.0), axis=-1, keepdims=True)
    s1 = jnp.sum(jnp.where(lo, 0.0, t2), axis=-1, keepdims=True)
    r = jnp.where(lo, lax.rsqrt(s0 * (1.0 / NSA_HD) + EPS), lax.rsqrt(s1 * (1.0 / NSA_HD) + EPS))
    return t * r * g2


def _tilenorm(t, g, width):
    return t * lax.rsqrt(jnp.sum(t * t, axis=-1, keepdims=True) * (1.0 / width) + EPS) * g


def _rot(x, cos, sin):
    lane = lax.broadcasted_iota(jnp.int32, x.shape, 1)
    first = (lane % RET_DK) < (RET_DK // 2)
    n = x.shape[1]
    sw = jnp.where(first, pltpu.roll(x, n - RET_DK // 2, 1), pltpu.roll(x, RET_DK // 2, 1))
    return x * cos + sw * sin


C_Q, C_KV, C_NG, C_RET, C_MQ = 1024, 768, 128, 1536, 512
O_KV = C_Q
O_NG = O_KV + C_KV
O_RET = O_NG + C_NG
O_MQ = O_RET + C_RET
W1_COLS = O_MQ + C_MQ


def _proj_kernel(nbp, xp_ref, xs_ref, g_ref, w_ref, cos_ref, sin_ref, gq_ref, gks_ref, gkw_ref, gqm_ref,
                 q_ref, kvc_ref, kvs_ref, kvw_ref, ng_ref, rq_ref, rk_ref, rv_ref, rg_ref, mq_ref,
                 kvct_ref, kvst_ref, kvwt_ref):
    i = pl.program_id(0)
    x = jnp.where(i < nbp, xp_ref[...], xs_ref[...])
    xn = (x * lax.rsqrt(jnp.mean(x * x, axis=-1, keepdims=True) + EPS) * g_ref[...]).astype(BF16)

    hq = _dot(xn, w_ref[:, 0:C_Q])
    for hh in range(NSA_HEADS):
        t = hq[:, LANES * hh:LANES * (hh + 1)]
        q_ref[:, LANES * hh:LANES * (hh + 1)] = (_tilenorm(t, gq_ref[...], NSA_HD) * NSA_HD ** -0.5).astype(BF16)

    hkv = _dot(xn, w_ref[:, O_KV:O_KV + C_KV])
    kvc = hkv[:, 0:256]
    kvs = jnp.concatenate([_halfnorm(hkv[:, 256:384], gks_ref[...]), hkv[:, 384:512]], axis=1)
    kvw = jnp.concatenate([_halfnorm(hkv[:, 512:640], gkw_ref[...]), hkv[:, 640:768]], axis=1)
    kvc_ref[...] = kvc
    kvs_ref[...] = kvs
    kvw_ref[...] = kvw

    @pl.when(i < nbp)
    def _():
        kvct_ref[0] = kvc.T
        kvst_ref[0] = kvs.T
        kvwt_ref[0] = kvw.T

    ng_ref[...] = _sigmoid(_dot(xn, w_ref[:, O_NG:O_NG + C_NG]))

    hr = _dot(xn, w_ref[:, O_RET:O_RET + C_RET])
    cos = cos_ref[...]
    sin = sin_ref[...]
    rq_ref[...] = _rot(hr[:, 0:256], cos, sin)
    rk_ref[...] = _rot(hr[:, 256:512], cos, sin) * RET_DK ** -0.5
    rv_ref[...] = hr[:, 512:1024]
    rg = hr[:, 1024:1536]
    rg_ref[...] = rg * _sigmoid(rg)

    hm = _dot(xn, w_ref[:, O_MQ:O_MQ + C_MQ])
    for hd in range(MEM_HEADS):
        t = hm[:, LANES * hd:LANES * (hd + 1)]
        mq_ref[:, LANES * hd:LANES * (hd + 1)] = (_tilenorm(t, gqm_ref[...], MEM_HD) * MEM_HD ** -0.5).astype(BF16)


def _project(xp, xs, g_attn, w1, cos_t, sin_t, gq2, gks2, gkw2, gqm, seq, tm):
    n_p, n_s = xp.shape[0], xs.shape[0]
    nbp, nbs = n_p // tm, -(-n_s // tm)
    n_all = n_p + n_s
    bps = seq // tm
    row = lambda w: pl.BlockSpec((tm, w), lambda i: (i, 0))
    const = lambda a: pl.BlockSpec(a.shape, lambda i: (0,) * a.ndim)
    rope = pl.BlockSpec((tm, 256), lambda i: (jnp.where(i < nbp, i % bps, bps + i - nbp), 0))
    widths = (1024, 256, 256, 256, 128, 256, 256, 512, 512, 512)
    dtypes = (BF16, F32, F32, F32, F32, F32, F32, F32, F32, BF16)
    kvt = pl.BlockSpec((1, 256, tm), lambda i: (jnp.minimum(i, nbp - 1) // bps, 0, jnp.minimum(i, nbp - 1) % bps))
    return pl.pallas_call(
        functools.partial(_proj_kernel, nbp),
        grid=(nbp + nbs,),
        in_specs=[pl.BlockSpec((tm, D_MODEL), lambda i: (jnp.minimum(i, nbp - 1), 0)),
                  pl.BlockSpec((tm, D_MODEL), lambda i: (jnp.maximum(i - nbp, 0), 0)),
                  const(g_attn), const(w1), rope, rope, const(gq2), const(gks2), const(gkw2), const(gqm)],
        out_specs=[row(w) for w in widths] + [kvt] * 3,
        out_shape=[jax.ShapeDtypeStruct((n_all, w), d) for w, d in zip(widths, dtypes)]
        + [jax.ShapeDtypeStruct((n_p // seq, 256, seq), F32)] * 3,
        compiler_params=pltpu.CompilerParams(dimension_semantics=("arbitrary",), vmem_limit_bytes=VMEM_LIMIT),
        name="proj",
    )(xp, xs, g_attn, w1, cos_t, sin_t, gq2, gks2, gkw2, gqm)


CHUNK_W = CMP_STRIDE * 256
CHUNKS_PER_PAGE = PAGE_SIZE // CMP_STRIDE


def _compress_kernel(n_pages, pt_ref, pool_ref, wbig_ref, pe_ref, w2k_ref, w2v_ref, gk_ref,
                     ck_ref, cv_ref, x_scr, r_scr, sem):
    b = pl.program_id(0)
    n = n_pages * CHUNKS_PER_PAGE

    def page_copy(j):
        return pltpu.make_async_copy(pool_ref.at[pt_ref[b, j]],
                                     x_scr.at[pl.ds(CHUNKS_PER_PAGE * j, CHUNKS_PER_PAGE)], sem)

    for j in range(n_pages):
        page_copy(j).start()
    x_scr[pl.ds(n, 8), :] = pe_ref[...]
    for j in range(n_pages):
        page_copy(j).wait()

    r_scr[...] = _dot(x_scr[...].astype(BF16), wbig_ref[...])
    cvec = r_scr[n:n + 1, 0:512] + r_scr[n + 1:n + 2, 512:1024]
    hid = r_scr[0:n, 0:512] + r_scr[pl.ds(1, n), 512:1024] + cvec
    hb = (hid * _sigmoid(hid)).astype(BF16)
    ck_ref[0] = _halfnorm(_dot(hb[:, 0:256], w2k_ref[...]), gk_ref[...])
    cv_ref[0] = _dot(hb[:, 256:512], w2v_ref[...])


def _compress(page_table, pool, wbig, pe2, w2k, w2v, gk2):
    nb, n_pages = page_table.shape
    n = n_pages * CHUNKS_PER_PAGE
    const = lambda a: pl.BlockSpec(a.shape, lambda b, pt: (0,) * a.ndim)
    out = pl.BlockSpec((1, n, LANES), lambda b, pt: (b, 0, 0))
    return pl.pallas_call(
        functools.partial(_compress_kernel, n_pages),
        grid_spec=pltpu.PrefetchScalarGridSpec(
            num_scalar_prefetch=1, grid=(nb,),
            in_specs=[pl.BlockSpec(memory_space=pl.ANY), const(wbig), const(pe2), const(w2k), const(w2v), const(gk2)],
            out_specs=[out, out],
            scratch_shapes=[pltpu.VMEM((n + 8, CHUNK_W), F32), pltpu.VMEM((n + 8, 1024), F32),
                            pltpu.SemaphoreType.DMA(())]),
        out_shape=[jax.ShapeDtypeStruct((nb, n, LANES), F32)] * 2,
        compiler_params=pltpu.CompilerParams(dimension_semantics=("arbitrary",), vmem_limit_bytes=VMEM_LIMIT),
        name="compress",
    )(page_table, pool, wbig, pe2, w2k, w2v, gk2)


def _compress_t_kernel(n_pages, pt_ref, pool_ref, wbig_ref, pe_ref, w2k_ref, w2v_ref, gk_ref,
                       ck_ref, cv_ref, pg_scr, t_scr, r_scr, cvec_scr, sem):
    b = pl.program_id(0)
    slot = b % 2
    n = n_pages * CHUNKS_PER_PAGE

    def page_copy(bb, sl, j):
        return pltpu.make_async_copy(pool_ref.at[pt_ref[bb, j]], pg_scr.at[sl, j], sem.at[sl])

    @pl.when(b == 0)
    def _():
        for j in range(n_pages):
            page_copy(0, 0, j).start()
        r = _dot(pe_ref[...].astype(BF16), wbig_ref[...])
        cvec_scr[...] = jnp.broadcast_to(r[0:1, 0:512] + r[1:2, 512:1024], cvec_scr.shape)
        r_scr[pl.ds(n, 8), :] = jnp.zeros((8, 1024), F32)

    @pl.when(b + 1 < pl.num_programs(0))
    def _():
        for j in range(n_pages):
            page_copy(b + 1, 1 - slot, j).start()

    for j in range(n_pages):
        page_copy(b, slot, j).wait()

    src = lax.broadcasted_iota(jnp.int32, (PAGE_SIZE, PAGE_SIZE), 0)
    dst = lax.broadcasted_iota(jnp.int32, (PAGE_SIZE, PAGE_SIZE), 1)
    perm = (dst == CHUNKS_PER_PAGE * (src % CMP_STRIDE) + src // CMP_STRIDE).astype(BF16)
    for j in range(n_pages):
        xp = _dot(pg_scr[slot, j].astype(BF16), perm)
        tk = xp[0:LANES, :].T
        tv = xp[LANES:2 * LANES, :].T
        for p in range(CMP_STRIDE):
            rows = pl.ds(CHUNKS_PER_PAGE * j, CHUNKS_PER_PAGE)
            t_scr[p, rows, 0:LANES] = tk[CHUNKS_PER_PAGE * p:CHUNKS_PER_PAGE * (p + 1)]
            t_scr[p, rows, LANES:2 * LANES] = tv[CHUNKS_PER_PAGE * p:CHUNKS_PER_PAGE * (p + 1)]

    acc = jnp.zeros((n, 1024), F32)
    for p in range(CMP_STRIDE):
        acc = acc + _dot(t_scr[p].astype(BF16), wbig_ref[256 * p:256 * (p + 1), :])
    r_scr[pl.ds(0, n), :] = acc
    hid = acc[:, 0:512] + r_scr[pl.ds(1, n), 512:1024] + cvec_scr[0:1, :]
    hb = (hid * _sigmoid(hid)).astype(BF16)
    ck_ref[0] = _halfnorm(_dot(hb[:, 0:256], w2k_ref[...]), gk_ref[...])
    cv_ref[0] = _dot(hb[:, 256:512], w2v_ref[...])


def _compress_t(page_table, pool_t, wbig, pe2, w2k, w2v, gk2):
    nb, n_pages = page_table.shape
    n = n_pages * CHUNKS_PER_PAGE
    const = lambda a: pl.BlockSpec(a.shape, lambda b, pt: (0,) * a.ndim)
    out = pl.BlockSpec((1, n, LANES), lambda b, pt: (b, 0, 0))
    return pl.pallas_call(
        functools.partial(_compress_t_kernel, n_pages),
        grid_spec=pltpu.PrefetchScalarGridSpec(
            num_scalar_prefetch=1, grid=(nb,),
            in_specs=[pl.BlockSpec(memory_space=pl.ANY), const(wbig), const(pe2), const(w2k), const(w2v), const(gk2)],
            out_specs=[out, out],
            scratch_shapes=[pltpu.VMEM((2, n_pages, 256, PAGE_SIZE), F32), pltpu.VMEM((CMP_STRIDE, n, 2 * LANES), F32),
                            pltpu.VMEM((n + 8, 1024), F32), pltpu.VMEM((8, 512), F32),
                            pltpu.SemaphoreType.DMA((2,))]),
        out_shape=[jax.ShapeDtypeStruct((nb, n, LANES), F32)] * 2,
        compiler_params=pltpu.CompilerParams(dimension_semantics=("arbitrary",), vmem_limit_bytes=VMEM_LIMIT),
        name="compress_t",
    )(page_table, pool_t, wbig, pe2, w2k, w2v, gk2)


def _overlap_t(ns_rows, n_cmp_cols, n_cmp):
    s = lax.broadcasted_iota(jnp.int32, (ns_rows, n_cmp_cols), 0)
    n = lax.broadcasted_iota(jnp.int32, (ns_rows, n_cmp_cols), 1)
    ov = (n * CMP_STRIDE < s * SLC_BLOCK + SLC_BLOCK) & (n * CMP_STRIDE + CMP_BLOCK > s * SLC_BLOCK) & (n < n_cmp)
    return ov.astype(BF16)


def _importance_t(ov_t, pcs):
    a, b, c = _split3(pcs)
    return _dot_nt(ov_t, a) + _dot_nt(ov_t, b) + _dot_nt(ov_t, c)


def _block_scores_t(imp_t, tl):
    blk = lax.broadcasted_iota(jnp.int32, imp_t.shape, 0)
    cur = tl // SLC_BLOCK
    valid = blk <= cur
    forced = (blk == 0) | (valid & (blk > cur - N_LOCAL_FORCED))
    return jnp.where(valid, imp_t + jnp.where(forced, FORCE_BONUS, 0.0), NEG), blk


SLC_CHUNK = 1024
RANK_ACCS = 4


def _nsa_prompt_kernel(seq, n_cmp, q_ref, kvs_ref, kvw_ref, ck_ref, cv_ref, ng_ref, et_ref, eg_ref, o_ref):
    i = pl.program_id(1)
    ns = seq // SLC_BLOCK
    nh = NSA_HEADS
    q8 = q_ref[...]
    qs = jnp.concatenate([q8[:, LANES * hh:LANES * (hh + 1)] for hh in range(nh)], axis=0)
    tq = i * Q_BLOCK + lax.broadcasted_iota(jnp.int32, (Q_BLOCK, 1), 0)
    rep = lambda a: jnp.concatenate([a] * nh, axis=0)
    t8 = rep(tq)

    ckb = ck_ref[0].astype(BF16)
    sc = _dot_nt(qs, ckb)
    jn = lax.broadcasted_iota(jnp.int32, sc.shape, 1)
    pc = _msoftmax(sc, (jn * CMP_STRIDE + CMP_BLOCK - 1) <= t8)
    o_cmp = _dot(pc.astype(BF16), cv_ref[0].astype(BF16))

    ov_t = _overlap_t(ns, pc.shape[1], n_cmp)
    tl = i * Q_BLOCK + lax.broadcasted_iota(jnp.int32, (ns, Q_BLOCK), 1)
    eye = (lax.broadcasted_iota(jnp.int32, (Q_BLOCK, Q_BLOCK), 0)
           == lax.broadcasted_iota(jnp.int32, (Q_BLOCK, Q_BLOCK), 1)).astype(BF16)
    lane_q = lax.broadcasted_iota(jnp.int32, (Q_BLOCK, LANES), 1)
    zpad = lambda n: [jnp.zeros((n, Q_BLOCK), BF16)] if n else []
    qxs = []
    for h in range(NSA_KV_HEADS):
        r0 = h * NSA_GROUP * Q_BLOCK
        pcs = pc[r0:r0 + 128] + pc[r0 + 128:r0 + 256] + pc[r0 + 256:r0 + 384] + pc[r0 + 384:r0 + 512]
        score, blk = _block_scores_t(_importance_t(ov_t, pcs), tl)
        ranks = [jnp.zeros(score.shape, F32) for _ in range(RANK_ACCS)]
        for k in range(ns):
            sk = score[k:k + 1, :]
            ranks[k % RANK_ACCS] += jnp.where((sk > score) | ((sk == score) & (blk > k)), 1.0, 0.0)
        rank = functools.reduce(lambda x, y: x + y, ranks)
        sel_t = jnp.where(rank < min(N_SELECT, ns), 1.0, 0.0).astype(BF16)
        sel_t = jnp.concatenate(zpad(NSA_HD * (1 - h)) + [sel_t] + zpad(LANES - ns - NSA_HD * (1 - h)), axis=0)
        in_blk = (lane_q >= NSA_HD) if h == 0 else (lane_q < NSA_HD)
        pen = jnp.where(in_blk, (_dot_nt(eye, sel_t) - 1.0) * -NEG, 0.0)
        qh = qs[r0:r0 + NSA_GROUP * Q_BLOCK]
        qxs.append(qh + jnp.concatenate([pen] * NSA_GROUP, axis=0).astype(BF16))

    kc = min(SLC_CHUNK, seq)
    lane_1 = lax.broadcasted_iota(jnp.int32, (1, LANES), 1)
    keep_lo = jnp.where(lane_1 < NSA_HD, 1.0, 0.0).astype(BF16)
    keep_hi = jnp.where(lane_1 < NSA_HD, 0.0, 1.0).astype(BF16)
    ones = jnp.ones((kc, LANES), BF16)

    def chunk(c, carry, bias):
        m, l, acc = carry
        k0 = pl.multiple_of(c * kc, kc)
        kk = kvs_ref[pl.ds(k0, kc), 0:128].astype(BF16)
        et = et_ref[pl.ds(k0, kc), :]
        vx = jnp.concatenate([kvs_ref[pl.ds(k0, kc), 128:256].astype(BF16), ones], axis=1)
        s = jnp.concatenate([_dot_nt(qxs[0], kk * keep_lo + et * keep_hi),
                             _dot_nt(qxs[1], kk * keep_hi + et * keep_lo)], axis=0)
        if bias is not None:
            s = s + bias
        m_new = jnp.maximum(m, jnp.max(s, axis=-1, keepdims=True))
        alpha = jnp.exp(m - m_new)
        pv = _dot(jnp.exp(s - m_new).astype(BF16), vx)
        return m_new, alpha * l + pv[:, LANES:LANES + 1], alpha * acc + pv[:, 0:LANES]

    rows = nh * Q_BLOCK
    init = (jnp.full((rows, 1), NEG, F32), jnp.zeros((rows, 1), F32), jnp.zeros((rows, LANES), F32))
    c_last = (i * Q_BLOCK) // kc
    carry = lax.fori_loop(0, c_last, lambda c, cr: chunk(c, cr, None), init)
    kpos = c_last * kc + lax.broadcasted_iota(jnp.int32, (Q_BLOCK, kc), 1)
    _, l, acc = chunk(c_last, carry, rep(jnp.where(kpos <= tq, 0.0, NEG)))
    o_slc = acc / l

    wk = WINDOW + Q_BLOCK
    start = pl.multiple_of(Q_BLOCK * jnp.maximum(i - WINDOW // Q_BLOCK, 0), Q_BLOCK)
    kk = kvw_ref[pl.ds(start, wk), 0:128].astype(BF16)
    vx = jnp.concatenate([kvw_ref[pl.ds(start, wk), 128:256].astype(BF16), jnp.ones((wk, LANES), BF16)], axis=1)
    d = tq - (start + lax.broadcasted_iota(jnp.int32, (Q_BLOCK, wk), 1))
    sw = _dot_nt(qs, kk) + rep(jnp.where((d >= 0) & (d < WINDOW), 0.0, NEG))
    aw = _dot(jnp.exp(sw - jnp.max(sw, axis=-1, keepdims=True)).astype(BF16), vx)
    o_win = aw[:, 0:LANES] / aw[:, LANES:LANES + 1]

    a, b, _ = _split3(ng_ref[...])
    ab = jnp.concatenate([a, b], axis=1)
    for h in range(NSA_KV_HEADS):
        gexp = _dot(ab, eg_ref[h])
        for g in range(NSA_GROUP):
            hh = NSA_GROUP * h + g
            rs = slice(Q_BLOCK * hh, Q_BLOCK * (hh + 1))
            gate = lambda br: gexp[:, LANES * (NSA_GROUP * br + g):LANES * (NSA_GROUP * br + g + 1)]
            o = gate(0) * o_cmp[rs] + gate(1) * o_slc[rs] + gate(2) * o_win[rs]
            o_ref[:, LANES * hh:LANES * (hh + 1)] = o.astype(BF16)


def _nsa_prompt(q, kvs, kvw, ck, cv, ng, nb, seq, n_cmp):
    nq = seq // Q_BLOCK
    ncp = ck.shape[1]
    ns = seq // SLC_BLOCK
    assert ns <= NSA_HD and seq >= WINDOW + Q_BLOCK
    et = (jnp.arange(seq)[:, None] // SLC_BLOCK == jnp.arange(LANES)[None, :] % NSA_HD).astype(BF16)
    n_g = 3 * NSA_GROUP
    col = jnp.arange(n_g * LANES)[None, None, :] // LANES
    src = (col // NSA_GROUP) * NSA_HEADS + NSA_GROUP * jnp.arange(NSA_KV_HEADS)[:, None, None] + col % NSA_GROUP
    eg = (jnp.arange(2 * LANES)[None, :, None] % LANES == src).astype(BF16)
    return pl.pallas_call(
        functools.partial(_nsa_prompt_kernel, seq, n_cmp),
        grid=(nb, nq),
        in_specs=[pl.BlockSpec((Q_BLOCK, 1024), lambda b, i: (b * nq + i, 0)),
                  pl.BlockSpec((seq, 256), lambda b, i: (b, 0)),
                  pl.BlockSpec((seq, 256), lambda b, i: (b, 0)),
                  pl.BlockSpec((1, ncp, LANES), lambda b, i: (b, 0, 0)),
                  pl.BlockSpec((1, ncp, LANES), lambda b, i: (b, 0, 0)),
                  pl.BlockSpec((Q_BLOCK, LANES), lambda b, i: (b * nq + i, 0)),
                  pl.BlockSpec((seq, LANES), lambda b, i: (0, 0)),
                  pl.BlockSpec((NSA_KV_HEADS, 2 * LANES, n_g * LANES), lambda b, i: (0, 0, 0))],
        out_specs=pl.BlockSpec((Q_BLOCK, 1024), lambda b, i: (b * nq + i, 0)),
        out_shape=jax.ShapeDtypeStruct((nb * seq, 1024), BF16),
        compiler_params=pltpu.CompilerParams(dimension_semantics=("arbitrary",) * 2, vmem_limit_bytes=VMEM_LIMIT),
        name="nsa_prompt",
    )(q, kvs, kvw, ck, cv, ng, et, eg)


def _nsa_sample_kernel(past, n_pages, dseq, wb, pt_ref, q_ref, kvs_ref, kvw_ref, ck_ref, cv_ref, ng_ref, cwin_ref,
                       e_ref, pool_ref, o_ref, nwin_ref, kvt_scr, kw_scr, sem):
    b = pl.program_id(0)
    slot = b % 2

    def page_copy(bb, sl, j):
        return pltpu.make_async_copy(pool_ref.at[pt_ref[bb, j]], kvt_scr.at[sl, :, pl.ds(PAGE_SIZE * j, PAGE_SIZE)],
                                     sem.at[sl])

    @pl.when(b == 0)
    def _():
        for j in range(n_pages):
            page_copy(0, 0, j).start()

    @pl.when(b + 1 < pl.num_programs(0))
    def _():
        for j in range(n_pages):
            page_copy(b + 1, 1 - slot, j).start()

    n_heads = NSA_KV_HEADS * NSA_GROUP
    rows = n_heads * dseq
    kpad = past + PAGE_SIZE
    ns = (past + dseq + SLC_BLOCK - 1) // SLC_BLOCK
    nsp = e_ref.shape[0]
    qf = q_ref[0].astype(F32)
    qs = jnp.concatenate([qf[:, LANES * hh:LANES * (hh + 1)] for hh in range(n_heads)], axis=0).astype(BF16)
    t1 = past + lax.broadcasted_iota(jnp.int32, (dseq, 1), 0)
    t = jnp.concatenate([t1] * n_heads, axis=0)

    sc = _dot_nt(qs, ck_ref[0].astype(BF16))
    jn = lax.broadcasted_iota(jnp.int32, sc.shape, 1)
    pc = _msoftmax(sc, (jn * CMP_STRIDE + CMP_BLOCK - 1) <= t)
    o_cmp = _dot(pc.astype(BF16), cv_ref[0].astype(BF16))

    per_kv = []
    for kvh in range(NSA_KV_HEADS):
        base = kvh * NSA_GROUP * dseq
        s = pc[base:base + dseq]
        for g in range(1, NSA_GROUP):
            s = s + pc[base + g * dseq:base + (g + 1) * dseq]
        per_kv += [s] * NSA_GROUP
    pcs = jnp.concatenate(per_kv, axis=0)
    n_cmp = (ns * SLC_BLOCK) // CMP_STRIDE - CMP_BLOCK // CMP_STRIDE + 1
    imp_t = _importance_t(_overlap_t(nsp, pcs.shape[1], n_cmp), pcs)
    tl = past + lax.broadcasted_iota(jnp.int32, (nsp, rows), 1) % dseq
    score, blk = _block_scores_t(imp_t, tl)
    blk_f = blk.astype(F32)
    sel_f = jnp.zeros(score.shape, F32)
    for _ in range(min(N_SELECT, ns)):
        top = jnp.max(score, axis=0, keepdims=True)
        first = jnp.min(jnp.where(score == top, blk_f, 1e9), axis=0, keepdims=True)
        hit = blk_f == first
        sel_f = jnp.where(hit, 1.0, sel_f)
        score = jnp.where(hit, -3.0e38, score)
    sel_t = sel_f.astype(BF16)
    eye = (lax.broadcasted_iota(jnp.int32, (rows, rows), 0)
           == lax.broadcasted_iota(jnp.int32, (rows, rows), 1)).astype(BF16)
    sel = _dot_nt(eye, sel_t)
    sele = _dot(sel.astype(BF16), e_ref[...])

    knew = jnp.concatenate([kvs_ref[0], jnp.zeros((PAGE_SIZE - dseq, 256), F32)], axis=0).astype(BF16)
    for j in range(n_pages):
        page_copy(b, slot, j).wait()
    s = jnp.concatenate([_dot(qs, kvt_scr[slot, 0:128, :].astype(BF16)), _dot_nt(qs, knew[:, 0:128])], axis=1)
    kpos = lax.broadcasted_iota(jnp.int32, (rows, kpad), 1)
    p = _msoftmax(s, (sele > 0.5) & (kpos <= t)).astype(BF16)
    o_slc = _dot_nt(p[:, 0:past], kvt_scr[slot, 128:256, :].astype(BF16)) + _dot(p[:, past:kpad], knew[:, 128:256])

    wpad = kw_scr.shape[0]
    kw_scr[pl.ds(0, wb), :] = cwin_ref[0]
    kw_scr[pl.ds(wb, dseq), :] = kvw_ref[0]
    kw_scr[pl.ds(wb + dseq, wpad - wb - dseq), :] = jnp.zeros((wpad - wb - dseq, 256), F32)
    sw = _dot_nt(qs, kw_scr[:, 0:128].astype(BF16))
    d = t - (past - wb + lax.broadcasted_iota(jnp.int32, (rows, wpad), 1))
    pw = _msoftmax(sw, (d >= 0) & (d < WINDOW))
    o_win = _dot(pw.astype(BF16), kw_scr[:, 128:256].astype(BF16))
    nwin_ref[0, pl.ds(0, wb - dseq), :] = cwin_ref[0, pl.ds(dseq, wb - dseq), :]
    nwin_ref[0, pl.ds(wb - dseq, dseq), :] = kvw_ref[0]

    ngv = ng_ref[0]
    for hh in range(n_heads):
        rs = slice(dseq * hh, dseq * (hh + 1))
        gate = lambda br: ngv[:, br * NSA_HEADS + hh:br * NSA_HEADS + hh + 1]
        o_ref[0, :, LANES * hh:LANES * (hh + 1)] = gate(0) * o_cmp[rs] + gate(1) * o_slc[rs] + gate(2) * o_win[rs]


def _nsa_sample(page_table, q, kvs, kvw, ck, cv, ng, cwin, e_s, pool, past):
    nb, n_pages = page_table.shape
    dseq = q.shape[1]
    wb = cwin.shape[1]
    kpad = past + PAGE_SIZE
    wpad = wb + PAGE_SIZE
    nsp = e_s.shape[0]
    rows = NSA_HEADS * dseq
    per_b = lambda a: pl.BlockSpec((1,) + a.shape[1:], lambda b, pt: (b,) + (0,) * (a.ndim - 1))
    return pl.pallas_call(
        functools.partial(_nsa_sample_kernel, past, n_pages, dseq, wb),
        grid_spec=pltpu.PrefetchScalarGridSpec(
            num_scalar_prefetch=1, grid=(nb,),
            in_specs=[per_b(q), per_b(kvs), per_b(kvw), per_b(ck), per_b(cv), per_b(ng), per_b(cwin),
                      pl.BlockSpec(e_s.shape, lambda b, pt: (0, 0)), pl.BlockSpec(memory_space=pl.ANY)],
            out_specs=[pl.BlockSpec((1, dseq, 1024), lambda b, pt: (b, 0, 0)),
                       pl.BlockSpec((1, wb, 256), lambda b, pt: (b, 0, 0))],
            scratch_shapes=[pltpu.VMEM((2, 256, past), F32), pltpu.VMEM((wpad, 256), F32),
                            pltpu.SemaphoreType.DMA((2,))]),
        out_shape=[jax.ShapeDtypeStruct((nb, dseq, 1024), F32), jax.ShapeDtypeStruct((nb, wb, 256), F32)],
        compiler_params=pltpu.CompilerParams(dimension_semantics=("arbitrary",), vmem_limit_bytes=VMEM_LIMIT),
        name="nsa_sample",
    )(page_table, q, kvs, kvw, ck, cv, ng, cwin, e_s, pool)


RET_GROUP = 4


def _ret_kernel(bg, c_true, *refs):
    rq_refs, rk_refs, rv_refs, rg_refs = (refs[k * bg:(k + 1) * bg] for k in range(4))
    s0_ref, gro_ref, o_ref, sout_ref, s_scr = refs[4 * bg:]
    c = pl.program_id(1)
    cs = RET_CHUNK

    @pl.when(c == 0)
    def _():
        for b in range(bg):
            for tile in range(RET_HEADS // 2):
                s_scr[b, tile] = jnp.concatenate([s0_ref[b, 2 * tile], s0_ref[b, 2 * tile + 1]], axis=0)

    diff = (lax.broadcasted_iota(jnp.int32, (cs, cs), 0) - lax.broadcasted_iota(jnp.int32, (cs, cs), 1)).astype(F32)
    ic = lax.broadcasted_iota(jnp.int32, (cs, 1), 0).astype(F32)
    lane = lax.broadcasted_iota(jnp.int32, (cs, LANES), 1)
    for hd in range(RET_HEADS):
        lg = math.log(1.0 - 2.0 ** (-5.0 - hd))
        tile, half = hd // 2, hd % 2
        in_half = (lane >= RET_DK * half) & (lane < RET_DK * (half + 1))
        dmat = jnp.where(diff >= 0, jnp.exp(jnp.maximum(diff, 0.0) * lg), 0.0)
        dec_q = jnp.exp((ic + 1.0) * lg)
        dec_k = jnp.exp((c_true - 1.0 - ic) * lg)
        for b in range(bg):
            qm = jnp.where(in_half, rq_refs[b][:, LANES * tile:LANES * (tile + 1)], 0.0).astype(BF16)
            k2 = rk_refs[b][:, LANES * tile:LANES * (tile + 1)]
            vb = rv_refs[b][:, LANES * hd:LANES * (hd + 1)].astype(BF16)
            o = _dot((_dot_nt(qm, k2.astype(BF16)) * dmat).astype(BF16), vb)
            st = s_scr[b, tile]
            o = o + _dot(qm, st.astype(BF16)) * dec_q
            kdec_t = (k2 * dec_k).T
            upd = _dot(kdec_t[RET_DK * half:RET_DK * (half + 1)].astype(BF16), vb)
            s_new = st[RET_DK * half:RET_DK * (half + 1)] * math.exp(c_true * lg) + upd
            s_scr[b, tile, pl.ds(RET_DK * half, RET_DK), :] = s_new
            on = _tilenorm(o, gro_ref[...], RET_DV)
            o_ref[b, :, LANES * hd:LANES * (hd + 1)] = (rg_refs[b][:, LANES * hd:LANES * (hd + 1)] * on).astype(BF16)

    @pl.when(c == pl.num_programs(1) - 1)
    def _():
        for b in range(bg):
            for hd in range(RET_HEADS):
                sout_ref[b, hd] = s_scr[b, hd // 2, pl.ds(RET_DK * (hd % 2), RET_DK), :]


def _retention(rq, rk, rv, rg, s0, gro, nb, n_chunks, c_true):
    cs = RET_CHUNK
    bg = math.gcd(nb, RET_GROUP)
    rows = lambda w: [pl.BlockSpec((cs, w), lambda g, c, b=b: ((g * bg + b) * n_chunks + c, 0)) for b in range(bg)]
    state = pl.BlockSpec((bg, RET_HEADS, RET_DK, RET_DV), lambda g, c: (g, 0, 0, 0))
    return pl.pallas_call(
        functools.partial(_ret_kernel, bg, float(c_true)),
        grid=(nb // bg, n_chunks),
        in_specs=rows(256) + rows(256) + rows(512) + rows(512) + [state, pl.BlockSpec(gro.shape, lambda g, c: (0, 0))],
        out_specs=[pl.BlockSpec((bg, cs, 512), lambda g, c: (g, c, 0)), state],
        out_shape=[jax.ShapeDtypeStruct((nb, n_chunks * cs, 512), BF16),
                   jax.ShapeDtypeStruct((nb, RET_HEADS, RET_DK, RET_DV), F32)],
        scratch_shapes=[pltpu.VMEM((bg, RET_HEADS // 2, LANES, RET_DV), F32)],
        compiler_params=pltpu.CompilerParams(dimension_semantics=("arbitrary", "arbitrary")),
        name="retention",
    )(*([rq] * bg + [rk] * bg + [rv] * bg + [rg] * bg), s0, gro)


def _memkv_kernel(m_ref, g_ref, w_ref, gk_ref, o_ref):
    x = m_ref[...]
    xn = (x * lax.rsqrt(jnp.mean(x * x, axis=-1, keepdims=True) + EPS) * g_ref[...]).astype(BF16)
    hk = _dot(xn, w_ref[...])
    half = MEM_HEADS * MEM_HD
    for hd in range(MEM_HEADS):
        o_ref[:, LANES * hd:LANES * (hd + 1)] = _tilenorm(hk[:, LANES * hd:LANES * (hd + 1)], gk_ref[...], MEM_HD)
    o_ref[:, half:2 * half] = hk[:, half:2 * half]


def _mem_kv(mem2d, g_mem, w_mem, gk_mem, tm):
    n = mem2d.shape[0]
    const = lambda a: pl.BlockSpec(a.shape, lambda i: (0,) * a.ndim)
    return pl.pallas_call(
        _memkv_kernel,
        grid=(n // tm,),
        in_specs=[pl.BlockSpec((tm, D_MODEL), lambda i: (i, 0)), const(g_mem), const(w_mem), const(gk_mem)],
        out_specs=pl.BlockSpec((tm, 1024), lambda i: (i, 0)),
        out_shape=jax.ShapeDtypeStruct((n, 1024), F32),
        compiler_params=pltpu.CompilerParams(dimension_semantics=("arbitrary",)),
        name="mem_kv",
    )(mem2d, g_mem, w_mem, gk_mem)


def _memattn_kernel(q_ref, kv_ref, o_ref):
    half = MEM_HEADS * MEM_HD
    for hd in range(MEM_HEADS):
        cols = slice(LANES * hd, LANES * (hd + 1))
        s = _dot_nt(q_ref[:, cols].astype(BF16), kv_ref[:, cols].astype(BF16))
        p = jnp.exp(s - jnp.max(s, axis=-1, keepdims=True))
        p = p / jnp.sum(p, axis=-1, keepdims=True)
        o = _dot(p.astype(BF16), kv_ref[:, half + LANES * hd:half + LANES * (hd + 1)].astype(BF16))
        o_ref[:, cols] = o.astype(o_ref.dtype)


def _memattn_cache_kernel(q_ref, kv_ref, o_ref):
    stride = 2 * MEM_HEADS
    mem_len = kv_ref.shape[1] // stride
    for hd in range(MEM_HEADS):
        cols = slice(LANES * hd, LANES * (hd + 1))
        k = kv_ref[0, pl.ds(hd, mem_len, stride=stride), :].astype(BF16)
        v = kv_ref[0, pl.ds(MEM_HEADS + hd, mem_len, stride=stride), :].astype(BF16)
        s = _dot_nt(q_ref[:, cols].astype(BF16), k)
        p = jnp.exp(s - jnp.max(s, axis=-1, keepdims=True))
        p = p / jnp.sum(p, axis=-1, keepdims=True)
        o_ref[:, cols] = _dot(p.astype(BF16), v).astype(o_ref.dtype)


def _mem_attend_cache(q2d, cache_rows, dseq):
    nb, n_rows, _ = cache_rows.shape
    return pl.pallas_call(
        _memattn_cache_kernel,
        grid=(nb,),
        in_specs=[pl.BlockSpec((dseq, 512), lambda b: (b, 0)), pl.BlockSpec((1, n_rows, LANES), lambda b: (b, 0, 0))],
        out_specs=pl.BlockSpec((dseq, 512), lambda b: (b, 0)),
        out_shape=jax.ShapeDtypeStruct((nb * dseq, 512), F32),
        compiler_params=pltpu.CompilerParams(dimension_semantics=("arbitrary",)),
        name="mem_attend_cache",
    )(q2d, cache_rows)


def _mem_attend(q2d, kv2d, nb, blocks_per_b, tq, mem_len, out_rows, out_dtype):
    return pl.pallas_call(
        _memattn_kernel,
        grid=(nb, blocks_per_b),
        in_specs=[pl.BlockSpec((tq, 512), lambda b, i: (b * blocks_per_b + i, 0)),
                  pl.BlockSpec((mem_len, 1024), lambda b, i: (b, 0))],
        out_specs=pl.BlockSpec((tq, 512), lambda b, i: (b * blocks_per_b + i, 0)),
        out_shape=jax.ShapeDtypeStruct((out_rows, 512), out_dtype),
        compiler_params=pltpu.CompilerParams(dimension_semantics=("arbitrary", "arbitrary")),
        name="mem_attend",
    )(q2d, kv2d)


def _merge_kernel(nbp, xp_ref, xs_ref, onsap_ref, onsas_ref, oretp_ref, orets_ref, omemp_ref, omems_ref, g_ref,
                  wbg_ref, wn_ref, wr_ref, wm_ref, wo_ref, gffn_ref, wrt_ref, brt_ref, wgu_ref, wd_ref,
                  h_ref, hn_ref, comb_ref, wgu_bf_ref, wd_bf_ref):
    i = pl.program_id(0)
    wgu_bf_ref[...] = wgu_ref[...].astype(BF16)
    wd_bf_ref[...] = wd_ref[...].astype(BF16)
    pick = lambda p_ref, s_ref: jnp.where(i < nbp, p_ref[...], s_ref[...])
    x = pick(xp_ref, xs_ref)
    xn = (x * lax.rsqrt(jnp.mean(x * x, axis=-1, keepdims=True) + EPS) * g_ref[...]).astype(BF16)
    bg = _sigmoid(_dot(xn, wbg_ref[...]))
    mixed = (bg[:, 0:1024] * _dot(pick(onsap_ref, onsas_ref), wn_ref[...])
             + bg[:, 1024:2048] * _dot(pick(oretp_ref, orets_ref), wr_ref[...])
             + bg[:, 2048:3072] * _dot(pick(omemp_ref, omems_ref), wm_ref[...]))
    hres = x + _dot(mixed.astype(BF16), wo_ref[...])
    h_ref[...] = hres
    hn = hres * lax.rsqrt(jnp.mean(hres * hres, axis=-1, keepdims=True) + EPS) * gffn_ref[...]
    hn_ref[...] = hn.astype(BF16)

    a, b, _ = _split3(hn)
    logits = _dot(a, wrt_ref[0]) + _dot(a, wrt_ref[1]) + _dot(b, wrt_ref[0]) + brt_ref[...]
    lane = lax.broadcasted_iota(jnp.int32, logits.shape, 1).astype(F32)
    vals, hots = [], []
    for _k in range(TOP_K):
        mx = jnp.max(logits, axis=-1, keepdims=True)
        idx = jnp.min(jnp.where(logits == mx, lane, float(LANES)), axis=-1, keepdims=True)
        hot = lane == idx
        vals.append(mx)
        hots.append(hot)
        logits = jnp.where(hot, -3.0e38, logits)
    es = [jnp.exp(v - vals[0]) for v in vals]
    den = es[0] + es[1] + es[2] + es[3]
    comb = jnp.zeros(logits.shape, F32)
    for e_k, hot in zip(es, hots):
        comb = comb + jnp.where(hot, e_k / den, 0.0)
    comb_ref[...] = comb


def _merge(xp, xs, onsa, onsa_s, oret, oret_s, omem, omem_s, g_attn, wbg, wn, wr, wm, wo, gffn, wrt, brt,
           w_gate_up, w_down, tm):
    n_p, n_s = xp.shape[0], xs.shape[0]
    nbp, nbs = n_p // tm, n_s // tm
    n_all = n_p + n_s
    n_halves = 2 * N_EXPERTS
    assert nbp + nbs >= n_halves
    row = lambda w: pl.BlockSpec((tm, w), lambda i: (i, 0))
    rowp = lambda w: pl.BlockSpec((tm, w), lambda i: (jnp.minimum(i, nbp - 1), 0))
    rows = lambda w: pl.BlockSpec((tm, w), lambda i: (jnp.maximum(i - nbp, 0), 0))
    const = lambda a: pl.BlockSpec(a.shape, lambda i: (0,) * a.ndim, pipeline_mode=pl.Buffered(1))
    half = lambda i: jnp.minimum(i, n_halves - 1)
    wgu_spec = pl.BlockSpec((1, D_MODEL, D_FF), lambda i: (half(i) // 2, 0, half(i) % 2))
    wd_spec = pl.BlockSpec((1, D_FF // 2, D_MODEL), lambda i: (half(i) // 2, half(i) % 2, 0))
    return pl.pallas_call(
        functools.partial(_merge_kernel, nbp),
        grid=(nbp + nbs,),
        in_specs=[rowp(1024), rows(1024), rowp(1024), rows(1024), rowp(512), rows(512), rowp(512), rows(512),
                  const(g_attn), const(wbg), const(wn), const(wr), const(wm),
                  const(wo), const(gffn), const(wrt), const(brt), wgu_spec, wd_spec],
        out_specs=[row(1024), row(1024), row(LANES), wgu_spec, wd_spec],
        out_shape=[jax.ShapeDtypeStruct((n_all, 1024), F32), jax.ShapeDtypeStruct((n_all, 1024), BF16),
                   jax.ShapeDtypeStruct((n_all, LANES), F32),
                   jax.ShapeDtypeStruct(w_gate_up.shape, BF16), jax.ShapeDtypeStruct(w_down.shape, BF16)],
        compiler_params=pltpu.CompilerParams(dimension_semantics=("arbitrary",), vmem_limit_bytes=VMEM_LIMIT),
        name="merge",
    )(xp, xs, onsa, onsa_s, oret, oret_s, omem, omem_s, g_attn, wbg, wn, wr, wm, wo, gffn, wrt, brt,
      w_gate_up, w_down)


MOE_BLOCK_CAP = 1664
ROUTE_TILE = 128
SLAB_MARGIN = 1.3


def _slabs(tb):
    tok_a = ROUTE_TILE * (-(-(tb // ROUTE_TILE) // 2))
    tok_b = tb - tok_a
    cap = lambda tok: 16 * max(1, -(-int(tok * TOP_K / N_EXPERTS * SLAB_MARGIN) // 16))
    return tok_a, tok_b, cap(tok_a), (cap(tok_b) if tok_b else 0)


def _route_kernel(comb_ref, pos_ref, post_ref, cnt_ref):
    tb = comb_ref.shape[0]
    rt = ROUTE_TILE
    tiles_a = _slabs(tb)[0] // rt
    r = lax.broadcasted_iota(jnp.int32, (rt, rt), 0)
    c = lax.broadcasted_iota(jnp.int32, (rt, rt), 1)
    ltri = (c < r).astype(BF16)
    eye = (c == r).astype(BF16)
    carry = jnp.zeros((1, LANES), F32)
    counts = []
    for s in range(tb // rt):
        if s == tiles_a:
            counts.append(carry)
            carry = jnp.zeros((1, LANES), F32)
        comb = comb_ref[pl.ds(rt * s, rt), :]
        sel = jnp.where(comb > 0.0, 1.0, 0.0)
        pos = jnp.where(sel > 0.0, _dot(ltri, sel.astype(BF16)) + carry, -1.0)
        carry = carry + jnp.sum(sel, axis=0, keepdims=True)
        pos_ref[pl.ds(rt * s, rt), :] = jnp.where(sel > 0.0, pos + 0.5 * comb, -1.0)
        hi = jnp.floor((pos + 1.0) * (1.0 / 256.0))
        lo = (pos + 1.0) - 256.0 * hi
        post_ref[:, rt * s:rt * (s + 1)] = 256.0 * _dot_nt(eye, hi.astype(BF16)) + _dot_nt(eye, lo.astype(BF16)) - 1.0
    counts.append(carry)
    counts += [jnp.zeros((1, LANES), F32)] * (8 - len(counts))
    cnt_ref[0] = jnp.concatenate(counts, axis=0)


def _route(comb, tb):
    n_all = comb.shape[0]
    nblk = n_all // tb
    return pl.pallas_call(
        _route_kernel,
        grid=(nblk,),
        in_specs=[pl.BlockSpec((tb, LANES), lambda i: (i, 0))],
        out_specs=[pl.BlockSpec((tb, LANES), lambda i: (i, 0)), pl.BlockSpec((LANES, tb), lambda i: (0, i)),
                   pl.BlockSpec((1, 8, LANES), lambda i: (i, 0, 0))],
        out_shape=[jax.ShapeDtypeStruct((n_all, LANES), F32), jax.ShapeDtypeStruct((LANES, n_all), F32),
                   jax.ShapeDtypeStruct((nblk, 8, LANES), F32)],
        compiler_params=pltpu.CompilerParams(dimension_semantics=("arbitrary",)),
        name="route",
    )(comb)


def _moe_kernel(n_s, cnt_ref, hn_ref, h_ref, pos_ref, post_ref, wgu_ref, wd_ref, bgu_ref, bd_ref, y_ref, ys_ref):
    i = pl.program_id(0)
    e = pl.program_id(1)
    tb = hn_ref.shape[0]
    tok_a, tok_b, cap_a, cap_b = _slabs(tb)
    halves = [(0, tok_a, 0, cap_a)] + ([(tok_a, tok_b, cap_a, cap_b)] if tok_b else [])

    @pl.when(e == 0)
    def _():
        y_ref[...] = h_ref[...]

    lane = lax.broadcasted_iota(jnp.int32, (tb, LANES), 1)
    packed = jnp.sum(jnp.where(lane == e, pos_ref[...], 0.0), axis=-1, keepdims=True)
    pcol = jnp.floor(packed)
    ccol = 2.0 * (packed - pcol)
    prow = post_ref[pl.ds(e, 1), :]

    def chunk(c, carry):
        cf = c.astype(F32)
        parts = []
        for t0, tok, _, cap in halves:
            slot_g = cf * cap + lax.broadcasted_iota(jnp.int32, (cap, tok), 0).astype(F32)
            gather = jnp.where(prow[:, t0:t0 + tok] == slot_g, 1.0, 0.0).astype(BF16)
            parts.append(_dot(gather, hn_ref[pl.ds(t0, tok), :]).astype(BF16))
        xg = jnp.concatenate(parts, axis=0)
        gu = _dot(xg, wgu_ref[0]) + bgu_ref[0]
        gate = jnp.minimum(gu[:, 0:D_FF], SWIGLU_LIMIT)
        up = jnp.clip(gu[:, D_FF:2 * D_FF], -SWIGLU_LIMIT, SWIGLU_LIMIT)
        act = (up + 1.0) * gate * _sigmoid(SWIGLU_ALPHA * gate)
        yc = (_dot(act.astype(BF16), wd_ref[0]) + bd_ref[0]).astype(BF16)
        for t0, tok, r0, cap in halves:
            slot_s = cf * cap + lax.broadcasted_iota(jnp.int32, (tok, cap), 1).astype(F32)
            scatter = jnp.where(pcol[t0:t0 + tok] == slot_s, ccol[t0:t0 + tok], 0.0).astype(BF16)
            y_ref[pl.ds(t0, tok), :] += _dot(scatter, yc[r0:r0 + cap])
        return carry

    lax.fori_loop(0, cnt_ref[i * N_EXPERTS + e], chunk, 0)

    @pl.when((i == pl.num_programs(0) - 1) & (e == N_EXPERTS - 1))
    def _():
        ys_ref[...] = y_ref[pl.ds(tb - n_s, n_s), :]


def _moe(cnt, hn, h, pos, post, wgu, wd, b_gate_up, b_down, tb, n_s):
    n_all = hn.shape[0]
    n_p = n_all - n_s
    assert n_s <= tb and n_s % 8 == 0 and (n_all // tb - 1) * tb < n_p
    row = lambda w: pl.BlockSpec((tb, w), lambda i, e, cnt: (i, 0))
    return pl.pallas_call(
        functools.partial(_moe_kernel, n_s),
        grid_spec=pltpu.PrefetchScalarGridSpec(
            num_scalar_prefetch=1, grid=(n_all // tb, N_EXPERTS),
            in_specs=[row(1024), row(1024), row(LANES),
                      pl.BlockSpec((LANES, tb), lambda i, e, cnt: (0, i)),
                      pl.BlockSpec((1, D_MODEL, 2 * D_FF), lambda i, e, cnt: (e, 0, 0)),
                      pl.BlockSpec((1, D_FF, D_MODEL), lambda i, e, cnt: (e, 0, 0)),
                      pl.BlockSpec((1, 1, 2 * D_FF), lambda i, e, cnt: (e, 0, 0)),
                      pl.BlockSpec((1, 1, D_MODEL), lambda i, e, cnt: (e, 0, 0))],
            out_specs=[row(1024), pl.BlockSpec((n_s, 1024), lambda i, e, cnt: (0, 0))]),
        out_shape=[jax.ShapeDtypeStruct((n_p, 1024), F32), jax.ShapeDtypeStruct((n_s, 1024), F32)],
        compiler_params=pltpu.CompilerParams(dimension_semantics=("arbitrary",) * 2, vmem_limit_bytes=VMEM_LIMIT),
        name="moe",
    )(cnt, hn, h, pos, post, wgu, wd, b_gate_up, b_down)


def _pick_tile(n, pref):
    t = pref
    while n % t:
        t //= 2
    return t


def kernel(x_prompt, x_sample, cache_cmp, cache_slc, cache_win, state_ret, cache_mem, page_table, mem_prompt,
           g_attn, w_in, g_q_nsa, g_k_cmp, g_k_slc, g_k_win, pe_ck, w_ck1, w_ck2, pe_cv, w_cv1, w_cv2,
           g_ret_out, g_mem, w_mem_kv, g_q_mem, g_k_mem, w_br_nsa, w_br_ret, w_br_mem, w_out,
           g_ffn, w_router, b_router, w_gate_up, b_gate_up, w_down, b_down):
    nb, seq, _ = x_prompt.shape
    db, dseq, _ = x_sample.shape
    n_pages = page_table.shape[1]
    past = n_pages * PAGE_SIZE
    wb = cache_win.shape[1]
    mem_len = mem_prompt.shape[1]
    n_p, n_s = nb * seq, db * dseq
    n_all = n_p + n_s
    tm = _pick_tile(seq, ROW_TILE)

    o = 0
    cols = {}
    for name, wdt in (("q", 512), ("kv", 768), ("ng", 24), ("rq", 256), ("rk", 256), ("rv", 512), ("rg", 512),
                      ("mq", 512), ("bg", 3072)):
        cols[name] = w_in[:, o:o + wdt]
        o += wdt
    wq = cols["q"].reshape(D_MODEL, NSA_HEADS, NSA_HD)
    zq = jnp.zeros_like(wq)
    wq_pad = jnp.concatenate(
        [jnp.concatenate([wq[:, hh], zq[:, hh]] if hh < NSA_GROUP else [zq[:, hh], wq[:, hh]], axis=-1)
         for hh in range(NSA_HEADS)], axis=-1)
    w_ng = jnp.pad(cols["ng"], ((0, 0), (0, C_NG - 24)))
    w1 = jnp.concatenate([wq_pad, cols["kv"], w_ng, cols["rq"], cols["rk"], cols["rv"], cols["rg"], cols["mq"]],
                         axis=-1).astype(BF16)
    w_bg = cols["bg"].astype(BF16)
    two = lambda g: jnp.concatenate([g, g]).reshape(1, LANES)
    r1 = lambda g: g.reshape(1, -1)

    half = RET_DK // 2
    inv = ROPE_BASE ** (-jnp.arange(half, dtype=F32) / half)
    pos = jnp.concatenate([jnp.arange(seq, dtype=jnp.int32),
                           jnp.tile(past + jnp.arange(dseq, dtype=jnp.int32), db)]).astype(F32)
    ang = pos[:, None] * inv[None, :]
    cos_t = jnp.tile(jnp.cos(ang), (1, 2 * RET_HEADS))
    sin_t = jnp.tile(jnp.concatenate([-jnp.sin(ang), jnp.sin(ang)], axis=-1), (1, RET_HEADS))

    xp = x_prompt.reshape(n_p, D_MODEL)
    xs = x_sample.reshape(n_s, D_MODEL)
    (q, kvc, kvs, kvw, ng, rq, rk, rv, rg, mq, kvct, kvst, kvwt) = _project(
        xp, xs, r1(g_attn), w1, cos_t, sin_t, two(g_q_nsa), two(g_k_slc), two(g_k_win), r1(g_q_mem), seq, tm)

    w1s = jnp.stack([w_ck1, w_ck1, w_cv1, w_cv1]).reshape(4, 2, CMP_STRIDE, NSA_HD, CMP_HIDDEN)
    wbig = jnp.einsum("shpdc,st->psdhtc", w1s, jnp.eye(4, dtype=F32)).reshape(CHUNK_W, 2 * 4 * CMP_HIDDEN).astype(BF16)
    pes = jnp.stack([pe_ck, pe_ck, pe_cv, pe_cv]).reshape(4, 2, CMP_STRIDE, NSA_HD)
    pe2 = jnp.pad(jnp.transpose(pes, (1, 2, 0, 3)).reshape(2, CHUNK_W), ((0, 6), (0, 0)))
    zc = jnp.zeros_like(w_ck2)
    bd2 = lambda w: jnp.concatenate([jnp.concatenate([w, zc], 1), jnp.concatenate([zc, w], 1)], 0).astype(BF16)
    gk2 = two(g_k_cmp)

    pages_p = seq // PAGE_SIZE
    pool_p = (kvc if n_all % PAGE_SIZE == 0 else kvc[:n_p]).reshape(-1, CHUNKS_PER_PAGE, CHUNK_W)
    pt_p = jnp.arange(nb * pages_p, dtype=jnp.int32).reshape(nb, pages_p)
    ck_p, cv_p = _compress(pt_p, pool_p, wbig, pe2, bd2(w_ck2), bd2(w_cv2), gk2)
    pages_t = lambda c: jnp.transpose(c, (0, 2, 3, 4, 1)).reshape(-1, 256, PAGE_SIZE)
    ck_s, cv_s = _compress_t(page_table, pages_t(cache_cmp), wbig, pe2, bd2(w_ck2), bd2(w_cv2), gk2)

    n_cmp_p = seq // CMP_STRIDE - CMP_BLOCK // CMP_STRIDE + 1
    onsa = _nsa_prompt(q, kvs, kvw, ck_p, cv_p, ng, nb, seq, n_cmp_p)

    ns_s = (past + dseq + SLC_BLOCK - 1) // SLC_BLOCK
    nsp = (ns_s + 7) // 8 * 8
    kpad = past + PAGE_SIZE
    e_s = (jnp.arange(kpad)[None, :] // SLC_BLOCK == jnp.arange(nsp)[:, None]).astype(BF16)
    s3 = lambda a: a[n_p:].reshape(db, dseq, a.shape[1])
    onsa_s, new_win_s = _nsa_sample(page_table, s3(q), s3(kvs), s3(kvw), ck_s, cv_s, s3(ng),
                                    cache_win.reshape(db, wb, 256), e_s,
                                    pages_t(cache_slc), past)
    onsa_s = onsa_s.reshape(n_s, 1024).astype(BF16)

    gro = r1(g_ret_out)
    oret, ret_state_p = _retention(rq, rk, rv, rg, jnp.zeros((nb, RET_HEADS, RET_DK, RET_DV), F32), gro,
                                   nb, seq // RET_CHUNK, RET_CHUNK)
    padc = lambda a: jnp.pad(s3(a), ((0, 0), (0, RET_CHUNK - dseq), (0, 0))).reshape(db * RET_CHUNK, a.shape[1])
    oret_s, ret_state_s = _retention(padc(rq), padc(rk), padc(rv), padc(rg), state_ret, gro, db, 1, dseq)
    oret = oret.reshape(n_p, 512)
    oret_s = oret_s[:, :dseq].reshape(n_s, 512)

    mem_kv_p = _mem_kv(mem_prompt.reshape(nb * mem_len, D_MODEL), r1(g_mem), w_mem_kv.astype(BF16), r1(g_k_mem),
                       _pick_tile(nb * mem_len, 256))
    tq = _pick_tile(seq, 512)
    omem = _mem_attend(mq, mem_kv_p, nb, seq // tq, tq, mem_len, n_p, BF16)
    omem_s = _mem_attend_cache(mq[n_p:].astype(F32), cache_mem.reshape(db, mem_len * 2 * MEM_HEADS, MEM_HD),
                               dseq).astype(BF16)

    wn = w_br_nsa.reshape(NSA_HEADS, NSA_HD, D_MODEL)
    zn = jnp.zeros_like(wn)
    wn_pad = jnp.concatenate(
        [jnp.concatenate([wn[hh], zn[hh]] if hh < NSA_GROUP else [zn[hh], wn[hh]], axis=0) for hh in range(NSA_HEADS)],
        axis=0).astype(BF16)
    wr_pad = jnp.pad(w_router, ((0, 0), (0, LANES - N_EXPERTS)))
    wr_hi = wr_pad.astype(BF16)
    wr_lo = (wr_pad - wr_hi.astype(F32)).astype(BF16)
    brt = jnp.concatenate([b_router, jnp.full((LANES - N_EXPERTS,), NEG, F32)]).reshape(1, LANES)
    tmm = _pick_tile(math.gcd(seq, n_s), 256)
    h, hn, comb, wgu_bf, wd_bf = _merge(
        xp, xs, onsa, onsa_s, oret, oret_s, omem, omem_s, r1(g_attn), w_bg, wn_pad, w_br_ret.astype(BF16),
        w_br_mem.astype(BF16), w_out.astype(BF16), r1(g_ffn), jnp.stack([wr_hi, wr_lo]), brt, w_gate_up, w_down, tmm)

    tb = max(c for c in range(LANES, MOE_BLOCK_CAP + 1, LANES) if n_all % c == 0)
    pos, post, cnt = _route(comb, tb)
    _, _, cap_a, cap_b = _slabs(tb)
    passes = jnp.ceil(cnt[:, 0, :N_EXPERTS] / cap_a)
    if cap_b:
        passes = jnp.maximum(passes, jnp.ceil(cnt[:, 1, :N_EXPERTS] / cap_b))
    cnt = passes.astype(jnp.int32).reshape(-1)
    y_p, y_s = _moe(cnt, hn, h, pos, post, wgu_bf, wd_bf,
                    b_gate_up.reshape(N_EXPERTS, 1, 2 * D_FF), b_down.reshape(N_EXPERTS, 1, D_MODEL), tb, n_s)

    kv5 = lambda a, bsz, t: a.reshape(bsz, t, 2, NSA_KV_HEADS, NSA_HD)
    wp = min(WINDOW, seq)
    kv5t = lambda a: jnp.transpose(a.reshape(nb, 2, NSA_KV_HEADS, NSA_HD, a.shape[2]), (0, 4, 1, 2, 3))
    return (y_p.reshape(nb, seq, D_MODEL), y_s.reshape(db, dseq, D_MODEL),
            kv5t(kvct), kv5t(kvst), kv5t(kvwt[:, :, seq - wp:]),
            ret_state_p, mem_kv_p.reshape(nb, mem_len, 2, MEM_HEADS, MEM_HD),
            kv5(kvc[n_p:], db, dseq), kv5(kvs[n_p:], db, dseq), kv5(new_win_s, db, wb), ret_state_s)
```

```python
import functools
import math

import jax
import jax.numpy as jnp
from jax import lax
from jax.experimental import pallas as pl
from jax.experimental.pallas import tpu as pltpu

F32 = jnp.float32
BF16 = jnp.bfloat16

D_MODEL = 1024
NSA_HEADS = 8
NSA_KV_HEADS = 2
NSA_GROUP = 4
NSA_HD = 64
CMP_BLOCK = 32
CMP_STRIDE = 16
CMP_HIDDEN = 128
SLC_BLOCK = 64
N_SELECT = 16
N_LOCAL_FORCED = 2
FORCE_BONUS = 1.0e4
WINDOW = 512
Q_BLOCK = 128
PAGE_SIZE = 128
RET_HEADS = 4
RET_DK = 64
RET_DV = 128
RET_CHUNK = 128
ROPE_BASE = 10000.0
MEM_HEADS = 4
MEM_HD = 128
N_EXPERTS = 32
TOP_K = 4
D_FF = 1024
SWIGLU_ALPHA = 1.702
SWIGLU_LIMIT = 7.0
EPS = 1e-6
NEG = -1e30

LANES = 128
ROW_TILE = 512
VMEM_LIMIT = 56 * 1024 * 1024


def _dot(a, b):
    return jnp.dot(a, b, preferred_element_type=F32)


def _dot_nt(a, b):
    return lax.dot_general(a, b, (((1,), (1,)), ((), ())), preferred_element_type=F32)


def _sigmoid(x):
    return 1.0 / (1.0 + jnp.exp(-x))


def _split3(x):
    a = x.astype(BF16)
    r = x - a.astype(F32)
    b = r.astype(BF16)
    c = (r - b.astype(F32)).astype(BF16)
    return a, b, c


def _msoftmax(s, m):
    s = jnp.where(m, s, NEG)
    mx = jnp.max(s, axis=-1, keepdims=True)
    p = jnp.where(m, jnp.exp(s - mx), 0.0)
    return p / jnp.maximum(jnp.sum(p, axis=-1, keepdims=True), 1e-30)


def _halfnorm(t, g2):
    lane = lax.broadcasted_iota(jnp.int32, t.shape, 1)
    lo = lane < NSA_HD
    t2 = t * t
    s0 = jnp.sum(jnp.where(lo, t2, 0.0), axis=-1, keepdims=True)
    s1 = jnp.sum(jnp.where(lo, 0.0, t2), axis=-1, keepdims=True)
    r = jnp.where(lo, lax.rsqrt(s0 * (1.0 / NSA_HD) + EPS), lax.rsqrt(s1 * (1.0 / NSA_HD) + EPS))
    return t * r * g2


def _tilenorm(t, g, width):
    return t * lax.rsqrt(jnp.sum(t * t, axis=-1, keepdims=True) * (1.0 / width) + EPS) * g


def _rot(x, cos, sin):
    lane = lax.broadcasted_iota(jnp.int32, x.shape, 1)
    first = (lane % RET_DK) < (RET_DK // 2)
    n = x.shape[1]
    sw = jnp.where(first, pltpu.roll(x, n - RET_DK // 2, 1), pltpu.roll(x, RET_DK // 2, 1))
    return x * cos + sw * sin


C_Q, C_KV, C_NG, C_RET, C_MQ = 1024, 768, 128, 1536, 512
O_KV = C_Q
O_NG = O_KV + C_KV
O_RET = O_NG + C_NG
O_MQ = O_RET + C_RET
W1_COLS = O_MQ + C_MQ


def _proj_kernel(nbp, xp_ref, xs_ref, g_ref, w_ref, cos_ref, sin_ref, gq_ref, gks_ref, gkw_ref, gqm_ref,
                 q_ref, kvc_ref, kvs_ref, kvw_ref, ng_ref, rq_ref, rk_ref, rv_ref, rg_ref, mq_ref,
                 kvct_ref, kvst_ref, kvwt_ref):
    i = pl.program_id(0)
    x = jnp.where(i < nbp, xp_ref[...], xs_ref[...])
    xn = (x * lax.rsqrt(jnp.mean(x * x, axis=-1, keepdims=True) + EPS) * g_ref[...]).astype(BF16)

    hq = _dot(xn, w_ref[:, 0:C_Q])
    for hh in range(NSA_HEADS):
        t = hq[:, LANES * hh:LANES * (hh + 1)]
        q_ref[:, LANES * hh:LANES * (hh + 1)] = (_tilenorm(t, gq_ref[...], NSA_HD) * NSA_HD ** -0.5).astype(BF16)

    hkv = _dot(xn, w_ref[:, O_KV:O_KV + C_KV])
    kvc = hkv[:, 0:256]
    kvs = jnp.concatenate([_halfnorm(hkv[:, 256:384], gks_ref[...]), hkv[:, 384:512]], axis=1)
    kvw = jnp.concatenate([_halfnorm(hkv[:, 512:640], gkw_ref[...]), hkv[:, 640:768]], axis=1)
    kvc_ref[...] = kvc
    kvs_ref[...] = kvs
    kvw_ref[...] = kvw

    @pl.when(i < nbp)
    def _():
        kvct_ref[0] = kvc.T
        kvst_ref[0] = kvs.T
        kvwt_ref[0] = kvw.T

    ng_ref[...] = _sigmoid(_dot(xn, w_ref[:, O_NG:O_NG + C_NG]))

    hr = _dot(xn, w_ref[:, O_RET:O_RET + C_RET])
    cos = cos_ref[...]
    sin = sin_ref[...]
    rq_ref[...] = _rot(hr[:, 0:256], cos, sin)
    rk_ref[...] = _rot(hr[:, 256:512], cos, sin) * RET_DK ** -0.5
    rv_ref[...] = hr[:, 512:1024]
    rg = hr[:, 1024:1536]
    rg_ref[...] = rg * _sigmoid(rg)

    hm = _dot(xn, w_ref[:, O_MQ:O_MQ + C_MQ])
    for hd in range(MEM_HEADS):
        t = hm[:, LANES * hd:LANES * (hd + 1)]
        mq_ref[:, LANES * hd:LANES * (hd + 1)] = (_tilenorm(t, gqm_ref[...], MEM_HD) * MEM_HD ** -0.5).astype(BF16)


def _project(xp, xs, g_attn, w1, cos_t, sin_t, gq2, gks2, gkw2, gqm, seq, tm):
    n_p, n_s = xp.shape[0], xs.shape[0]
    nbp, nbs = n_p // tm, -(-n_s // tm)
    n_all = n_p + n_s
    bps = seq // tm
    row = lambda w: pl.BlockSpec((tm, w), lambda i: (i, 0))
    const = lambda a: pl.BlockSpec(a.shape, lambda i: (0,) * a.ndim)
    rope = pl.BlockSpec((tm, 256), lambda i: (jnp.where(i < nbp, i % bps, bps + i - nbp), 0))
    widths = (1024, 256, 256, 256, 128, 256, 256, 512, 512, 512)
    dtypes = (BF16, F32, F32, F32, F32, F32, F32, F32, F32, BF16)
    kvt = pl.BlockSpec((1, 256, tm), lambda i: (jnp.minimum(i, nbp - 1) // bps, 0, jnp.minimum(i, nbp - 1) % bps))
    return pl.pallas_call(
        functools.partial(_proj_kernel, nbp),
        grid=(nbp + nbs,),
        in_specs=[pl.BlockSpec((tm, D_MODEL), lambda i: (jnp.minimum(i, nbp - 1), 0)),
                  pl.BlockSpec((tm, D_MODEL), lambda i: (jnp.maximum(i - nbp, 0), 0)),
                  const(g_attn), const(w1), rope, rope, const(gq2), const(gks2), const(gkw2), const(gqm)],
        out_specs=[row(w) for w in widths] + [kvt] * 3,
        out_shape=[jax.ShapeDtypeStruct((n_all, w), d) for w, d in zip(widths, dtypes)]
        + [jax.ShapeDtypeStruct((n_p // seq, 256, seq), F32)] * 3,
        compiler_params=pltpu.CompilerParams(dimension_semantics=("arbitrary",), vmem_limit_bytes=VMEM_LIMIT),
        name="proj",
    )(xp, xs, g_attn, w1, cos_t, sin_t, gq2, gks2, gkw2, gqm)


CHUNK_W = CMP_STRIDE * 256
CHUNKS_PER_PAGE = PAGE_SIZE // CMP_STRIDE


def _compress_kernel(n_pages, pt_ref, pool_ref, wbig_ref, pe_ref, w2k_ref, w2v_ref, gk_ref,
                     ck_ref, cv_ref, x_scr, r_scr, sem):
    b = pl.program_id(0)
    n = n_pages * CHUNKS_PER_PAGE

    def page_copy(j):
        return pltpu.make_async_copy(pool_ref.at[pt_ref[b, j]],
                                     x_scr.at[pl.ds(CHUNKS_PER_PAGE * j, CHUNKS_PER_PAGE)], sem)

    for j in range(n_pages):
        page_copy(j).start()
    x_scr[pl.ds(n, 8), :] = pe_ref[...]
    for j in range(n_pages):
        page_copy(j).wait()

    r_scr[...] = _dot(x_scr[...].astype(BF16), wbig_ref[...])
    cvec = r_scr[n:n + 1, 0:512] + r_scr[n + 1:n + 2, 512:1024]
    hid = r_scr[0:n, 0:512] + r_scr[pl.ds(1, n), 512:1024] + cvec
    hb = (hid * _sigmoid(hid)).astype(BF16)
    ck_ref[0] = _halfnorm(_dot(hb[:, 0:256], w2k_ref[...]), gk_ref[...])
    cv_ref[0] = _dot(hb[:, 256:512], w2v_ref[...])


def _compress(page_table, pool, wbig, pe2, w2k, w2v, gk2):
    nb, n_pages = page_table.shape
    n = n_pages * CHUNKS_PER_PAGE
    const = lambda a: pl.BlockSpec(a.shape, lambda b, pt: (0,) * a.ndim)
    out = pl.BlockSpec((1, n, LANES), lambda b, pt: (b, 0, 0))
    return pl.pallas_call(
        functools.partial(_compress_kernel, n_pages),
        grid_spec=pltpu.PrefetchScalarGridSpec(
            num_scalar_prefetch=1, grid=(nb,),
            in_specs=[pl.BlockSpec(memory_space=pl.ANY), const(wbig), const(pe2), const(w2k), const(w2v), const(gk2)],
            out_specs=[out, out],
            scratch_shapes=[pltpu.VMEM((n + 8, CHUNK_W), F32), pltpu.VMEM((n + 8, 1024), F32),
                            pltpu.SemaphoreType.DMA(())]),
        out_shape=[jax.ShapeDtypeStruct((nb, n, LANES), F32)] * 2,
        compiler_params=pltpu.CompilerParams(dimension_semantics=("arbitrary",), vmem_limit_bytes=VMEM_LIMIT),
        name="compress",
    )(page_table, pool, wbig, pe2, w2k, w2v, gk2)


def _compress_t_kernel(n_pages, pt_ref, pool_ref, wbig_ref, pe_ref, w2k_ref, w2v_ref, gk_ref,
                       ck_ref, cv_ref, pg_scr, t_scr, r_scr, cvec_scr, sem):
    b = pl.program_id(0)
    slot = b % 2
    n = n_pages * CHUNKS_PER_PAGE

    def page_copy(bb, sl, j):
        return pltpu.make_async_copy(pool_ref.at[pt_ref[bb, j]], pg_scr.at[sl, j], sem.at[sl])

    @pl.when(b == 0)
    def _():
        for j in range(n_pages):
            page_copy(0, 0, j).start()
        r = _dot(pe_ref[...].astype(BF16), wbig_ref[...])
        cvec_scr[...] = jnp.broadcast_to(r[0:1, 0:512] + r[1:2, 512:1024], cvec_scr.shape)
        r_scr[pl.ds(n, 8), :] = jnp.zeros((8, 1024), F32)

    @pl.when(b + 1 < pl.num_programs(0))
    def _():
        for j in range(n_pages):
            page_copy(b + 1, 1 - slot, j).start(priority=j % 2)

    for j in range(n_pages):
        page_copy(b, slot, j).wait()

    src = lax.broadcasted_iota(jnp.int32, (PAGE_SIZE, PAGE_SIZE), 0)
    dst = lax.broadcasted_iota(jnp.int32, (PAGE_SIZE, PAGE_SIZE), 1)
    perm = (dst == CHUNKS_PER_PAGE * (src % CMP_STRIDE) + src // CMP_STRIDE).astype(BF16)
    for j in range(n_pages):
        xp = _dot(pg_scr[slot, j].astype(BF16), perm)
        tk = xp[0:LANES, :].T
        tv = xp[LANES:2 * LANES, :].T
        for p in range(CMP_STRIDE):
            rows = pl.ds(CHUNKS_PER_PAGE * j, CHUNKS_PER_PAGE)
            t_scr[p, rows, 0:LANES] = tk[CHUNKS_PER_PAGE * p:CHUNKS_PER_PAGE * (p + 1)]
            t_scr[p, rows, LANES:2 * LANES] = tv[CHUNKS_PER_PAGE * p:CHUNKS_PER_PAGE * (p + 1)]

    acc = jnp.zeros((n, 1024), F32)
    for p in range(CMP_STRIDE):
        acc = acc + _dot(t_scr[p].astype(BF16), wbig_ref[256 * p:256 * (p + 1), :])
    r_scr[pl.ds(0, n), :] = acc
    hid = acc[:, 0:512] + r_scr[pl.ds(1, n), 512:1024] + cvec_scr[0:1, :]
    hb = (hid * _sigmoid(hid)).astype(BF16)
    ck_ref[0] = _halfnorm(_dot(hb[:, 0:256], w2k_ref[...]), gk_ref[...])
    cv_ref[0] = _dot(hb[:, 256:512], w2v_ref[...])


def _compress_t(page_table, pool_t, wbig, pe2, w2k, w2v, gk2):
    nb, n_pages = page_table.shape
    n = n_pages * CHUNKS_PER_PAGE
    const = lambda a: pl.BlockSpec(a.shape, lambda b, pt: (0,) * a.ndim)
    out = pl.BlockSpec((1, n, LANES), lambda b, pt: (b, 0, 0))
    return pl.pallas_call(
        functools.partial(_compress_t_kernel, n_pages),
        grid_spec=pltpu.PrefetchScalarGridSpec(
            num_scalar_prefetch=1, grid=(nb,),
            in_specs=[pl.BlockSpec(memory_space=pl.ANY), const(wbig), const(pe2), const(w2k), const(w2v), const(gk2)],
            out_specs=[out, out],
            scratch_shapes=[pltpu.VMEM((2, n_pages, 256, PAGE_SIZE), F32), pltpu.VMEM((CMP_STRIDE, n, 2 * LANES), F32),
                            pltpu.VMEM((n + 8, 1024), F32), pltpu.VMEM((8, 512), F32),
                            pltpu.SemaphoreType.DMA((2,))]),
        out_shape=[jax.ShapeDtypeStruct((nb, n, LANES), F32)] * 2,
        compiler_params=pltpu.CompilerParams(dimension_semantics=("arbitrary",), vmem_limit_bytes=VMEM_LIMIT),
        name="compress_t",
    )(page_table, pool_t, wbig, pe2, w2k, w2v, gk2)


def _overlap_t(ns_rows, n_cmp_cols, n_cmp):
    s = lax.broadcasted_iota(jnp.int32, (ns_rows, n_cmp_cols), 0)
    n = lax.broadcasted_iota(jnp.int32, (ns_rows, n_cmp_cols), 1)
    ov = (n * CMP_STRIDE < s * SLC_BLOCK + SLC_BLOCK) & (n * CMP_STRIDE + CMP_BLOCK > s * SLC_BLOCK) & (n < n_cmp)
    return ov.astype(BF16)


def _importance_t(ov_t, pcs):
    a, b, c = _split3(pcs)
    return _dot_nt(ov_t, a) + _dot_nt(ov_t, b) + _dot_nt(ov_t, c)


def _block_scores_t(imp_t, tl):
    blk = lax.broadcasted_iota(jnp.int32, imp_t.shape, 0)
    cur = tl // SLC_BLOCK
    valid = blk <= cur
    forced = (blk == 0) | (valid & (blk > cur - N_LOCAL_FORCED))
    return jnp.where(valid, imp_t + jnp.where(forced, FORCE_BONUS, 0.0), NEG), blk


SLC_CHUNK = 1024
RANK_ACCS = 4


def _nsa_prompt_kernel(seq, n_cmp, q_ref, kvs_ref, kvw_ref, ck_ref, cv_ref, ng_ref, et_ref, eg_ref, o_ref):
    i = pl.program_id(1)
    ns = seq // SLC_BLOCK
    nh = NSA_HEADS
    q8 = q_ref[...]
    qs = jnp.concatenate([q8[:, LANES * hh:LANES * (hh + 1)] for hh in range(nh)], axis=0)
    tq = i * Q_BLOCK + lax.broadcasted_iota(jnp.int32, (Q_BLOCK, 1), 0)
    rep = lambda a: jnp.concatenate([a] * nh, axis=0)
    t8 = rep(tq)

    ckb = ck_ref[0].astype(BF16)
    sc = _dot_nt(qs, ckb)
    jn = lax.broadcasted_iota(jnp.int32, sc.shape, 1)
    pc = _msoftmax(sc, (jn * CMP_STRIDE + CMP_BLOCK - 1) <= t8)
    o_cmp = _dot(pc.astype(BF16), cv_ref[0].astype(BF16))

    ov_t = _overlap_t(ns, pc.shape[1], n_cmp)
    tl = i * Q_BLOCK + lax.broadcasted_iota(jnp.int32, (ns, Q_BLOCK), 1)
    eye = (lax.broadcasted_iota(jnp.int32, (Q_BLOCK, Q_BLOCK), 0)
           == lax.broadcasted_iota(jnp.int32, (Q_BLOCK, Q_BLOCK), 1)).astype(BF16)
    lane_q = lax.broadcasted_iota(jnp.int32, (Q_BLOCK, LANES), 1)
    zpad = lambda n: [jnp.zeros((n, Q_BLOCK), BF16)] if n else []
    qxs = []
    for h in range(NSA_KV_HEADS):
        r0 = h * NSA_GROUP * Q_BLOCK
        pcs = pc[r0:r0 + 128] + pc[r0 + 128:r0 + 256] + pc[r0 + 256:r0 + 384] + pc[r0 + 384:r0 + 512]
        score, blk = _block_scores_t(_importance_t(ov_t, pcs), tl)
        ranks = [jnp.zeros(score.shape, F32) for _ in range(RANK_ACCS)]
        for k in range(ns):
            sk = score[k:k + 1, :]
            ranks[k % RANK_ACCS] += jnp.where((sk > score) | ((sk == score) & (blk > k)), 1.0, 0.0)
        rank = functools.reduce(lambda x, y: x + y, ranks)
        sel_t = jnp.where(rank < min(N_SELECT, ns), 1.0, 0.0).astype(BF16)
        sel_t = jnp.concatenate(zpad(NSA_HD * (1 - h)) + [sel_t] + zpad(LANES - ns - NSA_HD * (1 - h)), axis=0)
        in_blk = (lane_q >= NSA_HD) if h == 0 else (lane_q < NSA_HD)
        pen = jnp.where(in_blk, (_dot_nt(eye, sel_t) - 1.0) * -NEG, 0.0)
        qh = qs[r0:r0 + NSA_GROUP * Q_BLOCK]
        qxs.append(qh + jnp.concatenate([pen] * NSA_GROUP, axis=0).astype(BF16))

    kc = min(SLC_CHUNK, seq)
    lane_1 = lax.broadcasted_iota(jnp.int32, (1, LANES), 1)
    keep_lo = jnp.where(lane_1 < NSA_HD, 1.0, 0.0).astype(BF16)
    keep_hi = jnp.where(lane_1 < NSA_HD, 0.0, 1.0).astype(BF16)
    ones = jnp.ones((kc, LANES), BF16)

    def chunk(c, carry, bias):
        m, l, acc = carry
        k0 = pl.multiple_of(c * kc, kc)
        kk = kvs_ref[pl.ds(k0, kc), 0:128].astype(BF16)
        et = et_ref[pl.ds(k0, kc), :]
        vx = jnp.concatenate([kvs_ref[pl.ds(k0, kc), 128:256].astype(BF16), ones], axis=1)
        s = jnp.concatenate([_dot_nt(qxs[0], kk * keep_lo + et * keep_hi),
                             _dot_nt(qxs[1], kk * keep_hi + et * keep_lo)], axis=0)
        if bias is not None:
            s = s + bias
        m_new = jnp.maximum(m, jnp.max(s, axis=-1, keepdims=True))
        alpha = jnp.exp(m - m_new)
        pv = _dot(jnp.exp(s - m_new).astype(BF16), vx)
        return m_new, alpha * l + pv[:, LANES:LANES + 1], alpha * acc + pv[:, 0:LANES]

    rows = nh * Q_BLOCK
    init = (jnp.full((rows, 1), NEG, F32), jnp.zeros((rows, 1), F32), jnp.zeros((rows, LANES), F32))
    c_last = (i * Q_BLOCK) // kc
    carry = lax.fori_loop(0, c_last, lambda c, cr: chunk(c, cr, None), init)
    kpos = c_last * kc + lax.broadcasted_iota(jnp.int32, (Q_BLOCK, kc), 1)
    _, l, acc = chunk(c_last, carry, rep(jnp.where(kpos <= tq, 0.0, NEG)))
    o_slc = acc / l

    wk = WINDOW + Q_BLOCK
    start = pl.multiple_of(Q_BLOCK * jnp.maximum(i - WINDOW // Q_BLOCK, 0), Q_BLOCK)
    kk = kvw_ref[pl.ds(start, wk), 0:128].astype(BF16)
    vx = jnp.concatenate([kvw_ref[pl.ds(start, wk), 128:256].astype(BF16), jnp.ones((wk, LANES), BF16)], axis=1)
    d = tq - (start + lax.broadcasted_iota(jnp.int32, (Q_BLOCK, wk), 1))
    sw = _dot_nt(qs, kk) + rep(jnp.where((d >= 0) & (d < WINDOW), 0.0, NEG))
    aw = _dot(jnp.exp(sw - jnp.max(sw, axis=-1, keepdims=True)).astype(BF16), vx)
    o_win = aw[:, 0:LANES] / aw[:, LANES:LANES + 1]

    a, b, _ = _split3(ng_ref[...])
    ab = jnp.concatenate([a, b], axis=1)
    for h in range(NSA_KV_HEADS):
        gexp = _dot(ab, eg_ref[h])
        for g in range(NSA_GROUP):
            hh = NSA_GROUP * h + g
            rs = slice(Q_BLOCK * hh, Q_BLOCK * (hh + 1))
            gate = lambda br: gexp[:, LANES * (NSA_GROUP * br + g):LANES * (NSA_GROUP * br + g + 1)]
            o = gate(0) * o_cmp[rs] + gate(1) * o_slc[rs] + gate(2) * o_win[rs]
            o_ref[:, LANES * hh:LANES * (hh + 1)] = o.astype(BF16)


def _nsa_prompt(q, kvs, kvw, ck, cv, ng, nb, seq, n_cmp):
    nq = seq // Q_BLOCK
    ncp = ck.shape[1]
    ns = seq // SLC_BLOCK
    assert ns <= NSA_HD and seq >= WINDOW + Q_BLOCK
    et = (jnp.arange(seq)[:, None] // SLC_BLOCK == jnp.arange(LANES)[None, :] % NSA_HD).astype(BF16)
    n_g = 3 * NSA_GROUP
    col = jnp.arange(n_g * LANES)[None, None, :] // LANES
    src = (col // NSA_GROUP) * NSA_HEADS + NSA_GROUP * jnp.arange(NSA_KV_HEADS)[:, None, None] + col % NSA_GROUP
    eg = (jnp.arange(2 * LANES)[None, :, None] % LANES == src).astype(BF16)
    return pl.pallas_call(
        functools.partial(_nsa_prompt_kernel, seq, n_cmp),
        grid=(nb, nq),
        in_specs=[pl.BlockSpec((Q_BLOCK, 1024), lambda b, i: (b * nq + i, 0)),
                  pl.BlockSpec((seq, 256), lambda b, i: (b, 0)),
                  pl.BlockSpec((seq, 256), lambda b, i: (b, 0)),
                  pl.BlockSpec((1, ncp, LANES), lambda b, i: (b, 0, 0)),
                  pl.BlockSpec((1, ncp, LANES), lambda b, i: (b, 0, 0)),
                  pl.BlockSpec((Q_BLOCK, LANES), lambda b, i: (b * nq + i, 0)),
                  pl.BlockSpec((seq, LANES), lambda b, i: (0, 0)),
                  pl.BlockSpec((NSA_KV_HEADS, 2 * LANES, n_g * LANES), lambda b, i: (0, 0, 0))],
        out_specs=pl.BlockSpec((Q_BLOCK, 1024), lambda b, i: (b * nq + i, 0)),
        out_shape=jax.ShapeDtypeStruct((nb * seq, 1024), BF16),
        compiler_params=pltpu.CompilerParams(dimension_semantics=("arbitrary",) * 2, vmem_limit_bytes=VMEM_LIMIT),
        name="nsa_prompt",
    )(q, kvs, kvw, ck, cv, ng, et, eg)


def _nsa_sample_kernel(past, n_pages, dseq, wb, pt_ref, q_ref, kvs_ref, kvw_ref, ck_ref, cv_ref, ng_ref, cwin_ref,
                       e_ref, pool_ref, o_ref, nwin_ref, kvt_scr, kw_scr, sem):
    b = pl.program_id(0)
    slot = b % 2

    def page_copy(bb, sl, j):
        return pltpu.make_async_copy(pool_ref.at[pt_ref[bb, j]], kvt_scr.at[sl, :, pl.ds(PAGE_SIZE * j, PAGE_SIZE)],
                                     sem.at[sl])

    @pl.when(b == 0)
    def _():
        for j in range(n_pages):
            page_copy(0, 0, j).start()

    @pl.when(b + 1 < pl.num_programs(0))
    def _():
        for j in range(n_pages):
            page_copy(b + 1, 1 - slot, j).start(priority=j % 2)

    n_heads = NSA_KV_HEADS * NSA_GROUP
    rows = n_heads * dseq
    kpad = past + PAGE_SIZE
    ns = (past + dseq + SLC_BLOCK - 1) // SLC_BLOCK
    nsp = e_ref.shape[0]
    qf = q_ref[0].astype(F32)
    qs = jnp.concatenate([qf[:, LANES * hh:LANES * (hh + 1)] for hh in range(n_heads)], axis=0).astype(BF16)
    t1 = past + lax.broadcasted_iota(jnp.int32, (dseq, 1), 0)
    t = jnp.concatenate([t1] * n_heads, axis=0)

    sc = _dot_nt(qs, ck_ref[0].astype(BF16))
    jn = lax.broadcasted_iota(jnp.int32, sc.shape, 1)
    pc = _msoftmax(sc, (jn * CMP_STRIDE + CMP_BLOCK - 1) <= t)
    o_cmp = _dot(pc.astype(BF16), cv_ref[0].astype(BF16))

    per_kv = []
    for kvh in range(NSA_KV_HEADS):
        base = kvh * NSA_GROUP * dseq
        s = pc[base:base + dseq]
        for g in range(1, NSA_GROUP):
            s = s + pc[base + g * dseq:base + (g + 1) * dseq]
        per_kv += [s] * NSA_GROUP
    pcs = jnp.concatenate(per_kv, axis=0)
    n_cmp = (ns * SLC_BLOCK) // CMP_STRIDE - CMP_BLOCK // CMP_STRIDE + 1
    imp_t = _importance_t(_overlap_t(nsp, pcs.shape[1], n_cmp), pcs)
    tl = past + lax.broadcasted_iota(jnp.int32, (nsp, rows), 1) % dseq
    score, blk = _block_scores_t(imp_t, tl)
    blk_f = blk.astype(F32)
    sel_f = jnp.zeros(score.shape, F32)
    for _ in range(min(N_SELECT, ns)):
        top = jnp.max(score, axis=0, keepdims=True)
        first = jnp.min(jnp.where(score == top, blk_f, 1e9), axis=0, keepdims=True)
        hit = blk_f == first
        sel_f = jnp.where(hit, 1.0, sel_f)
        score = jnp.where(hit, -3.0e38, score)
    sel_t = sel_f.astype(BF16)
    eye = (lax.broadcasted_iota(jnp.int32, (rows, rows), 0)
           == lax.broadcasted_iota(jnp.int32, (rows, rows), 1)).astype(BF16)
    sel = _dot_nt(eye, sel_t)
    sele = _dot(sel.astype(BF16), e_ref[...])

    knew = jnp.concatenate([kvs_ref[0], jnp.zeros((PAGE_SIZE - dseq, 256), F32)], axis=0).astype(BF16)
    for j in range(n_pages):
        page_copy(b, slot, j).wait()
    s = jnp.concatenate([_dot(qs, kvt_scr[slot, 0:128, :].astype(BF16)), _dot_nt(qs, knew[:, 0:128])], axis=1)
    kpos = lax.broadcasted_iota(jnp.int32, (rows, kpad), 1)
    p = _msoftmax(s, (sele > 0.5) & (kpos <= t)).astype(BF16)
    o_slc = _dot_nt(p[:, 0:past], kvt_scr[slot, 128:256, :].astype(BF16)) + _dot(p[:, past:kpad], knew[:, 128:256])

    wpad = kw_scr.shape[0]
    kw_scr[pl.ds(0, wb), :] = cwin_ref[0]
    kw_scr[pl.ds(wb, dseq), :] = kvw_ref[0]
    kw_scr[pl.ds(wb + dseq, wpad - wb - dseq), :] = jnp.zeros((wpad - wb - dseq, 256), F32)
    sw = _dot_nt(qs, kw_scr[:, 0:128].astype(BF16))
    d = t - (past - wb + lax.broadcasted_iota(jnp.int32, (rows, wpad), 1))
    pw = _msoftmax(sw, (d >= 0) & (d < WINDOW))
    o_win = _dot(pw.astype(BF16), kw_scr[:, 128:256].astype(BF16))
    nwin_ref[0, pl.ds(0, wb - dseq), :] = cwin_ref[0, pl.ds(dseq, wb - dseq), :]
    nwin_ref[0, pl.ds(wb - dseq, dseq), :] = kvw_ref[0]

    ngv = ng_ref[0]
    for hh in range(n_heads):
        rs = slice(dseq * hh, dseq * (hh + 1))
        gate = lambda br: ngv[:, br * NSA_HEADS + hh:br * NSA_HEADS + hh + 1]
        o_ref[0, :, LANES * hh:LANES * (hh + 1)] = gate(0) * o_cmp[rs] + gate(1) * o_slc[rs] + gate(2) * o_win[rs]


def _nsa_sample(page_table, q, kvs, kvw, ck, cv, ng, cwin, e_s, pool, past):
    nb, n_pages = page_table.shape
    dseq = q.shape[1]
    wb = cwin.shape[1]
    kpad = past + PAGE_SIZE
    wpad = wb + PAGE_SIZE
    nsp = e_s.shape[0]
    rows = NSA_HEADS * dseq
    per_b = lambda a: pl.BlockSpec((1,) + a.shape[1:], lambda b, pt: (b,) + (0,) * (a.ndim - 1))
    return pl.pallas_call(
        functools.partial(_nsa_sample_kernel, past, n_pages, dseq, wb),
        grid_spec=pltpu.PrefetchScalarGridSpec(
            num_scalar_prefetch=1, grid=(nb,),
            in_specs=[per_b(q), per_b(kvs), per_b(kvw), per_b(ck), per_b(cv), per_b(ng), per_b(cwin),
                      pl.BlockSpec(e_s.shape, lambda b, pt: (0, 0)), pl.BlockSpec(memory_space=pl.ANY)],
            out_specs=[pl.BlockSpec((1, dseq, 1024), lambda b, pt: (b, 0, 0)),
                       pl.BlockSpec((1, wb, 256), lambda b, pt: (b, 0, 0))],
            scratch_shapes=[pltpu.VMEM((2, 256, past), F32), pltpu.VMEM((wpad, 256), F32),
                            pltpu.SemaphoreType.DMA((2,))]),
        out_shape=[jax.ShapeDtypeStruct((nb, dseq, 1024), F32), jax.ShapeDtypeStruct((nb, wb, 256), F32)],
        compiler_params=pltpu.CompilerParams(dimension_semantics=("arbitrary",), vmem_limit_bytes=VMEM_LIMIT),
        name="nsa_sample",
    )(page_table, q, kvs, kvw, ck, cv, ng, cwin, e_s, pool)


RET_GROUP = 4


def _ret_kernel(bg, c_true, *refs):
    rq_refs, rk_refs, rv_refs, rg_refs = (refs[k * bg:(k + 1) * bg] for k in range(4))
    s0_ref, gro_ref, o_ref, sout_ref, s_scr = refs[4 * bg:]
    c = pl.program_id(1)
    cs = RET_CHUNK

    @pl.when(c == 0)
    def _():
        for b in range(bg):
            for tile in range(RET_HEADS // 2):
                s_scr[b, tile] = jnp.concatenate([s0_ref[b, 2 * tile], s0_ref[b, 2 * tile + 1]], axis=0)

    diff = (lax.broadcasted_iota(jnp.int32, (cs, cs), 0) - lax.broadcasted_iota(jnp.int32, (cs, cs), 1)).astype(F32)
    ic = lax.broadcasted_iota(jnp.int32, (cs, 1), 0).astype(F32)
    lane = lax.broadcasted_iota(jnp.int32, (cs, LANES), 1)
    for hd in range(RET_HEADS):
        lg = math.log(1.0 - 2.0 ** (-5.0 - hd))
        tile, half = hd // 2, hd % 2
        in_half = (lane >= RET_DK * half) & (lane < RET_DK * (half + 1))
        dmat = jnp.where(diff >= 0, jnp.exp(jnp.maximum(diff, 0.0) * lg), 0.0)
        dec_q = jnp.exp((ic + 1.0) * lg)
        dec_k = jnp.exp((c_true - 1.0 - ic) * lg)
        for b in range(bg):
            qm = jnp.where(in_half, rq_refs[b][:, LANES * tile:LANES * (tile + 1)], 0.0).astype(BF16)
            k2 = rk_refs[b][:, LANES * tile:LANES * (tile + 1)]
            vb = rv_refs[b][:, LANES * hd:LANES * (hd + 1)].astype(BF16)
            o = _dot((_dot_nt(qm, k2.astype(BF16)) * dmat).astype(BF16), vb)
            st = s_scr[b, tile]
            o = o + _dot(qm, st.astype(BF16)) * dec_q
            kdec_t = (k2 * dec_k).T
            upd = _dot(kdec_t[RET_DK * half:RET_DK * (half + 1)].astype(BF16), vb)
            s_new = st[RET_DK * half:RET_DK * (half + 1)] * math.exp(c_true * lg) + upd
            s_scr[b, tile, pl.ds(RET_DK * half, RET_DK), :] = s_new
            on = _tilenorm(o, gro_ref[...], RET_DV)
            o_ref[b, :, LANES * hd:LANES * (hd + 1)] = (rg_refs[b][:, LANES * hd:LANES * (hd + 1)] * on).astype(BF16)

    @pl.when(c == pl.num_programs(1) - 1)
    def _():
        for b in range(bg):
            for hd in range(RET_HEADS):
                sout_ref[b, hd] = s_scr[b, hd // 2, pl.ds(RET_DK * (hd % 2), RET_DK), :]


def _retention(rq, rk, rv, rg, s0, gro, nb, n_chunks, c_true):
    cs = RET_CHUNK
    bg = math.gcd(nb, RET_GROUP)
    rows = lambda w: [pl.BlockSpec((cs, w), lambda g, c, b=b: ((g * bg + b) * n_chunks + c, 0)) for b in range(bg)]
    state = pl.BlockSpec((bg, RET_HEADS, RET_DK, RET_DV), lambda g, c: (g, 0, 0, 0))
    return pl.pallas_call(
        functools.partial(_ret_kernel, bg, float(c_true)),
        grid=(nb // bg, n_chunks),
        in_specs=rows(256) + rows(256) + rows(512) + rows(512) + [state, pl.BlockSpec(gro.shape, lambda g, c: (0, 0))],
        out_specs=[pl.BlockSpec((bg, cs, 512), lambda g, c: (g, c, 0)), state],
        out_shape=[jax.ShapeDtypeStruct((nb, n_chunks * cs, 512), BF16),
                   jax.ShapeDtypeStruct((nb, RET_HEADS, RET_DK, RET_DV), F32)],
        scratch_shapes=[pltpu.VMEM((bg, RET_HEADS // 2, LANES, RET_DV), F32)],
        compiler_params=pltpu.CompilerParams(dimension_semantics=("arbitrary", "arbitrary")),
        name="retention",
    )(*([rq] * bg + [rk] * bg + [rv] * bg + [rg] * bg), s0, gro)


def _memkv_kernel(m_ref, g_ref, w_ref, gk_ref, o_ref):
    x = m_ref[...]
    xn = (x * lax.rsqrt(jnp.mean(x * x, axis=-1, keepdims=True) + EPS) * g_ref[...]).astype(BF16)
    hk = _dot(xn, w_ref[...])
    half = MEM_HEADS * MEM_HD
    for hd in range(MEM_HEADS):
        o_ref[:, LANES * hd:LANES * (hd + 1)] = _tilenorm(hk[:, LANES * hd:LANES * (hd + 1)], gk_ref[...], MEM_HD)
    o_ref[:, half:2 * half] = hk[:, half:2 * half]


def _mem_kv(mem2d, g_mem, w_mem, gk_mem, tm):
    n = mem2d.shape[0]
    const = lambda a: pl.BlockSpec(a.shape, lambda i: (0,) * a.ndim)
    return pl.pallas_call(
        _memkv_kernel,
        grid=(n // tm,),
        in_specs=[pl.BlockSpec((tm, D_MODEL), lambda i: (i, 0)), const(g_mem), const(w_mem), const(gk_mem)],
        out_specs=pl.BlockSpec((tm, 1024), lambda i: (i, 0)),
        out_shape=jax.ShapeDtypeStruct((n, 1024), F32),
        compiler_params=pltpu.CompilerParams(dimension_semantics=("arbitrary",)),
        name="mem_kv",
    )(mem2d, g_mem, w_mem, gk_mem)


def _memattn_kernel(q_ref, kv_ref, o_ref):
    half = MEM_HEADS * MEM_HD
    for hd in range(MEM_HEADS):
        cols = slice(LANES * hd, LANES * (hd + 1))
        s = _dot_nt(q_ref[:, cols].astype(BF16), kv_ref[:, cols].astype(BF16))
        p = jnp.exp(s - jnp.max(s, axis=-1, keepdims=True))
        p = p / jnp.sum(p, axis=-1, keepdims=True)
        o = _dot(p.astype(BF16), kv_ref[:, half + LANES * hd:half + LANES * (hd + 1)].astype(BF16))
        o_ref[:, cols] = o.astype(o_ref.dtype)


def _memattn_cache_kernel(q_ref, kv_ref, o_ref):
    stride = 2 * MEM_HEADS
    mem_len = kv_ref.shape[1] // stride
    for hd in range(MEM_HEADS):
        cols = slice(LANES * hd, LANES * (hd + 1))
        k = kv_ref[0, pl.ds(hd, mem_len, stride=stride), :].astype(BF16)
        v = kv_ref[0, pl.ds(MEM_HEADS + hd, mem_len, stride=stride), :].astype(BF16)
        s = _dot_nt(q_ref[:, cols].astype(BF16), k)
        p = jnp.exp(s - jnp.max(s, axis=-1, keepdims=True))
        p = p / jnp.sum(p, axis=-1, keepdims=True)
        o_ref[:, cols] = _dot(p.astype(BF16), v).astype(o_ref.dtype)


def _mem_attend_cache(q2d, cache_rows, dseq):
    nb, n_rows, _ = cache_rows.shape
    return pl.pallas_call(
        _memattn_cache_kernel,
        grid=(nb,),
        in_specs=[pl.BlockSpec((dseq, 512), lambda b: (b, 0)), pl.BlockSpec((1, n_rows, LANES), lambda b: (b, 0, 0))],
        out_specs=pl.BlockSpec((dseq, 512), lambda b: (b, 0)),
        out_shape=jax.ShapeDtypeStruct((nb * dseq, 512), F32),
        compiler_params=pltpu.CompilerParams(dimension_semantics=("arbitrary",)),
        name="mem_attend_cache",
    )(q2d, cache_rows)


def _mem_attend(q2d, kv2d, nb, blocks_per_b, tq, mem_len, out_rows, out_dtype):
    return pl.pallas_call(
        _memattn_kernel,
        grid=(nb, blocks_per_b),
        in_specs=[pl.BlockSpec((tq, 512), lambda b, i: (b * blocks_per_b + i, 0)),
                  pl.BlockSpec((mem_len, 1024), lambda b, i: (b, 0))],
        out_specs=pl.BlockSpec((tq, 512), lambda b, i: (b * blocks_per_b + i, 0)),
        out_shape=jax.ShapeDtypeStruct((out_rows, 512), out_dtype),
        compiler_params=pltpu.CompilerParams(dimension_semantics=("arbitrary", "arbitrary")),
        name="mem_attend",
    )(q2d, kv2d)


def _merge_kernel(nbp, xp_ref, xs_ref, onsap_ref, onsas_ref, oretp_ref, orets_ref, omemp_ref, omems_ref, g_ref,
                  wbg_ref, wn_ref, wr_ref, wm_ref, wo_ref, gffn_ref, wrt_ref, brt_ref, wgu_ref, wd_ref,
                  h_ref, hn_ref, comb_ref, wgu_bf_ref, wd_bf_ref):
    i = pl.program_id(0)
    wgu_bf_ref[...] = wgu_ref[...].astype(BF16)
    wd_bf_ref[...] = wd_ref[...].astype(BF16)
    pick = lambda p_ref, s_ref: jnp.where(i < nbp, p_ref[...], s_ref[...])
    x = pick(xp_ref, xs_ref)
    xn = (x * lax.rsqrt(jnp.mean(x * x, axis=-1, keepdims=True) + EPS) * g_ref[...]).astype(BF16)
    bg = _sigmoid(_dot(xn, wbg_ref[...]))
    mixed = (bg[:, 0:1024] * _dot(pick(onsap_ref, onsas_ref), wn_ref[...])
             + bg[:, 1024:2048] * _dot(pick(oretp_ref, orets_ref), wr_ref[...])
             + bg[:, 2048:3072] * _dot(pick(omemp_ref, omems_ref), wm_ref[...]))
    hres = x + _dot(mixed.astype(BF16), wo_ref[...])
    h_ref[...] = hres
    hn = hres * lax.rsqrt(jnp.mean(hres * hres, axis=-1, keepdims=True) + EPS) * gffn_ref[...]
    hn_ref[...] = hn.astype(BF16)

    a, b, _ = _split3(hn)
    logits = _dot(a, wrt_ref[0]) + _dot(a, wrt_ref[1]) + _dot(b, wrt_ref[0]) + brt_ref[...]
    lane = lax.broadcasted_iota(jnp.int32, logits.shape, 1).astype(F32)
    vals, hots = [], []
    for _k in range(TOP_K):
        mx = jnp.max(logits, axis=-1, keepdims=True)
        idx = jnp.min(jnp.where(logits == mx, lane, float(LANES)), axis=-1, keepdims=True)
        hot = lane == idx
        vals.append(mx)
        hots.append(hot)
        logits = jnp.where(hot, -3.0e38, logits)
    es = [jnp.exp(v - vals[0]) for v in vals]
    den = es[0] + es[1] + es[2] + es[3]
    comb = jnp.zeros(logits.shape, F32)
    for e_k, hot in zip(es, hots):
        comb = comb + jnp.where(hot, e_k / den, 0.0)
    comb_ref[...] = comb


def _merge(xp, xs, onsa, onsa_s, oret, oret_s, omem, omem_s, g_attn, wbg, wn, wr, wm, wo, gffn, wrt, brt,
           w_gate_up, w_down, tm):
    n_p, n_s = xp.shape[0], xs.shape[0]
    nbp, nbs = n_p // tm, n_s // tm
    n_all = n_p + n_s
    n_halves = 2 * N_EXPERTS
    assert nbp + nbs >= n_halves
    row = lambda w: pl.BlockSpec((tm, w), lambda i: (i, 0))
    rowp = lambda w: pl.BlockSpec((tm, w), lambda i: (jnp.minimum(i, nbp - 1), 0))
    rows = lambda w: pl.BlockSpec((tm, w), lambda i: (jnp.maximum(i - nbp, 0), 0))
    const = lambda a: pl.BlockSpec(a.shape, lambda i: (0,) * a.ndim, pipeline_mode=pl.Buffered(1))
    half = lambda i: jnp.minimum(i, n_halves - 1)
    wgu_spec = pl.BlockSpec((1, D_MODEL, D_FF), lambda i: (half(i) // 2, 0, half(i) % 2))
    wd_spec = pl.BlockSpec((1, D_FF // 2, D_MODEL), lambda i: (half(i) // 2, half(i) % 2, 0))
    return pl.pallas_call(
        functools.partial(_merge_kernel, nbp),
        grid=(nbp + nbs,),
        in_specs=[rowp(1024), rows(1024), rowp(1024), rows(1024), rowp(512), rows(512), rowp(512), rows(512),
                  const(g_attn), const(wbg), const(wn), const(wr), const(wm),
                  const(wo), const(gffn), const(wrt), const(brt), wgu_spec, wd_spec],
        out_specs=[row(1024), row(1024), row(LANES), wgu_spec, wd_spec],
        out_shape=[jax.ShapeDtypeStruct((n_all, 1024), F32), jax.ShapeDtypeStruct((n_all, 1024), BF16),
                   jax.ShapeDtypeStruct((n_all, LANES), F32),
                   jax.ShapeDtypeStruct(w_gate_up.shape, BF16), jax.ShapeDtypeStruct(w_down.shape, BF16)],
        compiler_params=pltpu.CompilerParams(dimension_semantics=("arbitrary",), vmem_limit_bytes=VMEM_LIMIT),
        name="merge",
    )(xp, xs, onsa, onsa_s, oret, oret_s, omem, omem_s, g_attn, wbg, wn, wr, wm, wo, gffn, wrt, brt,
      w_gate_up, w_down)


MOE_BLOCK_CAP = 1664
ROUTE_TILE = 128
SLAB_MARGIN = 1.3


def _slabs(tb):
    tok_a = ROUTE_TILE * (-(-(tb // ROUTE_TILE) // 2))
    tok_b = tb - tok_a
    cap = lambda tok: 16 * max(1, -(-int(tok * TOP_K / N_EXPERTS * SLAB_MARGIN) // 16))
    return tok_a, tok_b, cap(tok_a), (cap(tok_b) if tok_b else 0)


def _route_kernel(comb_ref, pos_ref, post_ref, cnt_ref):
    tb = comb_ref.shape[0]
    rt = ROUTE_TILE
    tiles_a = _slabs(tb)[0] // rt
    r = lax.broadcasted_iota(jnp.int32, (rt, rt), 0)
    c = lax.broadcasted_iota(jnp.int32, (rt, rt), 1)
    ltri = (c < r).astype(BF16)
    eye = (c == r).astype(BF16)
    carry = jnp.zeros((1, LANES), F32)
    counts = []
    for s in range(tb // rt):
        if s == tiles_a:
            counts.append(carry)
            carry = jnp.zeros((1, LANES), F32)
        comb = comb_ref[pl.ds(rt * s, rt), :]
        sel = jnp.where(comb > 0.0, 1.0, 0.0)
        pos = jnp.where(sel > 0.0, _dot(ltri, sel.astype(BF16)) + carry, -1.0)
        carry = carry + jnp.sum(sel, axis=0, keepdims=True)
        pos_ref[pl.ds(rt * s, rt), :] = jnp.where(sel > 0.0, pos + 0.5 * comb, -1.0)
        hi = jnp.floor((pos + 1.0) * (1.0 / 256.0))
        lo = (pos + 1.0) - 256.0 * hi
        post_ref[:, rt * s:rt * (s + 1)] = 256.0 * _dot_nt(eye, hi.astype(BF16)) + _dot_nt(eye, lo.astype(BF16)) - 1.0
    counts.append(carry)
    counts += [jnp.zeros((1, LANES), F32)] * (8 - len(counts))
    cnt_ref[0] = jnp.concatenate(counts, axis=0)


def _route(comb, tb):
    n_all = comb.shape[0]
    nblk = n_all // tb
    return pl.pallas_call(
        _route_kernel,
        grid=(nblk,),
        in_specs=[pl.BlockSpec((tb, LANES), lambda i: (i, 0))],
        out_specs=[pl.BlockSpec((tb, LANES), lambda i: (i, 0)), pl.BlockSpec((LANES, tb), lambda i: (0, i)),
                   pl.BlockSpec((1, 8, LANES), lambda i: (i, 0, 0))],
        out_shape=[jax.ShapeDtypeStruct((n_all, LANES), F32), jax.ShapeDtypeStruct((LANES, n_all), F32),
                   jax.ShapeDtypeStruct((nblk, 8, LANES), F32)],
        compiler_params=pltpu.CompilerParams(dimension_semantics=("arbitrary",)),
        name="route",
    )(comb)


def _moe_kernel(n_s, cnt_ref, hn_ref, h_ref, pos_ref, post_ref, wgu_ref, wd_ref, bgu_ref, bd_ref, y_ref, ys_ref):
    i = pl.program_id(0)
    e = pl.program_id(1)
    tb = hn_ref.shape[0]
    tok_a, tok_b, cap_a, cap_b = _slabs(tb)
    halves = [(0, tok_a, 0, cap_a)] + ([(tok_a, tok_b, cap_a, cap_b)] if tok_b else [])

    @pl.when(e == 0)
    def _():
        y_ref[...] = h_ref[...]

    lane = lax.broadcasted_iota(jnp.int32, (tb, LANES), 1)
    packed = jnp.sum(jnp.where(lane == e, pos_ref[...], 0.0), axis=-1, keepdims=True)
    pcol = jnp.floor(packed)
    ccol = 2.0 * (packed - pcol)
    prow = post_ref[pl.ds(e, 1), :]

    def chunk(c, carry):
        cf = c.astype(F32)
        parts = []
        for t0, tok, _, cap in halves:
            slot_g = cf * cap + lax.broadcasted_iota(jnp.int32, (cap, tok), 0).astype(F32)
            gather = jnp.where(prow[:, t0:t0 + tok] == slot_g, 1.0, 0.0).astype(BF16)
            parts.append(_dot(gather, hn_ref[pl.ds(t0, tok), :]).astype(BF16))
        xg = jnp.concatenate(parts, axis=0)
        gu = _dot(xg, wgu_ref[0]) + bgu_ref[0]
        gate = jnp.minimum(gu[:, 0:D_FF], SWIGLU_LIMIT)
        up = jnp.clip(gu[:, D_FF:2 * D_FF], -SWIGLU_LIMIT, SWIGLU_LIMIT)
        act = (up + 1.0) * gate * _sigmoid(SWIGLU_ALPHA * gate)
        yc = (_dot(act.astype(BF16), wd_ref[0]) + bd_ref[0]).astype(BF16)
        for t0, tok, r0, cap in halves:
            slot_s = cf * cap + lax.broadcasted_iota(jnp.int32, (tok, cap), 1).astype(F32)
            scatter = jnp.where(pcol[t0:t0 + tok] == slot_s, ccol[t0:t0 + tok], 0.0).astype(BF16)
            y_ref[pl.ds(t0, tok), :] += _dot(scatter, yc[r0:r0 + cap])
        return carry

    lax.fori_loop(0, cnt_ref[i * N_EXPERTS + e], chunk, 0)

    @pl.when((i == pl.num_programs(0) - 1) & (e == N_EXPERTS - 1))
    def _():
        ys_ref[...] = y_ref[pl.ds(tb - n_s, n_s), :]


def _moe(cnt, hn, h, pos, post, wgu, wd, b_gate_up, b_down, tb, n_s):
    n_all = hn.shape[0]
    n_p = n_all - n_s
    assert n_s <= tb and n_s % 8 == 0 and (n_all // tb - 1) * tb < n_p
    row = lambda w: pl.BlockSpec((tb, w), lambda i, e, cnt: (i, 0))
    return pl.pallas_call(
        functools.partial(_moe_kernel, n_s),
        grid_spec=pltpu.PrefetchScalarGridSpec(
            num_scalar_prefetch=1, grid=(n_all // tb, N_EXPERTS),
            in_specs=[row(1024), row(1024), row(LANES),
                      pl.BlockSpec((LANES, tb), lambda i, e, cnt: (0, i)),
                      pl.BlockSpec((1, D_MODEL, 2 * D_FF), lambda i, e, cnt: (e, 0, 0)),
                      pl.BlockSpec((1, D_FF, D_MODEL), lambda i, e, cnt: (e, 0, 0)),
                      pl.BlockSpec((1, 1, 2 * D_FF), lambda i, e, cnt: (e, 0, 0)),
                      pl.BlockSpec((1, 1, D_MODEL), lambda i, e, cnt: (e, 0, 0))],
            out_specs=[row(1024), pl.BlockSpec((n_s, 1024), lambda i, e, cnt: (0, 0))]),
        out_shape=[jax.ShapeDtypeStruct((n_p, 1024), F32), jax.ShapeDtypeStruct((n_s, 1024), F32)],
        compiler_params=pltpu.CompilerParams(dimension_semantics=("arbitrary",) * 2, vmem_limit_bytes=VMEM_LIMIT),
        name="moe",
    )(cnt, hn, h, pos, post, wgu, wd, b_gate_up, b_down)


def _pick_tile(n, pref):
    t = pref
    while n % t:
        t //= 2
    return t


def kernel(x_prompt, x_sample, cache_cmp, cache_slc, cache_win, state_ret, cache_mem, page_table, mem_prompt,
           g_attn, w_in, g_q_nsa, g_k_cmp, g_k_slc, g_k_win, pe_ck, w_ck1, w_ck2, pe_cv, w_cv1, w_cv2,
           g_ret_out, g_mem, w_mem_kv, g_q_mem, g_k_mem, w_br_nsa, w_br_ret, w_br_mem, w_out,
           g_ffn, w_router, b_router, w_gate_up, b_gate_up, w_down, b_down):
    nb, seq, _ = x_prompt.shape
    db, dseq, _ = x_sample.shape
    n_pages = page_table.shape[1]
    past = n_pages * PAGE_SIZE
    wb = cache_win.shape[1]
    mem_len = mem_prompt.shape[1]
    n_p, n_s = nb * seq, db * dseq
    n_all = n_p + n_s
    tm = _pick_tile(seq, ROW_TILE)

    o = 0
    cols = {}
    for name, wdt in (("q", 512), ("kv", 768), ("ng", 24), ("rq", 256), ("rk", 256), ("rv", 512), ("rg", 512),
                      ("mq", 512), ("bg", 3072)):
        cols[name] = w_in[:, o:o + wdt]
        o += wdt
    wq = cols["q"].reshape(D_MODEL, NSA_HEADS, NSA_HD)
    zq = jnp.zeros_like(wq)
    wq_pad = jnp.concatenate(
        [jnp.concatenate([wq[:, hh], zq[:, hh]] if hh < NSA_GROUP else [zq[:, hh], wq[:, hh]], axis=-1)
         for hh in range(NSA_HEADS)], axis=-1)
    w_ng = jnp.pad(cols["ng"], ((0, 0), (0, C_NG - 24)))
    w1 = jnp.concatenate([wq_pad, cols["kv"], w_ng, cols["rq"], cols["rk"], cols["rv"], cols["rg"], cols["mq"]],
                         axis=-1).astype(BF16)
    w_bg = cols["bg"].astype(BF16)
    two = lambda g: jnp.concatenate([g, g]).reshape(1, LANES)
    r1 = lambda g: g.reshape(1, -1)

    half = RET_DK // 2
    inv = ROPE_BASE ** (-jnp.arange(half, dtype=F32) / half)
    pos = jnp.concatenate([jnp.arange(seq, dtype=jnp.int32),
                           jnp.tile(past + jnp.arange(dseq, dtype=jnp.int32), db)]).astype(F32)
    ang = pos[:, None] * inv[None, :]
    cos_t = jnp.tile(jnp.cos(ang), (1, 2 * RET_HEADS))
    sin_t = jnp.tile(jnp.concatenate([-jnp.sin(ang), jnp.sin(ang)], axis=-1), (1, RET_HEADS))

    xp = x_prompt.reshape(n_p, D_MODEL)
    xs = x_sample.reshape(n_s, D_MODEL)
    (q, kvc, kvs, kvw, ng, rq, rk, rv, rg, mq, kvct, kvst, kvwt) = _project(
        xp, xs, r1(g_attn), w1, cos_t, sin_t, two(g_q_nsa), two(g_k_slc), two(g_k_win), r1(g_q_mem), seq, tm)

    w1s = jnp.stack([w_ck1, w_ck1, w_cv1, w_cv1]).reshape(4, 2, CMP_STRIDE, NSA_HD, CMP_HIDDEN)
    wbig = jnp.einsum("shpdc,st->psdhtc", w1s, jnp.eye(4, dtype=F32)).reshape(CHUNK_W, 2 * 4 * CMP_HIDDEN).astype(BF16)
    pes = jnp.stack([pe_ck, pe_ck, pe_cv, pe_cv]).reshape(4, 2, CMP_STRIDE, NSA_HD)
    pe2 = jnp.pad(jnp.transpose(pes, (1, 2, 0, 3)).reshape(2, CHUNK_W), ((0, 6), (0, 0)))
    zc = jnp.zeros_like(w_ck2)
    bd2 = lambda w: jnp.concatenate([jnp.concatenate([w, zc], 1), jnp.concatenate([zc, w], 1)], 0).astype(BF16)
    gk2 = two(g_k_cmp)

    pages_p = seq // PAGE_SIZE
    pool_p = (kvc if n_all % PAGE_SIZE == 0 else kvc[:n_p]).reshape(-1, CHUNKS_PER_PAGE, CHUNK_W)
    pt_p = jnp.arange(nb * pages_p, dtype=jnp.int32).reshape(nb, pages_p)
    ck_p, cv_p = _compress(pt_p, pool_p, wbig, pe2, bd2(w_ck2), bd2(w_cv2), gk2)
    pages_t = lambda c: jnp.transpose(c, (0, 2, 3, 4, 1)).reshape(-1, 256, PAGE_SIZE)
    ck_s, cv_s = _compress_t(page_table, pages_t(cache_cmp), wbig, pe2, bd2(w_ck2), bd2(w_cv2), gk2)

    n_cmp_p = seq // CMP_STRIDE - CMP_BLOCK // CMP_STRIDE + 1
    onsa = _nsa_prompt(q, kvs, kvw, ck_p, cv_p, ng, nb, seq, n_cmp_p)

    ns_s = (past + dseq + SLC_BLOCK - 1) // SLC_BLOCK
    nsp = (ns_s + 7) // 8 * 8
    kpad = past + PAGE_SIZE
    e_s = (jnp.arange(kpad)[None, :] // SLC_BLOCK == jnp.arange(nsp)[:, None]).astype(BF16)
    s3 = lambda a: a[n_p:].reshape(db, dseq, a.shape[1])
    onsa_s, new_win_s = _nsa_sample(page_table, s3(q), s3(kvs), s3(kvw), ck_s, cv_s, s3(ng),
                                    cache_win.reshape(db, wb, 256), e_s,
                                    pages_t(cache_slc), past)
    onsa_s = onsa_s.reshape(n_s, 1024).astype(BF16)

    gro = r1(g_ret_out)
    oret, ret_state_p = _retention(rq, rk, rv, rg, jnp.zeros((nb, RET_HEADS, RET_DK, RET_DV), F32), gro,
                                   nb, seq // RET_CHUNK, RET_CHUNK)
    padc = lambda a: jnp.pad(s3(a), ((0, 0), (0, RET_CHUNK - dseq), (0, 0))).reshape(db * RET_CHUNK, a.shape[1])
    oret_s, ret_state_s = _retention(padc(rq), padc(rk), padc(rv), padc(rg), state_ret, gro, db, 1, dseq)
    oret = oret.reshape(n_p, 512)
    oret_s = oret_s[:, :dseq].reshape(n_s, 512)

    mem_kv_p = _mem_kv(mem_prompt.reshape(nb * mem_len, D_MODEL), r1(g_mem), w_mem_kv.astype(BF16), r1(g_k_mem),
                       _pick_tile(nb * mem_len, 256))
    tq = _pick_tile(seq, 512)
    omem = _mem_attend(mq, mem_kv_p, nb, seq // tq, tq, mem_len, n_p, BF16)
    omem_s = _mem_attend_cache(mq[n_p:].astype(F32), cache_mem.reshape(db, mem_len * 2 * MEM_HEADS, MEM_HD),
                               dseq).astype(BF16)

    wn = w_br_nsa.reshape(NSA_HEADS, NSA_HD, D_MODEL)
    zn = jnp.zeros_like(wn)
    wn_pad = jnp.concatenate(
        [jnp.concatenate([wn[hh], zn[hh]] if hh < NSA_GROUP else [zn[hh], wn[hh]], axis=0) for hh in range(NSA_HEADS)],
        axis=0).astype(BF16)
    wr_pad = jnp.pad(w_router, ((0, 0), (0, LANES - N_EXPERTS)))
    wr_hi = wr_pad.astype(BF16)
    wr_lo = (wr_pad - wr_hi.astype(F32)).astype(BF16)
    brt = jnp.concatenate([b_router, jnp.full((LANES - N_EXPERTS,), NEG, F32)]).reshape(1, LANES)
    tmm = _pick_tile(math.gcd(seq, n_s), 256)
    h, hn, comb, wgu_bf, wd_bf = _merge(
        xp, xs, onsa, onsa_s, oret, oret_s, omem, omem_s, r1(g_attn), w_bg, wn_pad, w_br_ret.astype(BF16),
        w_br_mem.astype(BF16), w_out.astype(BF16), r1(g_ffn), jnp.stack([wr_hi, wr_lo]), brt, w_gate_up, w_down, tmm)

    tb = max(c for c in range(LANES, MOE_BLOCK_CAP + 1, LANES) if n_all % c == 0)
    pos, post, cnt = _route(comb, tb)
    _, _, cap_a, cap_b = _slabs(tb)
    passes = jnp.ceil(cnt[:, 0, :N_EXPERTS] / cap_a)
    if cap_b:
        passes = jnp.maximum(passes, jnp.ceil(cnt[:, 1, :N_EXPERTS] / cap_b))
    cnt = passes.astype(jnp.int32).reshape(-1)
    y_p, y_s = _moe(cnt, hn, h, pos, post, wgu_bf, wd_bf,
                    b_gate_up.reshape(N_EXPERTS, 1, 2 * D_FF), b_down.reshape(N_EXPERTS, 1, D_MODEL), tb, n_s)

    kv5 = lambda a, bsz, t: a.reshape(bsz, t, 2, NSA_KV_HEADS, NSA_HD)
    wp = min(WINDOW, seq)
    kv5t = lambda a: jnp.transpose(a.reshape(nb, 2, NSA_KV_HEADS, NSA_HD, a.shape[2]), (0, 4, 1, 2, 3))
    return (y_p.reshape(nb, seq, D_MODEL), y_s.reshape(db, dseq, D_MODEL),
            kv5t(kvct), kv5t(kvst), kv5t(kvwt[:, :, seq - wp:]),
            ret_state_p, mem_kv_p.reshape(nb, mem_len, 2, MEM_HEADS, MEM_HD),
            kv5(kvc[n_p:], db, dseq), kv5(kvs[n_p:], db, dseq), kv5(new_win_s, db, wb), ret_state_s)
```
